```python
import math
import jax
import jax.numpy as jnp
from jax import lax
import numpy as np

D_MODEL = 1024
BATCH = 4
SEQ = 4096
DEPTH = 2
DEC_BATCH = 128
DEC_SEQ = 1
PAST_LEN = 8192
PAGE_SIZE = 128

N_EVEN = (DEPTH + 1) // 2
N_ODD = DEPTH // 2
EPS = 1e-6
BLOCK = 128
NEG = -1e30

SW_HEADS = 8
SW_KV_HEADS = 2
SW_GROUP = SW_HEADS // SW_KV_HEADS
SW_HEAD_DIM = 64
SW_WINDOW = 128
SW_ROT_DIM = SW_HEAD_DIM // 4
ROPE_THETA = 500000.0

ML_HEADS = 4
ML_QK_DIM = 64
ML_V_DIM = 128

SSD_HEADS = 16
SSD_HEAD_DIM = 64
SSD_INNER = SSD_HEADS * SSD_HEAD_DIM
SSD_GROUPS = 2
SSD_STATE = 128
SSD_CONV = 4
SSD_CONV_DIM = SSD_INNER + 2 * SSD_GROUPS * SSD_STATE

RET_HEADS = 4
RET_QK_DIM = 128
RET_V_DIM = 128
RET_ROPE_THETA = 10000.0

MEM_LEN = 256
MEM_HEADS = 4
MEM_HEAD_DIM = 128
MEM_WIDTH = MEM_HEADS * MEM_HEAD_DIM

FFN_DIM = 4 * D_MODEL

EVEN_SIZES = (SW_HEADS * SW_HEAD_DIM, SW_KV_HEADS * SW_HEAD_DIM, SW_KV_HEADS * SW_HEAD_DIM,
              ML_HEADS * ML_QK_DIM, ML_HEADS * ML_QK_DIM, ML_HEADS * ML_V_DIM, ML_HEADS * ML_V_DIM,
              ML_HEADS, ML_HEADS)
EVEN_IN = sum(EVEN_SIZES)
EVEN_OUT = SW_HEADS * SW_HEAD_DIM + ML_HEADS * ML_V_DIM
ODD_SIZES = (SSD_INNER, SSD_CONV_DIM, SSD_HEADS,
             RET_HEADS * RET_QK_DIM, RET_HEADS * RET_QK_DIM, RET_HEADS * RET_V_DIM, RET_HEADS * RET_V_DIM)
ODD_IN = sum(ODD_SIZES)
ODD_OUT = SSD_INNER + RET_HEADS * RET_V_DIM

kernel_name = 'hybrid_swa_mlstm_ssd_retention_decode_step'


def split_cols(a, sizes):
    idx = [int(i) for i in np.cumsum(sizes)[:-1]]
    return jnp.split(a, idx, axis=-1)


def rmsnorm(x, g):
    xf = x.astype(jnp.float32)
    y = xf * lax.rsqrt(jnp.mean(xf * xf, axis=-1, keepdims=True) + EPS)
    return (y * g.astype(jnp.float32)).astype(x.dtype)


def group_norm(x, g):
    xf = x.astype(jnp.float32)
    mu = jnp.mean(xf, axis=-1, keepdims=True)
    xc = xf - mu
    var = jnp.mean(xc * xc, axis=-1, keepdims=True)
    return xc * lax.rsqrt(var + EPS) * g.astype(jnp.float32)


def rope(x, pos, rot_dim, theta):
    half = rot_dim // 2
    inv = jnp.power(jnp.float32(theta), -jnp.arange(half, dtype=jnp.float32) * (2.0 / rot_dim))
    ang = pos.astype(jnp.float32)[:, None] * inv[None, :]
    cos = jnp.cos(ang)[:, None, :]
    sin = jnp.sin(ang)[:, None, :]
    xr = x[..., :rot_dim].astype(jnp.float32)
    x1, x2 = xr[..., :half], xr[..., half:]
    rot = jnp.concatenate([x1 * cos - x2 * sin, x2 * cos + x1 * sin], axis=-1).astype(x.dtype)
    return jnp.concatenate([rot, x[..., rot_dim:]], axis=-1)


def to_chunks(a, L):
    a = a.reshape((a.shape[0], a.shape[1] // L, L) + a.shape[2:])
    return a.transpose((1, 0, 3, 2) + tuple(range(4, a.ndim)))


def from_chunks(a):
    a = a.transpose((1, 0, 3, 2) + tuple(range(4, a.ndim)))
    return a.reshape((a.shape[0], a.shape[1] * a.shape[2]) + a.shape[3:])


def sink_probs(scores, valid, sink):
    s = jnp.where(valid, scores, NEG)
    m = jnp.maximum(jnp.max(s, axis=-1, keepdims=True), sink)
    p = jnp.exp(s - m)
    return p / (jnp.sum(p, axis=-1, keepdims=True) + jnp.exp(sink - m))


def swa_prompt(q, k, v, sinks):
    bsz, seq = q.shape[0], q.shape[1]
    nb = seq // BLOCK
    qb = q.astype(jnp.float32).reshape(bsz, nb, BLOCK, SW_KV_HEADS, SW_GROUP, SW_HEAD_DIM)
    kb = k.astype(jnp.float32).reshape(bsz, nb, BLOCK, SW_KV_HEADS, SW_HEAD_DIM)
    vb = v.astype(jnp.float32).reshape(bsz, nb, BLOCK, SW_KV_HEADS, SW_HEAD_DIM)
    kk = jnp.concatenate([jnp.concatenate([jnp.zeros_like(kb[:, :1]), kb[:, :-1]], axis=1), kb], axis=2)
    vv = jnp.concatenate([jnp.concatenate([jnp.zeros_like(vb[:, :1]), vb[:, :-1]], axis=1), vb], axis=2)
    scores = jnp.einsum('bnqkgd,bnskd->bnkgqs', qb, kk) * (SW_HEAD_DIM ** -0.5)
    qpos = BLOCK + jnp.arange(BLOCK)
    kpos = jnp.arange(2 * BLOCK)
    diff = qpos[:, None] - kpos[None, :]
    band = (diff >= 0) & (diff <= SW_WINDOW)
    has_prev = (jnp.arange(nb) > 0)[:, None, None] | (kpos >= BLOCK)[None, None, :]
    valid = (band[None] & has_prev)[None, :, None, None]
    sink = sinks.astype(jnp.float32).reshape(SW_KV_HEADS, SW_GROUP)[None, None, :, :, None, None]
    p = sink_probs(scores, valid, sink)
    o = jnp.einsum('bnkgqs,bnskd->bnqkgd', p, vv)
    return o.reshape(bsz, seq, SW_HEADS * SW_HEAD_DIM)


def swa_step(q, k, v, buf_k, buf_v, sinks):
    bsz, t = q.shape[0], q.shape[1]
    wb = buf_k.shape[1]
    kk = jnp.concatenate([buf_k.astype(jnp.float32), k.astype(jnp.float32)], axis=1)
    vv = jnp.concatenate([buf_v.astype(jnp.float32), v.astype(jnp.float32)], axis=1)
    qg = q.astype(jnp.float32).reshape(bsz, t, SW_KV_HEADS, SW_GROUP, SW_HEAD_DIM)
    scores = jnp.einsum('btkgd,bskd->bkgts', qg, kk) * (SW_HEAD_DIM ** -0.5)
    kpos = jnp.arange(wb + t) - wb
    diff = jnp.arange(t)[:, None] - kpos[None, :]
    valid = ((diff >= 0) & (diff <= SW_WINDOW))[None, None, None]
    sink = sinks.astype(jnp.float32).reshape(SW_KV_HEADS, SW_GROUP)[None, :, :, None, None]
    p = sink_probs(scores, valid, sink)
    o = jnp.einsum('bkgts,bskd->btkgd', p, vv).reshape(bsz, t, SW_HEADS * SW_HEAD_DIM)
    return o, kk[:, -SW_WINDOW:].astype(buf_k.dtype), vv[:, -SW_WINDOW:].astype(buf_v.dtype)


def mlstm_chunked(q, k, v, ig, fg_log, c0, n0, m0):
    seq = q.shape[1]
    L = math.gcd(seq, BLOCK)
    causal = jnp.tril(jnp.ones((L, L), dtype=bool))

    def step(carry, inp):
        c, n, m = carry
        qc, kc, vc, ic, fc = inp
        fcum = jnp.cumsum(fc, axis=-1)
        m_t = fcum + jnp.maximum(m[..., None], lax.cummax(ic - fcum, axis=2))
        logw = fcum[..., :, None] - fcum[..., None, :] + ic[..., None, :] - m_t[..., :, None]
        w = jnp.exp(jnp.where(causal, logw, -jnp.inf))
        carry_scale = jnp.exp(fcum + m[..., None] - m_t)
        sqk = jnp.einsum('bhtd,bhsd->bhts', qc, kc) * w
        num = carry_scale[..., None] * jnp.einsum('bhtd,bhde->bhte', qc, c) + jnp.einsum('bhts,bhse->bhte', sqk, vc)
        den = carry_scale * jnp.einsum('bhtd,bhd->bht', qc, n) + jnp.sum(sqk, axis=-1)
        h = num / jnp.maximum(jnp.abs(den), jnp.exp(-m_t))[..., None]
        m_end = m_t[..., -1]
        f_end = fcum[..., -1]
        w_end = jnp.exp(f_end[..., None] - fcum + ic - m_end[..., None])
        decay = jnp.exp(f_end + m - m_end)
        c = decay[..., None, None] * c + jnp.einsum('bhs,bhsd,bhse->bhde', w_end, kc, vc)
        n = decay[..., None] * n + jnp.einsum('bhs,bhsd->bhd', w_end, kc)
        return (c, n, m_end), h

    f32 = jnp.float32
    inputs = (to_chunks(q.astype(f32), L), to_chunks(k.astype(f32), L), to_chunks(v.astype(f32), L),
              to_chunks(ig.astype(f32), L), to_chunks(fg_log.astype(f32), L))
    (c, n, m), h = lax.scan(step, (c0, n0, m0), inputs)
    return from_chunks(h), c, n, m


def ssd_chunked(x, dt, a, bm, cm, s0):
    bsz, seq = x.shape[0], x.shape[1]
    L = math.gcd(seq, BLOCK)
    hg = SSD_HEADS // SSD_GROUPS
    causal = jnp.tril(jnp.ones((L, L), dtype=bool))
    da = dt * a

    def step(s, inp):
        xc, dtc, dac, bc, cc = inp
        xc = xc.reshape(bsz, SSD_GROUPS, hg, L, SSD_HEAD_DIM)
        dtc = dtc.reshape(bsz, SSD_GROUPS, hg, L)
        dac = dac.reshape(bsz, SSD_GROUPS, hg, L)
        cum = jnp.cumsum(dac, axis=-1)
        seg = cum[..., :, None] - cum[..., None, :]
        lmat = jnp.exp(jnp.where(causal, seg, -jnp.inf))
        cb = jnp.einsum('bgtn,bgsn->bgts', cc, bc)
        wmat = cb[:, :, None] * lmat * dtc[..., None, :]
        y = jnp.einsum('bghts,bghsp->bghtp', wmat, xc) + jnp.exp(cum)[..., None] * jnp.einsum('bgtn,bghpn->bghtp', cc, s)
        w_end = jnp.exp(cum[..., -1:] - cum) * dtc
        s = jnp.exp(cum[..., -1])[..., None, None] * s + jnp.einsum('bghs,bgsn,bghsp->bghpn', w_end, bc, xc)
        return s, y.reshape(bsz, SSD_HEADS, L, SSD_HEAD_DIM)

    inputs = (to_chunks(x, L), to_chunks(dt, L), to_chunks(da, L), to_chunks(bm, L), to_chunks(cm, L))
    s0 = s0.reshape(bsz, SSD_GROUPS, hg, SSD_HEAD_DIM, SSD_STATE)
    s, y = lax.scan(step, s0, inputs)
    return from_chunks(y), s.reshape(bsz, SSD_HEADS, SSD_HEAD_DIM, SSD_STATE)


def retention_chunked(q, k, v, r0):
    seq = q.shape[1]
    L = math.gcd(seq, BLOCK)
    lg = jnp.log(1.0 - jnp.exp2(-5.0 - jnp.arange(RET_HEADS, dtype=jnp.float32)))
    idx = jnp.arange(L, dtype=jnp.float32)
    diff = idx[:, None] - idx[None, :]
    dmat = jnp.exp(jnp.where(diff >= 0, diff[None] * lg[:, None, None], -jnp.inf))
    q_scale = jnp.exp((idx[None] + 1.0) * lg[:, None])
    k_scale = jnp.exp((L - 1.0 - idx[None]) * lg[:, None])
    chunk_decay = jnp.exp(L * lg)

    def step(r, inp):
        qc, kc, vc = inp
        att = jnp.einsum('bhtd,bhsd->bhts', qc, kc) * dmat
        o = jnp.einsum('bhts,bhse->bhte', att, vc) + q_scale[..., None] * jnp.einsum('bhtd,bhde->bhte', qc, r)
        r = chunk_decay[:, None, None] * r + jnp.einsum('bhsd,bhse->bhde', kc * k_scale[..., None], vc)
        return r, o

    f32 = jnp.float32
    inputs = (to_chunks(q.astype(f32), L), to_chunks(k.astype(f32), L), to_chunks(v.astype(f32), L))
    r, o = lax.scan(step, r0, inputs)
    return from_chunks(o), r


def even_mixer(h, pos, w_in, gate_bias, q_norm, k_norm, sinks, out_norm, w_out, swa_cache, ml_state):
    bsz, t, _ = h.shape
    f32 = jnp.float32
    sq, sk, sv, mq, mk, mv, mo, mi, mf = split_cols(h @ w_in, EVEN_SIZES)
    sq = rope(rmsnorm(sq.reshape(bsz, t, SW_HEADS, SW_HEAD_DIM), q_norm), pos, SW_ROT_DIM, ROPE_THETA)
    sk = rope(rmsnorm(sk.reshape(bsz, t, SW_KV_HEADS, SW_HEAD_DIM), k_norm), pos, SW_ROT_DIM, ROPE_THETA)
    sv = sv.reshape(bsz, t, SW_KV_HEADS, SW_HEAD_DIM)
    if swa_cache is None:
        ya = swa_prompt(sq, sk, sv, sinks)
        new_k, new_v = sk[:, -SW_WINDOW:], sv[:, -SW_WINDOW:]
    else:
        ya, new_k, new_v = swa_step(sq, sk, sv, swa_cache[0], swa_cache[1], sinks)
    gb = gate_bias.astype(f32)
    ig = mi.astype(f32) + gb[:ML_HEADS]
    fg_log = jax.nn.log_sigmoid(mf.astype(f32) + gb[ML_HEADS:])
    if ml_state is None:
        c0 = jnp.zeros((bsz, ML_HEADS, ML_QK_DIM, ML_V_DIM), f32)
        n0 = jnp.zeros((bsz, ML_HEADS, ML_QK_DIM), f32)
        m0 = jnp.zeros((bsz, ML_HEADS), f32)
    else:
        c0, n0, m0 = ml_state[0].astype(f32), ml_state[1].astype(f32), ml_state[2].astype(f32)
    hb, c, n, m = mlstm_chunked(mq.reshape(bsz, t, ML_HEADS, ML_QK_DIM),
                                mk.reshape(bsz, t, ML_HEADS, ML_QK_DIM).astype(f32) * (ML_QK_DIM ** -0.5),
                                mv.reshape(bsz, t, ML_HEADS, ML_V_DIM), ig, fg_log, c0, n0, m0)
    hb = rmsnorm(hb, out_norm.reshape(ML_HEADS, ML_V_DIM)).reshape(bsz, t, ML_HEADS * ML_V_DIM) * jax.nn.sigmoid(mo.astype(f32))
    y = jnp.concatenate([ya.astype(h.dtype), hb.astype(h.dtype)], axis=-1) @ w_out
    return y, (new_k, new_v, c, n, m)


def odd_mixer(h, pos, w_in, conv_w, conv_b, dt_bias, a_log, d_skip, ssd_norm, ret_norm, w_out, conv_buf, ssd_state, ret_state):
    bsz, t, _ = h.shape
    f32 = jnp.float32
    z, xbc, dt, rq, rk, rv, rg = split_cols(h @ w_in, ODD_SIZES)
    if conv_buf is None:
        conv_buf = jnp.zeros((bsz, SSD_CONV - 1, SSD_CONV_DIM), xbc.dtype)
    full = jnp.concatenate([conv_buf.astype(xbc.dtype), xbc], axis=1)
    acc = full[:, 0:t].astype(f32) * conv_w[0].astype(f32)
    for j in range(1, SSD_CONV):
        acc = acc + full[:, j:j + t].astype(f32) * conv_w[j].astype(f32)
    xbc_act = jax.nn.silu(acc + conv_b.astype(f32))
    new_buf = full[:, -(SSD_CONV - 1):]
    xs, bm, cm = split_cols(xbc_act, (SSD_INNER, SSD_GROUPS * SSD_STATE, SSD_GROUPS * SSD_STATE))
    dt = jax.nn.softplus(dt.astype(f32) + dt_bias.astype(f32))
    a = -jnp.exp(a_log.astype(f32))
    xs = xs.reshape(bsz, t, SSD_HEADS, SSD_HEAD_DIM)
    s0 = jnp.zeros((bsz, SSD_HEADS, SSD_HEAD_DIM, SSD_STATE), f32) if ssd_state is None else ssd_state.astype(f32)
    yc, s_new = ssd_chunked(xs, dt, a, bm.reshape(bsz, t, SSD_GROUPS, SSD_STATE), cm.reshape(bsz, t, SSD_GROUPS, SSD_STATE), s0)
    yc = yc + d_skip.astype(f32)[:, None] * xs
    yc = (yc.reshape(bsz, t, SSD_INNER) * jax.nn.silu(z.astype(f32))).reshape(bsz, t, SSD_GROUPS, SSD_INNER // SSD_GROUPS)
    yc = rmsnorm(yc, ssd_norm.reshape(SSD_GROUPS, SSD_INNER // SSD_GROUPS)).reshape(bsz, t, SSD_INNER)
    rq = rope(rq.reshape(bsz, t, RET_HEADS, RET_QK_DIM), pos, RET_QK_DIM, RET_ROPE_THETA)
    rk = rope(rk.reshape(bsz, t, RET_HEADS, RET_QK_DIM), pos, RET_QK_DIM, RET_ROPE_THETA).astype(f32) * (RET_QK_DIM ** -0.5)
    rv = rv.reshape(bsz, t, RET_HEADS, RET_V_DIM)
    r0 = jnp.zeros((bsz, RET_HEADS, RET_QK_DIM, RET_V_DIM), f32) if ret_state is None else ret_state.astype(f32)
    yr, r_new = retention_chunked(rq, rk, rv, r0)
    yr = group_norm(yr, ret_norm.reshape(RET_HEADS, RET_V_DIM)).reshape(bsz, t, RET_HEADS * RET_V_DIM) * jax.nn.silu(rg.astype(f32))
    y = jnp.concatenate([yc, yr], axis=-1).astype(h.dtype) @ w_out
    return y, (new_buf, s_new, r_new)


def memory_kv(mem, g, wk, wv, k_norm):
    bsz, mlen, _ = mem.shape
    hm = rmsnorm(mem, g)
    k = rmsnorm((hm @ wk).reshape(bsz, mlen, MEM_HEADS, MEM_HEAD_DIM), k_norm)
    v = (hm @ wv).reshape(bsz, mlen, MEM_HEADS, MEM_HEAD_DIM)
    return k, v


def cross_attn(h, mk, mv, wq, q_norm, wo):
    bsz, t, _ = h.shape
    q = rmsnorm((h @ wq).reshape(bsz, t, MEM_HEADS, MEM_HEAD_DIM), q_norm)
    s = jnp.einsum('bthd,bmhd->bhtm', q.astype(jnp.float32), mk.astype(jnp.float32)) * (MEM_HEAD_DIM ** -0.5)
    p = jax.nn.softmax(s, axis=-1)
    o = jnp.einsum('bhtm,bmhd->bthd', p, mv.astype(jnp.float32)).reshape(bsz, t, MEM_WIDTH)
    return o.astype(h.dtype) @ wo


def ffn(h, w1, w2):
    return jnp.square(jax.nn.relu(h @ w1)) @ w2


def setup_inputs(seed: int = 0) -> dict:
    key = jax.random.key(seed)
    ks = iter(jax.random.split(key, 64))
    f32 = jnp.float32

    def nrm(shape, scale=1.0):
        return scale * jax.random.normal(next(ks), shape, f32)

    def gain(shape):
        return 1.0 + nrm(shape, 0.02)

    win = min(SW_WINDOW, PAST_LEN)
    x_prompt = nrm((BATCH, SEQ, D_MODEL))
    x_sample = nrm((DEC_BATCH, DEC_SEQ, D_MODEL))
    cache_mem_k = nrm((DEPTH, DEC_BATCH, MEM_LEN, MEM_HEADS, MEM_HEAD_DIM))
    cache_mem_v = nrm((DEPTH, DEC_BATCH, MEM_LEN, MEM_HEADS, MEM_HEAD_DIM))
    cache_swa_k = nrm((N_EVEN, DEC_BATCH, win, SW_KV_HEADS, SW_HEAD_DIM))
    cache_swa_v = nrm((N_EVEN, DEC_BATCH, win, SW_KV_HEADS, SW_HEAD_DIM))
    state_mlstm_C = nrm((N_EVEN, DEC_BATCH, ML_HEADS, ML_QK_DIM, ML_V_DIM), 0.5)
    state_mlstm_n = nrm((N_EVEN, DEC_BATCH, ML_HEADS, ML_QK_DIM), 0.5)
    state_mlstm_m = nrm((N_EVEN, DEC_BATCH, ML_HEADS))
    state_ssd_conv = nrm((N_ODD, DEC_BATCH, SSD_CONV - 1, SSD_CONV_DIM))
    state_ssd = nrm((N_ODD, DEC_BATCH, SSD_HEADS, SSD_HEAD_DIM, SSD_STATE), 0.1)
    state_ret = nrm((N_ODD, DEC_BATCH, RET_HEADS, RET_QK_DIM, RET_V_DIM))
    mem_prompt = nrm((BATCH, MEM_LEN, D_MODEL))
    ml_i_bias = -1.0 + nrm((N_EVEN, ML_HEADS), 0.1)
    ml_f_bias = jnp.linspace(3.0, 6.0, ML_HEADS, dtype=f32)[None] + nrm((N_EVEN, ML_HEADS), 0.1)
    dt0 = jnp.exp(jax.random.uniform(next(ks), (N_ODD, SSD_HEADS), f32, math.log(1e-3), math.log(1e-1)))
    dt_bias = dt0 + jnp.log(-jnp.expm1(-dt0))
    a_log = jnp.log(jax.random.uniform(next(ks), (N_ODD, SSD_HEADS), f32, 1.0, 16.0))
    return {
        'x_prompt': x_prompt,
        'x_sample': x_sample,
        'cache_mem_k': cache_mem_k,
        'cache_mem_v': cache_mem_v,
        'cache_swa_k': cache_swa_k,
        'cache_swa_v': cache_swa_v,
        'state_mlstm_C': state_mlstm_C,
        'state_mlstm_n': state_mlstm_n,
        'state_mlstm_m': state_mlstm_m,
        'state_ssd_conv': state_ssd_conv,
        'state_ssd': state_ssd,
        'state_ret': state_ret,
        'mem_prompt': mem_prompt,
        'norm_mix': gain((DEPTH, D_MODEL)),
        'norm_xattn': gain((DEPTH, D_MODEL)),
        'norm_mem': gain((DEPTH, D_MODEL)),
        'norm_ffn': gain((DEPTH, D_MODEL)),
        'even_w_in': nrm((N_EVEN, D_MODEL, EVEN_IN), D_MODEL ** -0.5),
        'mlstm_gate_bias': jnp.concatenate([ml_i_bias, ml_f_bias], axis=-1),
        'swa_q_norm': gain((N_EVEN, SW_HEAD_DIM)),
        'swa_k_norm': gain((N_EVEN, SW_HEAD_DIM)),
        'swa_sinks': nrm((N_EVEN, SW_HEADS), 0.5),
        'mlstm_out_norm': gain((N_EVEN, ML_HEADS * ML_V_DIM)),
        'even_w_out': nrm((N_EVEN, EVEN_OUT, D_MODEL), EVEN_OUT ** -0.5),
        'odd_w_in': nrm((N_ODD, D_MODEL, ODD_IN), D_MODEL ** -0.5),
        'ssd_conv_w': nrm((N_ODD, SSD_CONV, SSD_CONV_DIM), SSD_CONV ** -0.5),
        'ssd_conv_b': nrm((N_ODD, SSD_CONV_DIM), 0.01),
        'ssd_dt_bias': dt_bias,
        'ssd_a_log': a_log,
        'ssd_d': 1.0 + nrm((N_ODD, SSD_HEADS), 0.1),
        'ssd_norm': gain((N_ODD, SSD_INNER)),
        'ret_norm': gain((N_ODD, RET_HEADS * RET_V_DIM)),
        'odd_w_out': nrm((N_ODD, ODD_OUT, D_MODEL), ODD_OUT ** -0.5),
        'mem_wq': nrm((DEPTH, D_MODEL, MEM_WIDTH), D_MODEL ** -0.5),
        'mem_wk': nrm((DEPTH, D_MODEL, MEM_WIDTH), D_MODEL ** -0.5),
        'mem_wv': nrm((DEPTH, D_MODEL, MEM_WIDTH), D_MODEL ** -0.5),
        'mem_q_norm': gain((DEPTH, MEM_HEAD_DIM)),
        'mem_k_norm': gain((DEPTH, MEM_HEAD_DIM)),
        'mem_wo': nrm((DEPTH, MEM_WIDTH, D_MODEL), MEM_WIDTH ** -0.5),
        'ffn_w1': nrm((DEPTH, D_MODEL, FFN_DIM), D_MODEL ** -0.5),
        'ffn_w2': nrm((DEPTH, FFN_DIM, D_MODEL), FFN_DIM ** -0.5),
    }


def reference(x_prompt, x_sample, cache_mem_k, cache_mem_v, cache_swa_k, cache_swa_v, state_mlstm_C, state_mlstm_n,
              state_mlstm_m, state_ssd_conv, state_ssd, state_ret, mem_prompt, norm_mix, norm_xattn, norm_mem, norm_ffn,
              even_w_in, mlstm_gate_bias, swa_q_norm, swa_k_norm, swa_sinks, mlstm_out_norm, even_w_out, odd_w_in,
              ssd_conv_w, ssd_conv_b, ssd_dt_bias, ssd_a_log, ssd_d, ssd_norm, ret_norm, odd_w_out, mem_wq, mem_wk,
              mem_wv, mem_q_norm, mem_k_norm, mem_wo, ffn_w1, ffn_w2):
    pos_p = jnp.arange(x_prompt.shape[1], dtype=jnp.int32)
    pos_s = PAST_LEN + jnp.arange(x_sample.shape[1], dtype=jnp.int32)
    yp, ys = x_prompt, x_sample
    p_mk, p_mv = [], []
    p_swk, p_swv, p_c, p_n, p_m = [], [], [], [], []
    s_swk, s_swv, s_c, s_n, s_m = [], [], [], [], []
    p_conv, p_ssd, p_ret = [], [], []
    s_conv, s_ssd, s_ret = [], [], []
    for l in range(DEPTH):
        hp = rmsnorm(yp, norm_mix[l])
        hs = rmsnorm(ys, norm_mix[l])
        if l % 2 == 0:
            e = l // 2
            w = (even_w_in[e], mlstm_gate_bias[e], swa_q_norm[e], swa_k_norm[e], swa_sinks[e], mlstm_out_norm[e], even_w_out[e])
            op, stp = even_mixer(hp, pos_p, *w, None, None)
            osm, sts = even_mixer(hs, pos_s, *w, (cache_swa_k[e], cache_swa_v[e]),
                                  (state_mlstm_C[e], state_mlstm_n[e], state_mlstm_m[e]))
            for lst, val in zip((p_swk, p_swv, p_c, p_n, p_m), stp):
                lst.append(val)
            for lst, val in zip((s_swk, s_swv, s_c, s_n, s_m), sts):
                lst.append(val)
        else:
            o = l // 2
            w = (odd_w_in[o], ssd_conv_w[o], ssd_conv_b[o], ssd_dt_bias[o], ssd_a_log[o], ssd_d[o], ssd_norm[o], ret_norm[o], odd_w_out[o])
            op, stp = odd_mixer(hp, pos_p, *w, None, None, None)
            osm, sts = odd_mixer(hs, pos_s, *w, state_ssd_conv[o], state_ssd[o], state_ret[o])
            for lst, val in zip((p_conv, p_ssd, p_ret), stp):
                lst.append(val)
            for lst, val in zip((s_conv, s_ssd, s_ret), sts):
                lst.append(val)
        yp = yp + op
        ys = ys + osm
        mk, mv = memory_kv(mem_prompt, norm_mem[l], mem_wk[l], mem_wv[l], mem_k_norm[l])
        p_mk.append(mk)
        p_mv.append(mv)
        yp = yp + cross_attn(rmsnorm(yp, norm_xattn[l]), mk, mv, mem_wq[l], mem_q_norm[l], mem_wo[l])
        ys = ys + cross_attn(rmsnorm(ys, norm_xattn[l]), cache_mem_k[l], cache_mem_v[l], mem_wq[l], mem_q_norm[l], mem_wo[l])
        yp = yp + ffn(rmsnorm(yp, norm_ffn[l]), ffn_w1[l], ffn_w2[l])
        ys = ys + ffn(rmsnorm(ys, norm_ffn[l]), ffn_w1[l], ffn_w2[l])
    return (yp, ys,
            jnp.stack(p_mk), jnp.stack(p_mv), jnp.stack(p_swk), jnp.stack(p_swv),
            jnp.stack(p_c), jnp.stack(p_n), jnp.stack(p_m),
            jnp.stack(p_conv), jnp.stack(p_ssd), jnp.stack(p_ret),
            jnp.stack(s_swk), jnp.stack(s_swv), jnp.stack(s_c), jnp.stack(s_n), jnp.stack(s_m),
            jnp.stack(s_conv), jnp.stack(s_ssd), jnp.stack(s_ret))
```

```python
import functools
import math

import jax
import jax.numpy as jnp
from jax import lax
from jax.experimental import pallas as pl
from jax.experimental.pallas import tpu as pltpu

F32 = jnp.float32
BF16 = jnp.bfloat16

D_MODEL = 1024
PAST_LEN = 8192
EPS = 1e-6
CHUNK = 128
NEG = -1e30

SW_HEADS, SW_KV_HEADS, SW_HEAD_DIM, SW_ROT_DIM = 8, 2, 64, 16
ROPE_THETA = 500000.0
ML_HEADS, ML_QK_DIM, ML_V_DIM = 4, 64, 128
SSD_HEADS, SSD_HEAD_DIM, SSD_GROUPS, SSD_STATE, SSD_CONV = 16, 64, 2, 128, 4
SSD_INNER = SSD_HEADS * SSD_HEAD_DIM
SSD_CONV_DIM = SSD_INNER + 2 * SSD_GROUPS * SSD_STATE
RET_HEADS, RET_QK_DIM, RET_V_DIM = 4, 128, 128
RET_ROPE_THETA = 10000.0
MEM_LEN, MEM_HEADS, MEM_HEAD_DIM = 256, 4, 128
MEM_WIDTH = MEM_HEADS * MEM_HEAD_DIM
FFN_DIM = 4 * D_MODEL
FFN_CHUNK = 512

LANES = 128
SUBLANES = 8
VMEM_LIMIT_BYTES = 56 * 1024 * 1024

E_SQ, E_SK, E_SV, E_MQ, E_MK, E_MV, E_MO, E_GATE, EVEN_COLS = 0, 512, 640, 768, 1024, 1280, 1792, 2304, 2432
GATE_F_LANE = 8
O_Z, O_XBC, O_RQ, O_RK, O_RV, O_RG, O_DT, ODD_COLS = 0, 1024, 2560, 3072, 3584, 4096, 4608, 4736


def _mm(a, b):
    return jnp.dot(a.astype(BF16), b.astype(BF16), preferred_element_type=F32)


def _mm_nt(a, b):
    return lax.dot_general(a.astype(BF16), b.astype(BF16), (((1,), (1,)), ((), ())), preferred_element_type=F32)


def _mm_tn(a, b):
    return lax.dot_general(a.astype(BF16), b.astype(BF16), (((0,), (0,)), ((), ())), preferred_element_type=F32)


def _dg(a, b, ca, cb):
    return lax.dot_general(a, b, (((ca,), (cb,)), ((), ())), preferred_element_type=F32)


def _split3(x):
    hi = x.astype(BF16).astype(F32)
    r1 = x - hi
    mid = r1.astype(BF16).astype(F32)
    lo = (r1 - mid).astype(BF16).astype(F32)
    return hi, mid, lo


def _mm_exact_rhs(x, e):
    hi, mid, lo = _split3(x)
    return _dg(hi, e, 1, 0) + _dg(mid, e, 1, 0) + _dg(lo, e, 1, 0)


def _mm_exact_lhs(e, x):
    hi, mid, lo = _split3(x)
    return _dg(e, hi, 1, 0) + _dg(e, mid, 1, 0) + _dg(e, lo, 1, 0)


def _mm_tn_exact_lhs(x, e):
    hi, mid, lo = _split3(x)
    return _dg(hi, e, 0, 0) + _dg(mid, e, 0, 0) + _dg(lo, e, 0, 0)


def _rms(x, g):
    return x * lax.rsqrt(jnp.mean(x * x, axis=-1, keepdims=True) + EPS) * g


def _seg_rms(x, g, seg_mean):
    return x * lax.rsqrt(_mm_exact_rhs(x * x, seg_mean) + EPS) * g


def _sigmoid(x):
    return 1.0 / (1.0 + jnp.exp(-x))


def _silu(x):
    return x * _sigmoid(x)


def _softplus(x):
    return jnp.maximum(x, 0.0) + jnp.log1p(jnp.exp(-jnp.abs(x)))


def _log_sigmoid(x):
    return -_softplus(-x)


def _rope16(x, cos, sin_lo, sin_hi):
    return x * cos + pltpu.roll(x, LANES - 8, 1) * sin_lo + pltpu.roll(x, 8, 1) * sin_hi


def _rope128(x, cos, sin):
    return x * cos + pltpu.roll(x, 64, 1) * sin


def _iota(shape, dim):
    return lax.broadcasted_iota(jnp.int32, shape, dim)


def _cummax_lanes(x):
    lane = _iota(x.shape, 1)
    shift = 1
    while shift < x.shape[1]:
        x = jnp.maximum(x, jnp.where(lane >= shift, pltpu.roll(x, shift, 1), -jnp.inf))
        shift *= 2
    return x


def _norm_proj_kernel(x_ref, g_ref, w_ref, hn_ref, o_ref, *, chunks, head_norm_cols):
    xn = _rms(x_ref[...], g_ref[...]).astype(BF16)
    for c0, cs in chunks:
        r = jnp.dot(xn, w_ref[:, c0:c0 + cs], preferred_element_type=F32)
        if c0 < head_norm_cols:
            parts = [_rms(r[:, i:i + LANES], hn_ref[...]) for i in range(0, cs, LANES)]
            r = jnp.concatenate(parts, axis=1)
        o_ref[:, c0:c0 + cs] = r


def _col_chunks(n, width=512):
    return tuple((c, min(width, n - c)) for c in range(0, n, width))


def norm_proj(x, g, w, *, tm, head_norm=None, head_norm_cols=0):
    n, d = x.shape
    m = w.shape[1]
    if head_norm is None:
        head_norm = jnp.ones((1, LANES), F32)
    kern = functools.partial(_norm_proj_kernel, chunks=_col_chunks(m), head_norm_cols=head_norm_cols)
    return pl.pallas_call(
        kern,
        grid=(n // tm,),
        in_specs=[
            pl.BlockSpec((tm, d), lambda i: (i, 0)),
            pl.BlockSpec((1, d), lambda i: (0, 0)),
            pl.BlockSpec((d, m), lambda i: (0, 0), pipeline_mode=pl.Buffered(1)),
            pl.BlockSpec((1, LANES), lambda i: (0, 0)),
        ],
        out_specs=pl.BlockSpec((tm, m), lambda i: (i, 0)),
        out_shape=jax.ShapeDtypeStruct((n, m), F32),
        compiler_params=pltpu.CompilerParams(dimension_semantics=("arbitrary",), vmem_limit_bytes=VMEM_LIMIT_BYTES),
        name="norm_proj",
    )(x, g, w, head_norm)


def _ffn(x, g_ref, w1_ref, w2_ref):
    h = _rms(x, g_ref[...]).astype(BF16)
    acc = None
    for c in range(0, FFN_DIM, FFN_CHUNK):
        u = jnp.maximum(jnp.dot(h, w1_ref[:, c:c + FFN_CHUNK], preferred_element_type=F32), 0.0)
        t = jnp.dot((u * u).astype(BF16), w2_ref[c:c + FFN_CHUNK, :], preferred_element_type=F32)
        acc = t if acc is None else acc + t
    return x + acc


def _post_prompt_kernel(x_ref, a_ref, wout_ref, gx_ref, wq_ref, qn_ref, mk_ref, mv_ref, wo_ref, gf_ref, w1_ref, w2_ref,
                        o_ref):
    x = x_ref[...] + _mm(a_ref[...], wout_ref[...])
    q = jnp.dot(_rms(x, gx_ref[...]).astype(BF16), wq_ref[...], preferred_element_type=F32)
    outs = []
    for h in range(MEM_HEADS):
        sl = slice(h * MEM_HEAD_DIM, (h + 1) * MEM_HEAD_DIM)
        qh = _rms(q[:, sl], qn_ref[...])
        s = _mm_nt(qh, mk_ref[:, sl]) * (MEM_HEAD_DIM ** -0.5)
        p = jnp.exp(s - jnp.max(s, axis=-1, keepdims=True))
        p = p / jnp.sum(p, axis=-1, keepdims=True)
        outs.append(_mm(p, mv_ref[:, sl]))
    x = x + _mm(jnp.concatenate(outs, axis=1), wo_ref[...])
    o_ref[...] = _ffn(x, gf_ref, w1_ref, w2_ref)


def post_prompt(x, a, wout, gx, wq, qn, mkv, wo, gf, w1, w2, *, rows_per_batch, tm):
    n, d = x.shape
    ka = a.shape[1]
    tiles = rows_per_batch // tm
    const = lambda shape: pl.BlockSpec(shape, lambda i: (0, 0), pipeline_mode=pl.Buffered(1))
    return pl.pallas_call(
        _post_prompt_kernel,
        grid=(n // tm,),
        in_specs=[
            pl.BlockSpec((tm, d), lambda i: (i, 0)),
            pl.BlockSpec((tm, ka), lambda i: (i, 0)),
            const((ka, d)),
            const((1, d)),
            const((d, MEM_WIDTH)),
            const((1, MEM_HEAD_DIM)),
            pl.BlockSpec((MEM_LEN, MEM_WIDTH), lambda i: (i // tiles, 0)),
            pl.BlockSpec((MEM_LEN, MEM_WIDTH), lambda i: (i // tiles, 1)),
            const((MEM_WIDTH, d)),
            const((1, d)),
            const((d, FFN_DIM)),
            const((FFN_DIM, d)),
        ],
        out_specs=pl.BlockSpec((tm, d), lambda i: (i, 0)),
        out_shape=jax.ShapeDtypeStruct((n, d), F32),
        compiler_params=pltpu.CompilerParams(dimension_semantics=("arbitrary",), vmem_limit_bytes=VMEM_LIMIT_BYTES),
        name="post_prompt",
    )(x, a, wout, gx, wq, qn, mkv, mkv, wo, gf, w1, w2)


def _res_proj_kernel(x_ref, a_ref, w_ref, o_ref):
    o_ref[...] = x_ref[...] + _mm(a_ref[...], w_ref[...])


def res_proj(x, a, w):
    n, d = x.shape
    return pl.pallas_call(
        _res_proj_kernel,
        out_shape=jax.ShapeDtypeStruct((n, d), F32),
        compiler_params=pltpu.CompilerParams(vmem_limit_bytes=VMEM_LIMIT_BYTES),
        name="res_proj",
    )(x, a, w)


def _res_proj_ffn_kernel(x_ref, a_ref, w_ref, gf_ref, w1_ref, w2_ref, o_ref):
    x = x_ref[...] + _mm(a_ref[...], w_ref[...])
    o_ref[...] = _ffn(x, gf_ref, w1_ref, w2_ref)


def res_proj_ffn(x, a, w, gf, w1, w2):
    n, d = x.shape
    return pl.pallas_call(
        _res_proj_ffn_kernel,
        out_shape=jax.ShapeDtypeStruct((n, d), F32),
        compiler_params=pltpu.CompilerParams(vmem_limit_bytes=VMEM_LIMIT_BYTES),
        name="res_proj_ffn",
    )(x, a, w, gf, w1, w2)


def _xattn_sample_kernel(q_ref, mk_ref, mv_ref, o_ref):
    row = _iota((SUBLANES, MEM_WIDTH), 0)
    lane = _iota((SUBLANES, MEM_WIDTH), 1)
    qrows = jnp.where(lane // MEM_HEAD_DIM == row, jnp.broadcast_to(q_ref[0], (SUBLANES, MEM_WIDTH)), 0.0)
    s = _dg(qrows, mk_ref[0], 1, 1) * (MEM_HEAD_DIM ** -0.5)
    p = jnp.exp(s - jnp.max(s, axis=-1, keepdims=True))
    p = p / jnp.sum(p, axis=-1, keepdims=True)
    o = _dg(p, mv_ref[0], 1, 0)
    o_ref[0] = jnp.concatenate(
        [o[h:h + 1, h * MEM_HEAD_DIM:(h + 1) * MEM_HEAD_DIM] for h in range(MEM_HEADS)], axis=1)


def xattn_sample(q, mk, mv):
    b = q.shape[0]
    out = pl.pallas_call(
        _xattn_sample_kernel,
        grid=(b,),
        in_specs=[
            pl.BlockSpec((1, 1, MEM_WIDTH), lambda i: (i, 0, 0)),
            pl.BlockSpec((1, MEM_LEN, MEM_WIDTH), lambda i: (i, 0, 0)),
            pl.BlockSpec((1, MEM_LEN, MEM_WIDTH), lambda i: (i, 0, 0)),
        ],
        out_specs=pl.BlockSpec((1, 1, MEM_WIDTH), lambda i: (i, 0, 0)),
        out_shape=jax.ShapeDtypeStruct((b, 1, MEM_WIDTH), F32),
        compiler_params=pltpu.CompilerParams(dimension_semantics=("arbitrary",)),
        name="xattn_sample",
    )(q.reshape(b, 1, MEM_WIDTH), mk, mv)
    return out.reshape(b, MEM_WIDTH)


def _even_prompt_kernel(p_ref, cos_ref, sinlo_ref, sinhi_ref, qn_ref, kn_ref, sink_ref, segm_ref, triu_ref, gb_ref,
                        onorm_ref,
                        y_ref, kc_ref, vc_ref, caug_ref, m_ref,
                        kprev, vprev, cst, mst):
    n = pl.program_id(1)

    @pl.when(n == 0)
    def _():
        kprev[...] = jnp.zeros_like(kprev)
        vprev[...] = jnp.zeros_like(vprev)
        cst[...] = jnp.zeros_like(cst)
        mst[...] = jnp.zeros_like(mst)

    cos, sinlo, sinhi = cos_ref[...], sinlo_ref[...], sinhi_ref[...]
    segm = segm_ref[...]
    lane = _iota((1, LANES), 1)
    low = lane < 64

    k = _rope16(_seg_rms(p_ref[:, E_SK:E_SK + LANES], kn_ref[...], segm), cos, sinlo, sinhi)
    v = p_ref[:, E_SV:E_SV + LANES]
    kk = jnp.concatenate([kprev[...], k], axis=0)
    vv = jnp.concatenate([vprev[...], v], axis=0)
    kk_sw = pltpu.roll(kk, 64, 1)
    vv_sw = pltpu.roll(vv, 64, 1)
    kvar = {(0, 0): jnp.where(low, kk, 0.0), (0, 1): jnp.where(low, 0.0, kk_sw),
            (1, 0): jnp.where(low, kk_sw, 0.0), (1, 1): jnp.where(low, 0.0, kk)}
    vvar = {(0, 0): vv, (0, 1): vv_sw, (1, 0): vv_sw, (1, 1): vv}
    qi = _iota((CHUNK, 2 * CHUNK), 0)
    si = _iota((CHUNK, 2 * CHUNK), 1)
    valid = (si >= qi) & (si <= qi + CHUNK) & ((si >= CHUNK) | (n > 0))
    for j in range(SW_HEADS // 2):
        sl = slice(E_SQ + j * LANES, E_SQ + (j + 1) * LANES)
        qb = _rope16(_seg_rms(p_ref[:, sl], qn_ref[:, sl], segm), cos, sinlo, sinhi)
        halves = []
        for pos in range(2):
            h = 2 * j + pos
            kv = h // (SW_HEADS // SW_KV_HEADS)
            s = jnp.where(valid, _mm_nt(qb, kvar[(kv, pos)]) * (SW_HEAD_DIM ** -0.5), NEG)
            sink = sink_ref[h]
            m = jnp.maximum(jnp.max(s, axis=-1, keepdims=True), sink)
            pr = jnp.exp(s - m)
            pr = pr / (jnp.sum(pr, axis=-1, keepdims=True) + jnp.exp(sink - m))
            halves.append(_mm(pr, vvar[(kv, pos)]))
        y_ref[:, j * LANES:(j + 1) * LANES] = jnp.where(low, halves[0], halves[1])
    kprev[...] = k
    vprev[...] = v

    gt = p_ref[:, E_GATE:E_GATE + LANES].T
    gi = gt[0:SUBLANES] + gb_ref[0:SUBLANES]
    fl = _log_sigmoid(gt[GATE_F_LANE:GATE_F_LANE + SUBLANES] + gb_ref[SUBLANES:2 * SUBLANES])
    fcum = _mm_exact_rhs(fl, triu_ref[...])
    dd = gi - fcum
    mprev = mst[...]
    mt = fcum + jnp.maximum(mprev, _cummax_lanes(dd))
    fend = jnp.broadcast_to(fcum[:, CHUNK - 1:CHUNK], fcum.shape)
    mend = jnp.broadcast_to(mt[:, CHUNK - 1:CHUNK], mt.shape)
    decay = jnp.exp(fend + mprev - mend)
    rows = jnp.concatenate([fcum - mt, jnp.exp(fcum + mprev - mt), jnp.exp(-mt), jnp.exp(fend - fcum + gi - mend),
                            jnp.zeros((CHUNK - 4 * SUBLANES, CHUNK), F32)], axis=0)
    cols = rows.T
    causal = _iota((CHUNK, CHUNK), 0) >= _iota((CHUNK, CHUNK), 1)
    ones_col = jnp.where(_iota((CHUNK, LANES), 1) == 0, 1.0, 0.0)
    row64 = _iota((CHUNK, 1), 0) < 64
    for j in range(ML_HEADS // 2):
        qblk = p_ref[:, E_MQ + j * LANES:E_MQ + (j + 1) * LANES]
        kblk = p_ref[:, E_MK + j * LANES:E_MK + (j + 1) * LANES] * (ML_QK_DIM ** -0.5)
        c_old = cst[j]
        upd = None
        for pos in range(2):
            h = 2 * j + pos
            msk = low if pos == 0 else jnp.logical_not(low)
            qm = jnp.where(msk, qblk, 0.0)
            logw = cols[:, h:h + 1] + dd[h:h + 1, :]
            w = jnp.exp(jnp.where(causal, logw, -jnp.inf))
            sqk = _mm_nt(qm, kblk) * w
            vh = p_ref[:, E_MV + h * LANES:E_MV + (h + 1) * LANES]
            qc = _mm(qm, c_old)
            cs = cols[:, SUBLANES + h:SUBLANES + h + 1]
            num = cs * qc[:, :ML_V_DIM] + _mm(sqk, vh)
            den = cs * qc[:, ML_V_DIM:ML_V_DIM + 1] + jnp.sum(sqk, axis=-1, keepdims=True)
            hh = num / jnp.maximum(jnp.abs(den), cols[:, 2 * SUBLANES + h:2 * SUBLANES + h + 1])
            hsl = slice(h * ML_V_DIM, (h + 1) * ML_V_DIM)
            hn = _rms(hh, onorm_ref[:, hsl])
            mo = p_ref[:, E_MO + h * ML_V_DIM:E_MO + (h + 1) * ML_V_DIM]
            y_ref[:, SW_HEADS * SW_HEAD_DIM + h * ML_V_DIM:SW_HEADS * SW_HEAD_DIM + (h + 1) * ML_V_DIM] = (
                hn * _sigmoid(mo))
            kw = jnp.where(msk, kblk, 0.0) * cols[:, 3 * SUBLANES + h:3 * SUBLANES + h + 1]
            u = _mm_tn(kw, jnp.concatenate([vh, ones_col], axis=1))
            upd = u if upd is None else upd + u
        dec = jnp.where(row64, decay[2 * j:2 * j + 1, 0:1], decay[2 * j + 1:2 * j + 2, 0:1])
        cst[j] = dec * c_old + upd
    mst[...] = mend

    @pl.when(n == pl.num_programs(1) - 1)
    def _():
        kc_ref[0] = k
        vc_ref[0] = v
        caug_ref[0] = cst[...]
        m_ref[0] = mst[...]


def even_prompt(proj, tabs, qn, kn, sinks, segm, triu, gb, onorm, *, batch, seq):
    nc = seq // CHUNK
    row = lambda b, n: (b * nc + n, 0)
    tab = pl.BlockSpec((CHUNK, LANES), lambda b, n: (n, 0))
    const = lambda shape: pl.BlockSpec(shape, lambda b, n: (0,) * len(shape))
    per_b = lambda shape: pl.BlockSpec((1,) + shape, lambda b, n: (b,) + (0,) * len(shape))
    return pl.pallas_call(
        _even_prompt_kernel,
        grid=(batch, nc),
        in_specs=[
            pl.BlockSpec((CHUNK, EVEN_COLS), row), tab, tab, tab,
            const((1, SW_HEADS * SW_HEAD_DIM)), const((1, LANES)),
            pl.BlockSpec(memory_space=pltpu.SMEM),
            const((LANES, LANES)), const((CHUNK, CHUNK)), const((2 * SUBLANES, LANES)),
            const((1, ML_HEADS * ML_V_DIM)),
        ],
        out_specs=[
            pl.BlockSpec((CHUNK, D_MODEL), row),
            per_b((CHUNK, LANES)), per_b((CHUNK, LANES)),
            per_b((ML_HEADS // 2, 2 * ML_QK_DIM, 2 * ML_V_DIM)), per_b((SUBLANES, LANES)),
        ],
        out_shape=[
            jax.ShapeDtypeStruct((batch * seq, D_MODEL), F32),
            jax.ShapeDtypeStruct((batch, CHUNK, LANES), F32),
            jax.ShapeDtypeStruct((batch, CHUNK, LANES), F32),
            jax.ShapeDtypeStruct((batch, ML_HEADS // 2, 2 * ML_QK_DIM, 2 * ML_V_DIM), F32),
            jax.ShapeDtypeStruct((batch, SUBLANES, LANES), F32),
        ],
        scratch_shapes=[
            pltpu.VMEM((CHUNK, LANES), F32), pltpu.VMEM((CHUNK, LANES), F32),
            pltpu.VMEM((ML_HEADS // 2, 2 * ML_QK_DIM, 2 * ML_V_DIM), F32), pltpu.VMEM((SUBLANES, LANES), F32),
        ],
        compiler_params=pltpu.CompilerParams(dimension_semantics=("arbitrary", "arbitrary"),
                                             vmem_limit_bytes=VMEM_LIMIT_BYTES),
        name="even_prompt",
    )(proj, *tabs, qn, kn, sinks, segm, triu, gb, onorm)


def _odd_prompt_kernel(p_ref, cosr_ref, sinr_ref, convw_ref, convb_ref, dtb_ref, arow_ref, drow_ref, snorm_ref,
                       tril_ref, dmat_ref, qs_ref, ks_ref, cd_ref, rnorm_ref,
                       y_ref, conv_ref, s_ref, r_ref,
                       ext, sst, rst):
    n = pl.program_id(1)
    tail = SUBLANES

    @pl.when(n == 0)
    def _():
        ext[0:tail] = jnp.zeros((tail, SSD_CONV_DIM), F32)
        sst[...] = jnp.zeros_like(sst)
        rst[...] = jnp.zeros_like(rst)

    lane = _iota((1, LANES), 1)
    low = lane < 64
    row64 = _iota((CHUNK, 1), 0) < 64
    causal = _iota((CHUNK, CHUNK), 0) >= _iota((CHUNK, CHUNK), 1)

    ext[tail:tail + CHUNK] = p_ref[:, O_XBC:O_XBC + SSD_CONV_DIM]
    first = tail - (SSD_CONV - 1)
    acc = ext[first:first + CHUNK] * convw_ref[0:1]
    for jj in range(1, SSD_CONV):
        acc = acc + ext[first + jj:first + jj + CHUNK] * convw_ref[jj:jj + 1]
    xact = _silu(acc + convb_ref[...])
    new_tail = ext[CHUNK:CHUNK + tail]
    ext[0:tail] = new_tail

    dt = _softplus(p_ref[:, O_DT:O_DT + LANES] + dtb_ref[...])
    cum = _mm_exact_lhs(tril_ref[...], dt * arow_ref[...])
    cum_t = cum.T
    dt_t = dt.T
    ecum = jnp.exp(cum)
    cend = cum[CHUNK - 1:CHUNK, :]
    wend = jnp.exp(cend - cum) * dt
    eend = jnp.exp(cend)
    pairs_per_group = SSD_HEADS // SSD_GROUPS // 2
    ys = []
    for g in range(SSD_GROUPS):
        bc = xact[:, SSD_INNER + g * SSD_STATE:SSD_INNER + (g + 1) * SSD_STATE]
        cc = xact[:, SSD_INNER + (SSD_GROUPS + g) * SSD_STATE:SSD_INNER + (SSD_GROUPS + g + 1) * SSD_STATE]
        cb = _mm_nt(cc, bc)
        for jg in range(pairs_per_group):
            j = g * pairs_per_group + jg
            ha, hb = 2 * j, 2 * j + 1
            xp = xact[:, j * LANES:(j + 1) * LANES]
            s_old = sst[j]
            y = jnp.where(low, ecum[:, ha:ha + 1], ecum[:, hb:hb + 1]) * _mm_nt(cc, s_old)
            for pos, h in ((0, ha), (1, hb)):
                seg = cum[:, h:h + 1] - cum_t[h:h + 1, :]
                wmat = cb * jnp.exp(jnp.where(causal, seg, -jnp.inf)) * dt_t[h:h + 1, :]
                y = y + _mm(wmat, jnp.where(low if pos == 0 else jnp.logical_not(low), xp, 0.0))
            xw = xp * jnp.where(low, wend[:, ha:ha + 1], wend[:, hb:hb + 1])
            sst[j] = jnp.where(row64, eend[:, ha:ha + 1], eend[:, hb:hb + 1]) * s_old + _mm_tn(xw, bc)
            ys.append(y)
        gs = slice(g * SSD_INNER // SSD_GROUPS, (g + 1) * SSD_INNER // SSD_GROUPS)
        yg = jnp.concatenate(ys[g * pairs_per_group:(g + 1) * pairs_per_group], axis=1)
        yg = (yg + drow_ref[:, gs] * xact[:, gs]) * _silu(p_ref[:, O_Z + gs.start:O_Z + gs.stop])
        y_ref[:, gs] = _rms(yg, snorm_ref[:, gs])

    cosr, sinr = cosr_ref[...], sinr_ref[...]
    for h in range(RET_HEADS):
        hs = h * LANES
        q = _rope128(p_ref[:, O_RQ + hs:O_RQ + hs + LANES], cosr, sinr)
        k = _rope128(p_ref[:, O_RK + hs:O_RK + hs + LANES], cosr, sinr) * (RET_QK_DIM ** -0.5)
        v = p_ref[:, O_RV + hs:O_RV + hs + LANES]
        r_old = rst[h]
        o = _mm(_mm_nt(q, k) * dmat_ref[h], v) + qs_ref[h] * _mm(q, r_old)
        rst[h] = cd_ref[h] * r_old + _mm_tn(k * ks_ref[h], v)
        xc = o - jnp.mean(o, axis=-1, keepdims=True)
        yn = xc * lax.rsqrt(jnp.mean(xc * xc, axis=-1, keepdims=True) + EPS) * rnorm_ref[:, hs:hs + LANES]
        y_ref[:, SSD_INNER + hs:SSD_INNER + hs + LANES] = yn * _silu(p_ref[:, O_RG + hs:O_RG + hs + LANES])

    @pl.when(n == pl.num_programs(1) - 1)
    def _():
        conv_ref[0] = new_tail
        s_ref[0] = sst[...]
        r_ref[0] = rst[...]


def odd_prompt(proj, tabs, convw, convb, dtb, arow, drow, snorm, tril, ret_consts, rnorm, *, batch, seq):
    nc = seq // CHUNK
    row = lambda b, n: (b * nc + n, 0)
    tab = pl.BlockSpec((CHUNK, LANES), lambda b, n: (n, 0))
    const = lambda shape: pl.BlockSpec(shape, lambda b, n: (0,) * len(shape))
    per_b = lambda shape: pl.BlockSpec((1,) + shape, lambda b, n: (b,) + (0,) * len(shape))
    npair = SSD_HEADS // 2
    hc = (RET_HEADS, CHUNK, LANES)
    return pl.pallas_call(
        _odd_prompt_kernel,
        grid=(batch, nc),
        in_specs=[
            pl.BlockSpec((CHUNK, ODD_COLS), row), tab, tab,
            const((SSD_CONV, SSD_CONV_DIM)), const((1, SSD_CONV_DIM)), const((1, LANES)), const((1, LANES)),
            const((1, SSD_INNER)), const((1, SSD_INNER)), const((CHUNK, CHUNK)),
            const(hc), const(hc), const(hc), const(hc), const((1, RET_HEADS * RET_V_DIM)),
        ],
        out_specs=[
            pl.BlockSpec((CHUNK, SSD_INNER + RET_HEADS * RET_V_DIM), row),
            per_b((SUBLANES, SSD_CONV_DIM)), per_b((npair, LANES, SSD_STATE)), per_b(hc),
        ],
        out_shape=[
            jax.ShapeDtypeStruct((batch * seq, SSD_INNER + RET_HEADS * RET_V_DIM), F32),
            jax.ShapeDtypeStruct((batch, SUBLANES, SSD_CONV_DIM), F32),
            jax.ShapeDtypeStruct((batch, npair, LANES, SSD_STATE), F32),
            jax.ShapeDtypeStruct((batch,) + hc, F32),
        ],
        scratch_shapes=[
            pltpu.VMEM((SUBLANES + CHUNK, SSD_CONV_DIM), F32),
            pltpu.VMEM((npair, LANES, SSD_STATE), F32), pltpu.VMEM(hc, F32),
        ],
        compiler_params=pltpu.CompilerParams(dimension_semantics=("arbitrary", "arbitrary"),
                                             vmem_limit_bytes=VMEM_LIMIT_BYTES),
        name="odd_prompt",
    )(proj, *tabs, convw, convb, dtb, arow, drow, snorm, tril, *ret_consts, rnorm)


def _lane_to_rows(g, offset):
    sel = _iota(g.shape, 1) == _iota(g.shape, 0) + offset
    return jnp.sum(jnp.where(sel, g, 0.0), axis=-1, keepdims=True)


def _rows_of_blocks(x, nblk, width):
    row = _iota((SUBLANES, width), 0)
    out = jnp.zeros((SUBLANES, width), F32)
    for b in range(nblk):
        out = jnp.where(row == b, x[:, b * width:(b + 1) * width], out)
    return out


def _even_sample_kernel(p_ref, bk_ref, bv_ref, c_ref, nrow_ref, mrow_ref, cos_ref, sinlo_ref, sinhi_ref, qn_ref,
                        kn_ref, sink_ref, segm_ref, gb_ref, onorm_ref,
                        y_ref, nk_ref, nv_ref, nc_ref, nn_ref, nm_ref):
    R = SUBLANES
    cos, sinlo, sinhi = cos_ref[...], sinlo_ref[...], sinhi_ref[...]
    segm = segm_ref[...]
    P = jnp.broadcast_to(p_ref[0], (R, EVEN_COLS))
    row = _iota((R, LANES), 0)
    lane = _iota((R, LANES), 1)
    low = lane < 64

    k = _rope16(_seg_rms(P[:, E_SK:E_SK + LANES], kn_ref[...], segm), cos, sinlo, sinhi)
    v = P[:, E_SV:E_SV + LANES]
    group = SW_HEADS // SW_KV_HEADS
    qm = jnp.zeros((R, LANES), F32)
    for j in range(SW_HEADS // 2):
        sl = slice(E_SQ + j * LANES, E_SQ + (j + 1) * LANES)
        qb = _rope16(_seg_rms(P[:, sl], qn_ref[:, sl], segm), cos, sinlo, sinhi)
        qb_sw = pltpu.roll(qb, 64, 1)
        for pos in range(2):
            h = 2 * j + pos
            kv = h // group
            src = qb if pos == kv else qb_sw
            qm = jnp.where((row == h) & (low if kv == 0 else jnp.logical_not(low)), src, qm)
    scale = SW_HEAD_DIM ** -0.5
    bk, bv = bk_ref[0], bv_ref[0]
    s = _dg(qm, bk, 1, 1) * scale
    s_new = jnp.sum(qm * k, axis=-1, keepdims=True) * scale
    sink = sink_ref[:, 0:1]
    m = jnp.maximum(jnp.maximum(jnp.max(s, axis=-1, keepdims=True), s_new), sink)
    pr = jnp.exp(s - m)
    p_new = jnp.exp(s_new - m)
    den = jnp.sum(pr, axis=-1, keepdims=True) + p_new + jnp.exp(sink - m)
    o = (_dg(pr, bv, 1, 0) + p_new * v) / den
    o_sw = pltpu.roll(o, 64, 1)
    for j in range(SW_HEADS // 2):
        halves = []
        for pos in range(2):
            h = 2 * j + pos
            src = o if pos == h // group else o_sw
            halves.append(src[h:h + 1, :])
        y_ref[0, :, j * LANES:(j + 1) * LANES] = jnp.where(low[0:1], halves[0], halves[1])
    last = _iota((CHUNK, LANES), 0) == CHUNK - 1
    nk_ref[0] = jnp.where(last, jnp.broadcast_to(k[0:1], (CHUNK, LANES)), pltpu.roll(bk, CHUNK - 1, 0))
    nv_ref[0] = jnp.where(last, jnp.broadcast_to(v[0:1], (CHUNK, LANES)), pltpu.roll(bv, CHUNK - 1, 0))

    g = P[:, E_GATE:E_GATE + LANES]
    ic = _lane_to_rows(g, 0) + gb_ref[0:R, 0:1]
    fl = _log_sigmoid(_lane_to_rows(g, GATE_F_LANE) + gb_ref[R:2 * R, 0:1])
    mprev = _lane_to_rows(jnp.broadcast_to(mrow_ref[0], (R, LANES)), 0)
    mt = jnp.maximum(fl + mprev, ic)
    w = jnp.exp(ic - mt)
    cs = jnp.exp(fl + mprev - mt)
    width = ML_HEADS * ML_QK_DIM
    hrow = _iota((R, width), 0)
    hlane = _iota((R, width), 1) // ML_QK_DIM
    own = hlane == hrow
    qrows = jnp.where(own, P[:, E_MQ:E_MQ + width], 0.0)
    krows = jnp.where(own, P[:, E_MK:E_MK + width] * (ML_QK_DIM ** -0.5), 0.0)
    nrows = jnp.broadcast_to(nrow_ref[0], (R, width))
    c_old = c_ref[0]
    qc = _dg(qrows, c_old, 1, 0)
    qn_dot = jnp.sum(qrows * nrows, axis=-1, keepdims=True)
    sqk = jnp.sum(qrows * krows, axis=-1, keepdims=True) * w
    v4 = _rows_of_blocks(P[:, E_MV:E_MV + ML_HEADS * ML_V_DIM], ML_HEADS, ML_V_DIM)
    mo4 = _rows_of_blocks(P[:, E_MO:E_MO + ML_HEADS * ML_V_DIM], ML_HEADS, ML_V_DIM)
    num = cs * qc + sqk * v4
    dn = cs * qn_dot + sqk
    hh = num / jnp.maximum(jnp.abs(dn), jnp.exp(-mt))
    hn = _rms(hh, onorm_ref[...]) * _sigmoid(mo4)
    for h in range(ML_HEADS):
        y_ref[0, :, SW_HEADS * SW_HEAD_DIM + h * ML_V_DIM:SW_HEADS * SW_HEAD_DIM + (h + 1) * ML_V_DIM] = hn[h:h + 1]
    dec_col = jnp.concatenate(
        [jnp.broadcast_to(cs[h:h + 1, 0:1], (ML_QK_DIM, ML_V_DIM)) for h in range(ML_HEADS)], axis=0)
    nc_ref[0] = dec_col * c_old + _dg(krows * w, v4, 0, 0)
    dec_lanes = jnp.sum(jnp.where(own, cs, 0.0), axis=0, keepdims=True)
    nn_ref[0] = dec_lanes * nrow_ref[0] + jnp.sum(krows * w, axis=0, keepdims=True)
    nm_ref[0] = jnp.broadcast_to(mt, (R, LANES))


def even_sample(proj, bk, bv, c, nrow, mrow, tabs, qn, kn, sink_rows, segm, gb, onorm_rows):
    b = proj.shape[0]
    width = ML_HEADS * ML_QK_DIM
    per_b = lambda shape: pl.BlockSpec((1,) + shape, lambda i: (i,) + (0,) * len(shape))
    const = lambda shape: pl.BlockSpec(shape, lambda i: (0,) * len(shape))
    tab = const((SUBLANES, LANES))
    return pl.pallas_call(
        _even_sample_kernel,
        grid=(b,),
        in_specs=[
            per_b((1, EVEN_COLS)), per_b((CHUNK, LANES)), per_b((CHUNK, LANES)), per_b((width, ML_V_DIM)),
            per_b((1, width)), per_b((1, LANES)), tab, tab, tab,
            const((1, SW_HEADS * SW_HEAD_DIM)), const((1, LANES)), const((SUBLANES, LANES)),
            const((LANES, LANES)), const((2 * SUBLANES, LANES)), const((SUBLANES, ML_V_DIM)),
        ],
        out_specs=[
            per_b((1, D_MODEL)), per_b((CHUNK, LANES)), per_b((CHUNK, LANES)), per_b((width, ML_V_DIM)),
            per_b((1, width)), per_b((SUBLANES, LANES)),
        ],
        out_shape=[
            jax.ShapeDtypeStruct((b, 1, D_MODEL), F32),
            jax.ShapeDtypeStruct((b, CHUNK, LANES), F32), jax.ShapeDtypeStruct((b, CHUNK, LANES), F32),
            jax.ShapeDtypeStruct((b, width, ML_V_DIM), F32), jax.ShapeDtypeStruct((b, 1, width), F32),
            jax.ShapeDtypeStruct((b, SUBLANES, LANES), F32),
        ],
        compiler_params=pltpu.CompilerParams(dimension_semantics=("arbitrary",)),
        name="even_sample",
    )(proj.reshape(b, 1, EVEN_COLS), bk, bv, c, nrow, mrow, *tabs, qn, kn, sink_rows, segm, gb, onorm_rows)


def _odd_sample_kernel(p_ref, cb_ref, s_ref, r_ref, cosr_ref, sinr_ref, convw_ref, convb_ref, dtb_ref, arow_ref,
                       drow_ref, snorm_ref, spread_ref, rdec_ref, rnorm_ref,
                       y_ref, ncb_ref, ns_ref, nr_ref):
    R = SUBLANES
    P = jnp.broadcast_to(p_ref[0], (R, ODD_COLS))
    xbc = P[:, O_XBC:O_XBC + SSD_CONV_DIM]
    buf = cb_ref[0]
    acc = buf[0:1] * convw_ref[0:1]
    for jj in range(1, SSD_CONV - 1):
        acc = acc + buf[jj:jj + 1] * convw_ref[jj:jj + 1]
    acc = acc + xbc[0:1] * convw_ref[SSD_CONV - 1:SSD_CONV]
    xact = jnp.broadcast_to(_silu(acc + convb_ref[...]), (R, SSD_CONV_DIM))
    ncb_ref[0] = jnp.where(_iota((R, SSD_CONV_DIM), 0) == SSD_CONV - 2, xbc, pltpu.roll(buf, R - 1, 0))

    dt = _softplus(P[:, O_DT:O_DT + LANES] + dtb_ref[...])
    dt_l = _mm_exact_rhs(dt, spread_ref[...])
    dec_l = _mm_exact_rhs(jnp.exp(dt * arow_ref[...]), spread_ref[...])
    xs = xact[:, 0:SSD_INNER]
    grow = _iota((R, SSD_INNER), 0)
    glane = _iota((R, SSD_INNER), 1) // (SSD_INNER // SSD_GROUPS)
    gown = glane == grow
    brows = _rows_of_blocks(xact[:, SSD_INNER:SSD_INNER + SSD_GROUPS * SSD_STATE], SSD_GROUPS, SSD_STATE)
    crows = _rows_of_blocks(xact[:, SSD_INNER + SSD_GROUPS * SSD_STATE:SSD_CONV_DIM], SSD_GROUPS, SSD_STATE)
    s_old = s_ref[0]
    xw = jnp.where(gown, xs * dt_l, 0.0)
    ones_rows = jnp.where(_iota((R, LANES), 0) < SSD_GROUPS, 1.0, 0.0)
    dec_col = _mm_tn_exact_lhs(jnp.where(gown, dec_l, 0.0), ones_rows)
    s_new = dec_col * s_old + _dg(xw, brows, 0, 0)
    ns_ref[0] = s_new
    yrows = _dg(crows, s_new, 1, 1)
    y = jnp.sum(jnp.where(gown, yrows, 0.0), axis=0, keepdims=True)
    y = (y + drow_ref[...] * xs[0:1]) * _silu(p_ref[0, :, O_Z:O_Z + SSD_INNER])
    gw = SSD_INNER // SSD_GROUPS
    for g in range(SSD_GROUPS):
        y_ref[0, :, g * gw:(g + 1) * gw] = _rms(y[:, g * gw:(g + 1) * gw], snorm_ref[:, g * gw:(g + 1) * gw])

    cosr, sinr = cosr_ref[...], sinr_ref[...]
    width = RET_HEADS * RET_QK_DIM
    q4 = jnp.concatenate([_rope128(P[:, O_RQ + h * LANES:O_RQ + (h + 1) * LANES], cosr, sinr)
                          for h in range(RET_HEADS)], axis=1)
    k4 = jnp.concatenate([_rope128(P[:, O_RK + h * LANES:O_RK + (h + 1) * LANES], cosr, sinr)
                          for h in range(RET_HEADS)], axis=1) * (RET_QK_DIM ** -0.5)
    own = _iota((R, width), 1) // RET_QK_DIM == _iota((R, width), 0)
    qrows = jnp.where(own, q4, 0.0)
    krows = jnp.where(own, k4, 0.0)
    v4 = _rows_of_blocks(P[:, O_RV:O_RV + RET_HEADS * RET_V_DIM], RET_HEADS, RET_V_DIM)
    g4 = _rows_of_blocks(P[:, O_RG:O_RG + RET_HEADS * RET_V_DIM], RET_HEADS, RET_V_DIM)
    r_old = r_ref[0]
    gam = rdec_ref[:, 0:1]
    att = jnp.sum(qrows * krows, axis=-1, keepdims=True)
    o = att * v4 + gam * _dg(qrows, r_old, 1, 0)
    gam_col = jnp.concatenate(
        [jnp.broadcast_to(rdec_ref[h:h + 1, :], (RET_QK_DIM, RET_V_DIM)) for h in range(RET_HEADS)], axis=0)
    nr_ref[0] = gam_col * r_old + _dg(krows, v4, 0, 0)
    xc = o - jnp.mean(o, axis=-1, keepdims=True)
    yn = xc * lax.rsqrt(jnp.mean(xc * xc, axis=-1, keepdims=True) + EPS) * rnorm_ref[...] * _silu(g4)
    for h in range(RET_HEADS):
        y_ref[0, :, SSD_INNER + h * RET_V_DIM:SSD_INNER + (h + 1) * RET_V_DIM] = yn[h:h + 1]


def odd_sample(proj, cbuf, s, r, tabs, convw, convb, dtb, arow, drow, snorm, spread, rdec_rows, rnorm_rows):
    b = proj.shape[0]
    per_b = lambda shape: pl.BlockSpec((1,) + shape, lambda i: (i,) + (0,) * len(shape))
    const = lambda shape: pl.BlockSpec(shape, lambda i: (0,) * len(shape))
    tab = const((SUBLANES, LANES))
    ywidth = SSD_INNER + RET_HEADS * RET_V_DIM
    sshape = (SSD_HEADS * SSD_HEAD_DIM, SSD_STATE)
    rshape = (RET_HEADS * RET_QK_DIM, RET_V_DIM)
    return pl.pallas_call(
        _odd_sample_kernel,
        grid=(b,),
        in_specs=[
            per_b((1, ODD_COLS)), per_b((SUBLANES, SSD_CONV_DIM)), per_b(sshape), per_b(rshape), tab, tab,
            const((SSD_CONV, SSD_CONV_DIM)), const((1, SSD_CONV_DIM)), const((1, LANES)), const((1, LANES)),
            const((1, SSD_INNER)), const((1, SSD_INNER)), const((LANES, SSD_INNER)),
            const((SUBLANES, LANES)), const((SUBLANES, RET_V_DIM)),
        ],
        out_specs=[per_b((1, ywidth)), per_b((SUBLANES, SSD_CONV_DIM)), per_b(sshape), per_b(rshape)],
        out_shape=[
            jax.ShapeDtypeStruct((b, 1, ywidth), F32),
            jax.ShapeDtypeStruct((b, SUBLANES, SSD_CONV_DIM), F32),
            jax.ShapeDtypeStruct((b,) + sshape, F32), jax.ShapeDtypeStruct((b,) + rshape, F32),
        ],
        compiler_params=pltpu.CompilerParams(dimension_semantics=("arbitrary",)),
        name="odd_sample",
    )(proj.reshape(b, 1, ODD_COLS), cbuf, s, r, *tabs, convw, convb, dtb, arow, drow, snorm, spread, rdec_rows,
      rnorm_rows)


def _pad_cols(w, n):
    return jnp.pad(w, ((0, 0), (0, n - w.shape[1])))


def _even_w_in(w):
    sq, sk, sv, mq, mk, mv, mo, mi, mf = jnp.split(w, [512, 640, 768, 1024, 1280, 1792, 2304, 2308], axis=1)
    gates = jnp.concatenate([_pad_cols(mi, GATE_F_LANE), _pad_cols(mf, LANES - GATE_F_LANE)], axis=1)
    return jnp.concatenate([sq, sk, sv, mq, mk, mv, mo, gates], axis=1).astype(BF16)


def _odd_w_in(w):
    z, xbc, dt, rq, rk, rv, rg = jnp.split(w, [1024, 2560, 2576, 3088, 3600, 4112], axis=1)
    return jnp.concatenate([z, xbc, rq, rk, rv, rg, _pad_cols(dt, LANES)], axis=1).astype(BF16)


def _rope16_tables(pos):
    half = SW_ROT_DIM // 2
    inv = jnp.power(jnp.float32(ROPE_THETA), -jnp.arange(half, dtype=F32) * (2.0 / SW_ROT_DIM))
    ang = pos.astype(F32)[:, None] * inv[None, :]
    cos, sin = jnp.cos(ang), jnp.sin(ang)
    n = pos.shape[0]
    rest = SW_HEAD_DIM - SW_ROT_DIM
    one, zero, zh = jnp.ones((n, rest), F32), jnp.zeros((n, rest), F32), jnp.zeros((n, half), F32)
    tile2 = lambda t: jnp.concatenate([t, t], axis=1)
    return (tile2(jnp.concatenate([cos, cos, one], axis=1)),
            tile2(jnp.concatenate([-sin, zh, zero], axis=1)),
            tile2(jnp.concatenate([zh, sin, zero], axis=1)))


def _rope128_tables(pos):
    half = RET_QK_DIM // 2
    inv = jnp.power(jnp.float32(RET_ROPE_THETA), -jnp.arange(half, dtype=F32) * (2.0 / RET_QK_DIM))
    ang = pos.astype(F32)[:, None] * inv[None, :]
    cos, sin = jnp.cos(ang), jnp.sin(ang)
    return jnp.concatenate([cos, cos], axis=1), jnp.concatenate([-sin, sin], axis=1)


def _ret_consts():
    L = CHUNK
    lg = jnp.log(1.0 - jnp.exp2(-5.0 - jnp.arange(RET_HEADS, dtype=F32)))
    idx = jnp.arange(L, dtype=F32)
    diff = idx[:, None] - idx[None, :]
    dmat = jnp.exp(jnp.where(diff >= 0, diff[None] * lg[:, None, None], -jnp.inf))
    q_scale = jnp.exp((idx[None] + 1.0) * lg[:, None])
    k_scale = jnp.exp((L - 1.0 - idx[None]) * lg[:, None])
    chunk_decay = jnp.exp(L * lg)
    bc = lambda t: jnp.broadcast_to(t[:, :, None], (RET_HEADS, L, LANES))
    cd = jnp.broadcast_to(chunk_decay[:, None, None], (RET_HEADS, L, LANES))
    return dmat, bc(q_scale), bc(k_scale), cd, lg


def _rows8(t):
    return jnp.pad(t, ((0, SUBLANES - t.shape[0]), (0, 0)))


def _gate_bias_rows(gb):
    ib = jnp.broadcast_to(gb[:ML_HEADS, None], (ML_HEADS, LANES))
    fb = jnp.broadcast_to(gb[ML_HEADS:, None], (ML_HEADS, LANES))
    return jnp.concatenate([_rows8(ib), _rows8(fb)], axis=0)


def kernel(x_prompt, x_sample, cache_mem_k, cache_mem_v, cache_swa_k, cache_swa_v, state_mlstm_C, state_mlstm_n,
           state_mlstm_m, state_ssd_conv, state_ssd, state_ret, mem_prompt, norm_mix, norm_xattn, norm_mem, norm_ffn,
           even_w_in, mlstm_gate_bias, swa_q_norm, swa_k_norm, swa_sinks, mlstm_out_norm, even_w_out, odd_w_in,
           ssd_conv_w, ssd_conv_b, ssd_dt_bias, ssd_a_log, ssd_d, ssd_norm, ret_norm, odd_w_out, mem_wq, mem_wk,
           mem_wv, mem_q_norm, mem_k_norm, mem_wo, ffn_w1, ffn_w2):
    bp, seq, d = x_prompt.shape
    bs = x_sample.shape[0]
    depth = norm_mix.shape[0]
    tm = 256

    pos_p = jnp.arange(seq, dtype=jnp.int32)
    pos_s = jnp.full((SUBLANES,), PAST_LEN, dtype=jnp.int32)
    tab16_p, tab16_s = _rope16_tables(pos_p), _rope16_tables(pos_s)
    tab128_p, tab128_s = _rope128_tables(pos_p), _rope128_tables(pos_s)
    dmat, q_scale, k_scale, chunk_decay, lg = _ret_consts()
    rdec_rows = _rows8(jnp.broadcast_to(jnp.exp(lg)[:, None], (RET_HEADS, LANES)))
    ii = jnp.arange(CHUNK)
    triu = (ii[:, None] <= ii[None, :]).astype(F32)
    tril = (ii[:, None] >= ii[None, :]).astype(F32)
    jj = jnp.arange(LANES)
    segm = jnp.where(jj[:, None] // SW_HEAD_DIM == jj[None, :] // SW_HEAD_DIM, 1.0 / SW_HEAD_DIM, 0.0).astype(F32)
    spread = (jj[:, None] == jnp.arange(SSD_INNER)[None, :] // SSD_HEAD_DIM).astype(F32)
    row1 = lambda t: t.reshape(1, -1).astype(F32)
    pad_lanes = lambda t: jnp.pad(t.reshape(1, -1).astype(F32), ((0, 0), (0, LANES - t.shape[-1])))

    yp = x_prompt.reshape(bp * seq, d)
    ys = x_sample.reshape(bs, d)
    mem = mem_prompt.reshape(bp * MEM_LEN, d)
    p_mk, p_mv = [], []
    outs = {}
    for l in range(depth):
        g_mix = row1(norm_mix[l])
        if l % 2 == 0:
            e = l // 2
            w_in = _even_w_in(even_w_in[e])
            w_out = even_w_out[e].astype(BF16)
            qn = row1(jnp.tile(swa_q_norm[e], SW_HEADS))
            kn = row1(jnp.tile(swa_k_norm[e], SW_KV_HEADS))
            gb = _gate_bias_rows(mlstm_gate_bias[e].astype(F32))
            onorm = row1(mlstm_out_norm[e])
            sinks = swa_sinks[e].astype(F32)
            proj_p = norm_proj(yp, g_mix, w_in, tm=tm)
            mix_p, kc, vc, caug, mm = even_prompt(proj_p, tab16_p, qn, kn, sinks, segm, triu, gb, onorm,
                                                  batch=bp, seq=seq)
            outs["p_swk"] = kc.reshape(1, bp, CHUNK, SW_KV_HEADS, SW_HEAD_DIM)
            outs["p_swv"] = vc.reshape(1, bp, CHUNK, SW_KV_HEADS, SW_HEAD_DIM)
            outs["p_c"] = caug[..., :ML_V_DIM].reshape(1, bp, ML_HEADS, ML_QK_DIM, ML_V_DIM)
            outs["p_n"] = caug[..., ML_V_DIM].reshape(1, bp, ML_HEADS, ML_QK_DIM)
            outs["p_m"] = mm[:, :ML_HEADS, 0].reshape(1, bp, ML_HEADS)

            proj_s = norm_proj(ys, g_mix, w_in, tm=bs)
            sink_rows = jnp.broadcast_to(sinks[:, None], (SW_HEADS, LANES))
            onorm_rows = _rows8(mlstm_out_norm[e].astype(F32).reshape(ML_HEADS, ML_V_DIM))
            mix_s, nk, nv, ncst, nn, nm = even_sample(
                proj_s,
                cache_swa_k[e].reshape(bs, CHUNK, LANES), cache_swa_v[e].reshape(bs, CHUNK, LANES),
                state_mlstm_C[e].reshape(bs, ML_HEADS * ML_QK_DIM, ML_V_DIM),
                state_mlstm_n[e].reshape(bs, 1, ML_HEADS * ML_QK_DIM),
                jnp.pad(state_mlstm_m[e], ((0, 0), (0, LANES - ML_HEADS))).reshape(bs, 1, LANES),
                tab16_s, qn, kn, sink_rows, segm, gb, onorm_rows)
            outs["s_swk"] = nk.reshape(1, bs, CHUNK, SW_KV_HEADS, SW_HEAD_DIM)
            outs["s_swv"] = nv.reshape(1, bs, CHUNK, SW_KV_HEADS, SW_HEAD_DIM)
            outs["s_c"] = ncst.reshape(1, bs, ML_HEADS, ML_QK_DIM, ML_V_DIM)
            outs["s_n"] = nn.reshape(1, bs, ML_HEADS, ML_QK_DIM)
            outs["s_m"] = nm[:, :ML_HEADS, 0].reshape(1, bs, ML_HEADS)
        else:
            o = l // 2
            w_in = _odd_w_in(odd_w_in[o])
            w_out = odd_w_out[o].astype(BF16)
            convw = ssd_conv_w[o].astype(F32)
            convb = row1(ssd_conv_b[o])
            dtb = pad_lanes(ssd_dt_bias[o])
            arow = pad_lanes(-jnp.exp(ssd_a_log[o].astype(F32)))
            drow = row1(jnp.repeat(ssd_d[o].astype(F32), SSD_HEAD_DIM))
            snorm = row1(ssd_norm[o])
            rnorm = row1(ret_norm[o])
            proj_p = norm_proj(yp, g_mix, w_in, tm=tm)
            mix_p, ctail, sst, rst = odd_prompt(proj_p, tab128_p, convw, convb, dtb, arow, drow, snorm, tril,
                                                (dmat, q_scale, k_scale, chunk_decay), rnorm, batch=bp, seq=seq)
            outs["p_conv"] = ctail[:, SUBLANES - (SSD_CONV - 1):, :].reshape(1, bp, SSD_CONV - 1, SSD_CONV_DIM)
            outs["p_ssd"] = sst.reshape(1, bp, SSD_HEADS, SSD_HEAD_DIM, SSD_STATE)
            outs["p_ret"] = rst.reshape(1, bp, RET_HEADS, RET_QK_DIM, RET_V_DIM)

            proj_s = norm_proj(ys, g_mix, w_in, tm=bs)
            rnorm_rows = _rows8(ret_norm[o].astype(F32).reshape(RET_HEADS, RET_V_DIM))
            mix_s, ncb, ns, nr = odd_sample(
                proj_s, jnp.pad(state_ssd_conv[o], ((0, 0), (0, SUBLANES - (SSD_CONV - 1)), (0, 0))),
                state_ssd[o].reshape(bs, SSD_HEADS * SSD_HEAD_DIM, SSD_STATE),
                state_ret[o].reshape(bs, RET_HEADS * RET_QK_DIM, RET_V_DIM),
                tab128_s, convw, convb, dtb, arow, drow, snorm, spread, rdec_rows, rnorm_rows)
            outs["s_conv"] = ncb[:, :SSD_CONV - 1].reshape(1, bs, SSD_CONV - 1, SSD_CONV_DIM)
            outs["s_ssd"] = ns.reshape(1, bs, SSD_HEADS, SSD_HEAD_DIM, SSD_STATE)
            outs["s_ret"] = nr.reshape(1, bs, RET_HEADS, RET_QK_DIM, RET_V_DIM)

        wkv = jnp.concatenate([mem_wk[l], mem_wv[l]], axis=1).astype(BF16)
        qnorm = row1(mem_q_norm[l])
        mkv = norm_proj(mem, row1(norm_mem[l]), wkv, tm=tm, head_norm=row1(mem_k_norm[l]), head_norm_cols=MEM_WIDTH)
        p_mk.append(mkv[:, :MEM_WIDTH].reshape(bp, MEM_LEN, MEM_HEADS, MEM_HEAD_DIM))
        p_mv.append(mkv[:, MEM_WIDTH:].reshape(bp, MEM_LEN, MEM_HEADS, MEM_HEAD_DIM))
        wq, wo = mem_wq[l].astype(BF16), mem_wo[l].astype(BF16)
        gx, gf = row1(norm_xattn[l]), row1(norm_ffn[l])
        w1, w2 = ffn_w1[l].astype(BF16), ffn_w2[l].astype(BF16)
        yp = post_prompt(yp, mix_p, w_out, gx, wq, qnorm, mkv, wo, gf, w1, w2, rows_per_batch=seq, tm=tm)

        ys = res_proj(ys, mix_s.reshape(bs, -1), w_out)
        qs = norm_proj(ys, gx, wq, tm=bs, head_norm=qnorm, head_norm_cols=MEM_WIDTH)
        att = xattn_sample(qs, cache_mem_k[l].reshape(bs, MEM_LEN, MEM_WIDTH),
                           cache_mem_v[l].reshape(bs, MEM_LEN, MEM_WIDTH))
        ys = res_proj_ffn(ys, att, wo, gf, w1, w2)

    return (yp.reshape(bp, seq, d), ys.reshape(bs, 1, d),
            jnp.stack(p_mk), jnp.stack(p_mv), outs["p_swk"], outs["p_swv"], outs["p_c"], outs["p_n"], outs["p_m"],
            outs["p_conv"], outs["p_ssd"], outs["p_ret"],
            outs["s_swk"], outs["s_swv"], outs["s_c"], outs["s_n"], outs["s_m"],
            outs["s_conv"], outs["s_ssd"], outs["s_ret"])
```

```python
import functools
import math

import jax
import jax.numpy as jnp
from jax import lax
from jax.experimental import pallas as pl
from jax.experimental.pallas import tpu as pltpu

F32 = jnp.float32
BF16 = jnp.bfloat16

D_MODEL = 1024
PAST_LEN = 8192
EPS = 1e-6
CHUNK = 128
NEG = -1e30

SW_HEADS, SW_KV_HEADS, SW_HEAD_DIM, SW_ROT_DIM = 8, 2, 64, 16
ROPE_THETA = 500000.0
ML_HEADS, ML_QK_DIM, ML_V_DIM = 4, 64, 128
SSD_HEADS, SSD_HEAD_DIM, SSD_GROUPS, SSD_STATE, SSD_CONV = 16, 64, 2, 128, 4
SSD_INNER = SSD_HEADS * SSD_HEAD_DIM
SSD_CONV_DIM = SSD_INNER + 2 * SSD_GROUPS * SSD_STATE
RET_HEADS, RET_QK_DIM, RET_V_DIM = 4, 128, 128
RET_ROPE_THETA = 10000.0
MEM_LEN, MEM_HEADS, MEM_HEAD_DIM = 256, 4, 128
MEM_WIDTH = MEM_HEADS * MEM_HEAD_DIM
FFN_DIM = 4 * D_MODEL
FFN_CHUNK = 512

LANES = 128
SUBLANES = 8
VMEM_LIMIT_BYTES = 56 * 1024 * 1024

E_SQ, E_SK, E_SV, E_MQ, E_MK, E_MV, E_MO, E_GATE, EVEN_COLS = 0, 512, 640, 768, 1024, 1280, 1792, 2304, 2432
GATE_F_LANE = 8
O_Z, O_XBC, O_RQ, O_RK, O_RV, O_RG, O_DT, ODD_COLS = 0, 1024, 2560, 3072, 3584, 4096, 4608, 4736


def _mm(a, b):
    return jnp.dot(a.astype(BF16), b.astype(BF16), preferred_element_type=F32)


def _mm_nt(a, b):
    return lax.dot_general(a.astype(BF16), b.astype(BF16), (((1,), (1,)), ((), ())), preferred_element_type=F32)


def _mm_tn(a, b):
    return lax.dot_general(a.astype(BF16), b.astype(BF16), (((0,), (0,)), ((), ())), preferred_element_type=F32)


def _dg(a, b, ca, cb):
    return lax.dot_general(a, b, (((ca,), (cb,)), ((), ())), preferred_element_type=F32)


def _split3(x):
    hi = x.astype(BF16).astype(F32)
    r1 = x - hi
    mid = r1.astype(BF16).astype(F32)
    lo = (r1 - mid).astype(BF16).astype(F32)
    return hi, mid, lo


def _mm_exact_rhs(x, e):
    hi, mid, lo = _split3(x)
    return _dg(hi, e, 1, 0) + _dg(mid, e, 1, 0) + _dg(lo, e, 1, 0)


def _mm_exact_lhs(e, x):
    hi, mid, lo = _split3(x)
    return _dg(e, hi, 1, 0) + _dg(e, mid, 1, 0) + _dg(e, lo, 1, 0)


def _mm_tn_exact_lhs(x, e):
    hi, mid, lo = _split3(x)
    return _dg(hi, e, 0, 0) + _dg(mid, e, 0, 0) + _dg(lo, e, 0, 0)


def _rms(x, g):
    return x * lax.rsqrt(jnp.mean(x * x, axis=-1, keepdims=True) + EPS) * g


def _seg_rms(x, g, seg_mean):
    return x * lax.rsqrt(_mm_exact_rhs(x * x, seg_mean) + EPS) * g


def _sigmoid(x):
    return 1.0 / (1.0 + jnp.exp(-x))


def _silu(x):
    return x * _sigmoid(x)


def _softplus(x):
    return jnp.maximum(x, 0.0) + jnp.log1p(jnp.exp(-jnp.abs(x)))


def _log_sigmoid(x):
    return -_softplus(-x)


def _rope16(x, cos, sin_lo, sin_hi):
    return x * cos + pltpu.roll(x, LANES - 8, 1) * sin_lo + pltpu.roll(x, 8, 1) * sin_hi


def _rope128(x, cos, sin):
    return x * cos + pltpu.roll(x, 64, 1) * sin


def _iota(shape, dim):
    return lax.broadcasted_iota(jnp.int32, shape, dim)


def _cummax_lanes(x):
    lane = _iota(x.shape, 1)
    shift = 1
    while shift < x.shape[1]:
        x = jnp.maximum(x, jnp.where(lane >= shift, pltpu.roll(x, shift, 1), -jnp.inf))
        shift *= 2
    return x


def _norm_proj_kernel(x_ref, g_ref, w_ref, hn_ref, o_ref, *, chunks, head_norm_cols):
    xn = _rms(x_ref[...], g_ref[...]).astype(BF16)
    for c0, cs in chunks:
        r = jnp.dot(xn, w_ref[:, c0:c0 + cs], preferred_element_type=F32)
        if c0 < head_norm_cols:
            parts = [_rms(r[:, i:i + LANES], hn_ref[...]) for i in range(0, cs, LANES)]
            r = jnp.concatenate(parts, axis=1)
        o_ref[:, c0:c0 + cs] = r


def _col_chunks(n, width=512):
    return tuple((c, min(width, n - c)) for c in range(0, n, width))


def norm_proj(x, g, w, *, tm, head_norm=None, head_norm_cols=0):
    n, d = x.shape
    m = w.shape[1]
    if head_norm is None:
        head_norm = jnp.ones((1, LANES), F32)
    kern = functools.partial(_norm_proj_kernel, chunks=_col_chunks(m), head_norm_cols=head_norm_cols)
    return pl.pallas_call(
        kern,
        grid=(n // tm,),
        in_specs=[
            pl.BlockSpec((tm, d), lambda i: (i, 0)),
            pl.BlockSpec((1, d), lambda i: (0, 0)),
            pl.BlockSpec((d, m), lambda i: (0, 0), pipeline_mode=pl.Buffered(1)),
            pl.BlockSpec((1, LANES), lambda i: (0, 0)),
        ],
        out_specs=pl.BlockSpec((tm, m), lambda i: (i, 0)),
        out_shape=jax.ShapeDtypeStruct((n, m), F32),
        compiler_params=pltpu.CompilerParams(dimension_semantics=("arbitrary",), vmem_limit_bytes=VMEM_LIMIT_BYTES),
        name="norm_proj",
    )(x, g, w, head_norm)


def _ffn(x, g_ref, w1_ref, w2_ref):
    h = _rms(x, g_ref[...]).astype(BF16)
    acc = None
    for c in range(0, FFN_DIM, FFN_CHUNK):
        u = jnp.maximum(jnp.dot(h, w1_ref[:, c:c + FFN_CHUNK], preferred_element_type=F32), 0.0)
        t = jnp.dot((u * u).astype(BF16), w2_ref[c:c + FFN_CHUNK, :], preferred_element_type=F32)
        acc = t if acc is None else acc + t
    return x + acc


def _post_prompt_kernel(x_ref, a_ref, wout_ref, gx_ref, wq_ref, qn_ref, mk_ref, mv_ref, wo_ref, gf_ref, w1_ref, w2_ref,
                        o_ref):
    x = x_ref[...] + _mm(a_ref[...], wout_ref[...])
    q = jnp.dot(_rms(x, gx_ref[...]).astype(BF16), wq_ref[...], preferred_element_type=F32)
    outs = []
    for h in range(MEM_HEADS):
        sl = slice(h * MEM_HEAD_DIM, (h + 1) * MEM_HEAD_DIM)
        qh = _rms(q[:, sl], qn_ref[...])
        s = _mm_nt(qh, mk_ref[:, sl]) * (MEM_HEAD_DIM ** -0.5)
        p = jnp.exp(s - jnp.max(s, axis=-1, keepdims=True))
        p = p / jnp.sum(p, axis=-1, keepdims=True)
        outs.append(_mm(p, mv_ref[:, sl]))
    x = x + _mm(jnp.concatenate(outs, axis=1), wo_ref[...])
    o_ref[...] = _ffn(x, gf_ref, w1_ref, w2_ref)


def post_prompt(x, a, wout, gx, wq, qn, mkv, wo, gf, w1, w2, *, rows_per_batch, tm):
    n, d = x.shape
    ka = a.shape[1]
    tiles = rows_per_batch // tm
    const = lambda shape: pl.BlockSpec(shape, lambda i: (0, 0), pipeline_mode=pl.Buffered(1))
    return pl.pallas_call(
        _post_prompt_kernel,
        grid=(n // tm,),
        in_specs=[
            pl.BlockSpec((tm, d), lambda i: (i, 0)),
            pl.BlockSpec((tm, ka), lambda i: (i, 0)),
            const((ka, d)),
            const((1, d)),
            const((d, MEM_WIDTH)),
            const((1, MEM_HEAD_DIM)),
            pl.BlockSpec((MEM_LEN, MEM_WIDTH), lambda i: (i // tiles, 0)),
            pl.BlockSpec((MEM_LEN, MEM_WIDTH), lambda i: (i // tiles, 1)),
            const((MEM_WIDTH, d)),
            const((1, d)),
            const((d, FFN_DIM)),
            const((FFN_DIM, d)),
        ],
        out_specs=pl.BlockSpec((tm, d), lambda i: (i, 0)),
        out_shape=jax.ShapeDtypeStruct((n, d), F32),
        compiler_params=pltpu.CompilerParams(dimension_semantics=("arbitrary",), vmem_limit_bytes=VMEM_LIMIT_BYTES),
        name="post_prompt",
    )(x, a, wout, gx, wq, qn, mkv, mkv, wo, gf, w1, w2)


def _res_proj_kernel(x_ref, a_ref, w_ref, o_ref):
    o_ref[...] = x_ref[...] + _mm(a_ref[...], w_ref[...])


def res_proj(x, a, w):
    n, d = x.shape
    return pl.pallas_call(
        _res_proj_kernel,
        out_shape=jax.ShapeDtypeStruct((n, d), F32),
        compiler_params=pltpu.CompilerParams(vmem_limit_bytes=VMEM_LIMIT_BYTES),
        name="res_proj",
    )(x, a, w)


def _res_proj_ffn_kernel(x_ref, a_ref, w_ref, gf_ref, w1_ref, w2_ref, o_ref):
    x = x_ref[...] + _mm(a_ref[...], w_ref[...])
    o_ref[...] = _ffn(x, gf_ref, w1_ref, w2_ref)


def res_proj_ffn(x, a, w, gf, w1, w2):
    n, d = x.shape
    return pl.pallas_call(
        _res_proj_ffn_kernel,
        out_shape=jax.ShapeDtypeStruct((n, d), F32),
        compiler_params=pltpu.CompilerParams(vmem_limit_bytes=VMEM_LIMIT_BYTES),
        name="res_proj_ffn",
    )(x, a, w, gf, w1, w2)


def _pair_rows(x, op):
    xb = jnp.broadcast_to(x, (SUBLANES, LANES))
    return op(xb, pltpu.roll(xb, SUBLANES // 2, 0))


def _xattn_sample_kernel(q_ref, mk_ref, mv_ref, o_ref):
    row = _iota((SUBLANES, LANES), 0)
    groups = MEM_LEN * MEM_HEADS // SUBLANES
    for i in range(SUBLANES):
        q8 = jnp.zeros((SUBLANES, LANES), F32)
        for h in range(MEM_HEADS):
            q8 = jnp.where(row % MEM_HEADS == h, q_ref[i:i + 1, h * MEM_HEAD_DIM:(h + 1) * MEM_HEAD_DIM], q8)
        k3 = mk_ref[0, i].reshape(groups, SUBLANES, LANES)
        s = jnp.sum(k3 * q8[None], axis=-1, keepdims=True) * (MEM_HEAD_DIM ** -0.5)
        mx = _pair_rows(jnp.max(s, axis=0), jnp.maximum)
        p = jnp.exp(s - mx[None, :, 0:1])
        den = _pair_rows(jnp.sum(p, axis=0), jnp.add)
        v3 = mv_ref[0, i].reshape(groups, SUBLANES, LANES)
        o8 = _pair_rows(jnp.sum(p * v3, axis=0), jnp.add) / den
        for h in range(MEM_HEADS):
            o_ref[i:i + 1, h * MEM_HEAD_DIM:(h + 1) * MEM_HEAD_DIM] = o8[h:h + 1]


def xattn_sample(q, mk, mv, layer):
    b = q.shape[0]
    mem = pl.BlockSpec((1, SUBLANES, MEM_LEN * MEM_HEADS, MEM_HEAD_DIM), lambda i: (layer, i, 0, 0))
    return pl.pallas_call(
        _xattn_sample_kernel,
        grid=(b // SUBLANES,),
        in_specs=[pl.BlockSpec((SUBLANES, MEM_WIDTH), lambda i: (i, 0)), mem, mem],
        out_specs=pl.BlockSpec((SUBLANES, MEM_WIDTH), lambda i: (i, 0)),
        out_shape=jax.ShapeDtypeStruct((b, MEM_WIDTH), F32),
        compiler_params=pltpu.CompilerParams(dimension_semantics=("arbitrary",), vmem_limit_bytes=VMEM_LIMIT_BYTES),
        name="xattn_sample",
    )(q, mk, mv)


def _even_prompt_kernel(p_ref, cos_ref, sinlo_ref, sinhi_ref, qn_ref, kn_ref, sink_ref, segm_ref, triu_ref, gb_ref,
                        onorm_ref,
                        y_ref, kc_ref, vc_ref, caug_ref, m_ref,
                        kprev, vprev, cst, mst):
    n = pl.program_id(1)

    @pl.when(n == 0)
    def _():
        kprev[...] = jnp.zeros_like(kprev)
        vprev[...] = jnp.zeros_like(vprev)
        cst[...] = jnp.zeros_like(cst)
        mst[...] = jnp.zeros_like(mst)

    cos, sinlo, sinhi = cos_ref[...], sinlo_ref[...], sinhi_ref[...]
    segm = segm_ref[...]
    lane = _iota((1, LANES), 1)
    low = lane < 64

    k = _rope16(_seg_rms(p_ref[:, E_SK:E_SK + LANES], kn_ref[...], segm), cos, sinlo, sinhi)
    v = p_ref[:, E_SV:E_SV + LANES]
    kk = jnp.concatenate([kprev[...], k], axis=0)
    vv = jnp.concatenate([vprev[...], v], axis=0)
    kk_sw = pltpu.roll(kk, 64, 1)
    vv_sw = pltpu.roll(vv, 64, 1)
    kvar = {(0, 0): jnp.where(low, kk, 0.0), (0, 1): jnp.where(low, 0.0, kk_sw),
            (1, 0): jnp.where(low, kk_sw, 0.0), (1, 1): jnp.where(low, 0.0, kk)}
    vvar = {(0, 0): vv, (0, 1): vv_sw, (1, 0): vv_sw, (1, 1): vv}
    qi = _iota((CHUNK, 2 * CHUNK), 0)
    si = _iota((CHUNK, 2 * CHUNK), 1)
    valid = (si >= qi) & (si <= qi + CHUNK) & ((si >= CHUNK) | (n > 0))
    for j in range(SW_HEADS // 2):
        sl = slice(E_SQ + j * LANES, E_SQ + (j + 1) * LANES)
        qb = _rope16(_seg_rms(p_ref[:, sl], qn_ref[:, sl], segm), cos, sinlo, sinhi)
        halves = []
        for pos in range(2):
            h = 2 * j + pos
            kv = h // (SW_HEADS // SW_KV_HEADS)
            s = jnp.where(valid, _mm_nt(qb, kvar[(kv, pos)]) * (SW_HEAD_DIM ** -0.5), NEG)
            sink = sink_ref[h]
            m = jnp.maximum(jnp.max(s, axis=-1, keepdims=True), sink)
            pr = jnp.exp(s - m)
            pr = pr / (jnp.sum(pr, axis=-1, keepdims=True) + jnp.exp(sink - m))
            halves.append(_mm(pr, vvar[(kv, pos)]))
        y_ref[:, j * LANES:(j + 1) * LANES] = jnp.where(low, halves[0], halves[1])
    kprev[...] = k
    vprev[...] = v

    gt = p_ref[:, E_GATE:E_GATE + LANES].T
    gi = gt[0:SUBLANES] + gb_ref[0:SUBLANES]
    fl = _log_sigmoid(gt[GATE_F_LANE:GATE_F_LANE + SUBLANES] + gb_ref[SUBLANES:2 * SUBLANES])
    fcum = _mm_exact_rhs(fl, triu_ref[...])
    dd = gi - fcum
    mprev = mst[...]
    mt = fcum + jnp.maximum(mprev, _cummax_lanes(dd))
    fend = jnp.broadcast_to(fcum[:, CHUNK - 1:CHUNK], fcum.shape)
    mend = jnp.broadcast_to(mt[:, CHUNK - 1:CHUNK], mt.shape)
    decay = jnp.exp(fend + mprev - mend)
    rows = jnp.concatenate([fcum - mt, jnp.exp(fcum + mprev - mt), jnp.exp(-mt), jnp.exp(fend - fcum + gi - mend),
                            jnp.zeros((CHUNK - 4 * SUBLANES, CHUNK), F32)], axis=0)
    cols = rows.T
    causal = _iota((CHUNK, CHUNK), 0) >= _iota((CHUNK, CHUNK), 1)
    ones_col = jnp.where(_iota((CHUNK, LANES), 1) == 0, 1.0, 0.0)
    row64 = _iota((CHUNK, 1), 0) < 64
    for j in range(ML_HEADS // 2):
        qblk = p_ref[:, E_MQ + j * LANES:E_MQ + (j + 1) * LANES]
        kblk = p_ref[:, E_MK + j * LANES:E_MK + (j + 1) * LANES] * (ML_QK_DIM ** -0.5)
        c_old = cst[j]
        upd = None
        for pos in range(2):
            h = 2 * j + pos
            msk = low if pos == 0 else jnp.logical_not(low)
            qm = jnp.where(msk, qblk, 0.0)
            logw = cols[:, h:h + 1] + dd[h:h + 1, :]
            w = jnp.exp(jnp.where(causal, logw, -jnp.inf))
            sqk = _mm_nt(qm, kblk) * w
            vh = p_ref[:, E_MV + h * LANES:E_MV + (h + 1) * LANES]
            qc = _mm(qm, c_old)
            cs = cols[:, SUBLANES + h:SUBLANES + h + 1]
            num = cs * qc[:, :ML_V_DIM] + _mm(sqk, vh)
            den = cs * qc[:, ML_V_DIM:ML_V_DIM + 1] + jnp.sum(sqk, axis=-1, keepdims=True)
            hh = num / jnp.maximum(jnp.abs(den), cols[:, 2 * SUBLANES + h:2 * SUBLANES + h + 1])
            hsl = slice(h * ML_V_DIM, (h + 1) * ML_V_DIM)
            hn = _rms(hh, onorm_ref[:, hsl])
            mo = p_ref[:, E_MO + h * ML_V_DIM:E_MO + (h + 1) * ML_V_DIM]
            y_ref[:, SW_HEADS * SW_HEAD_DIM + h * ML_V_DIM:SW_HEADS * SW_HEAD_DIM + (h + 1) * ML_V_DIM] = (
                hn * _sigmoid(mo))
            kw = jnp.where(msk, kblk, 0.0) * cols[:, 3 * SUBLANES + h:3 * SUBLANES + h + 1]
            u = _mm_tn(kw, jnp.concatenate([vh, ones_col], axis=1))
            upd = u if upd is None else upd + u
        dec = jnp.where(row64, decay[2 * j:2 * j + 1, 0:1], decay[2 * j + 1:2 * j + 2, 0:1])
        cst[j] = dec * c_old + upd
    mst[...] = mend

    @pl.when(n == pl.num_programs(1) - 1)
    def _():
        kc_ref[0] = k
        vc_ref[0] = v
        caug_ref[0] = cst[...]
        m_ref[0] = mst[...]


def even_prompt(proj, tabs, qn, kn, sinks, segm, triu, gb, onorm, *, batch, seq):
    nc = seq // CHUNK
    row = lambda b, n: (b * nc + n, 0)
    tab = pl.BlockSpec((CHUNK, LANES), lambda b, n: (n, 0))
    const = lambda shape: pl.BlockSpec(shape, lambda b, n: (0,) * len(shape))
    per_b = lambda shape: pl.BlockSpec((1,) + shape, lambda b, n: (b,) + (0,) * len(shape))
    return pl.pallas_call(
        _even_prompt_kernel,
        grid=(batch, nc),
        in_specs=[
            pl.BlockSpec((CHUNK, EVEN_COLS), row), tab, tab, tab,
            const((1, SW_HEADS * SW_HEAD_DIM)), const((1, LANES)),
            pl.BlockSpec(memory_space=pltpu.SMEM),
            const((LANES, LANES)), const((CHUNK, CHUNK)), const((2 * SUBLANES, LANES)),
            const((1, ML_HEADS * ML_V_DIM)),
        ],
        out_specs=[
            pl.BlockSpec((CHUNK, D_MODEL), row),
            per_b((CHUNK, LANES)), per_b((CHUNK, LANES)),
            per_b((ML_HEADS // 2, 2 * ML_QK_DIM, 2 * ML_V_DIM)), per_b((SUBLANES, LANES)),
        ],
        out_shape=[
            jax.ShapeDtypeStruct((batch * seq, D_MODEL), F32),
            jax.ShapeDtypeStruct((batch, CHUNK, LANES), F32),
            jax.ShapeDtypeStruct((batch, CHUNK, LANES), F32),
            jax.ShapeDtypeStruct((batch, ML_HEADS // 2, 2 * ML_QK_DIM, 2 * ML_V_DIM), F32),
            jax.ShapeDtypeStruct((batch, SUBLANES, LANES), F32),
        ],
        scratch_shapes=[
            pltpu.VMEM((CHUNK, LANES), F32), pltpu.VMEM((CHUNK, LANES), F32),
            pltpu.VMEM((ML_HEADS // 2, 2 * ML_QK_DIM, 2 * ML_V_DIM), F32), pltpu.VMEM((SUBLANES, LANES), F32),
        ],
        compiler_params=pltpu.CompilerParams(dimension_semantics=("arbitrary", "arbitrary"),
                                             vmem_limit_bytes=VMEM_LIMIT_BYTES),
        name="even_prompt",
    )(proj, *tabs, qn, kn, sinks, segm, triu, gb, onorm)


def _odd_prompt_kernel(p_ref, cosr_ref, sinr_ref, convw_ref, convb_ref, dtb_ref, arow_ref, drow_ref, snorm_ref,
                       tril_ref, dmat_ref, qs_ref, ks_ref, cd_ref, rnorm_ref,
                       y_ref, conv_ref, s_ref, r_ref,
                       ext, sst, rst):
    n = pl.program_id(1)
    tail = SUBLANES

    @pl.when(n == 0)
    def _():
        ext[0:tail] = jnp.zeros((tail, SSD_CONV_DIM), F32)
        sst[...] = jnp.zeros_like(sst)
        rst[...] = jnp.zeros_like(rst)

    lane = _iota((1, LANES), 1)
    low = lane < 64
    row64 = _iota((CHUNK, 1), 0) < 64
    causal = _iota((CHUNK, CHUNK), 0) >= _iota((CHUNK, CHUNK), 1)

    ext[tail:tail + CHUNK] = p_ref[:, O_XBC:O_XBC + SSD_CONV_DIM]
    first = tail - (SSD_CONV - 1)
    acc = ext[first:first + CHUNK] * convw_ref[0:1]
    for jj in range(1, SSD_CONV):
        acc = acc + ext[first + jj:first + jj + CHUNK] * convw_ref[jj:jj + 1]
    xact = _silu(acc + convb_ref[...])
    new_tail = ext[CHUNK:CHUNK + tail]
    ext[0:tail] = new_tail

    dt = _softplus(p_ref[:, O_DT:O_DT + LANES] + dtb_ref[...])
    cum = _mm_exact_lhs(tril_ref[...], dt * arow_ref[...])
    cum_t = cum.T
    dt_t = dt.T
    ecum = jnp.exp(cum)
    cend = cum[CHUNK - 1:CHUNK, :]
    wend = jnp.exp(cend - cum) * dt
    eend = jnp.exp(cend)
    pairs_per_group = SSD_HEADS // SSD_GROUPS // 2
    ys = []
    for g in range(SSD_GROUPS):
        bc = xact[:, SSD_INNER + g * SSD_STATE:SSD_INNER + (g + 1) * SSD_STATE]
        cc = xact[:, SSD_INNER + (SSD_GROUPS + g) * SSD_STATE:SSD_INNER + (SSD_GROUPS + g + 1) * SSD_STATE]
        cb = _mm_nt(cc, bc)
        for jg in range(pairs_per_group):
            j = g * pairs_per_group + jg
            ha, hb = 2 * j, 2 * j + 1
            xp = xact[:, j * LANES:(j + 1) * LANES]
            s_old = sst[j]
            y = jnp.where(low, ecum[:, ha:ha + 1], ecum[:, hb:hb + 1]) * _mm_nt(cc, s_old)
            for pos, h in ((0, ha), (1, hb)):
                seg = cum[:, h:h + 1] - cum_t[h:h + 1, :]
                wmat = cb * jnp.exp(jnp.where(causal, seg, -jnp.inf)) * dt_t[h:h + 1, :]
                y = y + _mm(wmat, jnp.where(low if pos == 0 else jnp.logical_not(low), xp, 0.0))
            xw = xp * jnp.where(low, wend[:, ha:ha + 1], wend[:, hb:hb + 1])
            sst[j] = jnp.where(row64, eend[:, ha:ha + 1], eend[:, hb:hb + 1]) * s_old + _mm_tn(xw, bc)
            ys.append(y)
        gs = slice(g * SSD_INNER // SSD_GROUPS, (g + 1) * SSD_INNER // SSD_GROUPS)
        yg = jnp.concatenate(ys[g * pairs_per_group:(g + 1) * pairs_per_group], axis=1)
        yg = (yg + drow_ref[:, gs] * xact[:, gs]) * _silu(p_ref[:, O_Z + gs.start:O_Z + gs.stop])
        y_ref[:, gs] = _rms(yg, snorm_ref[:, gs])

    cosr, sinr = cosr_ref[...], sinr_ref[...]
    for h in range(RET_HEADS):
        hs = h * LANES
        q = _rope128(p_ref[:, O_RQ + hs:O_RQ + hs + LANES], cosr, sinr)
        k = _rope128(p_ref[:, O_RK + hs:O_RK + hs + LANES], cosr, sinr) * (RET_QK_DIM ** -0.5)
        v = p_ref[:, O_RV + hs:O_RV + hs + LANES]
        r_old = rst[h]
        o = _mm(_mm_nt(q, k) * dmat_ref[h], v) + qs_ref[h] * _mm(q, r_old)
        rst[h] = cd_ref[h] * r_old + _mm_tn(k * ks_ref[h], v)
        xc = o - jnp.mean(o, axis=-1, keepdims=True)
        yn = xc * lax.rsqrt(jnp.mean(xc * xc, axis=-1, keepdims=True) + EPS) * rnorm_ref[:, hs:hs + LANES]
        y_ref[:, SSD_INNER + hs:SSD_INNER + hs + LANES] = yn * _silu(p_ref[:, O_RG + hs:O_RG + hs + LANES])

    @pl.when(n == pl.num_programs(1) - 1)
    def _():
        conv_ref[0] = new_tail
        s_ref[0] = sst[...]
        r_ref[0] = rst[...]


def odd_prompt(proj, tabs, convw, convb, dtb, arow, drow, snorm, tril, ret_consts, rnorm, *, batch, seq):
    nc = seq // CHUNK
    row = lambda b, n: (b * nc + n, 0)
    tab = pl.BlockSpec((CHUNK, LANES), lambda b, n: (n, 0))
    const = lambda shape: pl.BlockSpec(shape, lambda b, n: (0,) * len(shape))
    per_b = lambda shape: pl.BlockSpec((1,) + shape, lambda b, n: (b,) + (0,) * len(shape))
    npair = SSD_HEADS // 2
    hc = (RET_HEADS, CHUNK, LANES)
    return pl.pallas_call(
        _odd_prompt_kernel,
        grid=(batch, nc),
        in_specs=[
            pl.BlockSpec((CHUNK, ODD_COLS), row), tab, tab,
            const((SSD_CONV, SSD_CONV_DIM)), const((1, SSD_CONV_DIM)), const((1, LANES)), const((1, LANES)),
            const((1, SSD_INNER)), const((1, SSD_INNER)), const((CHUNK, CHUNK)),
            const(hc), const(hc), const(hc), const(hc), const((1, RET_HEADS * RET_V_DIM)),
        ],
        out_specs=[
            pl.BlockSpec((CHUNK, SSD_INNER + RET_HEADS * RET_V_DIM), row),
            per_b((SUBLANES, SSD_CONV_DIM)), per_b((npair, LANES, SSD_STATE)), per_b(hc),
        ],
        out_shape=[
            jax.ShapeDtypeStruct((batch * seq, SSD_INNER + RET_HEADS * RET_V_DIM), F32),
            jax.ShapeDtypeStruct((batch, SUBLANES, SSD_CONV_DIM), F32),
            jax.ShapeDtypeStruct((batch, npair, LANES, SSD_STATE), F32),
            jax.ShapeDtypeStruct((batch,) + hc, F32),
        ],
        scratch_shapes=[
            pltpu.VMEM((SUBLANES + CHUNK, SSD_CONV_DIM), F32),
            pltpu.VMEM((npair, LANES, SSD_STATE), F32), pltpu.VMEM(hc, F32),
        ],
        compiler_params=pltpu.CompilerParams(dimension_semantics=("arbitrary", "arbitrary"),
                                             vmem_limit_bytes=VMEM_LIMIT_BYTES),
        name="odd_prompt",
    )(proj, *tabs, convw, convb, dtb, arow, drow, snorm, tril, *ret_consts, rnorm)


def _lane_to_rows(g, offset):
    sel = _iota(g.shape, 1) == _iota(g.shape, 0) + offset
    return jnp.sum(jnp.where(sel, g, 0.0), axis=-1, keepdims=True)


def _block_rows(x, i, nblk, blk):
    row = _iota((SUBLANES, blk), 0)
    out = jnp.zeros((SUBLANES, blk), F32)
    for b in range(nblk):
        out = jnp.where(row == b, x[i:i + 1, b * blk:(b + 1) * blk], out)
    return out


def _even_sample_kernel(p_ref, bk_ref, bv_ref, c_ref, nrow_ref, mrow_ref, cos_ref, sinlo_ref, sinhi_ref, qn_ref,
                        kn_ref, sink_ref, segm_ref, gb_ref, onorm_ref,
                        y_ref, nk_ref, nv_ref, nc_ref, nn_ref, nm_ref):
    R = SUBLANES
    cos, sinlo, sinhi = cos_ref[...], sinlo_ref[...], sinhi_ref[...]
    segm = segm_ref[...]
    row = _iota((R, LANES), 0)
    lane = _iota((R, LANES), 1)
    low = lane < 64
    group = SW_HEADS // SW_KV_HEADS
    scale = SW_HEAD_DIM ** -0.5
    sink = sink_ref[:, 0:1]
    last = _iota((CHUNK, LANES), 0) == CHUNK - 1

    k = _rope16(_seg_rms(p_ref[:, E_SK:E_SK + LANES], kn_ref[...], segm), cos, sinlo, sinhi)
    v = p_ref[:, E_SV:E_SV + LANES]
    qb, qb_sw = [], []
    for j in range(SW_HEADS // 2):
        sl = slice(E_SQ + j * LANES, E_SQ + (j + 1) * LANES)
        qb.append(_rope16(_seg_rms(p_ref[:, sl], qn_ref[:, sl], segm), cos, sinlo, sinhi))
        qb_sw.append(pltpu.roll(qb[j], 64, 1))

    g = p_ref[:, E_GATE:E_GATE + LANES]
    ic = g + gb_ref[0:1]
    fl = _log_sigmoid(pltpu.roll(g, LANES - GATE_F_LANE, 1) + gb_ref[1:2])
    mprev = mrow_ref[...]
    mt = jnp.maximum(fl + mprev, ic)
    w_all = jnp.exp(ic - mt)
    cs_all = jnp.exp(fl + mprev - mt)
    em_all = jnp.exp(-mt)
    nm_ref[...] = mt
    width = ML_HEADS * ML_QK_DIM
    own = _iota((R, width), 1) // ML_QK_DIM == _iota((R, width), 0)
    kscaled = p_ref[:, E_MK:E_MK + width] * (ML_QK_DIM ** -0.5)

    for i in range(R):
        qm = jnp.zeros((R, LANES), F32)
        for j in range(SW_HEADS // 2):
            for pos in range(2):
                h = 2 * j + pos
                kv = h // group
                src = (qb[j] if pos == kv else qb_sw[j])[i:i + 1]
                qm = jnp.where((row == h) & (low if kv == 0 else jnp.logical_not(low)), src, qm)
        bk, bv = bk_ref[i], bv_ref[i]
        ki, vi = k[i:i + 1], v[i:i + 1]
        s = _dg(qm, bk, 1, 1) * scale
        s_new = jnp.sum(qm * ki, axis=-1, keepdims=True) * scale
        m = jnp.maximum(jnp.maximum(jnp.max(s, axis=-1, keepdims=True), s_new), sink)
        pr = jnp.exp(s - m)
        p_new = jnp.exp(s_new - m)
        den = jnp.sum(pr, axis=-1, keepdims=True) + p_new + jnp.exp(sink - m)
        o = (_dg(pr, bv, 1, 0) + p_new * vi) / den
        o_sw = pltpu.roll(o, 64, 1)
        for j in range(SW_HEADS // 2):
            halves = []
            for pos in range(2):
                h = 2 * j + pos
                halves.append((o if pos == h // group else o_sw)[h:h + 1, :])
            y_ref[i:i + 1, j * LANES:(j + 1) * LANES] = jnp.where(low[0:1], halves[0], halves[1])
        nk_ref[i] = jnp.where(last, ki, pltpu.roll(bk, CHUNK - 1, 0))
        nv_ref[i] = jnp.where(last, vi, pltpu.roll(bv, CHUNK - 1, 0))

        w = _lane_to_rows(jnp.broadcast_to(w_all[i:i + 1], (R, LANES)), 0)
        cs = _lane_to_rows(jnp.broadcast_to(cs_all[i:i + 1], (R, LANES)), 0)
        em = _lane_to_rows(jnp.broadcast_to(em_all[i:i + 1], (R, LANES)), 0)
        qrows = jnp.where(own, p_ref[i:i + 1, E_MQ:E_MQ + width], 0.0)
        krows = jnp.where(own, kscaled[i:i + 1], 0.0)
        c_old = c_ref[i]
        qc = _dg(qrows, c_old, 1, 0)
        qn_dot = jnp.sum(qrows * nrow_ref[i:i + 1], axis=-1, keepdims=True)
        sqk = jnp.sum(qrows * krows, axis=-1, keepdims=True) * w
        v4 = _block_rows(p_ref[:, E_MV:E_MV + ML_HEADS * ML_V_DIM], i, ML_HEADS, ML_V_DIM)
        mo4 = _block_rows(p_ref[:, E_MO:E_MO + ML_HEADS * ML_V_DIM], i, ML_HEADS, ML_V_DIM)
        num = cs * qc + sqk * v4
        dn = cs * qn_dot + sqk
        hh = num / jnp.maximum(jnp.abs(dn), em)
        hn = _rms(hh, onorm_ref[...]) * _sigmoid(mo4)
        for h in range(ML_HEADS):
            c0 = SW_HEADS * SW_HEAD_DIM + h * ML_V_DIM
            y_ref[i:i + 1, c0:c0 + ML_V_DIM] = hn[h:h + 1]
        dec_col = jnp.concatenate(
            [jnp.broadcast_to(cs[h:h + 1, 0:1], (ML_QK_DIM, ML_V_DIM)) for h in range(ML_HEADS)], axis=0)
        nc_ref[i] = dec_col * c_old + _dg(krows * w, v4, 0, 0)
        dec_lanes = jnp.sum(jnp.where(own, cs, 0.0), axis=0, keepdims=True)
        nn_ref[i:i + 1] = dec_lanes * nrow_ref[i:i + 1] + jnp.sum(krows * w, axis=0, keepdims=True)


def even_sample(proj, bk, bv, c, nrow, mrow, tabs, qn, kn, sink_rows, segm, gb, onorm_rows):
    b = proj.shape[0]
    R = SUBLANES
    width = ML_HEADS * ML_QK_DIM
    rows = lambda w: pl.BlockSpec((R, w), lambda i: (i, 0))
    per_b = lambda shape: pl.BlockSpec((R,) + shape, lambda i: (i,) + (0,) * len(shape))
    const = lambda shape: pl.BlockSpec(shape, lambda i: (0,) * len(shape))
    tab = const((R, LANES))
    return pl.pallas_call(
        _even_sample_kernel,
        grid=(b // R,),
        in_specs=[
            rows(EVEN_COLS), per_b((CHUNK, LANES)), per_b((CHUNK, LANES)), per_b((width, ML_V_DIM)),
            rows(width), rows(LANES), tab, tab, tab,
            const((1, SW_HEADS * SW_HEAD_DIM)), const((1, LANES)), const((R, LANES)),
            const((LANES, LANES)), const((2, LANES)), const((R, ML_V_DIM)),
        ],
        out_specs=[
            rows(D_MODEL), per_b((CHUNK, LANES)), per_b((CHUNK, LANES)), per_b((width, ML_V_DIM)),
            rows(width), rows(LANES),
        ],
        out_shape=[
            jax.ShapeDtypeStruct((b, D_MODEL), F32),
            jax.ShapeDtypeStruct((b, CHUNK, LANES), F32), jax.ShapeDtypeStruct((b, CHUNK, LANES), F32),
            jax.ShapeDtypeStruct((b, width, ML_V_DIM), F32), jax.ShapeDtypeStruct((b, width), F32),
            jax.ShapeDtypeStruct((b, LANES), F32),
        ],
        compiler_params=pltpu.CompilerParams(dimension_semantics=("arbitrary",), vmem_limit_bytes=VMEM_LIMIT_BYTES),
        name="even_sample",
    )(proj, bk, bv, c, nrow, mrow, *tabs, qn, kn, sink_rows, segm, gb, onorm_rows)


def _odd_sample_kernel(p_ref, cb_ref, s_ref, r_ref, cosr_ref, sinr_ref, convw_ref, convb_ref, dtb_ref, arow_ref,
                       drow_ref, snorm_ref, spread_ref, rdec_ref, rnorm_ref,
                       y_ref, ncb_ref, ns_ref, nr_ref):
    R = SUBLANES
    xbc = p_ref[:, O_XBC:O_XBC + SSD_CONV_DIM]
    acc = cb_ref[0] * convw_ref[0:1]
    for jj in range(1, SSD_CONV - 1):
        acc = acc + cb_ref[jj] * convw_ref[jj:jj + 1]
    acc = acc + xbc * convw_ref[SSD_CONV - 1:SSD_CONV]
    xact = _silu(acc + convb_ref[...])
    for jj in range(SSD_CONV - 2):
        ncb_ref[jj] = cb_ref[jj + 1]
    ncb_ref[SSD_CONV - 2] = xbc

    dt = _softplus(p_ref[:, O_DT:O_DT + LANES] + dtb_ref[...])
    dec = jnp.exp(dt * arow_ref[...])
    xs = xact[:, 0:SSD_INNER]
    xdt = xs * _mm_exact_rhs(dt, spread_ref[...])
    gown = _iota((R, SSD_INNER), 1) // (SSD_INNER // SSD_GROUPS) == _iota((R, SSD_INNER), 0)
    bpart = xact[:, SSD_INNER:SSD_INNER + SSD_GROUPS * SSD_STATE]
    cpart = xact[:, SSD_INNER + SSD_GROUPS * SSD_STATE:SSD_CONV_DIM]

    cosr, sinr = cosr_ref[...], sinr_ref[...]
    width = RET_HEADS * RET_QK_DIM
    q4 = jnp.concatenate([_rope128(p_ref[:, O_RQ + h * LANES:O_RQ + (h + 1) * LANES], cosr, sinr)
                          for h in range(RET_HEADS)], axis=1)
    k4 = jnp.concatenate([_rope128(p_ref[:, O_RK + h * LANES:O_RK + (h + 1) * LANES], cosr, sinr)
                          for h in range(RET_HEADS)], axis=1) * (RET_QK_DIM ** -0.5)
    own = _iota((R, width), 1) // RET_QK_DIM == _iota((R, width), 0)
    gam = rdec_ref[:, 0:1]
    gam_col = jnp.concatenate(
        [jnp.broadcast_to(rdec_ref[h:h + 1, :], (RET_QK_DIM, RET_V_DIM)) for h in range(RET_HEADS)], axis=0)

    ys = []
    for i in range(R):
        brows = _block_rows(bpart, i, SSD_GROUPS, SSD_STATE)
        crows = _block_rows(cpart, i, SSD_GROUPS, SSD_STATE)
        xw = jnp.where(gown, xdt[i:i + 1], 0.0)
        dec_col = jnp.concatenate(
            [jnp.broadcast_to(dec[i:i + 1, h:h + 1], (SSD_HEAD_DIM, SSD_STATE)) for h in range(SSD_HEADS)], axis=0)
        s_new = dec_col * s_ref[i] + _dg(xw, brows, 0, 0)
        ns_ref[i] = s_new
        yrows = _dg(crows, s_new, 1, 1)
        ys.append(jnp.sum(jnp.where(gown, yrows, 0.0), axis=0, keepdims=True))

        qrows = jnp.where(own, q4[i:i + 1], 0.0)
        krows = jnp.where(own, k4[i:i + 1], 0.0)
        v4 = _block_rows(p_ref[:, O_RV:O_RV + RET_HEADS * RET_V_DIM], i, RET_HEADS, RET_V_DIM)
        g4 = _block_rows(p_ref[:, O_RG:O_RG + RET_HEADS * RET_V_DIM], i, RET_HEADS, RET_V_DIM)
        r_old = r_ref[i]
        att = jnp.sum(qrows * krows, axis=-1, keepdims=True)
        o = att * v4 + gam * _dg(qrows, r_old, 1, 0)
        nr_ref[i] = gam_col * r_old + _dg(krows, v4, 0, 0)
        xc = o - jnp.mean(o, axis=-1, keepdims=True)
        yn = xc * lax.rsqrt(jnp.mean(xc * xc, axis=-1, keepdims=True) + EPS) * rnorm_ref[...] * _silu(g4)
        for h in range(RET_HEADS):
            y_ref[i:i + 1, SSD_INNER + h * RET_V_DIM:SSD_INNER + (h + 1) * RET_V_DIM] = yn[h:h + 1]

    y = (jnp.concatenate(ys, axis=0) + drow_ref[...] * xs) * _silu(p_ref[:, O_Z:O_Z + SSD_INNER])
    gw = SSD_INNER // SSD_GROUPS
    for g in range(SSD_GROUPS):
        y_ref[:, g * gw:(g + 1) * gw] = _rms(y[:, g * gw:(g + 1) * gw], snorm_ref[:, g * gw:(g + 1) * gw])


def odd_sample(proj, cbuf, s, r, tabs, convw, convb, dtb, arow, drow, snorm, spread, rdec_rows, rnorm_rows):
    b = proj.shape[0]
    R = SUBLANES
    rows = lambda w: pl.BlockSpec((R, w), lambda i: (i, 0))
    per_b = lambda shape: pl.BlockSpec((R,) + shape, lambda i: (i,) + (0,) * len(shape))
    const = lambda shape: pl.BlockSpec(shape, lambda i: (0,) * len(shape))
    tab = const((R, LANES))
    conv = pl.BlockSpec((SSD_CONV - 1, R, SSD_CONV_DIM), lambda i: (0, i, 0))
    ywidth = SSD_INNER + RET_HEADS * RET_V_DIM
    sshape = (SSD_HEADS * SSD_HEAD_DIM, SSD_STATE)
    rshape = (RET_HEADS * RET_QK_DIM, RET_V_DIM)
    return pl.pallas_call(
        _odd_sample_kernel,
        grid=(b // R,),
        in_specs=[
            rows(ODD_COLS), conv, per_b(sshape), per_b(rshape), tab, tab,
            const((SSD_CONV, SSD_CONV_DIM)), const((1, SSD_CONV_DIM)), const((1, LANES)), const((1, LANES)),
            const((1, SSD_INNER)), const((1, SSD_INNER)), const((LANES, SSD_INNER)),
            const((R, LANES)), const((R, RET_V_DIM)),
        ],
        out_specs=[rows(ywidth), conv, per_b(sshape), per_b(rshape)],
        out_shape=[
            jax.ShapeDtypeStruct((b, ywidth), F32),
            jax.ShapeDtypeStruct((SSD_CONV - 1, b, SSD_CONV_DIM), F32),
            jax.ShapeDtypeStruct((b,) + sshape, F32), jax.ShapeDtypeStruct((b,) + rshape, F32),
        ],
        compiler_params=pltpu.CompilerParams(dimension_semantics=("arbitrary",), vmem_limit_bytes=VMEM_LIMIT_BYTES),
        name="odd_sample",
    )(proj, cbuf, s, r, *tabs, convw, convb, dtb, arow, drow, snorm, spread, rdec_rows, rnorm_rows)


def _pad_cols(w, n):
    return jnp.pad(w, ((0, 0), (0, n - w.shape[1])))


def _even_w_in(w):
    sq, sk, sv, mq, mk, mv, mo, mi, mf = jnp.split(w, [512, 640, 768, 1024, 1280, 1792, 2304, 2308], axis=1)
    gates = jnp.concatenate([_pad_cols(mi, GATE_F_LANE), _pad_cols(mf, LANES - GATE_F_LANE)], axis=1)
    return jnp.concatenate([sq, sk, sv, mq, mk, mv, mo, gates], axis=1).astype(BF16)


def _odd_w_in(w):
    z, xbc, dt, rq, rk, rv, rg = jnp.split(w, [1024, 2560, 2576, 3088, 3600, 4112], axis=1)
    return jnp.concatenate([z, xbc, rq, rk, rv, rg, _pad_cols(dt, LANES)], axis=1).astype(BF16)


def _rope16_tables(pos):
    half = SW_ROT_DIM // 2
    inv = jnp.power(jnp.float32(ROPE_THETA), -jnp.arange(half, dtype=F32) * (2.0 / SW_ROT_DIM))
    ang = pos.astype(F32)[:, None] * inv[None, :]
    cos, sin = jnp.cos(ang), jnp.sin(ang)
    n = pos.shape[0]
    rest = SW_HEAD_DIM - SW_ROT_DIM
    one, zero, zh = jnp.ones((n, rest), F32), jnp.zeros((n, rest), F32), jnp.zeros((n, half), F32)
    tile2 = lambda t: jnp.concatenate([t, t], axis=1)
    return (tile2(jnp.concatenate([cos, cos, one], axis=1)),
            tile2(jnp.concatenate([-sin, zh, zero], axis=1)),
            tile2(jnp.concatenate([zh, sin, zero], axis=1)))


def _rope128_tables(pos):
    half = RET_QK_DIM // 2
    inv = jnp.power(jnp.float32(RET_ROPE_THETA), -jnp.arange(half, dtype=F32) * (2.0 / RET_QK_DIM))
    ang = pos.astype(F32)[:, None] * inv[None, :]
    cos, sin = jnp.cos(ang), jnp.sin(ang)
    return jnp.concatenate([cos, cos], axis=1), jnp.concatenate([-sin, sin], axis=1)


def _ret_consts():
    L = CHUNK
    lg = jnp.log(1.0 - jnp.exp2(-5.0 - jnp.arange(RET_HEADS, dtype=F32)))
    idx = jnp.arange(L, dtype=F32)
    diff = idx[:, None] - idx[None, :]
    dmat = jnp.exp(jnp.where(diff >= 0, diff[None] * lg[:, None, None], -jnp.inf))
    q_scale = jnp.exp((idx[None] + 1.0) * lg[:, None])
    k_scale = jnp.exp((L - 1.0 - idx[None]) * lg[:, None])
    chunk_decay = jnp.exp(L * lg)
    bc = lambda t: jnp.broadcast_to(t[:, :, None], (RET_HEADS, L, LANES))
    cd = jnp.broadcast_to(chunk_decay[:, None, None], (RET_HEADS, L, LANES))
    return dmat, bc(q_scale), bc(k_scale), cd, lg


def _rows8(t):
    return jnp.pad(t, ((0, SUBLANES - t.shape[0]), (0, 0)))


def _gate_bias_rows(gb):
    ib = jnp.broadcast_to(gb[:ML_HEADS, None], (ML_HEADS, LANES))
    fb = jnp.broadcast_to(gb[ML_HEADS:, None], (ML_HEADS, LANES))
    return jnp.concatenate([_rows8(ib), _rows8(fb)], axis=0)


def kernel(x_prompt, x_sample, cache_mem_k, cache_mem_v, cache_swa_k, cache_swa_v, state_mlstm_C, state_mlstm_n,
           state_mlstm_m, state_ssd_conv, state_ssd, state_ret, mem_prompt, norm_mix, norm_xattn, norm_mem, norm_ffn,
           even_w_in, mlstm_gate_bias, swa_q_norm, swa_k_norm, swa_sinks, mlstm_out_norm, even_w_out, odd_w_in,
           ssd_conv_w, ssd_conv_b, ssd_dt_bias, ssd_a_log, ssd_d, ssd_norm, ret_norm, odd_w_out, mem_wq, mem_wk,
           mem_wv, mem_q_norm, mem_k_norm, mem_wo, ffn_w1, ffn_w2):
    bp, seq, d = x_prompt.shape
    bs = x_sample.shape[0]
    depth = norm_mix.shape[0]
    tm = 256

    pos_p = jnp.arange(seq, dtype=jnp.int32)
    pos_s = jnp.full((SUBLANES,), PAST_LEN, dtype=jnp.int32)
    tab16_p, tab16_s = _rope16_tables(pos_p), _rope16_tables(pos_s)
    tab128_p, tab128_s = _rope128_tables(pos_p), _rope128_tables(pos_s)
    dmat, q_scale, k_scale, chunk_decay, lg = _ret_consts()
    rdec_rows = _rows8(jnp.broadcast_to(jnp.exp(lg)[:, None], (RET_HEADS, LANES)))
    ii = jnp.arange(CHUNK)
    triu = (ii[:, None] <= ii[None, :]).astype(F32)
    tril = (ii[:, None] >= ii[None, :]).astype(F32)
    jj = jnp.arange(LANES)
    segm = jnp.where(jj[:, None] // SW_HEAD_DIM == jj[None, :] // SW_HEAD_DIM, 1.0 / SW_HEAD_DIM, 0.0).astype(F32)
    spread = (jj[:, None] == jnp.arange(SSD_INNER)[None, :] // SSD_HEAD_DIM).astype(F32)
    row1 = lambda t: t.reshape(1, -1).astype(F32)
    pad_lanes = lambda t: jnp.pad(t.reshape(1, -1).astype(F32), ((0, 0), (0, LANES - t.shape[-1])))

    yp = x_prompt.reshape(bp * seq, d)
    ys = x_sample.reshape(bs, d)
    mem = mem_prompt.reshape(bp * MEM_LEN, d)
    cmk = cache_mem_k.reshape(depth, bs, MEM_LEN * MEM_HEADS, MEM_HEAD_DIM)
    cmv = cache_mem_v.reshape(depth, bs, MEM_LEN * MEM_HEADS, MEM_HEAD_DIM)
    p_mk, p_mv = [], []
    outs = {}
    for l in range(depth):
        g_mix = row1(norm_mix[l])
        if l % 2 == 0:
            e = l // 2
            w_in = _even_w_in(even_w_in[e])
            w_out = even_w_out[e].astype(BF16)
            qn = row1(jnp.tile(swa_q_norm[e], SW_HEADS))
            kn = row1(jnp.tile(swa_k_norm[e], SW_KV_HEADS))
            gb = _gate_bias_rows(mlstm_gate_bias[e].astype(F32))
            onorm = row1(mlstm_out_norm[e])
            sinks = swa_sinks[e].astype(F32)
            proj_p = norm_proj(yp, g_mix, w_in, tm=tm)
            mix_p, kc, vc, caug, mm = even_prompt(proj_p, tab16_p, qn, kn, sinks, segm, triu, gb, onorm,
                                                  batch=bp, seq=seq)
            outs["p_swk"] = kc.reshape(1, bp, CHUNK, SW_KV_HEADS, SW_HEAD_DIM)
            outs["p_swv"] = vc.reshape(1, bp, CHUNK, SW_KV_HEADS, SW_HEAD_DIM)
            outs["p_c"] = caug[..., :ML_V_DIM].reshape(1, bp, ML_HEADS, ML_QK_DIM, ML_V_DIM)
            outs["p_n"] = caug[..., ML_V_DIM].reshape(1, bp, ML_HEADS, ML_QK_DIM)
            outs["p_m"] = mm[:, :ML_HEADS, 0].reshape(1, bp, ML_HEADS)

            proj_s = norm_proj(ys, g_mix, w_in, tm=bs)
            sink_rows = jnp.broadcast_to(sinks[:, None], (SW_HEADS, LANES))
            onorm_rows = _rows8(mlstm_out_norm[e].astype(F32).reshape(ML_HEADS, ML_V_DIM))
            gb_lanes = jnp.concatenate([pad_lanes(mlstm_gate_bias[e][:ML_HEADS]),
                                        pad_lanes(mlstm_gate_bias[e][ML_HEADS:])], axis=0)
            mix_s, nk, nv, ncst, nn, nm = even_sample(
                proj_s,
                cache_swa_k[e].reshape(bs, CHUNK, LANES), cache_swa_v[e].reshape(bs, CHUNK, LANES),
                state_mlstm_C[e].reshape(bs, ML_HEADS * ML_QK_DIM, ML_V_DIM),
                state_mlstm_n[e].reshape(bs, ML_HEADS * ML_QK_DIM),
                jnp.pad(state_mlstm_m[e], ((0, 0), (0, LANES - ML_HEADS))),
                tab16_s, qn, kn, sink_rows, segm, gb_lanes, onorm_rows)
            outs["s_swk"] = nk.reshape(1, bs, CHUNK, SW_KV_HEADS, SW_HEAD_DIM)
            outs["s_swv"] = nv.reshape(1, bs, CHUNK, SW_KV_HEADS, SW_HEAD_DIM)
            outs["s_c"] = ncst.reshape(1, bs, ML_HEADS, ML_QK_DIM, ML_V_DIM)
            outs["s_n"] = nn.reshape(1, bs, ML_HEADS, ML_QK_DIM)
            outs["s_m"] = nm[:, :ML_HEADS].reshape(1, bs, ML_HEADS)
        else:
            o = l // 2
            w_in = _odd_w_in(odd_w_in[o])
            w_out = odd_w_out[o].astype(BF16)
            convw = ssd_conv_w[o].astype(F32)
            convb = row1(ssd_conv_b[o])
            dtb = pad_lanes(ssd_dt_bias[o])
            arow = pad_lanes(-jnp.exp(ssd_a_log[o].astype(F32)))
            drow = row1(jnp.repeat(ssd_d[o].astype(F32), SSD_HEAD_DIM))
            snorm = row1(ssd_norm[o])
            rnorm = row1(ret_norm[o])
            proj_p = norm_proj(yp, g_mix, w_in, tm=tm)
            mix_p, ctail, sst, rst = odd_prompt(proj_p, tab128_p, convw, convb, dtb, arow, drow, snorm, tril,
                                                (dmat, q_scale, k_scale, chunk_decay), rnorm, batch=bp, seq=seq)
            outs["p_conv"] = ctail[:, SUBLANES - (SSD_CONV - 1):, :].reshape(1, bp, SSD_CONV - 1, SSD_CONV_DIM)
            outs["p_ssd"] = sst.reshape(1, bp, SSD_HEADS, SSD_HEAD_DIM, SSD_STATE)
            outs["p_ret"] = rst.reshape(1, bp, RET_HEADS, RET_QK_DIM, RET_V_DIM)

            proj_s = norm_proj(ys, g_mix, w_in, tm=bs)
            rnorm_rows = _rows8(ret_norm[o].astype(F32).reshape(RET_HEADS, RET_V_DIM))
            mix_s, ncb, ns, nr = odd_sample(
                proj_s, jnp.swapaxes(state_ssd_conv[o], 0, 1),
                state_ssd[o].reshape(bs, SSD_HEADS * SSD_HEAD_DIM, SSD_STATE),
                state_ret[o].reshape(bs, RET_HEADS * RET_QK_DIM, RET_V_DIM),
                tab128_s, convw, convb, dtb, arow, drow, snorm, spread, rdec_rows, rnorm_rows)
            outs["s_conv"] = jnp.swapaxes(ncb, 0, 1).reshape(1, bs, SSD_CONV - 1, SSD_CONV_DIM)
            outs["s_ssd"] = ns.reshape(1, bs, SSD_HEADS, SSD_HEAD_DIM, SSD_STATE)
            outs["s_ret"] = nr.reshape(1, bs, RET_HEADS, RET_QK_DIM, RET_V_DIM)

        wkv = jnp.concatenate([mem_wk[l], mem_wv[l]], axis=1).astype(BF16)
        qnorm = row1(mem_q_norm[l])
        mkv = norm_proj(mem, row1(norm_mem[l]), wkv, tm=tm, head_norm=row1(mem_k_norm[l]), head_norm_cols=MEM_WIDTH)
        p_mk.append(mkv[:, :MEM_WIDTH].reshape(bp, MEM_LEN, MEM_HEADS, MEM_HEAD_DIM))
        p_mv.append(mkv[:, MEM_WIDTH:].reshape(bp, MEM_LEN, MEM_HEADS, MEM_HEAD_DIM))
        wq, wo = mem_wq[l].astype(BF16), mem_wo[l].astype(BF16)
        gx, gf = row1(norm_xattn[l]), row1(norm_ffn[l])
        w1, w2 = ffn_w1[l].astype(BF16), ffn_w2[l].astype(BF16)
        yp = post_prompt(yp, mix_p, w_out, gx, wq, qnorm, mkv, wo, gf, w1, w2, rows_per_batch=seq, tm=tm)

        ys = res_proj(ys, mix_s, w_out)
        qs = norm_proj(ys, gx, wq, tm=bs, head_norm=qnorm, head_norm_cols=MEM_WIDTH)
        ys = res_proj_ffn(ys, xattn_sample(qs, cmk, cmv, l), wo, gf, w1, w2)

    return (yp.reshape(bp, seq, d), ys.reshape(bs, 1, d),
            jnp.stack(p_mk), jnp.stack(p_mv), outs["p_swk"], outs["p_swv"], outs["p_c"], outs["p_n"], outs["p_m"],
            outs["p_conv"], outs["p_ssd"], outs["p_ret"],
            outs["s_swk"], outs["s_swv"], outs["s_c"], outs["s_n"], outs["s_m"],
            outs["s_conv"], outs["s_ssd"], outs["s_ret"])
```

```python
import functools
import math

import jax
import jax.numpy as jnp
from jax import lax
from jax.experimental import pallas as pl
from jax.experimental.pallas import tpu as pltpu

F32 = jnp.float32
BF16 = jnp.bfloat16

D_MODEL = 1024
PAST_LEN = 8192
EPS = 1e-6
CHUNK = 128
NEG = -1e30

SW_HEADS, SW_KV_HEADS, SW_HEAD_DIM, SW_ROT_DIM = 8, 2, 64, 16
ROPE_THETA = 500000.0
ML_HEADS, ML_QK_DIM, ML_V_DIM = 4, 64, 128
SSD_HEADS, SSD_HEAD_DIM, SSD_GROUPS, SSD_STATE, SSD_CONV = 16, 64, 2, 128, 4
SSD_INNER = SSD_HEADS * SSD_HEAD_DIM
SSD_CONV_DIM = SSD_INNER + 2 * SSD_GROUPS * SSD_STATE
RET_HEADS, RET_QK_DIM, RET_V_DIM = 4, 128, 128
RET_ROPE_THETA = 10000.0
MEM_LEN, MEM_HEADS, MEM_HEAD_DIM = 256, 4, 128
MEM_WIDTH = MEM_HEADS * MEM_HEAD_DIM
FFN_DIM = 4 * D_MODEL
FFN_CHUNK = 512

LANES = 128
SUBLANES = 8
VMEM_LIMIT_BYTES = 56 * 1024 * 1024

E_SQ, E_SK, E_SV, E_MQ, E_MK, E_MV, E_MO, E_GATE, EVEN_COLS = 0, 512, 640, 768, 1024, 1280, 1792, 2304, 2432
GATE_F_LANE = 8
O_Z, O_XBC, O_RQ, O_RK, O_RV, O_RG, O_DT, ODD_COLS = 0, 1024, 2560, 3072, 3584, 4096, 4608, 4736


def _mm(a, b):
    return jnp.dot(a.astype(BF16), b.astype(BF16), preferred_element_type=F32)


def _mm_nt(a, b):
    return lax.dot_general(a.astype(BF16), b.astype(BF16), (((1,), (1,)), ((), ())), preferred_element_type=F32)


def _mm_tn(a, b):
    return lax.dot_general(a.astype(BF16), b.astype(BF16), (((0,), (0,)), ((), ())), preferred_element_type=F32)


def _dg(a, b, ca, cb):
    return lax.dot_general(a, b, (((ca,), (cb,)), ((), ())), preferred_element_type=F32)


def _split3(x):
    hi = x.astype(BF16).astype(F32)
    r1 = x - hi
    mid = r1.astype(BF16).astype(F32)
    lo = (r1 - mid).astype(BF16).astype(F32)
    return hi, mid, lo


def _mm_exact_rhs(x, e):
    hi, mid, lo = _split3(x)
    return _dg(hi, e, 1, 0) + _dg(mid, e, 1, 0) + _dg(lo, e, 1, 0)


def _mm_exact_lhs(e, x):
    hi, mid, lo = _split3(x)
    return _dg(e, hi, 1, 0) + _dg(e, mid, 1, 0) + _dg(e, lo, 1, 0)


def _mm_tn_exact_lhs(x, e):
    hi, mid, lo = _split3(x)
    return _dg(hi, e, 0, 0) + _dg(mid, e, 0, 0) + _dg(lo, e, 0, 0)


def _rms(x, g):
    return x * lax.rsqrt(jnp.mean(x * x, axis=-1, keepdims=True) + EPS) * g


def _seg_rms(x, g, seg_mean):
    return x * lax.rsqrt(_mm_exact_rhs(x * x, seg_mean) + EPS) * g


def _sigmoid(x):
    return 1.0 / (1.0 + jnp.exp(-x))


def _silu(x):
    return x * _sigmoid(x)


def _softplus(x):
    return jnp.maximum(x, 0.0) + jnp.log1p(jnp.exp(-jnp.abs(x)))


def _log_sigmoid(x):
    return -_softplus(-x)


def _rope16(x, cos, sin_lo, sin_hi):
    return x * cos + pltpu.roll(x, LANES - 8, 1) * sin_lo + pltpu.roll(x, 8, 1) * sin_hi


def _rope128(x, cos, sin):
    return x * cos + pltpu.roll(x, 64, 1) * sin


def _iota(shape, dim):
    return lax.broadcasted_iota(jnp.int32, shape, dim)


def _cummax_lanes(x):
    lane = _iota(x.shape, 1)
    shift = 1
    while shift < x.shape[1]:
        x = jnp.maximum(x, jnp.where(lane >= shift, pltpu.roll(x, shift, 1), -jnp.inf))
        shift *= 2
    return x


def _norm_proj_kernel(x_ref, g_ref, w_ref, hn_ref, o_ref, *, chunks, head_norm_cols):
    xn = _rms(x_ref[...], g_ref[...]).astype(BF16)
    for c0, cs in chunks:
        r = jnp.dot(xn, w_ref[:, c0:c0 + cs], preferred_element_type=F32)
        if c0 < head_norm_cols:
            parts = [_rms(r[:, i:i + LANES], hn_ref[...]) for i in range(0, cs, LANES)]
            r = jnp.concatenate(parts, axis=1)
        o_ref[:, c0:c0 + cs] = r


def _col_chunks(n, width=512):
    return tuple((c, min(width, n - c)) for c in range(0, n, width))


def norm_proj(x, g, w, *, tm, head_norm=None, head_norm_cols=0):
    n, d = x.shape
    m = w.shape[1]
    if head_norm is None:
        head_norm = jnp.ones((1, LANES), F32)
    kern = functools.partial(_norm_proj_kernel, chunks=_col_chunks(m), head_norm_cols=head_norm_cols)
    return pl.pallas_call(
        kern,
        grid=(n // tm,),
        in_specs=[
            pl.BlockSpec((tm, d), lambda i: (i, 0)),
            pl.BlockSpec((1, d), lambda i: (0, 0)),
            pl.BlockSpec((d, m), lambda i: (0, 0), pipeline_mode=pl.Buffered(1)),
            pl.BlockSpec((1, LANES), lambda i: (0, 0)),
        ],
        out_specs=pl.BlockSpec((tm, m), lambda i: (i, 0)),
        out_shape=jax.ShapeDtypeStruct((n, m), F32),
        compiler_params=pltpu.CompilerParams(dimension_semantics=("arbitrary",), vmem_limit_bytes=VMEM_LIMIT_BYTES),
        name="norm_proj",
    )(x, g, w, head_norm)


def _ffn(x, g_ref, w1_ref, w2_ref):
    h = _rms(x, g_ref[...]).astype(BF16)
    acc = None
    for c in range(0, FFN_DIM, FFN_CHUNK):
        u = jnp.maximum(jnp.dot(h, w1_ref[:, c:c + FFN_CHUNK], preferred_element_type=F32), 0.0)
        t = jnp.dot((u * u).astype(BF16), w2_ref[c:c + FFN_CHUNK, :], preferred_element_type=F32)
        acc = t if acc is None else acc + t
    return x + acc


def _post_prompt_kernel(x_ref, a_ref, wout_ref, gx_ref, wq_ref, qn_ref, mk_ref, mv_ref, wo_ref, gf_ref, w1_ref, w2_ref,
                        o_ref):
    x = x_ref[...] + _mm(a_ref[...], wout_ref[...])
    q = jnp.dot(_rms(x, gx_ref[...]).astype(BF16), wq_ref[...], preferred_element_type=F32)
    outs = []
    for h in range(MEM_HEADS):
        sl = slice(h * MEM_HEAD_DIM, (h + 1) * MEM_HEAD_DIM)
        qh = _rms(q[:, sl], qn_ref[...])
        s = _mm_nt(qh, mk_ref[:, sl]) * (MEM_HEAD_DIM ** -0.5)
        p = jnp.exp(s - jnp.max(s, axis=-1, keepdims=True))
        p = p / jnp.sum(p, axis=-1, keepdims=True)
        outs.append(_mm(p, mv_ref[:, sl]))
    x = x + _mm(jnp.concatenate(outs, axis=1), wo_ref[...])
    o_ref[...] = _ffn(x, gf_ref, w1_ref, w2_ref)


def post_prompt(x, a, wout, gx, wq, qn, mkv, wo, gf, w1, w2, *, rows_per_batch, tm):
    n, d = x.shape
    ka = a.shape[1]
    tiles = rows_per_batch // tm
    const = lambda shape: pl.BlockSpec(shape, lambda i: (0, 0), pipeline_mode=pl.Buffered(1))
    return pl.pallas_call(
        _post_prompt_kernel,
        grid=(n // tm,),
        in_specs=[
            pl.BlockSpec((tm, d), lambda i: (i, 0)),
            pl.BlockSpec((tm, ka), lambda i: (i, 0)),
            const((ka, d)),
            const((1, d)),
            const((d, MEM_WIDTH)),
            const((1, MEM_HEAD_DIM)),
            pl.BlockSpec((MEM_LEN, MEM_WIDTH), lambda i: (i // tiles, 0)),
            pl.BlockSpec((MEM_LEN, MEM_WIDTH), lambda i: (i // tiles, 1)),
            const((MEM_WIDTH, d)),
            const((1, d)),
            const((d, FFN_DIM)),
            const((FFN_DIM, d)),
        ],
        out_specs=pl.BlockSpec((tm, d), lambda i: (i, 0)),
        out_shape=jax.ShapeDtypeStruct((n, d), F32),
        compiler_params=pltpu.CompilerParams(dimension_semantics=("arbitrary",), vmem_limit_bytes=VMEM_LIMIT_BYTES),
        name="post_prompt",
    )(x, a, wout, gx, wq, qn, mkv, mkv, wo, gf, w1, w2)


def _res_proj_kernel(x_ref, a_ref, w_ref, o_ref):
    o_ref[...] = x_ref[...] + _mm(a_ref[...], w_ref[...])


def res_proj(x, a, w):
    n, d = x.shape
    return pl.pallas_call(
        _res_proj_kernel,
        out_shape=jax.ShapeDtypeStruct((n, d), F32),
        compiler_params=pltpu.CompilerParams(vmem_limit_bytes=VMEM_LIMIT_BYTES),
        name="res_proj",
    )(x, a, w)


def _res_proj_ffn_kernel(x_ref, a_ref, w_ref, gf_ref, w1_ref, w2_ref, o_ref):
    x = x_ref[...] + _mm(a_ref[...], w_ref[...])
    o_ref[...] = _ffn(x, gf_ref, w1_ref, w2_ref)


def res_proj_ffn(x, a, w, gf, w1, w2):
    n, d = x.shape
    return pl.pallas_call(
        _res_proj_ffn_kernel,
        out_shape=jax.ShapeDtypeStruct((n, d), F32),
        compiler_params=pltpu.CompilerParams(vmem_limit_bytes=VMEM_LIMIT_BYTES),
        name="res_proj_ffn",
    )(x, a, w, gf, w1, w2)


def _pair_rows(x, op):
    xb = jnp.broadcast_to(x, (SUBLANES, LANES))
    return op(xb, pltpu.roll(xb, SUBLANES // 2, 0))


def _xattn_sample_kernel(q_ref, mk_ref, mv_ref, o_ref):
    row = _iota((SUBLANES, LANES), 0)
    groups = MEM_LEN * MEM_HEADS // SUBLANES
    for i in range(SUBLANES):
        q8 = jnp.zeros((SUBLANES, LANES), F32)
        for h in range(MEM_HEADS):
            q8 = jnp.where(row % MEM_HEADS == h, q_ref[i:i + 1, h * MEM_HEAD_DIM:(h + 1) * MEM_HEAD_DIM], q8)
        k3 = mk_ref[0, i].reshape(groups, SUBLANES, LANES)
        s = jnp.sum(k3 * q8[None], axis=-1, keepdims=True) * (MEM_HEAD_DIM ** -0.5)
        mx = _pair_rows(jnp.max(s, axis=0), jnp.maximum)
        p = jnp.exp(s - mx[None, :, 0:1])
        den = _pair_rows(jnp.sum(p, axis=0), jnp.add)
        v3 = mv_ref[0, i].reshape(groups, SUBLANES, LANES)
        o8 = _pair_rows(jnp.sum(p * v3, axis=0), jnp.add) / den
        for h in range(MEM_HEADS):
            o_ref[i:i + 1, h * MEM_HEAD_DIM:(h + 1) * MEM_HEAD_DIM] = o8[h:h + 1]


def xattn_sample(q, mk, mv, layer):
    b = q.shape[0]
    mem = pl.BlockSpec((1, SUBLANES, MEM_LEN * MEM_HEADS, MEM_HEAD_DIM), lambda i: (layer, i, 0, 0))
    return pl.pallas_call(
        _xattn_sample_kernel,
        grid=(b // SUBLANES,),
        in_specs=[pl.BlockSpec((SUBLANES, MEM_WIDTH), lambda i: (i, 0)), mem, mem],
        out_specs=pl.BlockSpec((SUBLANES, MEM_WIDTH), lambda i: (i, 0)),
        out_shape=jax.ShapeDtypeStruct((b, MEM_WIDTH), F32),
        compiler_params=pltpu.CompilerParams(dimension_semantics=("arbitrary",), vmem_limit_bytes=VMEM_LIMIT_BYTES),
        name="xattn_sample",
    )(q, mk, mv)


def _even_prompt_kernel(p_ref, cos_ref, sinlo_ref, sinhi_ref, qn_ref, kn_ref, sink_ref, segm_ref, triu_ref, gb_ref,
                        onorm_ref,
                        y_ref, kc_ref, vc_ref, caug_ref, m_ref,
                        kprev, vprev, cst, mst):
    n = pl.program_id(0)
    batch = p_ref.shape[0]

    @pl.when(n == 0)
    def _():
        kprev[...] = jnp.zeros_like(kprev)
        vprev[...] = jnp.zeros_like(vprev)
        cst[...] = jnp.zeros_like(cst)
        mst[...] = jnp.zeros_like(mst)

    cos, sinlo, sinhi = cos_ref[...], sinlo_ref[...], sinhi_ref[...]
    segm = segm_ref[...]
    lane = _iota((1, LANES), 1)
    low = lane < 64
    qi = _iota((CHUNK, 2 * CHUNK), 0)
    si = _iota((CHUNK, 2 * CHUNK), 1)
    valid = (si >= qi) & (si <= qi + CHUNK) & ((si >= CHUNK) | (n > 0))
    causal = _iota((CHUNK, CHUNK), 0) >= _iota((CHUNK, CHUNK), 1)
    ones_col = jnp.where(_iota((CHUNK, LANES), 1) == 0, 1.0, 0.0)
    row64 = _iota((CHUNK, 1), 0) < 64
    new_kv = [
        _even_prompt_chunk(p_ref.at[b], y_ref.at[b], kprev.at[b], vprev.at[b], cst.at[b], mst.at[b],
                           (cos, sinlo, sinhi), qn_ref, kn_ref, sink_ref, segm, triu_ref, gb_ref, onorm_ref,
                           low, valid, causal, ones_col, row64)
        for b in range(batch)]

    @pl.when(n == pl.num_programs(0) - 1)
    def _():
        for b in range(batch):
            kc_ref[b], vc_ref[b] = new_kv[b]
        caug_ref[...] = cst[...]
        m_ref[...] = mst[...]


def _even_prompt_chunk(p_ref, y_ref, kprev, vprev, cst, mst, tabs, qn_ref, kn_ref, sink_ref, segm, triu_ref, gb_ref,
                       onorm_ref, low, valid, causal, ones_col, row64):
    cos, sinlo, sinhi = tabs
    k = _rope16(_seg_rms(p_ref[:, E_SK:E_SK + LANES], kn_ref[...], segm), cos, sinlo, sinhi)
    v = p_ref[:, E_SV:E_SV + LANES]
    kk = jnp.concatenate([kprev[...], k], axis=0)
    vv = jnp.concatenate([vprev[...], v], axis=0)
    kk_sw = pltpu.roll(kk, 64, 1)
    vv_sw = pltpu.roll(vv, 64, 1)
    kvar = {(0, 0): jnp.where(low, kk, 0.0), (0, 1): jnp.where(low, 0.0, kk_sw),
            (1, 0): jnp.where(low, kk_sw, 0.0), (1, 1): jnp.where(low, 0.0, kk)}
    vvar = {(0, 0): vv, (0, 1): vv_sw, (1, 0): vv_sw, (1, 1): vv}
    for j in range(SW_HEADS // 2):
        sl = slice(E_SQ + j * LANES, E_SQ + (j + 1) * LANES)
        qb = _rope16(_seg_rms(p_ref[:, sl], qn_ref[:, sl], segm), cos, sinlo, sinhi)
        halves = []
        for pos in range(2):
            h = 2 * j + pos
            kv = h // (SW_HEADS // SW_KV_HEADS)
            s = jnp.where(valid, _mm_nt(qb, kvar[(kv, pos)]) * (SW_HEAD_DIM ** -0.5), NEG)
            sink = sink_ref[h]
            m = jnp.maximum(jnp.max(s, axis=-1, keepdims=True), sink)
            pr = jnp.exp(s - m)
            pr = pr / (jnp.sum(pr, axis=-1, keepdims=True) + jnp.exp(sink - m))
            halves.append(_mm(pr, vvar[(kv, pos)]))
        y_ref[:, j * LANES:(j + 1) * LANES] = jnp.where(low, halves[0], halves[1])
    kprev[...] = k
    vprev[...] = v

    gt = p_ref[:, E_GATE:E_GATE + LANES].T
    gi = gt[0:SUBLANES] + gb_ref[0:SUBLANES]
    fl = _log_sigmoid(gt[GATE_F_LANE:GATE_F_LANE + SUBLANES] + gb_ref[SUBLANES:2 * SUBLANES])
    fcum = _mm_exact_rhs(fl, triu_ref[...])
    dd = gi - fcum
    mprev = mst[...]
    mt = fcum + jnp.maximum(mprev, _cummax_lanes(dd))
    fend = jnp.broadcast_to(fcum[:, CHUNK - 1:CHUNK], fcum.shape)
    mend = jnp.broadcast_to(mt[:, CHUNK - 1:CHUNK], mt.shape)
    decay = jnp.exp(fend + mprev - mend)
    rows = jnp.concatenate([fcum - mt, jnp.exp(fcum + mprev - mt), jnp.exp(-mt), jnp.exp(fend - fcum + gi - mend),
                            jnp.zeros((CHUNK - 4 * SUBLANES, CHUNK), F32)], axis=0)
    cols = rows.T
    for j in range(ML_HEADS // 2):
        qblk = p_ref[:, E_MQ + j * LANES:E_MQ + (j + 1) * LANES]
        kblk = p_ref[:, E_MK + j * LANES:E_MK + (j + 1) * LANES] * (ML_QK_DIM ** -0.5)
        c_old = cst[j]
        upd = None
        for pos in range(2):
            h = 2 * j + pos
            msk = low if pos == 0 else jnp.logical_not(low)
            qm = jnp.where(msk, qblk, 0.0)
            logw = cols[:, h:h + 1] + dd[h:h + 1, :]
            w = jnp.exp(jnp.where(causal, logw, -jnp.inf))
            sqk = _mm_nt(qm, kblk) * w
            vh = p_ref[:, E_MV + h * LANES:E_MV + (h + 1) * LANES]
            qc = _mm(qm, c_old)
            cs = cols[:, SUBLANES + h:SUBLANES + h + 1]
            num = cs * qc[:, :ML_V_DIM] + _mm(sqk, vh)
            den = cs * qc[:, ML_V_DIM:ML_V_DIM + 1] + jnp.sum(sqk, axis=-1, keepdims=True)
            hh = num / jnp.maximum(jnp.abs(den), cols[:, 2 * SUBLANES + h:2 * SUBLANES + h + 1])
            hsl = slice(h * ML_V_DIM, (h + 1) * ML_V_DIM)
            hn = _rms(hh, onorm_ref[:, hsl])
            mo = p_ref[:, E_MO + h * ML_V_DIM:E_MO + (h + 1) * ML_V_DIM]
            y_ref[:, SW_HEADS * SW_HEAD_DIM + h * ML_V_DIM:SW_HEADS * SW_HEAD_DIM + (h + 1) * ML_V_DIM] = (
                hn * _sigmoid(mo))
            kw = jnp.where(msk, kblk, 0.0) * cols[:, 3 * SUBLANES + h:3 * SUBLANES + h + 1]
            u = _mm_tn(kw, jnp.concatenate([vh, ones_col], axis=1))
            upd = u if upd is None else upd + u
        dec = jnp.where(row64, decay[2 * j:2 * j + 1, 0:1], decay[2 * j + 1:2 * j + 2, 0:1])
        cst[j] = dec * c_old + upd
    mst[...] = mend
    return k, v


def even_prompt(proj, tabs, qn, kn, sinks, segm, triu, gb, onorm):
    batch, seq, _ = proj.shape
    tab = pl.BlockSpec((CHUNK, LANES), lambda n: (n, 0))
    const = lambda shape: pl.BlockSpec(shape, lambda n: (0,) * len(shape))
    state_shapes = [(batch, CHUNK, LANES), (batch, CHUNK, LANES),
                    (batch, ML_HEADS // 2, 2 * ML_QK_DIM, 2 * ML_V_DIM), (batch, SUBLANES, LANES)]
    return pl.pallas_call(
        _even_prompt_kernel,
        grid=(seq // CHUNK,),
        in_specs=[
            pl.BlockSpec((batch, CHUNK, EVEN_COLS), lambda n: (0, n, 0)), tab, tab, tab,
            const((1, SW_HEADS * SW_HEAD_DIM)), const((1, LANES)),
            pl.BlockSpec(memory_space=pltpu.SMEM),
            const((LANES, LANES)), const((CHUNK, CHUNK)), const((2 * SUBLANES, LANES)),
            const((1, ML_HEADS * ML_V_DIM)),
        ],
        out_specs=[pl.BlockSpec((batch, CHUNK, D_MODEL), lambda n: (0, n, 0))] + [const(s) for s in state_shapes],
        out_shape=[jax.ShapeDtypeStruct((batch, seq, D_MODEL), F32)]
        + [jax.ShapeDtypeStruct(s, F32) for s in state_shapes],
        scratch_shapes=[pltpu.VMEM(s, F32) for s in state_shapes],
        compiler_params=pltpu.CompilerParams(dimension_semantics=("arbitrary",), vmem_limit_bytes=VMEM_LIMIT_BYTES),
        name="even_prompt",
    )(proj, *tabs, qn, kn, sinks, segm, triu, gb, onorm)


def _odd_prompt_kernel(p_ref, cosr_ref, sinr_ref, convw_ref, convb_ref, dtb_ref, arow_ref, drow_ref, snorm_ref,
                       tril_ref, dmat_ref, qs_ref, ks_ref, cd_ref, rnorm_ref,
                       y_ref, conv_ref, s_ref, r_ref,
                       ext, sst, rst):
    n = pl.program_id(1)
    batch = p_ref.shape[0]

    @pl.when(n == 0)
    def _():
        ext[:, 0:SUBLANES] = jnp.zeros((batch, SUBLANES, SSD_CONV_DIM), F32)
        sst[...] = jnp.zeros_like(sst)
        rst[...] = jnp.zeros_like(rst)

    lane = _iota((1, LANES), 1)
    low = lane < 64
    row64 = _iota((CHUNK, 1), 0) < 64
    causal = _iota((CHUNK, CHUNK), 0) >= _iota((CHUNK, CHUNK), 1)
    tails = [
        _odd_prompt_chunk(p_ref.at[b], y_ref.at[b], ext.at[b], sst.at[b], rst.at[b], cosr_ref, sinr_ref, convw_ref,
                          convb_ref, dtb_ref, arow_ref, drow_ref, snorm_ref, tril_ref, dmat_ref, qs_ref, ks_ref,
                          cd_ref, rnorm_ref, low, row64, causal)
        for b in range(batch)]

    @pl.when(n == pl.num_programs(1) - 1)
    def _():
        for b in range(batch):
            conv_ref[b] = tails[b]
        s_ref[...] = sst[...]
        r_ref[...] = rst[...]


def _odd_prompt_chunk(p_ref, y_ref, ext, sst, rst, cosr_ref, sinr_ref, convw_ref, convb_ref, dtb_ref, arow_ref,
                      drow_ref, snorm_ref, tril_ref, dmat_ref, qs_ref, ks_ref, cd_ref, rnorm_ref, low, row64, causal):
    tail = SUBLANES
    ext[tail:tail + CHUNK] = p_ref[:, O_XBC:O_XBC + SSD_CONV_DIM]
    first = tail - (SSD_CONV - 1)
    acc = ext[first:first + CHUNK] * convw_ref[0:1]
    for jj in range(1, SSD_CONV):
        acc = acc + ext[first + jj:first + jj + CHUNK] * convw_ref[jj:jj + 1]
    xact = _silu(acc + convb_ref[...])
    new_tail = ext[CHUNK:CHUNK + tail]
    ext[0:tail] = new_tail

    dt = _softplus(p_ref[:, O_DT:O_DT + LANES] + dtb_ref[...])
    cum = _mm_exact_lhs(tril_ref[...], dt * arow_ref[...])
    cum_t = cum.T
    dt_t = dt.T
    ecum = jnp.exp(cum)
    cend = cum[CHUNK - 1:CHUNK, :]
    wend = jnp.exp(cend - cum) * dt
    eend = jnp.exp(cend)
    pairs_per_group = SSD_HEADS // SSD_GROUPS // 2
    ys = []
    for g in range(SSD_GROUPS):
        bc = xact[:, SSD_INNER + g * SSD_STATE:SSD_INNER + (g + 1) * SSD_STATE]
        cc = xact[:, SSD_INNER + (SSD_GROUPS + g) * SSD_STATE:SSD_INNER + (SSD_GROUPS + g + 1) * SSD_STATE]
        cb = _mm_nt(cc, bc)
        for jg in range(pairs_per_group):
            j = g * pairs_per_group + jg
            ha, hb = 2 * j, 2 * j + 1
            xp = xact[:, j * LANES:(j + 1) * LANES]
            s_old = sst[j]
            y = jnp.where(low, ecum[:, ha:ha + 1], ecum[:, hb:hb + 1]) * _mm_nt(cc, s_old)
            for pos, h in ((0, ha), (1, hb)):
                seg = cum[:, h:h + 1] - cum_t[h:h + 1, :]
                wmat = cb * jnp.exp(jnp.where(causal, seg, -jnp.inf)) * dt_t[h:h + 1, :]
                y = y + _mm(wmat, jnp.where(low if pos == 0 else jnp.logical_not(low), xp, 0.0))
            xw = xp * jnp.where(low, wend[:, ha:ha + 1], wend[:, hb:hb + 1])
            sst[j] = jnp.where(row64, eend[:, ha:ha + 1], eend[:, hb:hb + 1]) * s_old + _mm_tn(xw, bc)
            ys.append(y)
        gs = slice(g * SSD_INNER // SSD_GROUPS, (g + 1) * SSD_INNER // SSD_GROUPS)
        yg = jnp.concatenate(ys[g * pairs_per_group:(g + 1) * pairs_per_group], axis=1)
        yg = (yg + drow_ref[:, gs] * xact[:, gs]) * _silu(p_ref[:, O_Z + gs.start:O_Z + gs.stop])
        y_ref[:, gs] = _rms(yg, snorm_ref[:, gs])

    cosr, sinr = cosr_ref[...], sinr_ref[...]
    for h in range(RET_HEADS):
        hs = h * LANES
        q = _rope128(p_ref[:, O_RQ + hs:O_RQ + hs + LANES], cosr, sinr)
        k = _rope128(p_ref[:, O_RK + hs:O_RK + hs + LANES], cosr, sinr) * (RET_QK_DIM ** -0.5)
        v = p_ref[:, O_RV + hs:O_RV + hs + LANES]
        r_old = rst[h]
        o = _mm(_mm_nt(q, k) * dmat_ref[h], v) + qs_ref[h] * _mm(q, r_old)
        rst[h] = cd_ref[h] * r_old + _mm_tn(k * ks_ref[h], v)
        xc = o - jnp.mean(o, axis=-1, keepdims=True)
        yn = xc * lax.rsqrt(jnp.mean(xc * xc, axis=-1, keepdims=True) + EPS) * rnorm_ref[:, hs:hs + LANES]
        y_ref[:, SSD_INNER + hs:SSD_INNER + hs + LANES] = yn * _silu(p_ref[:, O_RG + hs:O_RG + hs + LANES])
    return new_tail


def odd_prompt(proj, tabs, convw, convb, dtb, arow, drow, snorm, tril, ret_consts, rnorm, *, group):
    batch, seq, _ = proj.shape
    tab = pl.BlockSpec((CHUNK, LANES), lambda g, n: (n, 0))
    const = lambda shape: pl.BlockSpec(shape, lambda g, n: (0,) * len(shape))
    per_g = lambda shape: pl.BlockSpec((group,) + shape, lambda g, n: (g,) + (0,) * len(shape))
    hc = (RET_HEADS, CHUNK, LANES)
    ywidth = SSD_INNER + RET_HEADS * RET_V_DIM
    states = [(SUBLANES, SSD_CONV_DIM), (SSD_HEADS // 2, LANES, SSD_STATE), hc]
    return pl.pallas_call(
        _odd_prompt_kernel,
        grid=(batch // group, seq // CHUNK),
        in_specs=[
            pl.BlockSpec((group, CHUNK, ODD_COLS), lambda g, n: (g, n, 0)), tab, tab,
            const((SSD_CONV, SSD_CONV_DIM)), const((1, SSD_CONV_DIM)), const((1, LANES)), const((1, LANES)),
            const((1, SSD_INNER)), const((1, SSD_INNER)), const((CHUNK, CHUNK)),
            const(hc), const(hc), const(hc), const(hc), const((1, RET_HEADS * RET_V_DIM)),
        ],
        out_specs=[pl.BlockSpec((group, CHUNK, ywidth), lambda g, n: (g, n, 0))] + [per_g(s) for s in states],
        out_shape=[jax.ShapeDtypeStruct((batch, seq, ywidth), F32)]
        + [jax.ShapeDtypeStruct((batch,) + s, F32) for s in states],
        scratch_shapes=[pltpu.VMEM((group, SUBLANES + CHUNK, SSD_CONV_DIM), F32),
                        pltpu.VMEM((group,) + states[1], F32), pltpu.VMEM((group,) + states[2], F32)],
        compiler_params=pltpu.CompilerParams(dimension_semantics=("arbitrary", "arbitrary"),
                                             vmem_limit_bytes=VMEM_LIMIT_BYTES),
        name="odd_prompt",
    )(proj, *tabs, convw, convb, dtb, arow, drow, snorm, tril, *ret_consts, rnorm)


def _lane_to_rows(g, offset):
    sel = _iota(g.shape, 1) == _iota(g.shape, 0) + offset
    return jnp.sum(jnp.where(sel, g, 0.0), axis=-1, keepdims=True)


def _block_rows(x, i, nblk, blk):
    row = _iota((SUBLANES, blk), 0)
    out = jnp.zeros((SUBLANES, blk), F32)
    for b in range(nblk):
        out = jnp.where(row == b, x[i:i + 1, b * blk:(b + 1) * blk], out)
    return out


def _even_sample_kernel(p_ref, bk_ref, bv_ref, c_ref, nrow_ref, mrow_ref, cos_ref, sinlo_ref, sinhi_ref, qn_ref,
                        kn_ref, sink_ref, segm_ref, gb_ref, onorm_ref,
                        y_ref, nk_ref, nv_ref, nc_ref, nn_ref, nm_ref):
    R = SUBLANES
    cos, sinlo, sinhi = cos_ref[...], sinlo_ref[...], sinhi_ref[...]
    segm = segm_ref[...]
    row = _iota((R, LANES), 0)
    lane = _iota((R, LANES), 1)
    low = lane < 64
    group = SW_HEADS // SW_KV_HEADS
    scale = SW_HEAD_DIM ** -0.5
    sink = sink_ref[:, 0:1]
    last = _iota((CHUNK, LANES), 0) == CHUNK - 1

    k = _rope16(_seg_rms(p_ref[:, E_SK:E_SK + LANES], kn_ref[...], segm), cos, sinlo, sinhi)
    v = p_ref[:, E_SV:E_SV + LANES]
    qb, qb_sw = [], []
    for j in range(SW_HEADS // 2):
        sl = slice(E_SQ + j * LANES, E_SQ + (j + 1) * LANES)
        qb.append(_rope16(_seg_rms(p_ref[:, sl], qn_ref[:, sl], segm), cos, sinlo, sinhi))
        qb_sw.append(pltpu.roll(qb[j], 64, 1))

    g = p_ref[:, E_GATE:E_GATE + LANES]
    ic = g + gb_ref[0:1]
    fl = _log_sigmoid(pltpu.roll(g, LANES - GATE_F_LANE, 1) + gb_ref[1:2])
    mprev = mrow_ref[...]
    mt = jnp.maximum(fl + mprev, ic)
    w_all = jnp.exp(ic - mt)
    cs_all = jnp.exp(fl + mprev - mt)
    em_all = jnp.exp(-mt)
    nm_ref[...] = mt
    width = ML_HEADS * ML_QK_DIM
    own = _iota((R, width), 1) // ML_QK_DIM == _iota((R, width), 0)
    kscaled = p_ref[:, E_MK:E_MK + width] * (ML_QK_DIM ** -0.5)

    for i in range(R):
        qm = jnp.zeros((R, LANES), F32)
        for j in range(SW_HEADS // 2):
            for pos in range(2):
                h = 2 * j + pos
                kv = h // group
                src = (qb[j] if pos == kv else qb_sw[j])[i:i + 1]
                qm = jnp.where((row == h) & (low if kv == 0 else jnp.logical_not(low)), src, qm)
        bk, bv = bk_ref[i], bv_ref[i]
        ki, vi = k[i:i + 1], v[i:i + 1]
        s = _dg(qm, bk, 1, 1) * scale
        s_new = jnp.sum(qm * ki, axis=-1, keepdims=True) * scale
        m = jnp.maximum(jnp.maximum(jnp.max(s, axis=-1, keepdims=True), s_new), sink)
        pr = jnp.exp(s - m)
        p_new = jnp.exp(s_new - m)
        den = jnp.sum(pr, axis=-1, keepdims=True) + p_new + jnp.exp(sink - m)
        o = (_dg(pr, bv, 1, 0) + p_new * vi) / den
        o_sw = pltpu.roll(o, 64, 1)
        for j in range(SW_HEADS // 2):
            halves = []
            for pos in range(2):
                h = 2 * j + pos
                halves.append((o if pos == h // group else o_sw)[h:h + 1, :])
            y_ref[i:i + 1, j * LANES:(j + 1) * LANES] = jnp.where(low[0:1], halves[0], halves[1])
        nk_ref[i] = jnp.where(last, ki, pltpu.roll(bk, CHUNK - 1, 0))
        nv_ref[i] = jnp.where(last, vi, pltpu.roll(bv, CHUNK - 1, 0))

        w = _lane_to_rows(jnp.broadcast_to(w_all[i:i + 1], (R, LANES)), 0)
        cs = _lane_to_rows(jnp.broadcast_to(cs_all[i:i + 1], (R, LANES)), 0)
        em = _lane_to_rows(jnp.broadcast_to(em_all[i:i + 1], (R, LANES)), 0)
        qrows = jnp.where(own, p_ref[i:i + 1, E_MQ:E_MQ + width], 0.0)
        krows = jnp.where(own, kscaled[i:i + 1], 0.0)
        c_old = c_ref[i]
        qc = _dg(qrows, c_old, 1, 0)
        qn_dot = jnp.sum(qrows * nrow_ref[i:i + 1], axis=-1, keepdims=True)
        sqk = jnp.sum(qrows * krows, axis=-1, keepdims=True) * w
        v4 = _block_rows(p_ref[:, E_MV:E_MV + ML_HEADS * ML_V_DIM], i, ML_HEADS, ML_V_DIM)
        mo4 = _block_rows(p_ref[:, E_MO:E_MO + ML_HEADS * ML_V_DIM], i, ML_HEADS, ML_V_DIM)
        num = cs * qc + sqk * v4
        dn = cs * qn_dot + sqk
        hh = num / jnp.maximum(jnp.abs(dn), em)
        hn = _rms(hh, onorm_ref[...]) * _sigmoid(mo4)
        for h in range(ML_HEADS):
            c0 = SW_HEADS * SW_HEAD_DIM + h * ML_V_DIM
            y_ref[i:i + 1, c0:c0 + ML_V_DIM] = hn[h:h + 1]
        dec_col = jnp.concatenate(
            [jnp.broadcast_to(cs[h:h + 1, 0:1], (ML_QK_DIM, ML_V_DIM)) for h in range(ML_HEADS)], axis=0)
        nc_ref[i] = dec_col * c_old + _dg(krows * w, v4, 0, 0)
        dec_lanes = jnp.sum(jnp.where(own, cs, 0.0), axis=0, keepdims=True)
        nn_ref[i:i + 1] = dec_lanes * nrow_ref[i:i + 1] + jnp.sum(krows * w, axis=0, keepdims=True)


def even_sample(proj, bk, bv, c, nrow, mrow, tabs, qn, kn, sink_rows, segm, gb, onorm_rows):
    b = proj.shape[0]
    R = SUBLANES
    width = ML_HEADS * ML_QK_DIM
    rows = lambda w: pl.BlockSpec((R, w), lambda i: (i, 0))
    per_b = lambda shape: pl.BlockSpec((R,) + shape, lambda i: (i,) + (0,) * len(shape))
    const = lambda shape: pl.BlockSpec(shape, lambda i: (0,) * len(shape))
    tab = const((R, LANES))
    return pl.pallas_call(
        _even_sample_kernel,
        grid=(b // R,),
        in_specs=[
            rows(EVEN_COLS), per_b((CHUNK, LANES)), per_b((CHUNK, LANES)), per_b((width, ML_V_DIM)),
            rows(width), rows(LANES), tab, tab, tab,
            const((1, SW_HEADS * SW_HEAD_DIM)), const((1, LANES)), const((R, LANES)),
            const((LANES, LANES)), const((2, LANES)), const((R, ML_V_DIM)),
        ],
        out_specs=[
            rows(D_MODEL), per_b((CHUNK, LANES)), per_b((CHUNK, LANES)), per_b((width, ML_V_DIM)),
            rows(width), rows(LANES),
        ],
        out_shape=[
            jax.ShapeDtypeStruct((b, D_MODEL), F32),
            jax.ShapeDtypeStruct((b, CHUNK, LANES), F32), jax.ShapeDtypeStruct((b, CHUNK, LANES), F32),
            jax.ShapeDtypeStruct((b, width, ML_V_DIM), F32), jax.ShapeDtypeStruct((b, width), F32),
            jax.ShapeDtypeStruct((b, LANES), F32),
        ],
        compiler_params=pltpu.CompilerParams(dimension_semantics=("arbitrary",), vmem_limit_bytes=VMEM_LIMIT_BYTES),
        name="even_sample",
    )(proj, bk, bv, c, nrow, mrow, *tabs, qn, kn, sink_rows, segm, gb, onorm_rows)


def _odd_sample_kernel(p_ref, cb_ref, s_ref, r_ref, cosr_ref, sinr_ref, convw_ref, convb_ref, dtb_ref, arow_ref,
                       drow_ref, snorm_ref, spread_ref, rdec_ref, rnorm_ref,
                       y_ref, ncb_ref, ns_ref, nr_ref):
    R = SUBLANES
    xbc = p_ref[:, O_XBC:O_XBC + SSD_CONV_DIM]
    acc = cb_ref[0] * convw_ref[0:1]
    for jj in range(1, SSD_CONV - 1):
        acc = acc + cb_ref[jj] * convw_ref[jj:jj + 1]
    acc = acc + xbc * convw_ref[SSD_CONV - 1:SSD_CONV]
    xact = _silu(acc + convb_ref[...])
    for jj in range(SSD_CONV - 2):
        ncb_ref[jj] = cb_ref[jj + 1]
    ncb_ref[SSD_CONV - 2] = xbc

    dt = _softplus(p_ref[:, O_DT:O_DT + LANES] + dtb_ref[...])
    dec = jnp.exp(dt * arow_ref[...])
    xs = xact[:, 0:SSD_INNER]
    xdt = xs * _mm_exact_rhs(dt, spread_ref[...])
    gown = _iota((R, SSD_INNER), 1) // (SSD_INNER // SSD_GROUPS) == _iota((R, SSD_INNER), 0)
    bpart = xact[:, SSD_INNER:SSD_INNER + SSD_GROUPS * SSD_STATE]
    cpart = xact[:, SSD_INNER + SSD_GROUPS * SSD_STATE:SSD_CONV_DIM]

    cosr, sinr = cosr_ref[...], sinr_ref[...]
    width = RET_HEADS * RET_QK_DIM
    q4 = jnp.concatenate([_rope128(p_ref[:, O_RQ + h * LANES:O_RQ + (h + 1) * LANES], cosr, sinr)
                          for h in range(RET_HEADS)], axis=1)
    k4 = jnp.concatenate([_rope128(p_ref[:, O_RK + h * LANES:O_RK + (h + 1) * LANES], cosr, sinr)
                          for h in range(RET_HEADS)], axis=1) * (RET_QK_DIM ** -0.5)
    own = _iota((R, width), 1) // RET_QK_DIM == _iota((R, width), 0)
    gam = rdec_ref[:, 0:1]
    gam_col = jnp.concatenate(
        [jnp.broadcast_to(rdec_ref[h:h + 1, :], (RET_QK_DIM, RET_V_DIM)) for h in range(RET_HEADS)], axis=0)

    ys = []
    for i in range(R):
        brows = _block_rows(bpart, i, SSD_GROUPS, SSD_STATE)
        crows = _block_rows(cpart, i, SSD_GROUPS, SSD_STATE)
        xw = jnp.where(gown, xdt[i:i + 1], 0.0)
        dec_col = jnp.concatenate(
            [jnp.broadcast_to(dec[i:i + 1, h:h + 1], (SSD_HEAD_DIM, SSD_STATE)) for h in range(SSD_HEADS)], axis=0)
        s_new = dec_col * s_ref[i] + _dg(xw, brows, 0, 0)
        ns_ref[i] = s_new
        yrows = _dg(crows, s_new, 1, 1)
        ys.append(jnp.sum(jnp.where(gown, yrows, 0.0), axis=0, keepdims=True))

        qrows = jnp.where(own, q4[i:i + 1], 0.0)
        krows = jnp.where(own, k4[i:i + 1], 0.0)
        v4 = _block_rows(p_ref[:, O_RV:O_RV + RET_HEADS * RET_V_DIM], i, RET_HEADS, RET_V_DIM)
        g4 = _block_rows(p_ref[:, O_RG:O_RG + RET_HEADS * RET_V_DIM], i, RET_HEADS, RET_V_DIM)
        r_old = r_ref[i]
        att = jnp.sum(qrows * krows, axis=-1, keepdims=True)
        o = att * v4 + gam * _dg(qrows, r_old, 1, 0)
        nr_ref[i] = gam_col * r_old + _dg(krows, v4, 0, 0)
        xc = o - jnp.mean(o, axis=-1, keepdims=True)
        yn = xc * lax.rsqrt(jnp.mean(xc * xc, axis=-1, keepdims=True) + EPS) * rnorm_ref[...] * _silu(g4)
        for h in range(RET_HEADS):
            y_ref[i:i + 1, SSD_INNER + h * RET_V_DIM:SSD_INNER + (h + 1) * RET_V_DIM] = yn[h:h + 1]

    y = (jnp.concatenate(ys, axis=0) + drow_ref[...] * xs) * _silu(p_ref[:, O_Z:O_Z + SSD_INNER])
    gw = SSD_INNER // SSD_GROUPS
    for g in range(SSD_GROUPS):
        y_ref[:, g * gw:(g + 1) * gw] = _rms(y[:, g * gw:(g + 1) * gw], snorm_ref[:, g * gw:(g + 1) * gw])


def odd_sample(proj, cbuf, s, r, tabs, convw, convb, dtb, arow, drow, snorm, spread, rdec_rows, rnorm_rows):
    b = proj.shape[0]
    R = SUBLANES
    rows = lambda w: pl.BlockSpec((R, w), lambda i: (i, 0))
    per_b = lambda shape: pl.BlockSpec((R,) + shape, lambda i: (i,) + (0,) * len(shape))
    const = lambda shape: pl.BlockSpec(shape, lambda i: (0,) * len(shape))
    tab = const((R, LANES))
    conv = pl.BlockSpec((SSD_CONV - 1, R, SSD_CONV_DIM), lambda i: (0, i, 0))
    ywidth = SSD_INNER + RET_HEADS * RET_V_DIM
    sshape = (SSD_HEADS * SSD_HEAD_DIM, SSD_STATE)
    rshape = (RET_HEADS * RET_QK_DIM, RET_V_DIM)
    return pl.pallas_call(
        _odd_sample_kernel,
        grid=(b // R,),
        in_specs=[
            rows(ODD_COLS), conv, per_b(sshape), per_b(rshape), tab, tab,
            const((SSD_CONV, SSD_CONV_DIM)), const((1, SSD_CONV_DIM)), const((1, LANES)), const((1, LANES)),
            const((1, SSD_INNER)), const((1, SSD_INNER)), const((LANES, SSD_INNER)),
            const((R, LANES)), const((R, RET_V_DIM)),
        ],
        out_specs=[rows(ywidth), conv, per_b(sshape), per_b(rshape)],
        out_shape=[
            jax.ShapeDtypeStruct((b, ywidth), F32),
            jax.ShapeDtypeStruct((SSD_CONV - 1, b, SSD_CONV_DIM), F32),
            jax.ShapeDtypeStruct((b,) + sshape, F32), jax.ShapeDtypeStruct((b,) + rshape, F32),
        ],
        compiler_params=pltpu.CompilerParams(dimension_semantics=("arbitrary",), vmem_limit_bytes=VMEM_LIMIT_BYTES),
        name="odd_sample",
    )(proj, cbuf, s, r, *tabs, convw, convb, dtb, arow, drow, snorm, spread, rdec_rows, rnorm_rows)


def _pad_cols(w, n):
    return jnp.pad(w, ((0, 0), (0, n - w.shape[1])))


def _even_w_in(w):
    sq, sk, sv, mq, mk, mv, mo, mi, mf = jnp.split(w, [512, 640, 768, 1024, 1280, 1792, 2304, 2308], axis=1)
    gates = jnp.concatenate([_pad_cols(mi, GATE_F_LANE), _pad_cols(mf, LANES - GATE_F_LANE)], axis=1)
    return jnp.concatenate([sq, sk, sv, mq, mk, mv, mo, gates], axis=1).astype(BF16)


def _odd_w_in(w):
    z, xbc, dt, rq, rk, rv, rg = jnp.split(w, [1024, 2560, 2576, 3088, 3600, 4112], axis=1)
    return jnp.concatenate([z, xbc, rq, rk, rv, rg, _pad_cols(dt, LANES)], axis=1).astype(BF16)


def _rope16_tables(pos):
    half = SW_ROT_DIM // 2
    inv = jnp.power(jnp.float32(ROPE_THETA), -jnp.arange(half, dtype=F32) * (2.0 / SW_ROT_DIM))
    ang = pos.astype(F32)[:, None] * inv[None, :]
    cos, sin = jnp.cos(ang), jnp.sin(ang)
    n = pos.shape[0]
    rest = SW_HEAD_DIM - SW_ROT_DIM
    one, zero, zh = jnp.ones((n, rest), F32), jnp.zeros((n, rest), F32), jnp.zeros((n, half), F32)
    tile2 = lambda t: jnp.concatenate([t, t], axis=1)
    return (tile2(jnp.concatenate([cos, cos, one], axis=1)),
            tile2(jnp.concatenate([-sin, zh, zero], axis=1)),
            tile2(jnp.concatenate([zh, sin, zero], axis=1)))


def _rope128_tables(pos):
    half = RET_QK_DIM // 2
    inv = jnp.power(jnp.float32(RET_ROPE_THETA), -jnp.arange(half, dtype=F32) * (2.0 / RET_QK_DIM))
    ang = pos.astype(F32)[:, None] * inv[None, :]
    cos, sin = jnp.cos(ang), jnp.sin(ang)
    return jnp.concatenate([cos, cos], axis=1), jnp.concatenate([-sin, sin], axis=1)


def _ret_consts():
    L = CHUNK
    lg = jnp.log(1.0 - jnp.exp2(-5.0 - jnp.arange(RET_HEADS, dtype=F32)))
    idx = jnp.arange(L, dtype=F32)
    diff = idx[:, None] - idx[None, :]
    dmat = jnp.exp(jnp.where(diff >= 0, diff[None] * lg[:, None, None], -jnp.inf))
    q_scale = jnp.exp((idx[None] + 1.0) * lg[:, None])
    k_scale = jnp.exp((L - 1.0 - idx[None]) * lg[:, None])
    chunk_decay = jnp.exp(L * lg)
    bc = lambda t: jnp.broadcast_to(t[:, :, None], (RET_HEADS, L, LANES))
    cd = jnp.broadcast_to(chunk_decay[:, None, None], (RET_HEADS, L, LANES))
    return dmat, bc(q_scale), bc(k_scale), cd, lg


def _rows8(t):
    return jnp.pad(t, ((0, SUBLANES - t.shape[0]), (0, 0)))


def _gate_bias_rows(gb):
    ib = jnp.broadcast_to(gb[:ML_HEADS, None], (ML_HEADS, LANES))
    fb = jnp.broadcast_to(gb[ML_HEADS:, None], (ML_HEADS, LANES))
    return jnp.concatenate([_rows8(ib), _rows8(fb)], axis=0)


def kernel(x_prompt, x_sample, cache_mem_k, cache_mem_v, cache_swa_k, cache_swa_v, state_mlstm_C, state_mlstm_n,
           state_mlstm_m, state_ssd_conv, state_ssd, state_ret, mem_prompt, norm_mix, norm_xattn, norm_mem, norm_ffn,
           even_w_in, mlstm_gate_bias, swa_q_norm, swa_k_norm, swa_sinks, mlstm_out_norm, even_w_out, odd_w_in,
           ssd_conv_w, ssd_conv_b, ssd_dt_bias, ssd_a_log, ssd_d, ssd_norm, ret_norm, odd_w_out, mem_wq, mem_wk,
           mem_wv, mem_q_norm, mem_k_norm, mem_wo, ffn_w1, ffn_w2):
    bp, seq, d = x_prompt.shape
    bs = x_sample.shape[0]
    depth = norm_mix.shape[0]
    tm = 512

    pos_p = jnp.arange(seq, dtype=jnp.int32)
    pos_s = jnp.full((SUBLANES,), PAST_LEN, dtype=jnp.int32)
    tab16_p, tab16_s = _rope16_tables(pos_p), _rope16_tables(pos_s)
    tab128_p, tab128_s = _rope128_tables(pos_p), _rope128_tables(pos_s)
    dmat, q_scale, k_scale, chunk_decay, lg = _ret_consts()
    rdec_rows = _rows8(jnp.broadcast_to(jnp.exp(lg)[:, None], (RET_HEADS, LANES)))
    ii = jnp.arange(CHUNK)
    triu = (ii[:, None] <= ii[None, :]).astype(F32)
    tril = (ii[:, None] >= ii[None, :]).astype(F32)
    jj = jnp.arange(LANES)
    segm = jnp.where(jj[:, None] // SW_HEAD_DIM == jj[None, :] // SW_HEAD_DIM, 1.0 / SW_HEAD_DIM, 0.0).astype(F32)
    spread = (jj[:, None] == jnp.arange(SSD_INNER)[None, :] // SSD_HEAD_DIM).astype(F32)
    row1 = lambda t: t.reshape(1, -1).astype(F32)
    pad_lanes = lambda t: jnp.pad(t.reshape(1, -1).astype(F32), ((0, 0), (0, LANES - t.shape[-1])))

    yp = x_prompt.reshape(bp * seq, d)
    ys = x_sample.reshape(bs, d)
    mem = mem_prompt.reshape(bp * MEM_LEN, d)
    cmk = cache_mem_k.reshape(depth, bs, MEM_LEN * MEM_HEADS, MEM_HEAD_DIM)
    cmv = cache_mem_v.reshape(depth, bs, MEM_LEN * MEM_HEADS, MEM_HEAD_DIM)
    p_mk, p_mv = [], []
    outs = {}
    for l in range(depth):
        g_mix = row1(norm_mix[l])
        if l % 2 == 0:
            e = l // 2
            w_in = _even_w_in(even_w_in[e])
            w_out = even_w_out[e].astype(BF16)
            qn = row1(jnp.tile(swa_q_norm[e], SW_HEADS))
            kn = row1(jnp.tile(swa_k_norm[e], SW_KV_HEADS))
            gb = _gate_bias_rows(mlstm_gate_bias[e].astype(F32))
            onorm = row1(mlstm_out_norm[e])
            sinks = swa_sinks[e].astype(F32)
            proj_p = norm_proj(yp, g_mix, w_in, tm=tm)
            mix_p, kc, vc, caug, mm = even_prompt(proj_p.reshape(bp, seq, EVEN_COLS), tab16_p, qn, kn, sinks, segm,
                                                  triu, gb, onorm)
            mix_p = mix_p.reshape(bp * seq, -1)
            outs["p_swk"] = kc.reshape(1, bp, CHUNK, SW_KV_HEADS, SW_HEAD_DIM)
            outs["p_swv"] = vc.reshape(1, bp, CHUNK, SW_KV_HEADS, SW_HEAD_DIM)
            outs["p_c"] = caug[..., :ML_V_DIM].reshape(1, bp, ML_HEADS, ML_QK_DIM, ML_V_DIM)
            outs["p_n"] = caug[..., ML_V_DIM].reshape(1, bp, ML_HEADS, ML_QK_DIM)
            outs["p_m"] = mm[:, :ML_HEADS, 0].reshape(1, bp, ML_HEADS)

            proj_s = norm_proj(ys, g_mix, w_in, tm=bs)
            sink_rows = jnp.broadcast_to(sinks[:, None], (SW_HEADS, LANES))
            onorm_rows = _rows8(mlstm_out_norm[e].astype(F32).reshape(ML_HEADS, ML_V_DIM))
            gb_lanes = jnp.concatenate([pad_lanes(mlstm_gate_bias[e][:ML_HEADS]),
                                        pad_lanes(mlstm_gate_bias[e][ML_HEADS:])], axis=0)
            mix_s, nk, nv, ncst, nn, nm = even_sample(
                proj_s,
                cache_swa_k[e].reshape(bs, CHUNK, LANES), cache_swa_v[e].reshape(bs, CHUNK, LANES),
                state_mlstm_C[e].reshape(bs, ML_HEADS * ML_QK_DIM, ML_V_DIM),
                state_mlstm_n[e].reshape(bs, ML_HEADS * ML_QK_DIM),
                jnp.pad(state_mlstm_m[e], ((0, 0), (0, LANES - ML_HEADS))),
                tab16_s, qn, kn, sink_rows, segm, gb_lanes, onorm_rows)
            outs["s_swk"] = nk.reshape(1, bs, CHUNK, SW_KV_HEADS, SW_HEAD_DIM)
            outs["s_swv"] = nv.reshape(1, bs, CHUNK, SW_KV_HEADS, SW_HEAD_DIM)
            outs["s_c"] = ncst.reshape(1, bs, ML_HEADS, ML_QK_DIM, ML_V_DIM)
            outs["s_n"] = nn.reshape(1, bs, ML_HEADS, ML_QK_DIM)
            outs["s_m"] = nm[:, :ML_HEADS].reshape(1, bs, ML_HEADS)
        else:
            o = l // 2
            w_in = _odd_w_in(odd_w_in[o])
            w_out = odd_w_out[o].astype(BF16)
            convw = ssd_conv_w[o].astype(F32)
            convb = row1(ssd_conv_b[o])
            dtb = pad_lanes(ssd_dt_bias[o])
            arow = pad_lanes(-jnp.exp(ssd_a_log[o].astype(F32)))
            drow = row1(jnp.repeat(ssd_d[o].astype(F32), SSD_HEAD_DIM))
            snorm = row1(ssd_norm[o])
            rnorm = row1(ret_norm[o])
            proj_p = norm_proj(yp, g_mix, w_in, tm=tm)
            mix_p, ctail, sst, rst = odd_prompt(proj_p.reshape(bp, seq, ODD_COLS), tab128_p, convw, convb, dtb, arow,
                                                drow, snorm, tril, (dmat, q_scale, k_scale, chunk_decay), rnorm,
                                                group=1)
            mix_p = mix_p.reshape(bp * seq, -1)
            outs["p_conv"] = ctail[:, SUBLANES - (SSD_CONV - 1):, :].reshape(1, bp, SSD_CONV - 1, SSD_CONV_DIM)
            outs["p_ssd"] = sst.reshape(1, bp, SSD_HEADS, SSD_HEAD_DIM, SSD_STATE)
            outs["p_ret"] = rst.reshape(1, bp, RET_HEADS, RET_QK_DIM, RET_V_DIM)

            proj_s = norm_proj(ys, g_mix, w_in, tm=bs)
            rnorm_rows = _rows8(ret_norm[o].astype(F32).reshape(RET_HEADS, RET_V_DIM))
            mix_s, ncb, ns, nr = odd_sample(
                proj_s, jnp.swapaxes(state_ssd_conv[o], 0, 1),
                state_ssd[o].reshape(bs, SSD_HEADS * SSD_HEAD_DIM, SSD_STATE),
                state_ret[o].reshape(bs, RET_HEADS * RET_QK_DIM, RET_V_DIM),
                tab128_s, convw, convb, dtb, arow, drow, snorm, spread, rdec_rows, rnorm_rows)
            outs["s_conv"] = jnp.swapaxes(ncb, 0, 1).reshape(1, bs, SSD_CONV - 1, SSD_CONV_DIM)
            outs["s_ssd"] = ns.reshape(1, bs, SSD_HEADS, SSD_HEAD_DIM, SSD_STATE)
            outs["s_ret"] = nr.reshape(1, bs, RET_HEADS, RET_QK_DIM, RET_V_DIM)

        wkv = jnp.concatenate([mem_wk[l], mem_wv[l]], axis=1).astype(BF16)
        qnorm = row1(mem_q_norm[l])
        mkv = norm_proj(mem, row1(norm_mem[l]), wkv, tm=tm, head_norm=row1(mem_k_norm[l]), head_norm_cols=MEM_WIDTH)
        p_mk.append(mkv[:, :MEM_WIDTH].reshape(bp, MEM_LEN, MEM_HEADS, MEM_HEAD_DIM))
        p_mv.append(mkv[:, MEM_WIDTH:].reshape(bp, MEM_LEN, MEM_HEADS, MEM_HEAD_DIM))
        wq, wo = mem_wq[l].astype(BF16), mem_wo[l].astype(BF16)
        gx, gf = row1(norm_xattn[l]), row1(norm_ffn[l])
        w1, w2 = ffn_w1[l].astype(BF16), ffn_w2[l].astype(BF16)
        yp = post_prompt(yp, mix_p, w_out, gx, wq, qnorm, mkv, wo, gf, w1, w2, rows_per_batch=seq, tm=tm)

        ys = res_proj(ys, mix_s, w_out)
        qs = norm_proj(ys, gx, wq, tm=bs, head_norm=qnorm, head_norm_cols=MEM_WIDTH)
        ys = res_proj_ffn(ys, xattn_sample(qs, cmk, cmv, l), wo, gf, w1, w2)

    return (yp.reshape(bp, seq, d), ys.reshape(bs, 1, d),
            jnp.stack(p_mk), jnp.stack(p_mv), outs["p_swk"], outs["p_swv"], outs["p_c"], outs["p_n"], outs["p_m"],
            outs["p_conv"], outs["p_ssd"], outs["p_ret"],
            outs["s_swk"], outs["s_swv"], outs["s_c"], outs["s_n"], outs["s_m"],
            outs["s_conv"], outs["s_ssd"], outs["s_ret"])
```

```python
import functools
import math

import jax
import jax.numpy as jnp
from jax import lax
from jax.experimental import pallas as pl
from jax.experimental.pallas import tpu as pltpu

F32 = jnp.float32
BF16 = jnp.bfloat16

D_MODEL = 1024
PAST_LEN = 8192
EPS = 1e-6
CHUNK = 128
NEG = -1e30

SW_HEADS, SW_KV_HEADS, SW_HEAD_DIM, SW_ROT_DIM = 8, 2, 64, 16
ROPE_THETA = 500000.0
ML_HEADS, ML_QK_DIM, ML_V_DIM = 4, 64, 128
SSD_HEADS, SSD_HEAD_DIM, SSD_GROUPS, SSD_STATE, SSD_CONV = 16, 64, 2, 128, 4
SSD_INNER = SSD_HEADS * SSD_HEAD_DIM
SSD_CONV_DIM = SSD_INNER + 2 * SSD_GROUPS * SSD_STATE
RET_HEADS, RET_QK_DIM, RET_V_DIM = 4, 128, 128
RET_ROPE_THETA = 10000.0
MEM_LEN, MEM_HEADS, MEM_HEAD_DIM = 256, 4, 128
MEM_WIDTH = MEM_HEADS * MEM_HEAD_DIM
FFN_DIM = 4 * D_MODEL
FFN_CHUNK = 512

LANES = 128
SUBLANES = 8
VMEM_LIMIT_BYTES = 56 * 1024 * 1024

E_SQ, E_SK, E_SV, E_MQ, E_MK, E_MV, E_MO, E_GATE, EVEN_COLS = 0, 512, 640, 768, 1024, 1280, 1792, 2304, 2432
GATE_F_LANE = 8
O_Z, O_XBC, O_RQ, O_RK, O_RV, O_RG, O_DT, ODD_COLS = 0, 1024, 2560, 3072, 3584, 4096, 4608, 4736


def _mm(a, b):
    return jnp.dot(a.astype(BF16), b.astype(BF16), preferred_element_type=F32)


def _mm_nt(a, b):
    return lax.dot_general(a.astype(BF16), b.astype(BF16), (((1,), (1,)), ((), ())), preferred_element_type=F32)


def _mm_tn(a, b):
    return lax.dot_general(a.astype(BF16), b.astype(BF16), (((0,), (0,)), ((), ())), preferred_element_type=F32)


def _dg(a, b, ca, cb):
    return lax.dot_general(a, b, (((ca,), (cb,)), ((), ())), preferred_element_type=F32)


def _split3(x):
    hi = x.astype(BF16).astype(F32)
    r1 = x - hi
    mid = r1.astype(BF16).astype(F32)
    lo = (r1 - mid).astype(BF16).astype(F32)
    return hi, mid, lo


def _mm_exact_rhs(x, e):
    hi, mid, lo = _split3(x)
    return _dg(hi, e, 1, 0) + _dg(mid, e, 1, 0) + _dg(lo, e, 1, 0)


def _mm_exact_lhs(e, x):
    hi, mid, lo = _split3(x)
    return _dg(e, hi, 1, 0) + _dg(e, mid, 1, 0) + _dg(e, lo, 1, 0)


def _mm_tn_exact_lhs(x, e):
    hi, mid, lo = _split3(x)
    return _dg(hi, e, 0, 0) + _dg(mid, e, 0, 0) + _dg(lo, e, 0, 0)


def _rms(x, g):
    return x * lax.rsqrt(jnp.mean(x * x, axis=-1, keepdims=True) + EPS) * g


def _seg_rms(x, g, seg_mean):
    return x * lax.rsqrt(_mm_exact_rhs(x * x, seg_mean) + EPS) * g


def _sigmoid(x):
    return 1.0 / (1.0 + jnp.exp(-x))


def _silu(x):
    return x * _sigmoid(x)


def _softplus(x):
    return jnp.maximum(x, 0.0) + jnp.log1p(jnp.exp(-jnp.abs(x)))


def _log_sigmoid(x):
    return -_softplus(-x)


def _rope16(x, cos, sin_lo, sin_hi):
    return x * cos + pltpu.roll(x, LANES - 8, 1) * sin_lo + pltpu.roll(x, 8, 1) * sin_hi


def _rope128(x, cos, sin):
    return x * cos + pltpu.roll(x, 64, 1) * sin


def _iota(shape, dim):
    return lax.broadcasted_iota(jnp.int32, shape, dim)


def _cummax_lanes(x):
    lane = _iota(x.shape, 1)
    shift = 1
    while shift < x.shape[1]:
        x = jnp.maximum(x, jnp.where(lane >= shift, pltpu.roll(x, shift, 1), -jnp.inf))
        shift *= 2
    return x


def _norm_proj_kernel(x_ref, g_ref, w_ref, hn_ref, o_ref, *, chunks, head_norm_cols):
    xn = _rms(x_ref[...], g_ref[...]).astype(BF16)
    for c0, cs in chunks:
        r = jnp.dot(xn, w_ref[:, c0:c0 + cs], preferred_element_type=F32)
        if c0 < head_norm_cols:
            parts = [_rms(r[:, i:i + LANES], hn_ref[...]) for i in range(0, cs, LANES)]
            r = jnp.concatenate(parts, axis=1)
        o_ref[:, c0:c0 + cs] = r


def _col_chunks(n, width=512):
    return tuple((c, min(width, n - c)) for c in range(0, n, width))


def norm_proj(x, g, w, *, tm, head_norm=None, head_norm_cols=0):
    n, d = x.shape
    m = w.shape[1]
    if head_norm is None:
        head_norm = jnp.ones((1, LANES), F32)
    kern = functools.partial(_norm_proj_kernel, chunks=_col_chunks(m), head_norm_cols=head_norm_cols)
    return pl.pallas_call(
        kern,
        grid=(n // tm,),
        in_specs=[
            pl.BlockSpec((tm, d), lambda i: (i, 0)),
            pl.BlockSpec((1, d), lambda i: (0, 0)),
            pl.BlockSpec((d, m), lambda i: (0, 0), pipeline_mode=pl.Buffered(1)),
            pl.BlockSpec((1, LANES), lambda i: (0, 0)),
        ],
        out_specs=pl.BlockSpec((tm, m), lambda i: (i, 0)),
        out_shape=jax.ShapeDtypeStruct((n, m), F32),
        compiler_params=pltpu.CompilerParams(dimension_semantics=("arbitrary",), vmem_limit_bytes=VMEM_LIMIT_BYTES),
        name="norm_proj",
    )(x, g, w, head_norm)


def _ffn(x, g_ref, w1_ref, w2_ref):
    h = _rms(x, g_ref[...]).astype(BF16)
    acc = None
    for c in range(0, FFN_DIM, FFN_CHUNK):
        u = jnp.maximum(jnp.dot(h, w1_ref[:, c:c + FFN_CHUNK], preferred_element_type=F32), 0.0)
        t = jnp.dot((u * u).astype(BF16), w2_ref[c:c + FFN_CHUNK, :], preferred_element_type=F32)
        acc = t if acc is None else acc + t
    return x + acc


def _post_prompt_kernel(x_ref, a_ref, wout_ref, gx_ref, wq_ref, qn_ref, mk_ref, mv_ref, wo_ref, gf_ref, w1_ref, w2_ref,
                        o_ref):
    x = x_ref[...] + _mm(a_ref[...], wout_ref[...])
    q = jnp.dot(_rms(x, gx_ref[...]).astype(BF16), wq_ref[...], preferred_element_type=F32)
    outs = []
    for h in range(MEM_HEADS):
        sl = slice(h * MEM_HEAD_DIM, (h + 1) * MEM_HEAD_DIM)
        qh = _rms(q[:, sl], qn_ref[...])
        s = _mm_nt(qh, mk_ref[:, sl]) * (MEM_HEAD_DIM ** -0.5)
        p = jnp.exp(s - jnp.max(s, axis=-1, keepdims=True))
        p = p / jnp.sum(p, axis=-1, keepdims=True)
        outs.append(_mm(p, mv_ref[:, sl]))
    x = x + _mm(jnp.concatenate(outs, axis=1), wo_ref[...])
    o_ref[...] = _ffn(x, gf_ref, w1_ref, w2_ref)


def post_prompt(x, a, wout, gx, wq, qn, mkv, wo, gf, w1, w2, *, rows_per_batch, tm):
    n, d = x.shape
    ka = a.shape[1]
    tiles = rows_per_batch // tm
    const = lambda shape: pl.BlockSpec(shape, lambda i: (0, 0), pipeline_mode=pl.Buffered(1))
    return pl.pallas_call(
        _post_prompt_kernel,
        grid=(n // tm,),
        in_specs=[
            pl.BlockSpec((tm, d), lambda i: (i, 0)),
            pl.BlockSpec((tm, ka), lambda i: (i, 0)),
            const((ka, d)),
            const((1, d)),
            const((d, MEM_WIDTH)),
            const((1, MEM_HEAD_DIM)),
            pl.BlockSpec((MEM_LEN, MEM_WIDTH), lambda i: (i // tiles, 0)),
            pl.BlockSpec((MEM_LEN, MEM_WIDTH), lambda i: (i // tiles, 1)),
            const((MEM_WIDTH, d)),
            const((1, d)),
            const((d, FFN_DIM)),
            const((FFN_DIM, d)),
        ],
        out_specs=pl.BlockSpec((tm, d), lambda i: (i, 0)),
        out_shape=jax.ShapeDtypeStruct((n, d), F32),
        compiler_params=pltpu.CompilerParams(dimension_semantics=("arbitrary",), vmem_limit_bytes=VMEM_LIMIT_BYTES),
        name="post_prompt",
    )(x, a, wout, gx, wq, qn, mkv, mkv, wo, gf, w1, w2)


def _res_proj_kernel(x_ref, a_ref, w_ref, o_ref):
    o_ref[...] = x_ref[...] + _mm(a_ref[...], w_ref[...])


def res_proj(x, a, w):
    n, d = x.shape
    return pl.pallas_call(
        _res_proj_kernel,
        out_shape=jax.ShapeDtypeStruct((n, d), F32),
        compiler_params=pltpu.CompilerParams(vmem_limit_bytes=VMEM_LIMIT_BYTES),
        name="res_proj",
    )(x, a, w)


def _res_proj_ffn_kernel(x_ref, a_ref, w_ref, gf_ref, w1_ref, w2_ref, o_ref):
    x = x_ref[...] + _mm(a_ref[...], w_ref[...])
    o_ref[...] = _ffn(x, gf_ref, w1_ref, w2_ref)


def res_proj_ffn(x, a, w, gf, w1, w2):
    n, d = x.shape
    return pl.pallas_call(
        _res_proj_ffn_kernel,
        out_shape=jax.ShapeDtypeStruct((n, d), F32),
        compiler_params=pltpu.CompilerParams(vmem_limit_bytes=VMEM_LIMIT_BYTES),
        name="res_proj_ffn",
    )(x, a, w, gf, w1, w2)


def _pair_rows(x, op):
    xb = jnp.broadcast_to(x, (SUBLANES, LANES))
    return op(xb, pltpu.roll(xb, SUBLANES // 2, 0))


def _xattn_sample_kernel(q_ref, mk_ref, mv_ref, o_ref):
    row = _iota((SUBLANES, LANES), 0)
    groups = MEM_LEN * MEM_HEADS // SUBLANES
    for i in range(SUBLANES):
        q8 = jnp.zeros((SUBLANES, LANES), F32)
        for h in range(MEM_HEADS):
            q8 = jnp.where(row % MEM_HEADS == h, q_ref[i:i + 1, h * MEM_HEAD_DIM:(h + 1) * MEM_HEAD_DIM], q8)
        k3 = mk_ref[0, i].reshape(groups, SUBLANES, LANES)
        s = jnp.sum(k3 * q8[None], axis=-1, keepdims=True) * (MEM_HEAD_DIM ** -0.5)
        mx = _pair_rows(jnp.max(s, axis=0), jnp.maximum)
        p = jnp.exp(s - mx[None, :, 0:1])
        den = _pair_rows(jnp.sum(p, axis=0), jnp.add)
        v3 = mv_ref[0, i].reshape(groups, SUBLANES, LANES)
        o8 = _pair_rows(jnp.sum(p * v3, axis=0), jnp.add) / den
        for h in range(MEM_HEADS):
            o_ref[i:i + 1, h * MEM_HEAD_DIM:(h + 1) * MEM_HEAD_DIM] = o8[h:h + 1]


def xattn_sample(q, mk, mv, layer):
    b = q.shape[0]
    mem = pl.BlockSpec((1, SUBLANES, MEM_LEN * MEM_HEADS, MEM_HEAD_DIM), lambda i: (layer, i, 0, 0))
    return pl.pallas_call(
        _xattn_sample_kernel,
        grid=(b // SUBLANES,),
        in_specs=[pl.BlockSpec((SUBLANES, MEM_WIDTH), lambda i: (i, 0)), mem, mem],
        out_specs=pl.BlockSpec((SUBLANES, MEM_WIDTH), lambda i: (i, 0)),
        out_shape=jax.ShapeDtypeStruct((b, MEM_WIDTH), F32),
        compiler_params=pltpu.CompilerParams(dimension_semantics=("arbitrary",), vmem_limit_bytes=VMEM_LIMIT_BYTES),
        name="xattn_sample",
    )(q, mk, mv)


def _even_prompt_kernel(p_ref, cos_ref, sinlo_ref, sinhi_ref, qn_ref, kn_ref, sink_ref, segm_ref, triu_ref, gb_ref,
                        onorm_ref,
                        y_ref, kc_ref, vc_ref, caug_ref, m_ref,
                        kprev, vprev, cst, mst):
    n = pl.program_id(0)
    batch = p_ref.shape[0]

    @pl.when(n == 0)
    def _():
        kprev[...] = jnp.zeros_like(kprev)
        vprev[...] = jnp.zeros_like(vprev)
        cst[...] = jnp.zeros_like(cst)
        mst[...] = jnp.zeros_like(mst)

    cos, sinlo, sinhi = cos_ref[...], sinlo_ref[...], sinhi_ref[...]
    segm = segm_ref[...]
    lane = _iota((1, LANES), 1)
    low = lane < 64
    qi = _iota((CHUNK, 2 * CHUNK), 0)
    si = _iota((CHUNK, 2 * CHUNK), 1)
    valid = (si >= qi) & (si <= qi + CHUNK) & ((si >= CHUNK) | (n > 0))
    causal = _iota((CHUNK, CHUNK), 0) >= _iota((CHUNK, CHUNK), 1)
    ones_col = jnp.where(_iota((CHUNK, LANES), 1) == 0, 1.0, 0.0)
    row64 = _iota((CHUNK, 1), 0) < 64
    new_kv = _lockstep([
        _even_prompt_chunk(p_ref.at[b], y_ref.at[b], kprev.at[b], vprev.at[b], cst.at[b], mst.at[b],
                           (cos, sinlo, sinhi), qn_ref, kn_ref, sink_ref, segm, triu_ref, gb_ref, onorm_ref,
                           low, valid, causal, ones_col, row64)
        for b in range(batch)])

    @pl.when(n == pl.num_programs(0) - 1)
    def _():
        for b in range(batch):
            kc_ref[b], vc_ref[b] = new_kv[b]
        caug_ref[...] = cst[...]
        m_ref[...] = mst[...]


def _lockstep(chains):
    results = [None] * len(chains)
    live = list(range(len(chains)))
    while live:
        for i in list(live):
            try:
                next(chains[i])
            except StopIteration as stop:
                results[i] = stop.value
                live.remove(i)
    return results


def _even_prompt_chunk(p_ref, y_ref, kprev, vprev, cst, mst, tabs, qn_ref, kn_ref, sink_ref, segm, triu_ref, gb_ref,
                       onorm_ref, low, valid, causal, ones_col, row64):
    cos, sinlo, sinhi = tabs
    k = _rope16(_seg_rms(p_ref[:, E_SK:E_SK + LANES], kn_ref[...], segm), cos, sinlo, sinhi)
    v = p_ref[:, E_SV:E_SV + LANES]
    kk = jnp.concatenate([kprev[...], k], axis=0)
    vv = jnp.concatenate([vprev[...], v], axis=0)
    kk_sw = pltpu.roll(kk, 64, 1)
    vv_sw = pltpu.roll(vv, 64, 1)
    kvar = {(0, 0): jnp.where(low, kk, 0.0), (0, 1): jnp.where(low, 0.0, kk_sw),
            (1, 0): jnp.where(low, kk_sw, 0.0), (1, 1): jnp.where(low, 0.0, kk)}
    vvar = {(0, 0): vv, (0, 1): vv_sw, (1, 0): vv_sw, (1, 1): vv}
    yield
    for j in range(SW_HEADS // 2):
        sl = slice(E_SQ + j * LANES, E_SQ + (j + 1) * LANES)
        qb = _rope16(_seg_rms(p_ref[:, sl], qn_ref[:, sl], segm), cos, sinlo, sinhi)
        yield
        halves = []
        for pos in range(2):
            h = 2 * j + pos
            kv = h // (SW_HEADS // SW_KV_HEADS)
            s = jnp.where(valid, _mm_nt(qb, kvar[(kv, pos)]) * (SW_HEAD_DIM ** -0.5), NEG)
            yield
            sink = sink_ref[h]
            m = jnp.maximum(jnp.max(s, axis=-1, keepdims=True), sink)
            pr = jnp.exp(s - m)
            yield
            pr = pr / (jnp.sum(pr, axis=-1, keepdims=True) + jnp.exp(sink - m))
            halves.append(_mm(pr, vvar[(kv, pos)]))
            yield
        y_ref[:, j * LANES:(j + 1) * LANES] = jnp.where(low, halves[0], halves[1]).astype(y_ref.dtype)
    kprev[...] = k
    vprev[...] = v
    yield

    gt = p_ref[:, E_GATE:E_GATE + LANES].T
    gi = gt[0:SUBLANES] + gb_ref[0:SUBLANES]
    fl = _log_sigmoid(gt[GATE_F_LANE:GATE_F_LANE + SUBLANES] + gb_ref[SUBLANES:2 * SUBLANES])
    yield
    fcum = _mm_exact_rhs(fl, triu_ref[...])
    dd = gi - fcum
    mprev = mst[...]
    yield
    mt = fcum + jnp.maximum(mprev, _cummax_lanes(dd))
    fend = jnp.broadcast_to(fcum[:, CHUNK - 1:CHUNK], fcum.shape)
    mend = jnp.broadcast_to(mt[:, CHUNK - 1:CHUNK], mt.shape)
    decay = jnp.exp(fend + mprev - mend)
    rows = jnp.concatenate([fcum - mt, jnp.exp(fcum + mprev - mt), jnp.exp(-mt), jnp.exp(fend - fcum + gi - mend),
                            jnp.zeros((CHUNK - 4 * SUBLANES, CHUNK), F32)], axis=0)
    yield
    cols = rows.T
    yield
    for j in range(ML_HEADS // 2):
        qblk = p_ref[:, E_MQ + j * LANES:E_MQ + (j + 1) * LANES]
        kblk = p_ref[:, E_MK + j * LANES:E_MK + (j + 1) * LANES] * (ML_QK_DIM ** -0.5)
        c_old = cst[j]
        upd = None
        for pos in range(2):
            h = 2 * j + pos
            msk = low if pos == 0 else jnp.logical_not(low)
            qm = jnp.where(msk, qblk, 0.0)
            logw = cols[:, h:h + 1] + dd[h:h + 1, :]
            w = jnp.exp(jnp.where(causal, logw, -jnp.inf))
            sqk = _mm_nt(qm, kblk) * w
            yield
            vh = p_ref[:, E_MV + h * LANES:E_MV + (h + 1) * LANES]
            qc = _mm(qm, c_old)
            yield
            cs = cols[:, SUBLANES + h:SUBLANES + h + 1]
            num = cs * qc[:, :ML_V_DIM] + _mm(sqk, vh)
            den = cs * qc[:, ML_V_DIM:ML_V_DIM + 1] + jnp.sum(sqk, axis=-1, keepdims=True)
            hh = num / jnp.maximum(jnp.abs(den), cols[:, 2 * SUBLANES + h:2 * SUBLANES + h + 1])
            yield
            hsl = slice(h * ML_V_DIM, (h + 1) * ML_V_DIM)
            hn = _rms(hh, onorm_ref[:, hsl])
            mo = p_ref[:, E_MO + h * ML_V_DIM:E_MO + (h + 1) * ML_V_DIM]
            y_ref[:, SW_HEADS * SW_HEAD_DIM + h * ML_V_DIM:SW_HEADS * SW_HEAD_DIM + (h + 1) * ML_V_DIM] = (
                hn * _sigmoid(mo)).astype(y_ref.dtype)
            kw = jnp.where(msk, kblk, 0.0) * cols[:, 3 * SUBLANES + h:3 * SUBLANES + h + 1]
            u = _mm_tn(kw, jnp.concatenate([vh, ones_col], axis=1))
            upd = u if upd is None else upd + u
            yield
        dec = jnp.where(row64, decay[2 * j:2 * j + 1, 0:1], decay[2 * j + 1:2 * j + 2, 0:1])
        cst[j] = dec * c_old + upd
    mst[...] = mend
    return k, v


def even_prompt(proj, tabs, qn, kn, sinks, segm, triu, gb, onorm):
    batch, seq, _ = proj.shape
    tab = pl.BlockSpec((CHUNK, LANES), lambda n: (n, 0))
    const = lambda shape: pl.BlockSpec(shape, lambda n: (0,) * len(shape))
    state_shapes = [(batch, CHUNK, LANES), (batch, CHUNK, LANES),
                    (batch, ML_HEADS // 2, 2 * ML_QK_DIM, 2 * ML_V_DIM), (batch, SUBLANES, LANES)]
    return pl.pallas_call(
        _even_prompt_kernel,
        grid=(seq // CHUNK,),
        in_specs=[
            pl.BlockSpec((batch, CHUNK, EVEN_COLS), lambda n: (0, n, 0)), tab, tab, tab,
            const((1, SW_HEADS * SW_HEAD_DIM)), const((1, LANES)),
            pl.BlockSpec(memory_space=pltpu.SMEM),
            const((LANES, LANES)), const((CHUNK, CHUNK)), const((2 * SUBLANES, LANES)),
            const((1, ML_HEADS * ML_V_DIM)),
        ],
        out_specs=[pl.BlockSpec((batch, CHUNK, D_MODEL), lambda n: (0, n, 0))] + [const(s) for s in state_shapes],
        out_shape=[jax.ShapeDtypeStruct((batch, seq, D_MODEL), BF16)]
        + [jax.ShapeDtypeStruct(s, F32) for s in state_shapes],
        scratch_shapes=[pltpu.VMEM(s, F32) for s in state_shapes],
        compiler_params=pltpu.CompilerParams(dimension_semantics=("arbitrary",), vmem_limit_bytes=VMEM_LIMIT_BYTES),
        name="even_prompt",
    )(proj, *tabs, qn, kn, sinks, segm, triu, gb, onorm)


def _odd_prompt_kernel(p_ref, cosr_ref, sinr_ref, convw_ref, convb_ref, dtb_ref, arow_ref, drow_ref, snorm_ref,
                       tril_ref, dmat_ref, qs_ref, ks_ref, cd_ref, rnorm_ref,
                       y_ref, conv_ref, s_ref, r_ref,
                       ext, sst, rst):
    n = pl.program_id(1)
    batch = p_ref.shape[0]

    @pl.when(n == 0)
    def _():
        ext[:, 0:SUBLANES] = jnp.zeros((batch, SUBLANES, SSD_CONV_DIM), F32)
        sst[...] = jnp.zeros_like(sst)
        rst[...] = jnp.zeros_like(rst)

    lane = _iota((1, LANES), 1)
    low = lane < 64
    row64 = _iota((CHUNK, 1), 0) < 64
    causal = _iota((CHUNK, CHUNK), 0) >= _iota((CHUNK, CHUNK), 1)
    tails = _lockstep([
        _odd_prompt_chunk(p_ref.at[b], y_ref.at[b], ext.at[b], sst.at[b], rst.at[b], cosr_ref, sinr_ref, convw_ref,
                          convb_ref, dtb_ref, arow_ref, drow_ref, snorm_ref, tril_ref, dmat_ref, qs_ref, ks_ref,
                          cd_ref, rnorm_ref, low, row64, causal)
        for b in range(batch)])

    @pl.when(n == pl.num_programs(1) - 1)
    def _():
        for b in range(batch):
            conv_ref[b] = tails[b]
        s_ref[...] = sst[...]
        r_ref[...] = rst[...]


def _odd_prompt_chunk(p_ref, y_ref, ext, sst, rst, cosr_ref, sinr_ref, convw_ref, convb_ref, dtb_ref, arow_ref,
                      drow_ref, snorm_ref, tril_ref, dmat_ref, qs_ref, ks_ref, cd_ref, rnorm_ref, low, row64, causal):
    tail = SUBLANES
    ext[tail:tail + CHUNK] = p_ref[:, O_XBC:O_XBC + SSD_CONV_DIM]
    yield
    first = tail - (SSD_CONV - 1)
    acc = ext[first:first + CHUNK] * convw_ref[0:1]
    for jj in range(1, SSD_CONV):
        acc = acc + ext[first + jj:first + jj + CHUNK] * convw_ref[jj:jj + 1]
    xact = _silu(acc + convb_ref[...])
    new_tail = ext[CHUNK:CHUNK + tail]
    ext[0:tail] = new_tail
    yield

    dt = _softplus(p_ref[:, O_DT:O_DT + LANES] + dtb_ref[...])
    cum = _mm_exact_lhs(tril_ref[...], dt * arow_ref[...])
    yield
    cum_t = cum.T
    dt_t = dt.T
    ecum = jnp.exp(cum)
    cend = cum[CHUNK - 1:CHUNK, :]
    wend = jnp.exp(cend - cum) * dt
    eend = jnp.exp(cend)
    yield
    pairs_per_group = SSD_HEADS // SSD_GROUPS // 2
    ys = []
    for g in range(SSD_GROUPS):
        bc = xact[:, SSD_INNER + g * SSD_STATE:SSD_INNER + (g + 1) * SSD_STATE]
        cc = xact[:, SSD_INNER + (SSD_GROUPS + g) * SSD_STATE:SSD_INNER + (SSD_GROUPS + g + 1) * SSD_STATE]
        cb = _mm_nt(cc, bc)
        yield
        for jg in range(pairs_per_group):
            j = g * pairs_per_group + jg
            ha, hb = 2 * j, 2 * j + 1
            xp = xact[:, j * LANES:(j + 1) * LANES]
            s_old = sst[j]
            y = jnp.where(low, ecum[:, ha:ha + 1], ecum[:, hb:hb + 1]) * _mm_nt(cc, s_old)
            yield
            for pos, h in ((0, ha), (1, hb)):
                seg = cum[:, h:h + 1] - cum_t[h:h + 1, :]
                wmat = cb * jnp.exp(jnp.where(causal, seg, -jnp.inf)) * dt_t[h:h + 1, :]
                y = y + _mm(wmat, jnp.where(low if pos == 0 else jnp.logical_not(low), xp, 0.0))
                yield
            xw = xp * jnp.where(low, wend[:, ha:ha + 1], wend[:, hb:hb + 1])
            sst[j] = jnp.where(row64, eend[:, ha:ha + 1], eend[:, hb:hb + 1]) * s_old + _mm_tn(xw, bc)
            ys.append(y)
            yield
        gs = slice(g * SSD_INNER // SSD_GROUPS, (g + 1) * SSD_INNER // SSD_GROUPS)
        yg = jnp.concatenate(ys[g * pairs_per_group:(g + 1) * pairs_per_group], axis=1)
        yg = (yg + drow_ref[:, gs] * xact[:, gs]) * _silu(p_ref[:, O_Z + gs.start:O_Z + gs.stop])
        y_ref[:, gs] = _rms(yg, snorm_ref[:, gs]).astype(y_ref.dtype)
        yield

    cosr, sinr = cosr_ref[...], sinr_ref[...]
    for h in range(RET_HEADS):
        hs = h * LANES
        q = _rope128(p_ref[:, O_RQ + hs:O_RQ + hs + LANES], cosr, sinr)
        k = _rope128(p_ref[:, O_RK + hs:O_RK + hs + LANES], cosr, sinr) * (RET_QK_DIM ** -0.5)
        v = p_ref[:, O_RV + hs:O_RV + hs + LANES]
        yield
        r_old = rst[h]
        o = _mm(_mm_nt(q, k) * dmat_ref[h], v) + qs_ref[h] * _mm(q, r_old)
        yield
        rst[h] = cd_ref[h] * r_old + _mm_tn(k * ks_ref[h], v)
        xc = o - jnp.mean(o, axis=-1, keepdims=True)
        yn = xc * lax.rsqrt(jnp.mean(xc * xc, axis=-1, keepdims=True) + EPS) * rnorm_ref[:, hs:hs + LANES]
        y_ref[:, SSD_INNER + hs:SSD_INNER + hs + LANES] = (
            yn * _silu(p_ref[:, O_RG + hs:O_RG + hs + LANES])).astype(y_ref.dtype)
        yield
    return new_tail


def odd_prompt(proj, tabs, convw, convb, dtb, arow, drow, snorm, tril, ret_consts, rnorm, *, group):
    batch, seq, _ = proj.shape
    tab = pl.BlockSpec((CHUNK, LANES), lambda g, n: (n, 0))
    const = lambda shape: pl.BlockSpec(shape, lambda g, n: (0,) * len(shape))
    per_g = lambda shape: pl.BlockSpec((group,) + shape, lambda g, n: (g,) + (0,) * len(shape))
    hc = (RET_HEADS, CHUNK, LANES)
    ywidth = SSD_INNER + RET_HEADS * RET_V_DIM
    states = [(SUBLANES, SSD_CONV_DIM), (SSD_HEADS // 2, LANES, SSD_STATE), hc]
    return pl.pallas_call(
        _odd_prompt_kernel,
        grid=(batch // group, seq // CHUNK),
        in_specs=[
            pl.BlockSpec((group, CHUNK, ODD_COLS), lambda g, n: (g, n, 0)), tab, tab,
            const((SSD_CONV, SSD_CONV_DIM)), const((1, SSD_CONV_DIM)), const((1, LANES)), const((1, LANES)),
            const((1, SSD_INNER)), const((1, SSD_INNER)), const((CHUNK, CHUNK)),
            const(hc), const(hc), const(hc), const(hc), const((1, RET_HEADS * RET_V_DIM)),
        ],
        out_specs=[pl.BlockSpec((group, CHUNK, ywidth), lambda g, n: (g, n, 0))] + [per_g(s) for s in states],
        out_shape=[jax.ShapeDtypeStruct((batch, seq, ywidth), BF16)]
        + [jax.ShapeDtypeStruct((batch,) + s, F32) for s in states],
        scratch_shapes=[pltpu.VMEM((group, SUBLANES + CHUNK, SSD_CONV_DIM), F32),
                        pltpu.VMEM((group,) + states[1], F32), pltpu.VMEM((group,) + states[2], F32)],
        compiler_params=pltpu.CompilerParams(dimension_semantics=("arbitrary", "arbitrary"),
                                             vmem_limit_bytes=VMEM_LIMIT_BYTES),
        name="odd_prompt",
    )(proj, *tabs, convw, convb, dtb, arow, drow, snorm, tril, *ret_consts, rnorm)


def _lane_to_rows(g, offset):
    sel = _iota(g.shape, 1) == _iota(g.shape, 0) + offset
    return jnp.sum(jnp.where(sel, g, 0.0), axis=-1, keepdims=True)


def _block_rows(x, i, nblk, blk):
    row = _iota((SUBLANES, blk), 0)
    out = jnp.zeros((SUBLANES, blk), F32)
    for b in range(nblk):
        out = jnp.where(row == b, x[i:i + 1, b * blk:(b + 1) * blk], out)
    return out


def _even_sample_kernel(p_ref, bk_ref, bv_ref, c_ref, nrow_ref, mrow_ref, cos_ref, sinlo_ref, sinhi_ref, qn_ref,
                        kn_ref, sink_ref, segm_ref, gb_ref, onorm_ref,
                        y_ref, nk_ref, nv_ref, nc_ref, nn_ref, nm_ref):
    R = SUBLANES
    cos, sinlo, sinhi = cos_ref[...], sinlo_ref[...], sinhi_ref[...]
    segm = segm_ref[...]
    row = _iota((R, LANES), 0)
    lane = _iota((R, LANES), 1)
    low = lane < 64
    group = SW_HEADS // SW_KV_HEADS
    scale = SW_HEAD_DIM ** -0.5
    sink = sink_ref[:, 0:1]
    last = _iota((CHUNK, LANES), 0) == CHUNK - 1

    k = _rope16(_seg_rms(p_ref[:, E_SK:E_SK + LANES], kn_ref[...], segm), cos, sinlo, sinhi)
    v = p_ref[:, E_SV:E_SV + LANES]
    qb, qb_sw = [], []
    for j in range(SW_HEADS // 2):
        sl = slice(E_SQ + j * LANES, E_SQ + (j + 1) * LANES)
        qb.append(_rope16(_seg_rms(p_ref[:, sl], qn_ref[:, sl], segm), cos, sinlo, sinhi))
        qb_sw.append(pltpu.roll(qb[j], 64, 1))

    g = p_ref[:, E_GATE:E_GATE + LANES]
    ic = g + gb_ref[0:1]
    fl = _log_sigmoid(pltpu.roll(g, LANES - GATE_F_LANE, 1) + gb_ref[1:2])
    mprev = mrow_ref[...]
    mt = jnp.maximum(fl + mprev, ic)
    w_all = jnp.exp(ic - mt)
    cs_all = jnp.exp(fl + mprev - mt)
    em_all = jnp.exp(-mt)
    nm_ref[...] = mt
    width = ML_HEADS * ML_QK_DIM
    own = _iota((R, width), 1) // ML_QK_DIM == _iota((R, width), 0)
    kscaled = p_ref[:, E_MK:E_MK + width] * (ML_QK_DIM ** -0.5)

    for i in range(R):
        qm = jnp.zeros((R, LANES), F32)
        for j in range(SW_HEADS // 2):
            for pos in range(2):
                h = 2 * j + pos
                kv = h // group
                src = (qb[j] if pos == kv else qb_sw[j])[i:i + 1]
                qm = jnp.where((row == h) & (low if kv == 0 else jnp.logical_not(low)), src, qm)
        bk, bv = bk_ref[i], bv_ref[i]
        ki, vi = k[i:i + 1], v[i:i + 1]
        s = _dg(qm, bk, 1, 1) * scale
        s_new = jnp.sum(qm * ki, axis=-1, keepdims=True) * scale
        m = jnp.maximum(jnp.maximum(jnp.max(s, axis=-1, keepdims=True), s_new), sink)
        pr = jnp.exp(s - m)
        p_new = jnp.exp(s_new - m)
        den = jnp.sum(pr, axis=-1, keepdims=True) + p_new + jnp.exp(sink - m)
        o = (_dg(pr, bv, 1, 0) + p_new * vi) / den
        o_sw = pltpu.roll(o, 64, 1)
        for j in range(SW_HEADS // 2):
            halves = []
            for pos in range(2):
                h = 2 * j + pos
                halves.append((o if pos == h // group else o_sw)[h:h + 1, :])
            y_ref[i:i + 1, j * LANES:(j + 1) * LANES] = jnp.where(low[0:1], halves[0], halves[1])
        nk_ref[i] = jnp.where(last, ki, pltpu.roll(bk, CHUNK - 1, 0))
        nv_ref[i] = jnp.where(last, vi, pltpu.roll(bv, CHUNK - 1, 0))

        w = _lane_to_rows(jnp.broadcast_to(w_all[i:i + 1], (R, LANES)), 0)
        cs = _lane_to_rows(jnp.broadcast_to(cs_all[i:i + 1], (R, LANES)), 0)
        em = _lane_to_rows(jnp.broadcast_to(em_all[i:i + 1], (R, LANES)), 0)
        qrows = jnp.where(own, p_ref[i:i + 1, E_MQ:E_MQ + width], 0.0)
        krows = jnp.where(own, kscaled[i:i + 1], 0.0)
        c_old = c_ref[i]
        qc = _dg(qrows, c_old, 1, 0)
        qn_dot = jnp.sum(qrows * nrow_ref[i:i + 1], axis=-1, keepdims=True)
        sqk = jnp.sum(qrows * krows, axis=-1, keepdims=True) * w
        v4 = _block_rows(p_ref[:, E_MV:E_MV + ML_HEADS * ML_V_DIM], i, ML_HEADS, ML_V_DIM)
        mo4 = _block_rows(p_ref[:, E_MO:E_MO + ML_HEADS * ML_V_DIM], i, ML_HEADS, ML_V_DIM)
        num = cs * qc + sqk * v4
        dn = cs * qn_dot + sqk
        hh = num / jnp.maximum(jnp.abs(dn), em)
        hn = _rms(hh, onorm_ref[...]) * _sigmoid(mo4)
        for h in range(ML_HEADS):
            c0 = SW_HEADS * SW_HEAD_DIM + h * ML_V_DIM
            y_ref[i:i + 1, c0:c0 + ML_V_DIM] = hn[h:h + 1]
        dec_col = jnp.concatenate(
            [jnp.broadcast_to(cs[h:h + 1, 0:1], (ML_QK_DIM, ML_V_DIM)) for h in range(ML_HEADS)], axis=0)
        nc_ref[i] = dec_col * c_old + _dg(krows * w, v4, 0, 0)
        dec_lanes = jnp.sum(jnp.where(own, cs, 0.0), axis=0, keepdims=True)
        nn_ref[i:i + 1] = dec_lanes * nrow_ref[i:i + 1] + jnp.sum(krows * w, axis=0, keepdims=True)


def even_sample(proj, bk, bv, c, nrow, mrow, tabs, qn, kn, sink_rows, segm, gb, onorm_rows):
    b = proj.shape[0]
    R = SUBLANES
    width = ML_HEADS * ML_QK_DIM
    rows = lambda w: pl.BlockSpec((R, w), lambda i: (i, 0))
    per_b = lambda shape: pl.BlockSpec((R,) + shape, lambda i: (i,) + (0,) * len(shape))
    const = lambda shape: pl.BlockSpec(shape, lambda i: (0,) * len(shape))
    tab = const((R, LANES))
    return pl.pallas_call(
        _even_sample_kernel,
        grid=(b // R,),
        in_specs=[
            rows(EVEN_COLS), per_b((CHUNK, LANES)), per_b((CHUNK, LANES)), per_b((width, ML_V_DIM)),
            rows(width), rows(LANES), tab, tab, tab,
            const((1, SW_HEADS * SW_HEAD_DIM)), const((1, LANES)), const((R, LANES)),
            const((LANES, LANES)), const((2, LANES)), const((R, ML_V_DIM)),
        ],
        out_specs=[
            rows(D_MODEL), per_b((CHUNK, LANES)), per_b((CHUNK, LANES)), per_b((width, ML_V_DIM)),
            rows(width), rows(LANES),
        ],
        out_shape=[
            jax.ShapeDtypeStruct((b, D_MODEL), F32),
            jax.ShapeDtypeStruct((b, CHUNK, LANES), F32), jax.ShapeDtypeStruct((b, CHUNK, LANES), F32),
            jax.ShapeDtypeStruct((b, width, ML_V_DIM), F32), jax.ShapeDtypeStruct((b, width), F32),
            jax.ShapeDtypeStruct((b, LANES), F32),
        ],
        compiler_params=pltpu.CompilerParams(dimension_semantics=("arbitrary",), vmem_limit_bytes=VMEM_LIMIT_BYTES),
        name="even_sample",
    )(proj, bk, bv, c, nrow, mrow, *tabs, qn, kn, sink_rows, segm, gb, onorm_rows)


def _odd_sample_kernel(p_ref, cb_ref, s_ref, r_ref, cosr_ref, sinr_ref, convw_ref, convb_ref, dtb_ref, arow_ref,
                       drow_ref, snorm_ref, spread_ref, rdec_ref, rnorm_ref,
                       y_ref, ncb_ref, ns_ref, nr_ref):
    R = SUBLANES
    xbc = p_ref[:, O_XBC:O_XBC + SSD_CONV_DIM]
    acc = cb_ref[0] * convw_ref[0:1]
    for jj in range(1, SSD_CONV - 1):
        acc = acc + cb_ref[jj] * convw_ref[jj:jj + 1]
    acc = acc + xbc * convw_ref[SSD_CONV - 1:SSD_CONV]
    xact = _silu(acc + convb_ref[...])
    for jj in range(SSD_CONV - 2):
        ncb_ref[jj] = cb_ref[jj + 1]
    ncb_ref[SSD_CONV - 2] = xbc

    dt = _softplus(p_ref[:, O_DT:O_DT + LANES] + dtb_ref[...])
    dec = jnp.exp(dt * arow_ref[...])
    xs = xact[:, 0:SSD_INNER]
    xdt = xs * _mm_exact_rhs(dt, spread_ref[...])
    gown = _iota((R, SSD_INNER), 1) // (SSD_INNER // SSD_GROUPS) == _iota((R, SSD_INNER), 0)
    bpart = xact[:, SSD_INNER:SSD_INNER + SSD_GROUPS * SSD_STATE]
    cpart = xact[:, SSD_INNER + SSD_GROUPS * SSD_STATE:SSD_CONV_DIM]

    cosr, sinr = cosr_ref[...], sinr_ref[...]
    width = RET_HEADS * RET_QK_DIM
    q4 = jnp.concatenate([_rope128(p_ref[:, O_RQ + h * LANES:O_RQ + (h + 1) * LANES], cosr, sinr)
                          for h in range(RET_HEADS)], axis=1)
    k4 = jnp.concatenate([_rope128(p_ref[:, O_RK + h * LANES:O_RK + (h + 1) * LANES], cosr, sinr)
                          for h in range(RET_HEADS)], axis=1) * (RET_QK_DIM ** -0.5)
    own = _iota((R, width), 1) // RET_QK_DIM == _iota((R, width), 0)
    gam = rdec_ref[:, 0:1]
    gam_col = jnp.concatenate(
        [jnp.broadcast_to(rdec_ref[h:h + 1, :], (RET_QK_DIM, RET_V_DIM)) for h in range(RET_HEADS)], axis=0)

    ys = []
    for i in range(R):
        brows = _block_rows(bpart, i, SSD_GROUPS, SSD_STATE)
        crows = _block_rows(cpart, i, SSD_GROUPS, SSD_STATE)
        xw = jnp.where(gown, xdt[i:i + 1], 0.0)
        dec_col = jnp.concatenate(
            [jnp.broadcast_to(dec[i:i + 1, h:h + 1], (SSD_HEAD_DIM, SSD_STATE)) for h in range(SSD_HEADS)], axis=0)
        s_new = dec_col * s_ref[i] + _dg(xw, brows, 0, 0)
        ns_ref[i] = s_new
        yrows = _dg(crows, s_new, 1, 1)
        ys.append(jnp.sum(jnp.where(gown, yrows, 0.0), axis=0, keepdims=True))

        qrows = jnp.where(own, q4[i:i + 1], 0.0)
        krows = jnp.where(own, k4[i:i + 1], 0.0)
        v4 = _block_rows(p_ref[:, O_RV:O_RV + RET_HEADS * RET_V_DIM], i, RET_HEADS, RET_V_DIM)
        g4 = _block_rows(p_ref[:, O_RG:O_RG + RET_HEADS * RET_V_DIM], i, RET_HEADS, RET_V_DIM)
        r_old = r_ref[i]
        att = jnp.sum(qrows * krows, axis=-1, keepdims=True)
        o = att * v4 + gam * _dg(qrows, r_old, 1, 0)
        nr_ref[i] = gam_col * r_old + _dg(krows, v4, 0, 0)
        xc = o - jnp.mean(o, axis=-1, keepdims=True)
        yn = xc * lax.rsqrt(jnp.mean(xc * xc, axis=-1, keepdims=True) + EPS) * rnorm_ref[...] * _silu(g4)
        for h in range(RET_HEADS):
            y_ref[i:i + 1, SSD_INNER + h * RET_V_DIM:SSD_INNER + (h + 1) * RET_V_DIM] = yn[h:h + 1]

    y = (jnp.concatenate(ys, axis=0) + drow_ref[...] * xs) * _silu(p_ref[:, O_Z:O_Z + SSD_INNER])
    gw = SSD_INNER // SSD_GROUPS
    for g in range(SSD_GROUPS):
        y_ref[:, g * gw:(g + 1) * gw] = _rms(y[:, g * gw:(g + 1) * gw], snorm_ref[:, g * gw:(g + 1) * gw])


def odd_sample(proj, cbuf, s, r, tabs, convw, convb, dtb, arow, drow, snorm, spread, rdec_rows, rnorm_rows):
    b = proj.shape[0]
    R = SUBLANES
    rows = lambda w: pl.BlockSpec((R, w), lambda i: (i, 0))
    per_b = lambda shape: pl.BlockSpec((R,) + shape, lambda i: (i,) + (0,) * len(shape))
    const = lambda shape: pl.BlockSpec(shape, lambda i: (0,) * len(shape))
    tab = const((R, LANES))
    conv = pl.BlockSpec((SSD_CONV - 1, R, SSD_CONV_DIM), lambda i: (0, i, 0))
    ywidth = SSD_INNER + RET_HEADS * RET_V_DIM
    sshape = (SSD_HEADS * SSD_HEAD_DIM, SSD_STATE)
    rshape = (RET_HEADS * RET_QK_DIM, RET_V_DIM)
    return pl.pallas_call(
        _odd_sample_kernel,
        grid=(b // R,),
        in_specs=[
            rows(ODD_COLS), conv, per_b(sshape), per_b(rshape), tab, tab,
            const((SSD_CONV, SSD_CONV_DIM)), const((1, SSD_CONV_DIM)), const((1, LANES)), const((1, LANES)),
            const((1, SSD_INNER)), const((1, SSD_INNER)), const((LANES, SSD_INNER)),
            const((R, LANES)), const((R, RET_V_DIM)),
        ],
        out_specs=[rows(ywidth), conv, per_b(sshape), per_b(rshape)],
        out_shape=[
            jax.ShapeDtypeStruct((b, ywidth), F32),
            jax.ShapeDtypeStruct((SSD_CONV - 1, b, SSD_CONV_DIM), F32),
            jax.ShapeDtypeStruct((b,) + sshape, F32), jax.ShapeDtypeStruct((b,) + rshape, F32),
        ],
        compiler_params=pltpu.CompilerParams(dimension_semantics=("arbitrary",), vmem_limit_bytes=VMEM_LIMIT_BYTES),
        name="odd_sample",
    )(proj, cbuf, s, r, *tabs, convw, convb, dtb, arow, drow, snorm, spread, rdec_rows, rnorm_rows)


def _pad_cols(w, n):
    return jnp.pad(w, ((0, 0), (0, n - w.shape[1])))


def _even_w_in(w):
    sq, sk, sv, mq, mk, mv, mo, mi, mf = jnp.split(w, [512, 640, 768, 1024, 1280, 1792, 2304, 2308], axis=1)
    gates = jnp.concatenate([_pad_cols(mi, GATE_F_LANE), _pad_cols(mf, LANES - GATE_F_LANE)], axis=1)
    return jnp.concatenate([sq, sk, sv, mq, mk, mv, mo, gates], axis=1).astype(BF16)


def _odd_w_in(w):
    z, xbc, dt, rq, rk, rv, rg = jnp.split(w, [1024, 2560, 2576, 3088, 3600, 4112], axis=1)
    return jnp.concatenate([z, xbc, rq, rk, rv, rg, _pad_cols(dt, LANES)], axis=1).astype(BF16)


def _rope16_tables(pos):
    half = SW_ROT_DIM // 2
    inv = jnp.power(jnp.float32(ROPE_THETA), -jnp.arange(half, dtype=F32) * (2.0 / SW_ROT_DIM))
    ang = pos.astype(F32)[:, None] * inv[None, :]
    cos, sin = jnp.cos(ang), jnp.sin(ang)
    n = pos.shape[0]
    rest = SW_HEAD_DIM - SW_ROT_DIM
    one, zero, zh = jnp.ones((n, rest), F32), jnp.zeros((n, rest), F32), jnp.zeros((n, half), F32)
    tile2 = lambda t: jnp.concatenate([t, t], axis=1)
    return (tile2(jnp.concatenate([cos, cos, one], axis=1)),
            tile2(jnp.concatenate([-sin, zh, zero], axis=1)),
            tile2(jnp.concatenate([zh, sin, zero], axis=1)))


def _rope128_tables(pos):
    half = RET_QK_DIM // 2
    inv = jnp.power(jnp.float32(RET_ROPE_THETA), -jnp.arange(half, dtype=F32) * (2.0 / RET_QK_DIM))
    ang = pos.astype(F32)[:, None] * inv[None, :]
    cos, sin = jnp.cos(ang), jnp.sin(ang)
    return jnp.concatenate([cos, cos], axis=1), jnp.concatenate([-sin, sin], axis=1)


def _ret_consts():
    L = CHUNK
    lg = jnp.log(1.0 - jnp.exp2(-5.0 - jnp.arange(RET_HEADS, dtype=F32)))
    idx = jnp.arange(L, dtype=F32)
    diff = idx[:, None] - idx[None, :]
    dmat = jnp.exp(jnp.where(diff >= 0, diff[None] * lg[:, None, None], -jnp.inf))
    q_scale = jnp.exp((idx[None] + 1.0) * lg[:, None])
    k_scale = jnp.exp((L - 1.0 - idx[None]) * lg[:, None])
    chunk_decay = jnp.exp(L * lg)
    bc = lambda t: jnp.broadcast_to(t[:, :, None], (RET_HEADS, L, LANES))
    cd = jnp.broadcast_to(chunk_decay[:, None, None], (RET_HEADS, L, LANES))
    return dmat, bc(q_scale), bc(k_scale), cd, lg


def _rows8(t):
    return jnp.pad(t, ((0, SUBLANES - t.shape[0]), (0, 0)))


def _gate_bias_rows(gb):
    ib = jnp.broadcast_to(gb[:ML_HEADS, None], (ML_HEADS, LANES))
    fb = jnp.broadcast_to(gb[ML_HEADS:, None], (ML_HEADS, LANES))
    return jnp.concatenate([_rows8(ib), _rows8(fb)], axis=0)


def kernel(x_prompt, x_sample, cache_mem_k, cache_mem_v, cache_swa_k, cache_swa_v, state_mlstm_C, state_mlstm_n,
           state_mlstm_m, state_ssd_conv, state_ssd, state_ret, mem_prompt, norm_mix, norm_xattn, norm_mem, norm_ffn,
           even_w_in, mlstm_gate_bias, swa_q_norm, swa_k_norm, swa_sinks, mlstm_out_norm, even_w_out, odd_w_in,
           ssd_conv_w, ssd_conv_b, ssd_dt_bias, ssd_a_log, ssd_d, ssd_norm, ret_norm, odd_w_out, mem_wq, mem_wk,
           mem_wv, mem_q_norm, mem_k_norm, mem_wo, ffn_w1, ffn_w2):
    bp, seq, d = x_prompt.shape
    bs = x_sample.shape[0]
    depth = norm_mix.shape[0]
    tm = 512

    pos_p = jnp.arange(seq, dtype=jnp.int32)
    pos_s = jnp.full((SUBLANES,), PAST_LEN, dtype=jnp.int32)
    tab16_p, tab16_s = _rope16_tables(pos_p), _rope16_tables(pos_s)
    tab128_p, tab128_s = _rope128_tables(pos_p), _rope128_tables(pos_s)
    dmat, q_scale, k_scale, chunk_decay, lg = _ret_consts()
    rdec_rows = _rows8(jnp.broadcast_to(jnp.exp(lg)[:, None], (RET_HEADS, LANES)))
    ii = jnp.arange(CHUNK)
    triu = (ii[:, None] <= ii[None, :]).astype(F32)
    tril = (ii[:, None] >= ii[None, :]).astype(F32)
    jj = jnp.arange(LANES)
    segm = jnp.where(jj[:, None] // SW_HEAD_DIM == jj[None, :] // SW_HEAD_DIM, 1.0 / SW_HEAD_DIM, 0.0).astype(F32)
    spread = (jj[:, None] == jnp.arange(SSD_INNER)[None, :] // SSD_HEAD_DIM).astype(F32)
    row1 = lambda t: t.reshape(1, -1).astype(F32)
    pad_lanes = lambda t: jnp.pad(t.reshape(1, -1).astype(F32), ((0, 0), (0, LANES - t.shape[-1])))

    yp = x_prompt.reshape(bp * seq, d)
    ys = x_sample.reshape(bs, d)
    mem = mem_prompt.reshape(bp * MEM_LEN, d)
    cmk = cache_mem_k.reshape(depth, bs, MEM_LEN * MEM_HEADS, MEM_HEAD_DIM)
    cmv = cache_mem_v.reshape(depth, bs, MEM_LEN * MEM_HEADS, MEM_HEAD_DIM)
    p_mk, p_mv = [], []
    outs = {}
    for l in range(depth):
        g_mix = row1(norm_mix[l])
        if l % 2 == 0:
            e = l // 2
            w_in = _even_w_in(even_w_in[e])
            w_out = even_w_out[e].astype(BF16)
            qn = row1(jnp.tile(swa_q_norm[e], SW_HEADS))
            kn = row1(jnp.tile(swa_k_norm[e], SW_KV_HEADS))
            gb = _gate_bias_rows(mlstm_gate_bias[e].astype(F32))
            onorm = row1(mlstm_out_norm[e])
            sinks = swa_sinks[e].astype(F32)
            proj_p = norm_proj(yp, g_mix, w_in, tm=tm)
            mix_p, kc, vc, caug, mm = even_prompt(proj_p.reshape(bp, seq, EVEN_COLS), tab16_p, qn, kn, sinks, segm,
                                                  triu, gb, onorm)
            mix_p = mix_p.reshape(bp * seq, -1)
            outs["p_swk"] = kc.reshape(1, bp, CHUNK, SW_KV_HEADS, SW_HEAD_DIM)
            outs["p_swv"] = vc.reshape(1, bp, CHUNK, SW_KV_HEADS, SW_HEAD_DIM)
            outs["p_c"] = caug[..., :ML_V_DIM].reshape(1, bp, ML_HEADS, ML_QK_DIM, ML_V_DIM)
            outs["p_n"] = caug[..., ML_V_DIM].reshape(1, bp, ML_HEADS, ML_QK_DIM)
            outs["p_m"] = mm[:, :ML_HEADS, 0].reshape(1, bp, ML_HEADS)

            proj_s = norm_proj(ys, g_mix, w_in, tm=bs)
            sink_rows = jnp.broadcast_to(sinks[:, None], (SW_HEADS, LANES))
            onorm_rows = _rows8(mlstm_out_norm[e].astype(F32).reshape(ML_HEADS, ML_V_DIM))
            gb_lanes = jnp.concatenate([pad_lanes(mlstm_gate_bias[e][:ML_HEADS]),
                                        pad_lanes(mlstm_gate_bias[e][ML_HEADS:])], axis=0)
            mix_s, nk, nv, ncst, nn, nm = even_sample(
                proj_s,
                cache_swa_k[e].reshape(bs, CHUNK, LANES), cache_swa_v[e].reshape(bs, CHUNK, LANES),
                state_mlstm_C[e].reshape(bs, ML_HEADS * ML_QK_DIM, ML_V_DIM),
                state_mlstm_n[e].reshape(bs, ML_HEADS * ML_QK_DIM),
                jnp.pad(state_mlstm_m[e], ((0, 0), (0, LANES - ML_HEADS))),
                tab16_s, qn, kn, sink_rows, segm, gb_lanes, onorm_rows)
            outs["s_swk"] = nk.reshape(1, bs, CHUNK, SW_KV_HEADS, SW_HEAD_DIM)
            outs["s_swv"] = nv.reshape(1, bs, CHUNK, SW_KV_HEADS, SW_HEAD_DIM)
            outs["s_c"] = ncst.reshape(1, bs, ML_HEADS, ML_QK_DIM, ML_V_DIM)
            outs["s_n"] = nn.reshape(1, bs, ML_HEADS, ML_QK_DIM)
            outs["s_m"] = nm[:, :ML_HEADS].reshape(1, bs, ML_HEADS)
        else:
            o = l // 2
            w_in = _odd_w_in(odd_w_in[o])
            w_out = odd_w_out[o].astype(BF16)
            convw = ssd_conv_w[o].astype(F32)
            convb = row1(ssd_conv_b[o])
            dtb = pad_lanes(ssd_dt_bias[o])
            arow = pad_lanes(-jnp.exp(ssd_a_log[o].astype(F32)))
            drow = row1(jnp.repeat(ssd_d[o].astype(F32), SSD_HEAD_DIM))
            snorm = row1(ssd_norm[o])
            rnorm = row1(ret_norm[o])
            proj_p = norm_proj(yp, g_mix, w_in, tm=tm)
            mix_p, ctail, sst, rst = odd_prompt(proj_p.reshape(bp, seq, ODD_COLS), tab128_p, convw, convb, dtb, arow,
                                                drow, snorm, tril, (dmat, q_scale, k_scale, chunk_decay), rnorm,
                                                group=bp)
            mix_p = mix_p.reshape(bp * seq, -1)
            outs["p_conv"] = ctail[:, SUBLANES - (SSD_CONV - 1):, :].reshape(1, bp, SSD_CONV - 1, SSD_CONV_DIM)
            outs["p_ssd"] = sst.reshape(1, bp, SSD_HEADS, SSD_HEAD_DIM, SSD_STATE)
            outs["p_ret"] = rst.reshape(1, bp, RET_HEADS, RET_QK_DIM, RET_V_DIM)

            proj_s = norm_proj(ys, g_mix, w_in, tm=bs)
            rnorm_rows = _rows8(ret_norm[o].astype(F32).reshape(RET_HEADS, RET_V_DIM))
            mix_s, ncb, ns, nr = odd_sample(
                proj_s, jnp.swapaxes(state_ssd_conv[o], 0, 1),
                state_ssd[o].reshape(bs, SSD_HEADS * SSD_HEAD_DIM, SSD_STATE),
                state_ret[o].reshape(bs, RET_HEADS * RET_QK_DIM, RET_V_DIM),
                tab128_s, convw, convb, dtb, arow, drow, snorm, spread, rdec_rows, rnorm_rows)
            outs["s_conv"] = jnp.swapaxes(ncb, 0, 1).reshape(1, bs, SSD_CONV - 1, SSD_CONV_DIM)
            outs["s_ssd"] = ns.reshape(1, bs, SSD_HEADS, SSD_HEAD_DIM, SSD_STATE)
            outs["s_ret"] = nr.reshape(1, bs, RET_HEADS, RET_QK_DIM, RET_V_DIM)

        wkv = jnp.concatenate([mem_wk[l], mem_wv[l]], axis=1).astype(BF16)
        qnorm = row1(mem_q_norm[l])
        mkv = norm_proj(mem, row1(norm_mem[l]), wkv, tm=tm, head_norm=row1(mem_k_norm[l]), head_norm_cols=MEM_WIDTH)
        p_mk.append(mkv[:, :MEM_WIDTH].reshape(bp, MEM_LEN, MEM_HEADS, MEM_HEAD_DIM))
        p_mv.append(mkv[:, MEM_WIDTH:].reshape(bp, MEM_LEN, MEM_HEADS, MEM_HEAD_DIM))
        wq, wo = mem_wq[l].astype(BF16), mem_wo[l].astype(BF16)
        gx, gf = row1(norm_xattn[l]), row1(norm_ffn[l])
        w1, w2 = ffn_w1[l].astype(BF16), ffn_w2[l].astype(BF16)
        yp = post_prompt(yp, mix_p, w_out, gx, wq, qnorm, mkv, wo, gf, w1, w2, rows_per_batch=seq, tm=tm)

        ys = res_proj(ys, mix_s, w_out)
        qs = norm_proj(ys, gx, wq, tm=bs, head_norm=qnorm, head_norm_cols=MEM_WIDTH)
        ys = res_proj_ffn(ys, xattn_sample(qs, cmk, cmv, l), wo, gf, w1, w2)

    return (yp.reshape(bp, seq, d), ys.reshape(bs, 1, d),
            jnp.stack(p_mk), jnp.stack(p_mv), outs["p_swk"], outs["p_swv"], outs["p_c"], outs["p_n"], outs["p_m"],
            outs["p_conv"], outs["p_ssd"], outs["p_ret"],
            outs["s_swk"], outs["s_swv"], outs["s_c"], outs["s_n"], outs["s_m"],
            outs["s_conv"], outs["s_ssd"], outs["s_ret"])
```

```python
import functools
import math

import jax
import jax.numpy as jnp
from jax import lax
from jax.experimental import pallas as pl
from jax.experimental.pallas import tpu as pltpu

F32 = jnp.float32
BF16 = jnp.bfloat16

D_MODEL = 1024
PAST_LEN = 8192
EPS = 1e-6
CHUNK = 128
NEG = -1e30

SW_HEADS, SW_KV_HEADS, SW_HEAD_DIM, SW_ROT_DIM = 8, 2, 64, 16
ROPE_THETA = 500000.0
ML_HEADS, ML_QK_DIM, ML_V_DIM = 4, 64, 128
SSD_HEADS, SSD_HEAD_DIM, SSD_GROUPS, SSD_STATE, SSD_CONV = 16, 64, 2, 128, 4
SSD_INNER = SSD_HEADS * SSD_HEAD_DIM
SSD_CONV_DIM = SSD_INNER + 2 * SSD_GROUPS * SSD_STATE
RET_HEADS, RET_QK_DIM, RET_V_DIM = 4, 128, 128
RET_ROPE_THETA = 10000.0
MEM_LEN, MEM_HEADS, MEM_HEAD_DIM = 256, 4, 128
MEM_WIDTH = MEM_HEADS * MEM_HEAD_DIM
FFN_DIM = 4 * D_MODEL
FFN_CHUNK = 512
PROJ_COLS_PER_STAGE = 256

LANES = 128
SUBLANES = 8
VMEM_LIMIT_BYTES = 56 * 1024 * 1024

E_SQ, E_SK, E_SV, E_MQ, E_MK, E_MV, E_MO, E_GATE, EVEN_COLS = 0, 512, 640, 768, 1024, 1280, 1792, 2304, 2432
GATE_F_LANE = 8
O_Z, O_XBC, O_RQ, O_RK, O_RV, O_RG, O_DT, ODD_COLS = 0, 1024, 2560, 3072, 3584, 4096, 4608, 4736


def _mm(a, b):
    return jnp.dot(a.astype(BF16), b.astype(BF16), preferred_element_type=F32)


def _mm_nt(a, b):
    return lax.dot_general(a.astype(BF16), b.astype(BF16), (((1,), (1,)), ((), ())), preferred_element_type=F32)


def _mm_tn(a, b):
    return lax.dot_general(a.astype(BF16), b.astype(BF16), (((0,), (0,)), ((), ())), preferred_element_type=F32)


def _dg(a, b, ca, cb):
    return lax.dot_general(a, b, (((ca,), (cb,)), ((), ())), preferred_element_type=F32)


def _split3(x):
    hi = x.astype(BF16).astype(F32)
    r1 = x - hi
    mid = r1.astype(BF16).astype(F32)
    lo = (r1 - mid).astype(BF16).astype(F32)
    return hi, mid, lo


def _mm_exact_rhs(x, e):
    hi, mid, lo = _split3(x)
    return _dg(hi, e, 1, 0) + _dg(mid, e, 1, 0) + _dg(lo, e, 1, 0)


def _mm_exact_lhs(e, x):
    hi, mid, lo = _split3(x)
    return _dg(e, hi, 1, 0) + _dg(e, mid, 1, 0) + _dg(e, lo, 1, 0)


def _mm_tn_exact_lhs(x, e):
    hi, mid, lo = _split3(x)
    return _dg(hi, e, 0, 0) + _dg(mid, e, 0, 0) + _dg(lo, e, 0, 0)


def _rms(x, g):
    return x * lax.rsqrt(jnp.mean(x * x, axis=-1, keepdims=True) + EPS) * g


def _seg_rms(x, g, seg_mean):
    return x * lax.rsqrt(_mm_exact_rhs(x * x, seg_mean) + EPS) * g


def _sigmoid(x):
    return 1.0 / (1.0 + jnp.exp(-x))


def _silu(x):
    return x * _sigmoid(x)


def _softplus(x):
    return jnp.maximum(x, 0.0) + jnp.log1p(jnp.exp(-jnp.abs(x)))


def _log_sigmoid(x):
    return -_softplus(-x)


def _rope16(x, cos, sin_lo, sin_hi):
    return x * cos + pltpu.roll(x, LANES - 8, 1) * sin_lo + pltpu.roll(x, 8, 1) * sin_hi


def _rope128(x, cos, sin):
    return x * cos + pltpu.roll(x, 64, 1) * sin


def _iota(shape, dim):
    return lax.broadcasted_iota(jnp.int32, shape, dim)


def _cummax_lanes(x):
    lane = _iota(x.shape, 1)
    shift = 1
    while shift < x.shape[1]:
        x = jnp.maximum(x, jnp.where(lane >= shift, pltpu.roll(x, shift, 1), -jnp.inf))
        shift *= 2
    return x


def _norm_proj_kernel(x_ref, g_ref, w_ref, hn_ref, o_ref, *, chunks, head_norm_cols):
    xn = _rms(x_ref[...], g_ref[...]).astype(BF16)
    for c0, cs in chunks:
        r = jnp.dot(xn, w_ref[:, c0:c0 + cs], preferred_element_type=F32)
        if c0 < head_norm_cols:
            parts = [_rms(r[:, i:i + LANES], hn_ref[...]) for i in range(0, cs, LANES)]
            r = jnp.concatenate(parts, axis=1)
        o_ref[:, c0:c0 + cs] = r


def _col_chunks(n, width=512):
    return tuple((c, min(width, n - c)) for c in range(0, n, width))


def norm_proj(x, g, w, *, tm, head_norm=None, head_norm_cols=0):
    n, d = x.shape
    m = w.shape[1]
    if head_norm is None:
        head_norm = jnp.ones((1, LANES), F32)
    kern = functools.partial(_norm_proj_kernel, chunks=_col_chunks(m), head_norm_cols=head_norm_cols)
    return pl.pallas_call(
        kern,
        grid=(n // tm,),
        in_specs=[
            pl.BlockSpec((tm, d), lambda i: (i, 0)),
            pl.BlockSpec((1, d), lambda i: (0, 0)),
            pl.BlockSpec((d, m), lambda i: (0, 0), pipeline_mode=pl.Buffered(1)),
            pl.BlockSpec((1, LANES), lambda i: (0, 0)),
        ],
        out_specs=pl.BlockSpec((tm, m), lambda i: (i, 0)),
        out_shape=jax.ShapeDtypeStruct((n, m), F32),
        compiler_params=pltpu.CompilerParams(dimension_semantics=("arbitrary",), vmem_limit_bytes=VMEM_LIMIT_BYTES),
        name="norm_proj",
    )(x, g, w, head_norm)


def _ffn(x, g_ref, w1_ref, w2_ref):
    h = _rms(x, g_ref[...]).astype(BF16)
    acc = None
    for c in range(0, FFN_DIM, FFN_CHUNK):
        u = jnp.maximum(jnp.dot(h, w1_ref[:, c:c + FFN_CHUNK], preferred_element_type=F32), 0.0)
        t = jnp.dot((u * u).astype(BF16), w2_ref[c:c + FFN_CHUNK, :], preferred_element_type=F32)
        acc = t if acc is None else acc + t
    return x + acc


def _post_prompt_kernel(x_ref, a_ref, wout_ref, gx_ref, wq_ref, qn_ref, mk_ref, mv_ref, wo_ref, gf_ref, w1_ref, w2_ref,
                        o_ref):
    x = x_ref[...] + _mm(a_ref[...], wout_ref[...])
    q = jnp.dot(_rms(x, gx_ref[...]).astype(BF16), wq_ref[...], preferred_element_type=F32)
    outs = []
    for h in range(MEM_HEADS):
        sl = slice(h * MEM_HEAD_DIM, (h + 1) * MEM_HEAD_DIM)
        qh = _rms(q[:, sl], qn_ref[...])
        s = _mm_nt(qh, mk_ref[:, sl]) * (MEM_HEAD_DIM ** -0.5)
        p = jnp.exp(s - jnp.max(s, axis=-1, keepdims=True))
        p = p / jnp.sum(p, axis=-1, keepdims=True)
        outs.append(_mm(p, mv_ref[:, sl]))
    x = x + _mm(jnp.concatenate(outs, axis=1), wo_ref[...])
    o_ref[...] = _ffn(x, gf_ref, w1_ref, w2_ref)


def post_prompt(x, a, wout, gx, wq, qn, mkv, wo, gf, w1, w2, *, rows_per_batch, tm):
    n, d = x.shape
    ka = a.shape[1]
    tiles = rows_per_batch // tm
    const = lambda shape: pl.BlockSpec(shape, lambda i: (0, 0), pipeline_mode=pl.Buffered(1))
    return pl.pallas_call(
        _post_prompt_kernel,
        grid=(n // tm,),
        in_specs=[
            pl.BlockSpec((tm, d), lambda i: (i, 0)),
            pl.BlockSpec((tm, ka), lambda i: (i, 0)),
            const((ka, d)),
            const((1, d)),
            const((d, MEM_WIDTH)),
            const((1, MEM_HEAD_DIM)),
            pl.BlockSpec((MEM_LEN, MEM_WIDTH), lambda i: (i // tiles, 0)),
            pl.BlockSpec((MEM_LEN, MEM_WIDTH), lambda i: (i // tiles, 1)),
            const((MEM_WIDTH, d)),
            const((1, d)),
            const((d, FFN_DIM)),
            const((FFN_DIM, d)),
        ],
        out_specs=pl.BlockSpec((tm, d), lambda i: (i, 0)),
        out_shape=jax.ShapeDtypeStruct((n, d), F32),
        compiler_params=pltpu.CompilerParams(dimension_semantics=("arbitrary",), vmem_limit_bytes=VMEM_LIMIT_BYTES),
        name="post_prompt",
    )(x, a, wout, gx, wq, qn, mkv, mkv, wo, gf, w1, w2)


def _res_proj_kernel(x_ref, a_ref, w_ref, o_ref):
    o_ref[...] = x_ref[...] + _mm(a_ref[...], w_ref[...])


def res_proj(x, a, w):
    n, d = x.shape
    return pl.pallas_call(
        _res_proj_kernel,
        out_shape=jax.ShapeDtypeStruct((n, d), F32),
        compiler_params=pltpu.CompilerParams(vmem_limit_bytes=VMEM_LIMIT_BYTES),
        name="res_proj",
    )(x, a, w)


def _res_proj_ffn_kernel(x_ref, a_ref, w_ref, gf_ref, w1_ref, w2_ref, o_ref):
    x = x_ref[...] + _mm(a_ref[...], w_ref[...])
    o_ref[...] = _ffn(x, gf_ref, w1_ref, w2_ref)


def res_proj_ffn(x, a, w, gf, w1, w2):
    n, d = x.shape
    return pl.pallas_call(
        _res_proj_ffn_kernel,
        out_shape=jax.ShapeDtypeStruct((n, d), F32),
        compiler_params=pltpu.CompilerParams(vmem_limit_bytes=VMEM_LIMIT_BYTES),
        name="res_proj_ffn",
    )(x, a, w, gf, w1, w2)


def _pair_rows(x, op):
    xb = jnp.broadcast_to(x, (SUBLANES, LANES))
    return op(xb, pltpu.roll(xb, SUBLANES // 2, 0))


def _xattn_sample_kernel(q_ref, mk_ref, mv_ref, o_ref):
    row = _iota((SUBLANES, LANES), 0)
    groups = MEM_LEN * MEM_HEADS // SUBLANES
    for i in range(SUBLANES):
        q8 = jnp.zeros((SUBLANES, LANES), F32)
        for h in range(MEM_HEADS):
            q8 = jnp.where(row % MEM_HEADS == h, q_ref[i:i + 1, h * MEM_HEAD_DIM:(h + 1) * MEM_HEAD_DIM], q8)
        k3 = mk_ref[0, i].reshape(groups, SUBLANES, LANES)
        s = jnp.sum(k3 * q8[None], axis=-1, keepdims=True) * (MEM_HEAD_DIM ** -0.5)
        mx = _pair_rows(jnp.max(s, axis=0), jnp.maximum)
        p = jnp.exp(s - mx[None, :, 0:1])
        den = _pair_rows(jnp.sum(p, axis=0), jnp.add)
        v3 = mv_ref[0, i].reshape(groups, SUBLANES, LANES)
        o8 = _pair_rows(jnp.sum(p * v3, axis=0), jnp.add) / den
        for h in range(MEM_HEADS):
            o_ref[i:i + 1, h * MEM_HEAD_DIM:(h + 1) * MEM_HEAD_DIM] = o8[h:h + 1]


def xattn_sample(q, mk, mv, layer):
    b = q.shape[0]
    mem = pl.BlockSpec((1, SUBLANES, MEM_LEN * MEM_HEADS, MEM_HEAD_DIM), lambda i: (layer, i, 0, 0))
    return pl.pallas_call(
        _xattn_sample_kernel,
        grid=(b // SUBLANES,),
        in_specs=[pl.BlockSpec((SUBLANES, MEM_WIDTH), lambda i: (i, 0)), mem, mem],
        out_specs=pl.BlockSpec((SUBLANES, MEM_WIDTH), lambda i: (i, 0)),
        out_shape=jax.ShapeDtypeStruct((b, MEM_WIDTH), F32),
        compiler_params=pltpu.CompilerParams(dimension_semantics=("arbitrary",), vmem_limit_bytes=VMEM_LIMIT_BYTES),
        name="xattn_sample",
    )(q, mk, mv)


def _proj_chain(x_ref, g_ref, w_ref, dst, width):
    group, _, d = x_ref.shape
    xn = _rms(x_ref[...].reshape(group * CHUNK, d), g_ref[...]).astype(BF16)
    yield
    cols = w_ref.shape[1]
    for c0 in range(0, cols, width):
        cs = min(width, cols - c0)
        r = jnp.dot(xn, w_ref[:, c0:c0 + cs], preferred_element_type=F32)
        for b in range(group):
            dst[b, :, c0:c0 + cs] = r[b * CHUNK:(b + 1) * CHUNK]
        yield


def _even_prompt_kernel(xn_ref, x0_ref, g_ref, w_ref, cos_ref, sinlo_ref, sinhi_ref, qn_ref, kn_ref, sink_ref, segm_ref,
                        triu_ref, gb_ref, onorm_ref,
                        y_ref, kc_ref, vc_ref, caug_ref, m_ref,
                        kprev, vprev, cst, mst, proj):
    n = pl.program_id(0)
    batch = xn_ref.shape[0]
    slot = lax.rem(n, 2)

    @pl.when(n == 0)
    def _():
        kprev[...] = jnp.zeros_like(kprev)
        vprev[...] = jnp.zeros_like(vprev)
        cst[...] = jnp.zeros_like(cst)
        mst[...] = jnp.zeros_like(mst)
        for _ in _proj_chain(x0_ref, g_ref, w_ref, proj.at[0], PROJ_COLS_PER_STAGE):
            pass

    p_ref = proj.at[slot]

    cos, sinlo, sinhi = cos_ref[...], sinlo_ref[...], sinhi_ref[...]
    segm = segm_ref[...]
    lane = _iota((1, LANES), 1)
    low = lane < 64
    qi = _iota((CHUNK, 2 * CHUNK), 0)
    si = _iota((CHUNK, 2 * CHUNK), 1)
    valid = (si >= qi) & (si <= qi + CHUNK) & ((si >= CHUNK) | (n > 0))
    causal = _iota((CHUNK, CHUNK), 0) >= _iota((CHUNK, CHUNK), 1)
    ones_col = jnp.where(_iota((CHUNK, LANES), 1) == 0, 1.0, 0.0)
    row64 = _iota((CHUNK, 1), 0) < 64
    new_kv = _lockstep([
        _even_prompt_chunk(p_ref.at[b], y_ref.at[b], kprev.at[b], vprev.at[b], cst.at[b], mst.at[b],
                           (cos, sinlo, sinhi), qn_ref, kn_ref, sink_ref, segm, triu_ref, gb_ref, onorm_ref,
                           low, valid, causal, ones_col, row64)
        for b in range(batch)] + [_proj_chain(xn_ref, g_ref, w_ref, proj.at[1 - slot], 2 * PROJ_COLS_PER_STAGE)],
        every=[1] * batch + [8])[:batch]

    @pl.when(n == pl.num_programs(0) - 1)
    def _():
        for b in range(batch):
            kc_ref[b], vc_ref[b] = new_kv[b]
        caug_ref[...] = cst[...]
        m_ref[...] = mst[...]


def _lockstep(chains, every=None):
    every = every or [1] * len(chains)
    results = [None] * len(chains)
    live = list(range(len(chains)))
    rnd = 0
    while live:
        for i in list(live):
            if rnd % every[i]:
                continue
            try:
                next(chains[i])
            except StopIteration as stop:
                results[i] = stop.value
                live.remove(i)
        rnd += 1
    return results


def _even_prompt_chunk(p_ref, y_ref, kprev, vprev, cst, mst, tabs, qn_ref, kn_ref, sink_ref, segm, triu_ref, gb_ref,
                       onorm_ref, low, valid, causal, ones_col, row64):
    cos, sinlo, sinhi = tabs
    k = _rope16(_seg_rms(p_ref[:, E_SK:E_SK + LANES], kn_ref[...], segm), cos, sinlo, sinhi)
    v = p_ref[:, E_SV:E_SV + LANES]
    kk = jnp.concatenate([kprev[...], k], axis=0)
    vv = jnp.concatenate([vprev[...], v], axis=0)
    kk_sw = pltpu.roll(kk, 64, 1)
    vv_sw = pltpu.roll(vv, 64, 1)
    kvar = {(0, 0): jnp.where(low, kk, 0.0), (0, 1): jnp.where(low, 0.0, kk_sw),
            (1, 0): jnp.where(low, kk_sw, 0.0), (1, 1): jnp.where(low, 0.0, kk)}
    vvar = {(0, 0): vv, (0, 1): vv_sw, (1, 0): vv_sw, (1, 1): vv}
    yield
    for j in range(SW_HEADS // 2):
        sl = slice(E_SQ + j * LANES, E_SQ + (j + 1) * LANES)
        qb = _rope16(_seg_rms(p_ref[:, sl], qn_ref[:, sl], segm), cos, sinlo, sinhi)
        yield
        halves = []
        for pos in range(2):
            h = 2 * j + pos
            kv = h // (SW_HEADS // SW_KV_HEADS)
            s = jnp.where(valid, _mm_nt(qb, kvar[(kv, pos)]) * (SW_HEAD_DIM ** -0.5), NEG)
            yield
            sink = sink_ref[h]
            m = jnp.maximum(jnp.max(s, axis=-1, keepdims=True), sink)
            pr = jnp.exp(s - m)
            yield
            pr = pr / (jnp.sum(pr, axis=-1, keepdims=True) + jnp.exp(sink - m))
            halves.append(_mm(pr, vvar[(kv, pos)]))
            yield
        y_ref[:, j * LANES:(j + 1) * LANES] = jnp.where(low, halves[0], halves[1]).astype(y_ref.dtype)
    kprev[...] = k
    vprev[...] = v
    yield

    gt = p_ref[:, E_GATE:E_GATE + LANES].T
    gi = gt[0:SUBLANES] + gb_ref[0:SUBLANES]
    fl = _log_sigmoid(gt[GATE_F_LANE:GATE_F_LANE + SUBLANES] + gb_ref[SUBLANES:2 * SUBLANES])
    yield
    fcum = _mm_exact_rhs(fl, triu_ref[...])
    dd = gi - fcum
    mprev = mst[...]
    yield
    mt = fcum + jnp.maximum(mprev, _cummax_lanes(dd))
    fend = jnp.broadcast_to(fcum[:, CHUNK - 1:CHUNK], fcum.shape)
    mend = jnp.broadcast_to(mt[:, CHUNK - 1:CHUNK], mt.shape)
    decay = jnp.exp(fend + mprev - mend)
    rows = jnp.concatenate([fcum - mt, jnp.exp(fcum + mprev - mt), jnp.exp(-mt), jnp.exp(fend - fcum + gi - mend),
                            jnp.zeros((CHUNK - 4 * SUBLANES, CHUNK), F32)], axis=0)
    yield
    cols = rows.T
    yield
    for j in range(ML_HEADS // 2):
        qblk = p_ref[:, E_MQ + j * LANES:E_MQ + (j + 1) * LANES]
        kblk = p_ref[:, E_MK + j * LANES:E_MK + (j + 1) * LANES] * (ML_QK_DIM ** -0.5)
        c_old = cst[j]
        upd = None
        for pos in range(2):
            h = 2 * j + pos
            msk = low if pos == 0 else jnp.logical_not(low)
            qm = jnp.where(msk, qblk, 0.0)
            logw = cols[:, h:h + 1] + dd[h:h + 1, :]
            w = jnp.exp(jnp.where(causal, logw, -jnp.inf))
            sqk = _mm_nt(qm, kblk) * w
            yield
            vh = p_ref[:, E_MV + h * LANES:E_MV + (h + 1) * LANES]
            qc = _mm(qm, c_old)
            yield
            cs = cols[:, SUBLANES + h:SUBLANES + h + 1]
            num = cs * qc[:, :ML_V_DIM] + _mm(sqk, vh)
            den = cs * qc[:, ML_V_DIM:ML_V_DIM + 1] + jnp.sum(sqk, axis=-1, keepdims=True)
            hh = num / jnp.maximum(jnp.abs(den), cols[:, 2 * SUBLANES + h:2 * SUBLANES + h + 1])
            yield
            hsl = slice(h * ML_V_DIM, (h + 1) * ML_V_DIM)
            hn = _rms(hh, onorm_ref[:, hsl])
            mo = p_ref[:, E_MO + h * ML_V_DIM:E_MO + (h + 1) * ML_V_DIM]
            y_ref[:, SW_HEADS * SW_HEAD_DIM + h * ML_V_DIM:SW_HEADS * SW_HEAD_DIM + (h + 1) * ML_V_DIM] = (
                hn * _sigmoid(mo)).astype(y_ref.dtype)
            kw = jnp.where(msk, kblk, 0.0) * cols[:, 3 * SUBLANES + h:3 * SUBLANES + h + 1]
            u = _mm_tn(kw, jnp.concatenate([vh, ones_col], axis=1))
            upd = u if upd is None else upd + u
            yield
        dec = jnp.where(row64, decay[2 * j:2 * j + 1, 0:1], decay[2 * j + 1:2 * j + 2, 0:1])
        cst[j] = dec * c_old + upd
    mst[...] = mend
    return k, v


def even_prompt(x, g, w, tabs, qn, kn, sinks, segm, triu, gb, onorm):
    batch, seq, d = x.shape
    nc = seq // CHUNK
    tab = pl.BlockSpec((CHUNK, LANES), lambda n: (n, 0))
    const = lambda shape: pl.BlockSpec(shape, lambda n: (0,) * len(shape))
    state_shapes = [(batch, CHUNK, LANES), (batch, CHUNK, LANES),
                    (batch, ML_HEADS // 2, 2 * ML_QK_DIM, 2 * ML_V_DIM), (batch, SUBLANES, LANES)]
    return pl.pallas_call(
        _even_prompt_kernel,
        grid=(nc,),
        in_specs=[
            pl.BlockSpec((batch, CHUNK, d), lambda n: (0, jnp.minimum(n + 1, nc - 1), 0)),
            pl.BlockSpec((batch, CHUNK, d), lambda n: (0, 0, 0)),
            const((1, d)), pl.BlockSpec((d, EVEN_COLS), lambda n: (0, 0), pipeline_mode=pl.Buffered(1)),
            tab, tab, tab,
            const((1, SW_HEADS * SW_HEAD_DIM)), const((1, LANES)),
            pl.BlockSpec(memory_space=pltpu.SMEM),
            const((LANES, LANES)), const((CHUNK, CHUNK)), const((2 * SUBLANES, LANES)),
            const((1, ML_HEADS * ML_V_DIM)),
        ],
        out_specs=[pl.BlockSpec((batch, CHUNK, D_MODEL), lambda n: (0, n, 0))] + [const(s) for s in state_shapes],
        out_shape=[jax.ShapeDtypeStruct((batch, seq, D_MODEL), BF16)]
        + [jax.ShapeDtypeStruct(s, F32) for s in state_shapes],
        scratch_shapes=[pltpu.VMEM(s, F32) for s in state_shapes] + [pltpu.VMEM((2, batch, CHUNK, EVEN_COLS), F32)],
        compiler_params=pltpu.CompilerParams(dimension_semantics=("arbitrary",), vmem_limit_bytes=VMEM_LIMIT_BYTES),
        name="even_prompt",
    )(x, x, g, w, *tabs, qn, kn, sinks, segm, triu, gb, onorm)


def _odd_prompt_kernel(xn_ref, x0_ref, g_ref, w_ref, cosr_ref, sinr_ref, convw_ref, convb_ref, dtb_ref, arow_ref,
                       drow_ref, snorm_ref, tril_ref, dmat_ref, qs_ref, ks_ref, cd_ref, rnorm_ref,
                       y_ref, conv_ref, s_ref, r_ref,
                       ext, sst, rst, proj):
    n = pl.program_id(1)
    batch = xn_ref.shape[0]
    slot = lax.rem(n, 2)

    @pl.when(n == 0)
    def _():
        ext[:, 0:SUBLANES] = jnp.zeros((batch, SUBLANES, SSD_CONV_DIM), F32)
        sst[...] = jnp.zeros_like(sst)
        rst[...] = jnp.zeros_like(rst)
        for _ in _proj_chain(x0_ref, g_ref, w_ref, proj.at[0], PROJ_COLS_PER_STAGE):
            pass

    p_ref = proj.at[slot]

    lane = _iota((1, LANES), 1)
    low = lane < 64
    row64 = _iota((CHUNK, 1), 0) < 64
    causal = _iota((CHUNK, CHUNK), 0) >= _iota((CHUNK, CHUNK), 1)
    tails = _lockstep([
        _odd_prompt_chunk(p_ref.at[b], y_ref.at[b], ext.at[b], sst.at[b], rst.at[b], cosr_ref, sinr_ref, convw_ref,
                          convb_ref, dtb_ref, arow_ref, drow_ref, snorm_ref, tril_ref, dmat_ref, qs_ref, ks_ref,
                          cd_ref, rnorm_ref, low, row64, causal)
        for b in range(batch)] + [_proj_chain(xn_ref, g_ref, w_ref, proj.at[1 - slot], PROJ_COLS_PER_STAGE)],
        every=[1] * batch + [2])[:batch]

    @pl.when(n == pl.num_programs(1) - 1)
    def _():
        for b in range(batch):
            conv_ref[b] = tails[b]
        s_ref[...] = sst[...]
        r_ref[...] = rst[...]


def _odd_prompt_chunk(p_ref, y_ref, ext, sst, rst, cosr_ref, sinr_ref, convw_ref, convb_ref, dtb_ref, arow_ref,
                      drow_ref, snorm_ref, tril_ref, dmat_ref, qs_ref, ks_ref, cd_ref, rnorm_ref, low, row64, causal):
    tail = SUBLANES
    ext[tail:tail + CHUNK] = p_ref[:, O_XBC:O_XBC + SSD_CONV_DIM]
    yield
    xe = ext[...]
    acc = None
    for jj in range(SSD_CONV):
        shift = SSD_CONV - 1 - jj
        tap = (pltpu.roll(xe, shift, 0) if shift else xe)[tail:tail + CHUNK] * convw_ref[jj:jj + 1]
        acc = tap if acc is None else acc + tap
    xact = _silu(acc + convb_ref[...])
    new_tail = ext[CHUNK:CHUNK + tail]
    ext[0:tail] = new_tail
    yield

    dt = _softplus(p_ref[:, O_DT:O_DT + LANES] + dtb_ref[...])
    cum = _mm_exact_lhs(tril_ref[...], dt * arow_ref[...])
    yield
    cum_t = cum.T
    dt_t = dt.T
    ecum = jnp.exp(cum)
    cend = cum[CHUNK - 1:CHUNK, :]
    wend = jnp.exp(cend - cum) * dt
    eend = jnp.exp(cend)
    yield
    pairs_per_group = SSD_HEADS // SSD_GROUPS // 2
    ys = []
    for g in range(SSD_GROUPS):
        bc = xact[:, SSD_INNER + g * SSD_STATE:SSD_INNER + (g + 1) * SSD_STATE]
        cc = xact[:, SSD_INNER + (SSD_GROUPS + g) * SSD_STATE:SSD_INNER + (SSD_GROUPS + g + 1) * SSD_STATE]
        cb = _mm_nt(cc, bc)
        yield
        for jg in range(pairs_per_group):
            j = g * pairs_per_group + jg
            ha, hb = 2 * j, 2 * j + 1
            xp = xact[:, j * LANES:(j + 1) * LANES]
            s_old = sst[j]
            y = jnp.where(low, ecum[:, ha:ha + 1], ecum[:, hb:hb + 1]) * _mm_nt(cc, s_old)
            yield
            for pos, h in ((0, ha), (1, hb)):
                seg = cum[:, h:h + 1] - cum_t[h:h + 1, :]
                wmat = cb * jnp.exp(jnp.where(causal, seg, -jnp.inf)) * dt_t[h:h + 1, :]
                y = y + _mm(wmat, jnp.where(low if pos == 0 else jnp.logical_not(low), xp, 0.0))
                yield
            xw = xp * jnp.where(low, wend[:, ha:ha + 1], wend[:, hb:hb + 1])
            sst[j] = jnp.where(row64, eend[:, ha:ha + 1], eend[:, hb:hb + 1]) * s_old + _mm_tn(xw, bc)
            ys.append(y)
            yield
        gs = slice(g * SSD_INNER // SSD_GROUPS, (g + 1) * SSD_INNER // SSD_GROUPS)
        yg = jnp.concatenate(ys[g * pairs_per_group:(g + 1) * pairs_per_group], axis=1)
        yg = (yg + drow_ref[:, gs] * xact[:, gs]) * _silu(p_ref[:, O_Z + gs.start:O_Z + gs.stop])
        y_ref[:, gs] = _rms(yg, snorm_ref[:, gs]).astype(y_ref.dtype)
        yield

    cosr, sinr = cosr_ref[...], sinr_ref[...]
    for h in range(RET_HEADS):
        hs = h * LANES
        q = _rope128(p_ref[:, O_RQ + hs:O_RQ + hs + LANES], cosr, sinr)
        k = _rope128(p_ref[:, O_RK + hs:O_RK + hs + LANES], cosr, sinr) * (RET_QK_DIM ** -0.5)
        v = p_ref[:, O_RV + hs:O_RV + hs + LANES]
        yield
        r_old = rst[h]
        o = _mm(_mm_nt(q, k) * dmat_ref[h], v) + qs_ref[h] * _mm(q, r_old)
        yield
        rst[h] = cd_ref[h] * r_old + _mm_tn(k * ks_ref[h], v)
        xc = o - jnp.mean(o, axis=-1, keepdims=True)
        yn = xc * lax.rsqrt(jnp.mean(xc * xc, axis=-1, keepdims=True) + EPS) * rnorm_ref[:, hs:hs + LANES]
        y_ref[:, SSD_INNER + hs:SSD_INNER + hs + LANES] = (
            yn * _silu(p_ref[:, O_RG + hs:O_RG + hs + LANES])).astype(y_ref.dtype)
        yield
    return new_tail


def odd_prompt(x, g, w, tabs, convw, convb, dtb, arow, drow, snorm, tril, ret_consts, rnorm, *, group):
    batch, seq, d = x.shape
    nc = seq // CHUNK
    tab = pl.BlockSpec((CHUNK, LANES), lambda g, n: (n, 0))
    const = lambda shape: pl.BlockSpec(shape, lambda g, n: (0,) * len(shape))
    per_g = lambda shape: pl.BlockSpec((group,) + shape, lambda g, n: (g,) + (0,) * len(shape))
    hc = (RET_HEADS, CHUNK, LANES)
    ywidth = SSD_INNER + RET_HEADS * RET_V_DIM
    states = [(SUBLANES, SSD_CONV_DIM), (SSD_HEADS // 2, LANES, SSD_STATE), hc]
    return pl.pallas_call(
        _odd_prompt_kernel,
        grid=(batch // group, seq // CHUNK),
        in_specs=[
            pl.BlockSpec((group, CHUNK, d), lambda g, n: (g, jnp.minimum(n + 1, nc - 1), 0)),
            pl.BlockSpec((group, CHUNK, d), lambda g, n: (g, 0, 0)),
            const((1, d)), pl.BlockSpec((d, ODD_COLS), lambda g, n: (0, 0), pipeline_mode=pl.Buffered(1)),
            tab, tab,
            const((SSD_CONV, SSD_CONV_DIM)), const((1, SSD_CONV_DIM)), const((1, LANES)), const((1, LANES)),
            const((1, SSD_INNER)), const((1, SSD_INNER)), const((CHUNK, CHUNK)),
            const(hc), const(hc), const(hc), const(hc), const((1, RET_HEADS * RET_V_DIM)),
        ],
        out_specs=[pl.BlockSpec((group, CHUNK, ywidth), lambda g, n: (g, n, 0))] + [per_g(s) for s in states],
        out_shape=[jax.ShapeDtypeStruct((batch, seq, ywidth), BF16)]
        + [jax.ShapeDtypeStruct((batch,) + s, F32) for s in states],
        scratch_shapes=[pltpu.VMEM((group, SUBLANES + CHUNK, SSD_CONV_DIM), F32),
                        pltpu.VMEM((group,) + states[1], F32), pltpu.VMEM((group,) + states[2], F32),
                        pltpu.VMEM((2, group, CHUNK, ODD_COLS), F32)],
        compiler_params=pltpu.CompilerParams(dimension_semantics=("arbitrary", "arbitrary"),
                                             vmem_limit_bytes=VMEM_LIMIT_BYTES),
        name="odd_prompt",
    )(x, x, g, w, *tabs, convw, convb, dtb, arow, drow, snorm, tril, *ret_consts, rnorm)


def _lane_to_rows(g, offset):
    sel = _iota(g.shape, 1) == _iota(g.shape, 0) + offset
    return jnp.sum(jnp.where(sel, g, 0.0), axis=-1, keepdims=True)


def _block_rows(x, i, nblk, blk):
    row = _iota((SUBLANES, blk), 0)
    out = jnp.zeros((SUBLANES, blk), F32)
    for b in range(nblk):
        out = jnp.where(row == b, x[i:i + 1, b * blk:(b + 1) * blk], out)
    return out


def _even_sample_kernel(p_ref, bk_ref, bv_ref, c_ref, nrow_ref, mrow_ref, cos_ref, sinlo_ref, sinhi_ref, qn_ref,
                        kn_ref, sink_ref, segm_ref, gb_ref, onorm_ref,
                        y_ref, nk_ref, nv_ref, nc_ref, nn_ref, nm_ref):
    R = SUBLANES
    cos, sinlo, sinhi = cos_ref[...], sinlo_ref[...], sinhi_ref[...]
    segm = segm_ref[...]
    row = _iota((R, LANES), 0)
    lane = _iota((R, LANES), 1)
    low = lane < 64
    group = SW_HEADS // SW_KV_HEADS
    scale = SW_HEAD_DIM ** -0.5
    sink = sink_ref[:, 0:1]
    last = _iota((CHUNK, LANES), 0) == CHUNK - 1

    k = _rope16(_seg_rms(p_ref[:, E_SK:E_SK + LANES], kn_ref[...], segm), cos, sinlo, sinhi)
    v = p_ref[:, E_SV:E_SV + LANES]
    qb, qb_sw = [], []
    for j in range(SW_HEADS // 2):
        sl = slice(E_SQ + j * LANES, E_SQ + (j + 1) * LANES)
        qb.append(_rope16(_seg_rms(p_ref[:, sl], qn_ref[:, sl], segm), cos, sinlo, sinhi))
        qb_sw.append(pltpu.roll(qb[j], 64, 1))

    g = p_ref[:, E_GATE:E_GATE + LANES]
    ic = g + gb_ref[0:1]
    fl = _log_sigmoid(pltpu.roll(g, LANES - GATE_F_LANE, 1) + gb_ref[1:2])
    mprev = mrow_ref[...]
    mt = jnp.maximum(fl + mprev, ic)
    w_all = jnp.exp(ic - mt)
    cs_all = jnp.exp(fl + mprev - mt)
    em_all = jnp.exp(-mt)
    nm_ref[...] = mt
    width = ML_HEADS * ML_QK_DIM
    own = _iota((R, width), 1) // ML_QK_DIM == _iota((R, width), 0)
    kscaled = p_ref[:, E_MK:E_MK + width] * (ML_QK_DIM ** -0.5)

    for i in range(R):
        qm = jnp.zeros((R, LANES), F32)
        for j in range(SW_HEADS // 2):
            for pos in range(2):
                h = 2 * j + pos
                kv = h // group
                src = (qb[j] if pos == kv else qb_sw[j])[i:i + 1]
                qm = jnp.where((row == h) & (low if kv == 0 else jnp.logical_not(low)), src, qm)
        bk, bv = bk_ref[i], bv_ref[i]
        ki, vi = k[i:i + 1], v[i:i + 1]
        s = _dg(qm, bk, 1, 1) * scale
        s_new = jnp.sum(qm * ki, axis=-1, keepdims=True) * scale
        m = jnp.maximum(jnp.maximum(jnp.max(s, axis=-1, keepdims=True), s_new), sink)
        pr = jnp.exp(s - m)
        p_new = jnp.exp(s_new - m)
        den = jnp.sum(pr, axis=-1, keepdims=True) + p_new + jnp.exp(sink - m)
        o = (_dg(pr, bv, 1, 0) + p_new * vi) / den
        o_sw = pltpu.roll(o, 64, 1)
        for j in range(SW_HEADS // 2):
            halves = []
            for pos in range(2):
                h = 2 * j + pos
                halves.append((o if pos == h // group else o_sw)[h:h + 1, :])
            y_ref[i:i + 1, j * LANES:(j + 1) * LANES] = jnp.where(low[0:1], halves[0], halves[1])
        nk_ref[i] = jnp.where(last, ki, pltpu.roll(bk, CHUNK - 1, 0))
        nv_ref[i] = jnp.where(last, vi, pltpu.roll(bv, CHUNK - 1, 0))

        w = _lane_to_rows(jnp.broadcast_to(w_all[i:i + 1], (R, LANES)), 0)
        cs = _lane_to_rows(jnp.broadcast_to(cs_all[i:i + 1], (R, LANES)), 0)
        em = _lane_to_rows(jnp.broadcast_to(em_all[i:i + 1], (R, LANES)), 0)
        qrows = jnp.where(own, p_ref[i:i + 1, E_MQ:E_MQ + width], 0.0)
        krows = jnp.where(own, kscaled[i:i + 1], 0.0)
        c_old = c_ref[i]
        qc = _dg(qrows, c_old, 1, 0)
        qn_dot = jnp.sum(qrows * nrow_ref[i:i + 1], axis=-1, keepdims=True)
        sqk = jnp.sum(qrows * krows, axis=-1, keepdims=True) * w
        v4 = _block_rows(p_ref[:, E_MV:E_MV + ML_HEADS * ML_V_DIM], i, ML_HEADS, ML_V_DIM)
        mo4 = _block_rows(p_ref[:, E_MO:E_MO + ML_HEADS * ML_V_DIM], i, ML_HEADS, ML_V_DIM)
        num = cs * qc + sqk * v4
        dn = cs * qn_dot + sqk
        hh = num / jnp.maximum(jnp.abs(dn), em)
        hn = _rms(hh, onorm_ref[...]) * _sigmoid(mo4)
        for h in range(ML_HEADS):
            c0 = SW_HEADS * SW_HEAD_DIM + h * ML_V_DIM
            y_ref[i:i + 1, c0:c0 + ML_V_DIM] = hn[h:h + 1]
        dec_col = jnp.concatenate(
            [jnp.broadcast_to(cs[h:h + 1, 0:1], (ML_QK_DIM, ML_V_DIM)) for h in range(ML_HEADS)], axis=0)
        nc_ref[i] = dec_col * c_old + _dg(krows * w, v4, 0, 0)
        dec_lanes = jnp.sum(jnp.where(own, cs, 0.0), axis=0, keepdims=True)
        nn_ref[i:i + 1] = dec_lanes * nrow_ref[i:i + 1] + jnp.sum(krows * w, axis=0, keepdims=True)


def even_sample(proj, bk, bv, c, nrow, mrow, tabs, qn, kn, sink_rows, segm, gb, onorm_rows):
    b = proj.shape[0]
    R = SUBLANES
    width = ML_HEADS * ML_QK_DIM
    rows = lambda w: pl.BlockSpec((R, w), lambda i: (i, 0))
    per_b = lambda shape: pl.BlockSpec((R,) + shape, lambda i: (i,) + (0,) * len(shape))
    const = lambda shape: pl.BlockSpec(shape, lambda i: (0,) * len(shape))
    tab = const((R, LANES))
    return pl.pallas_call(
        _even_sample_kernel,
        grid=(b // R,),
        in_specs=[
            rows(EVEN_COLS), per_b((CHUNK, LANES)), per_b((CHUNK, LANES)), per_b((width, ML_V_DIM)),
            rows(width), rows(LANES), tab, tab, tab,
            const((1, SW_HEADS * SW_HEAD_DIM)), const((1, LANES)), const((R, LANES)),
            const((LANES, LANES)), const((2, LANES)), const((R, ML_V_DIM)),
        ],
        out_specs=[
            rows(D_MODEL), per_b((CHUNK, LANES)), per_b((CHUNK, LANES)), per_b((width, ML_V_DIM)),
            rows(width), rows(LANES),
        ],
        out_shape=[
            jax.ShapeDtypeStruct((b, D_MODEL), F32),
            jax.ShapeDtypeStruct((b, CHUNK, LANES), F32), jax.ShapeDtypeStruct((b, CHUNK, LANES), F32),
            jax.ShapeDtypeStruct((b, width, ML_V_DIM), F32), jax.ShapeDtypeStruct((b, width), F32),
            jax.ShapeDtypeStruct((b, LANES), F32),
        ],
        compiler_params=pltpu.CompilerParams(dimension_semantics=("arbitrary",), vmem_limit_bytes=VMEM_LIMIT_BYTES),
        name="even_sample",
    )(proj, bk, bv, c, nrow, mrow, *tabs, qn, kn, sink_rows, segm, gb, onorm_rows)


def _odd_sample_kernel(p_ref, cb_ref, s_ref, r_ref, cosr_ref, sinr_ref, convw_ref, convb_ref, dtb_ref, arow_ref,
                       drow_ref, snorm_ref, spread_ref, rdec_ref, rnorm_ref,
                       y_ref, ncb_ref, ns_ref, nr_ref):
    R = SUBLANES
    xbc = p_ref[:, O_XBC:O_XBC + SSD_CONV_DIM]
    acc = cb_ref[0] * convw_ref[0:1]
    for jj in range(1, SSD_CONV - 1):
        acc = acc + cb_ref[jj] * convw_ref[jj:jj + 1]
    acc = acc + xbc * convw_ref[SSD_CONV - 1:SSD_CONV]
    xact = _silu(acc + convb_ref[...])
    for jj in range(SSD_CONV - 2):
        ncb_ref[jj] = cb_ref[jj + 1]
    ncb_ref[SSD_CONV - 2] = xbc

    dt = _softplus(p_ref[:, O_DT:O_DT + LANES] + dtb_ref[...])
    dec = jnp.exp(dt * arow_ref[...])
    xs = xact[:, 0:SSD_INNER]
    xdt = xs * _mm_exact_rhs(dt, spread_ref[...])
    gown = _iota((R, SSD_INNER), 1) // (SSD_INNER // SSD_GROUPS) == _iota((R, SSD_INNER), 0)
    bpart = xact[:, SSD_INNER:SSD_INNER + SSD_GROUPS * SSD_STATE]
    cpart = xact[:, SSD_INNER + SSD_GROUPS * SSD_STATE:SSD_CONV_DIM]

    cosr, sinr = cosr_ref[...], sinr_ref[...]
    width = RET_HEADS * RET_QK_DIM
    q4 = jnp.concatenate([_rope128(p_ref[:, O_RQ + h * LANES:O_RQ + (h + 1) * LANES], cosr, sinr)
                          for h in range(RET_HEADS)], axis=1)
    k4 = jnp.concatenate([_rope128(p_ref[:, O_RK + h * LANES:O_RK + (h + 1) * LANES], cosr, sinr)
                          for h in range(RET_HEADS)], axis=1) * (RET_QK_DIM ** -0.5)
    own = _iota((R, width), 1) // RET_QK_DIM == _iota((R, width), 0)
    gam = rdec_ref[:, 0:1]
    gam_col = jnp.concatenate(
        [jnp.broadcast_to(rdec_ref[h:h + 1, :], (RET_QK_DIM, RET_V_DIM)) for h in range(RET_HEADS)], axis=0)

    ys = []
    for i in range(R):
        brows = _block_rows(bpart, i, SSD_GROUPS, SSD_STATE)
        crows = _block_rows(cpart, i, SSD_GROUPS, SSD_STATE)
        xw = jnp.where(gown, xdt[i:i + 1], 0.0)
        dec_col = jnp.concatenate(
            [jnp.broadcast_to(dec[i:i + 1, h:h + 1], (SSD_HEAD_DIM, SSD_STATE)) for h in range(SSD_HEADS)], axis=0)
        s_new = dec_col * s_ref[i] + _dg(xw, brows, 0, 0)
        ns_ref[i] = s_new
        yrows = _dg(crows, s_new, 1, 1)
        ys.append(jnp.sum(jnp.where(gown, yrows, 0.0), axis=0, keepdims=True))

        qrows = jnp.where(own, q4[i:i + 1], 0.0)
        krows = jnp.where(own, k4[i:i + 1], 0.0)
        v4 = _block_rows(p_ref[:, O_RV:O_RV + RET_HEADS * RET_V_DIM], i, RET_HEADS, RET_V_DIM)
        g4 = _block_rows(p_ref[:, O_RG:O_RG + RET_HEADS * RET_V_DIM], i, RET_HEADS, RET_V_DIM)
        r_old = r_ref[i]
        att = jnp.sum(qrows * krows, axis=-1, keepdims=True)
        o = att * v4 + gam * _dg(qrows, r_old, 1, 0)
        nr_ref[i] = gam_col * r_old + _dg(krows, v4, 0, 0)
        xc = o - jnp.mean(o, axis=-1, keepdims=True)
        yn = xc * lax.rsqrt(jnp.mean(xc * xc, axis=-1, keepdims=True) + EPS) * rnorm_ref[...] * _silu(g4)
        for h in range(RET_HEADS):
            y_ref[i:i + 1, SSD_INNER + h * RET_V_DIM:SSD_INNER + (h + 1) * RET_V_DIM] = yn[h:h + 1]

    y = (jnp.concatenate(ys, axis=0) + drow_ref[...] * xs) * _silu(p_ref[:, O_Z:O_Z + SSD_INNER])
    gw = SSD_INNER // SSD_GROUPS
    for g in range(SSD_GROUPS):
        y_ref[:, g * gw:(g + 1) * gw] = _rms(y[:, g * gw:(g + 1) * gw], snorm_ref[:, g * gw:(g + 1) * gw])


def odd_sample(proj, cbuf, s, r, tabs, convw, convb, dtb, arow, drow, snorm, spread, rdec_rows, rnorm_rows):
    b = proj.shape[0]
    R = SUBLANES
    rows = lambda w: pl.BlockSpec((R, w), lambda i: (i, 0))
    per_b = lambda shape: pl.BlockSpec((R,) + shape, lambda i: (i,) + (0,) * len(shape))
    const = lambda shape: pl.BlockSpec(shape, lambda i: (0,) * len(shape))
    tab = const((R, LANES))
    conv = pl.BlockSpec((SSD_CONV - 1, R, SSD_CONV_DIM), lambda i: (0, i, 0))
    ywidth = SSD_INNER + RET_HEADS * RET_V_DIM
    sshape = (SSD_HEADS * SSD_HEAD_DIM, SSD_STATE)
    rshape = (RET_HEADS * RET_QK_DIM, RET_V_DIM)
    return pl.pallas_call(
        _odd_sample_kernel,
        grid=(b // R,),
        in_specs=[
            rows(ODD_COLS), conv, per_b(sshape), per_b(rshape), tab, tab,
            const((SSD_CONV, SSD_CONV_DIM)), const((1, SSD_CONV_DIM)), const((1, LANES)), const((1, LANES)),
            const((1, SSD_INNER)), const((1, SSD_INNER)), const((LANES, SSD_INNER)),
            const((R, LANES)), const((R, RET_V_DIM)),
        ],
        out_specs=[rows(ywidth), conv, per_b(sshape), per_b(rshape)],
        out_shape=[
            jax.ShapeDtypeStruct((b, ywidth), F32),
            jax.ShapeDtypeStruct((SSD_CONV - 1, b, SSD_CONV_DIM), F32),
            jax.ShapeDtypeStruct((b,) + sshape, F32), jax.ShapeDtypeStruct((b,) + rshape, F32),
        ],
        compiler_params=pltpu.CompilerParams(dimension_semantics=("arbitrary",), vmem_limit_bytes=VMEM_LIMIT_BYTES),
        name="odd_sample",
    )(proj, cbuf, s, r, *tabs, convw, convb, dtb, arow, drow, snorm, spread, rdec_rows, rnorm_rows)


def _pad_cols(w, n):
    return jnp.pad(w, ((0, 0), (0, n - w.shape[1])))


def _even_w_in(w):
    sq, sk, sv, mq, mk, mv, mo, mi, mf = jnp.split(w, [512, 640, 768, 1024, 1280, 1792, 2304, 2308], axis=1)
    gates = jnp.concatenate([_pad_cols(mi, GATE_F_LANE), _pad_cols(mf, LANES - GATE_F_LANE)], axis=1)
    return jnp.concatenate([sq, sk, sv, mq, mk, mv, mo, gates], axis=1).astype(BF16)


def _odd_w_in(w):
    z, xbc, dt, rq, rk, rv, rg = jnp.split(w, [1024, 2560, 2576, 3088, 3600, 4112], axis=1)
    return jnp.concatenate([z, xbc, rq, rk, rv, rg, _pad_cols(dt, LANES)], axis=1).astype(BF16)


def _rope16_tables(pos):
    half = SW_ROT_DIM // 2
    inv = jnp.power(jnp.float32(ROPE_THETA), -jnp.arange(half, dtype=F32) * (2.0 / SW_ROT_DIM))
    ang = pos.astype(F32)[:, None] * inv[None, :]
    cos, sin = jnp.cos(ang), jnp.sin(ang)
    n = pos.shape[0]
    rest = SW_HEAD_DIM - SW_ROT_DIM
    one, zero, zh = jnp.ones((n, rest), F32), jnp.zeros((n, rest), F32), jnp.zeros((n, half), F32)
    tile2 = lambda t: jnp.concatenate([t, t], axis=1)
    return (tile2(jnp.concatenate([cos, cos, one], axis=1)),
            tile2(jnp.concatenate([-sin, zh, zero], axis=1)),
            tile2(jnp.concatenate([zh, sin, zero], axis=1)))


def _rope128_tables(pos):
    half = RET_QK_DIM // 2
    inv = jnp.power(jnp.float32(RET_ROPE_THETA), -jnp.arange(half, dtype=F32) * (2.0 / RET_QK_DIM))
    ang = pos.astype(F32)[:, None] * inv[None, :]
    cos, sin = jnp.cos(ang), jnp.sin(ang)
    return jnp.concatenate([cos, cos], axis=1), jnp.concatenate([-sin, sin], axis=1)


def _ret_consts():
    L = CHUNK
    lg = jnp.log(1.0 - jnp.exp2(-5.0 - jnp.arange(RET_HEADS, dtype=F32)))
    idx = jnp.arange(L, dtype=F32)
    diff = idx[:, None] - idx[None, :]
    dmat = jnp.exp(jnp.where(diff >= 0, diff[None] * lg[:, None, None], -jnp.inf))
    q_scale = jnp.exp((idx[None] + 1.0) * lg[:, None])
    k_scale = jnp.exp((L - 1.0 - idx[None]) * lg[:, None])
    chunk_decay = jnp.exp(L * lg)
    bc = lambda t: jnp.broadcast_to(t[:, :, None], (RET_HEADS, L, LANES))
    cd = jnp.broadcast_to(chunk_decay[:, None, None], (RET_HEADS, L, LANES))
    return dmat, bc(q_scale), bc(k_scale), cd, lg


def _rows8(t):
    return jnp.pad(t, ((0, SUBLANES - t.shape[0]), (0, 0)))


def _gate_bias_rows(gb):
    ib = jnp.broadcast_to(gb[:ML_HEADS, None], (ML_HEADS, LANES))
    fb = jnp.broadcast_to(gb[ML_HEADS:, None], (ML_HEADS, LANES))
    return jnp.concatenate([_rows8(ib), _rows8(fb)], axis=0)


def kernel(x_prompt, x_sample, cache_mem_k, cache_mem_v, cache_swa_k, cache_swa_v, state_mlstm_C, state_mlstm_n,
           state_mlstm_m, state_ssd_conv, state_ssd, state_ret, mem_prompt, norm_mix, norm_xattn, norm_mem, norm_ffn,
           even_w_in, mlstm_gate_bias, swa_q_norm, swa_k_norm, swa_sinks, mlstm_out_norm, even_w_out, odd_w_in,
           ssd_conv_w, ssd_conv_b, ssd_dt_bias, ssd_a_log, ssd_d, ssd_norm, ret_norm, odd_w_out, mem_wq, mem_wk,
           mem_wv, mem_q_norm, mem_k_norm, mem_wo, ffn_w1, ffn_w2):
    bp, seq, d = x_prompt.shape
    bs = x_sample.shape[0]
    depth = norm_mix.shape[0]
    tm = 512

    pos_p = jnp.arange(seq, dtype=jnp.int32)
    pos_s = jnp.full((SUBLANES,), PAST_LEN, dtype=jnp.int32)
    tab16_p, tab16_s = _rope16_tables(pos_p), _rope16_tables(pos_s)
    tab128_p, tab128_s = _rope128_tables(pos_p), _rope128_tables(pos_s)
    dmat, q_scale, k_scale, chunk_decay, lg = _ret_consts()
    rdec_rows = _rows8(jnp.broadcast_to(jnp.exp(lg)[:, None], (RET_HEADS, LANES)))
    ii = jnp.arange(CHUNK)
    triu = (ii[:, None] <= ii[None, :]).astype(F32)
    tril = (ii[:, None] >= ii[None, :]).astype(F32)
    jj = jnp.arange(LANES)
    segm = jnp.where(jj[:, None] // SW_HEAD_DIM == jj[None, :] // SW_HEAD_DIM, 1.0 / SW_HEAD_DIM, 0.0).astype(F32)
    spread = (jj[:, None] == jnp.arange(SSD_INNER)[None, :] // SSD_HEAD_DIM).astype(F32)
    row1 = lambda t: t.reshape(1, -1).astype(F32)
    pad_lanes = lambda t: jnp.pad(t.reshape(1, -1).astype(F32), ((0, 0), (0, LANES - t.shape[-1])))

    yp = x_prompt.reshape(bp * seq, d)
    ys = x_sample.reshape(bs, d)
    mem = mem_prompt.reshape(bp * MEM_LEN, d)
    cmk = cache_mem_k.reshape(depth, bs, MEM_LEN * MEM_HEADS, MEM_HEAD_DIM)
    cmv = cache_mem_v.reshape(depth, bs, MEM_LEN * MEM_HEADS, MEM_HEAD_DIM)
    p_mk, p_mv = [], []
    outs = {}
    for l in range(depth):
        g_mix = row1(norm_mix[l])
        if l % 2 == 0:
            e = l // 2
            w_in = _even_w_in(even_w_in[e])
            w_out = even_w_out[e].astype(BF16)
            qn = row1(jnp.tile(swa_q_norm[e], SW_HEADS))
            kn = row1(jnp.tile(swa_k_norm[e], SW_KV_HEADS))
            gb = _gate_bias_rows(mlstm_gate_bias[e].astype(F32))
            onorm = row1(mlstm_out_norm[e])
            sinks = swa_sinks[e].astype(F32)
            mix_p, kc, vc, caug, mm = even_prompt(yp.reshape(bp, seq, d), g_mix, w_in, tab16_p, qn, kn, sinks, segm,
                                                  triu, gb, onorm)
            mix_p = mix_p.reshape(bp * seq, -1)
            outs["p_swk"] = kc.reshape(1, bp, CHUNK, SW_KV_HEADS, SW_HEAD_DIM)
            outs["p_swv"] = vc.reshape(1, bp, CHUNK, SW_KV_HEADS, SW_HEAD_DIM)
            outs["p_c"] = caug[..., :ML_V_DIM].reshape(1, bp, ML_HEADS, ML_QK_DIM, ML_V_DIM)
            outs["p_n"] = caug[..., ML_V_DIM].reshape(1, bp, ML_HEADS, ML_QK_DIM)
            outs["p_m"] = mm[:, :ML_HEADS, 0].reshape(1, bp, ML_HEADS)

            proj_s = norm_proj(ys, g_mix, w_in, tm=bs)
            sink_rows = jnp.broadcast_to(sinks[:, None], (SW_HEADS, LANES))
            onorm_rows = _rows8(mlstm_out_norm[e].astype(F32).reshape(ML_HEADS, ML_V_DIM))
            gb_lanes = jnp.concatenate([pad_lanes(mlstm_gate_bias[e][:ML_HEADS]),
                                        pad_lanes(mlstm_gate_bias[e][ML_HEADS:])], axis=0)
            mix_s, nk, nv, ncst, nn, nm = even_sample(
                proj_s,
                cache_swa_k[e].reshape(bs, CHUNK, LANES), cache_swa_v[e].reshape(bs, CHUNK, LANES),
                state_mlstm_C[e].reshape(bs, ML_HEADS * ML_QK_DIM, ML_V_DIM),
                state_mlstm_n[e].reshape(bs, ML_HEADS * ML_QK_DIM),
                jnp.pad(state_mlstm_m[e], ((0, 0), (0, LANES - ML_HEADS))),
                tab16_s, qn, kn, sink_rows, segm, gb_lanes, onorm_rows)
            outs["s_swk"] = nk.reshape(1, bs, CHUNK, SW_KV_HEADS, SW_HEAD_DIM)
            outs["s_swv"] = nv.reshape(1, bs, CHUNK, SW_KV_HEADS, SW_HEAD_DIM)
            outs["s_c"] = ncst.reshape(1, bs, ML_HEADS, ML_QK_DIM, ML_V_DIM)
            outs["s_n"] = nn.reshape(1, bs, ML_HEADS, ML_QK_DIM)
            outs["s_m"] = nm[:, :ML_HEADS].reshape(1, bs, ML_HEADS)
        else:
            o = l // 2
            w_in = _odd_w_in(odd_w_in[o])
            w_out = odd_w_out[o].astype(BF16)
            convw = ssd_conv_w[o].astype(F32)
            convb = row1(ssd_conv_b[o])
            dtb = pad_lanes(ssd_dt_bias[o])
            arow = pad_lanes(-jnp.exp(ssd_a_log[o].astype(F32)))
            drow = row1(jnp.repeat(ssd_d[o].astype(F32), SSD_HEAD_DIM))
            snorm = row1(ssd_norm[o])
            rnorm = row1(ret_norm[o])
            mix_p, ctail, sst, rst = odd_prompt(yp.reshape(bp, seq, d), g_mix, w_in, tab128_p, convw, convb, dtb, arow,
                                                drow, snorm, tril, (dmat, q_scale, k_scale, chunk_decay), rnorm,
                                                group=2)
            mix_p = mix_p.reshape(bp * seq, -1)
            outs["p_conv"] = ctail[:, SUBLANES - (SSD_CONV - 1):, :].reshape(1, bp, SSD_CONV - 1, SSD_CONV_DIM)
            outs["p_ssd"] = sst.reshape(1, bp, SSD_HEADS, SSD_HEAD_DIM, SSD_STATE)
            outs["p_ret"] = rst.reshape(1, bp, RET_HEADS, RET_QK_DIM, RET_V_DIM)

            proj_s = norm_proj(ys, g_mix, w_in, tm=bs)
            rnorm_rows = _rows8(ret_norm[o].astype(F32).reshape(RET_HEADS, RET_V_DIM))
            mix_s, ncb, ns, nr = odd_sample(
                proj_s, jnp.swapaxes(state_ssd_conv[o], 0, 1),
                state_ssd[o].reshape(bs, SSD_HEADS * SSD_HEAD_DIM, SSD_STATE),
                state_ret[o].reshape(bs, RET_HEADS * RET_QK_DIM, RET_V_DIM),
                tab128_s, convw, convb, dtb, arow, drow, snorm, spread, rdec_rows, rnorm_rows)
            outs["s_conv"] = jnp.swapaxes(ncb, 0, 1).reshape(1, bs, SSD_CONV - 1, SSD_CONV_DIM)
            outs["s_ssd"] = ns.reshape(1, bs, SSD_HEADS, SSD_HEAD_DIM, SSD_STATE)
            outs["s_ret"] = nr.reshape(1, bs, RET_HEADS, RET_QK_DIM, RET_V_DIM)

        wkv = jnp.concatenate([mem_wk[l], mem_wv[l]], axis=1).astype(BF16)
        qnorm = row1(mem_q_norm[l])
        mkv = norm_proj(mem, row1(norm_mem[l]), wkv, tm=tm, head_norm=row1(mem_k_norm[l]), head_norm_cols=MEM_WIDTH)
        p_mk.append(mkv[:, :MEM_WIDTH].reshape(bp, MEM_LEN, MEM_HEADS, MEM_HEAD_DIM))
        p_mv.append(mkv[:, MEM_WIDTH:].reshape(bp, MEM_LEN, MEM_HEADS, MEM_HEAD_DIM))
        wq, wo = mem_wq[l].astype(BF16), mem_wo[l].astype(BF16)
        gx, gf = row1(norm_xattn[l]), row1(norm_ffn[l])
        w1, w2 = ffn_w1[l].astype(BF16), ffn_w2[l].astype(BF16)
        yp = post_prompt(yp, mix_p, w_out, gx, wq, qnorm, mkv, wo, gf, w1, w2, rows_per_batch=seq, tm=tm)

        ys = res_proj(ys, mix_s, w_out)
        qs = norm_proj(ys, gx, wq, tm=bs, head_norm=qnorm, head_norm_cols=MEM_WIDTH)
        ys = res_proj_ffn(ys, xattn_sample(qs, cmk, cmv, l), wo, gf, w1, w2)

    return (yp.reshape(bp, seq, d), ys.reshape(bs, 1, d),
            jnp.stack(p_mk), jnp.stack(p_mv), outs["p_swk"], outs["p_swv"], outs["p_c"], outs["p_n"], outs["p_m"],
            outs["p_conv"], outs["p_ssd"], outs["p_ret"],
            outs["s_swk"], outs["s_swv"], outs["s_c"], outs["s_n"], outs["s_m"],
            outs["s_conv"], outs["s_ssd"], outs["s_ret"])
```

```python
import functools
import math

import jax
import jax.numpy as jnp
import numpy as np
from jax import lax
from jax.experimental import pallas as pl
from jax.experimental.pallas import tpu as pltpu

F32 = jnp.float32
BF16 = jnp.bfloat16

D_MODEL = 1024
PAST_LEN = 8192
EPS = 1e-6
CHUNK = 128
NEG = -1e30

SW_HEADS, SW_KV_HEADS, SW_HEAD_DIM, SW_ROT_DIM = 8, 2, 64, 16
ROPE_THETA = 500000.0
ML_HEADS, ML_QK_DIM, ML_V_DIM = 4, 64, 128
SSD_HEADS, SSD_HEAD_DIM, SSD_GROUPS, SSD_STATE, SSD_CONV = 16, 64, 2, 128, 4
SSD_INNER = SSD_HEADS * SSD_HEAD_DIM
SSD_CONV_DIM = SSD_INNER + 2 * SSD_GROUPS * SSD_STATE
RET_HEADS, RET_QK_DIM, RET_V_DIM = 4, 128, 128
RET_ROPE_THETA = 10000.0
MEM_LEN, MEM_HEADS, MEM_HEAD_DIM = 256, 4, 128
MEM_WIDTH = MEM_HEADS * MEM_HEAD_DIM
FFN_DIM = 4 * D_MODEL
FFN_CHUNK = 512
PROJ_COLS_PER_STAGE = 256

LANES = 128
SUBLANES = 8
VMEM_LIMIT_BYTES = 56 * 1024 * 1024

E_SQ, E_SK, E_SV, E_MQ, E_MK, E_MV, E_MO, E_GATE, EVEN_COLS = 0, 512, 640, 768, 1024, 1280, 1792, 2304, 2432
GATE_F_LANE = 8
O_Z, O_XBC, O_RQ, O_RK, O_RV, O_RG, O_DT, ODD_COLS = 0, 1024, 2560, 3072, 3584, 4096, 4608, 4736


def _mm(a, b):
    return jnp.dot(a.astype(BF16), b.astype(BF16), preferred_element_type=F32)


def _mm_nt(a, b):
    return lax.dot_general(a.astype(BF16), b.astype(BF16), (((1,), (1,)), ((), ())), preferred_element_type=F32)


def _mm_tn(a, b):
    return lax.dot_general(a.astype(BF16), b.astype(BF16), (((0,), (0,)), ((), ())), preferred_element_type=F32)


def _dg(a, b, ca, cb):
    return lax.dot_general(a, b, (((ca,), (cb,)), ((), ())), preferred_element_type=F32)


def _split3(x):
    hi = x.astype(BF16).astype(F32)
    r1 = x - hi
    mid = r1.astype(BF16).astype(F32)
    lo = (r1 - mid).astype(BF16).astype(F32)
    return hi, mid, lo


def _mm_exact_rhs(x, e):
    hi, mid, lo = _split3(x)
    return _dg(hi, e, 1, 0) + _dg(mid, e, 1, 0) + _dg(lo, e, 1, 0)


def _mm_exact_lhs(e, x):
    hi, mid, lo = _split3(x)
    return _dg(e, hi, 1, 0) + _dg(e, mid, 1, 0) + _dg(e, lo, 1, 0)


def _mm_tn_exact_lhs(x, e):
    hi, mid, lo = _split3(x)
    return _dg(hi, e, 0, 0) + _dg(mid, e, 0, 0) + _dg(lo, e, 0, 0)


def _rms(x, g):
    return x * lax.rsqrt(jnp.mean(x * x, axis=-1, keepdims=True) + EPS) * g


def _seg_rms_mxu(x, g, seg_mean):
    return x * lax.rsqrt(_mm_exact_rhs(x * x, seg_mean) + EPS) * g


def _seg_rms(x, g, low):
    xx = x * x
    s_lo = jnp.sum(jnp.where(low, xx, 0.0), axis=-1, keepdims=True)
    s_hi = jnp.sum(jnp.where(low, 0.0, xx), axis=-1, keepdims=True)
    return x * lax.rsqrt(jnp.where(low, s_lo, s_hi) * (1.0 / SW_HEAD_DIM) + EPS) * g


def _sigmoid(x):
    return 1.0 / (1.0 + jnp.exp(-x))


def _silu(x):
    return x * _sigmoid(x)


def _softplus(x):
    return jnp.maximum(x, 0.0) + jnp.log1p(jnp.exp(-jnp.abs(x)))


def _log_sigmoid(x):
    return -_softplus(-x)


def _rope16(x, cos, sin_lo, sin_hi):
    return x * cos + pltpu.roll(x, LANES - 8, 1) * sin_lo + pltpu.roll(x, 8, 1) * sin_hi


def _rope128(x, cos, sin):
    return x * cos + pltpu.roll(x, 64, 1) * sin


def _iota(shape, dim):
    return lax.broadcasted_iota(jnp.int32, shape, dim)


def _cummax_lanes(x):
    lane = _iota(x.shape, 1)
    shift = 1
    while shift < x.shape[1]:
        x = jnp.maximum(x, jnp.where(lane >= shift, pltpu.roll(x, shift, 1), -jnp.inf))
        shift *= 2
    return x


def _norm_proj_kernel(x_ref, g_ref, w_ref, hn_ref, o_ref, *, chunks, head_norm_cols):
    xn = _rms(x_ref[...], g_ref[...]).astype(BF16)
    for c0, cs in chunks:
        r = jnp.dot(xn, w_ref[:, c0:c0 + cs], preferred_element_type=F32)
        if c0 < head_norm_cols:
            parts = [_rms(r[:, i:i + LANES], hn_ref[...]) for i in range(0, cs, LANES)]
            r = jnp.concatenate(parts, axis=1)
        o_ref[:, c0:c0 + cs] = r


def _col_chunks(n, width=512):
    return tuple((c, min(width, n - c)) for c in range(0, n, width))


def norm_proj(x, g, w, *, tm, head_norm=None, head_norm_cols=0):
    n, d = x.shape
    m = w.shape[1]
    if head_norm is None:
        head_norm = jnp.ones((1, LANES), F32)
    kern = functools.partial(_norm_proj_kernel, chunks=_col_chunks(m), head_norm_cols=head_norm_cols)
    return pl.pallas_call(
        kern,
        grid=(n // tm,),
        in_specs=[
            pl.BlockSpec((tm, d), lambda i: (i, 0)),
            pl.BlockSpec((1, d), lambda i: (0, 0)),
            pl.BlockSpec((d, m), lambda i: (0, 0), pipeline_mode=pl.Buffered(1)),
            pl.BlockSpec((1, LANES), lambda i: (0, 0)),
        ],
        out_specs=pl.BlockSpec((tm, m), lambda i: (i, 0)),
        out_shape=jax.ShapeDtypeStruct((n, m), F32),
        compiler_params=pltpu.CompilerParams(dimension_semantics=("arbitrary",), vmem_limit_bytes=VMEM_LIMIT_BYTES),
        name="norm_proj",
    )(x, g, w, head_norm)


def _ffn(x, g_ref, w1_ref, w2_ref):
    h = _rms(x, g_ref[...]).astype(BF16)
    acc = None
    for c in range(0, FFN_DIM, FFN_CHUNK):
        u = jnp.maximum(jnp.dot(h, w1_ref[:, c:c + FFN_CHUNK], preferred_element_type=F32), 0.0)
        t = jnp.dot((u * u).astype(BF16), w2_ref[c:c + FFN_CHUNK, :], preferred_element_type=F32)
        acc = t if acc is None else acc + t
    return x + acc


def _post_prompt_kernel(x_ref, a_ref, wout_ref, gx_ref, wq_ref, qn_ref, mk_ref, mv_ref, wo_ref, gf_ref, w1_ref, w2_ref,
                        o_ref):
    x = x_ref[...] + _mm(a_ref[...], wout_ref[...])
    q = jnp.dot(_rms(x, gx_ref[...]).astype(BF16), wq_ref[...], preferred_element_type=F32)
    outs = []
    for h in range(MEM_HEADS):
        sl = slice(h * MEM_HEAD_DIM, (h + 1) * MEM_HEAD_DIM)
        qh = _rms(q[:, sl], qn_ref[...])
        s = _mm_nt(qh, mk_ref[:, sl]) * (MEM_HEAD_DIM ** -0.5)
        p = jnp.exp(s - jnp.max(s, axis=-1, keepdims=True))
        p = p / jnp.sum(p, axis=-1, keepdims=True)
        outs.append(_mm(p, mv_ref[:, sl]))
    x = x + _mm(jnp.concatenate(outs, axis=1), wo_ref[...])
    o_ref[...] = _ffn(x, gf_ref, w1_ref, w2_ref)


def post_prompt(x, a, wout, gx, wq, qn, mkv, wo, gf, w1, w2, *, layer, rows_per_batch, tm):
    n, d = x.shape
    ka = a.shape[1]
    tiles = rows_per_batch // tm
    const = lambda shape: pl.BlockSpec(shape, lambda i: (0, 0), pipeline_mode=pl.Buffered(1))
    slab = lambda shape: pl.BlockSpec((None,) + shape, lambda i: (layer, 0, 0), pipeline_mode=pl.Buffered(1))
    return pl.pallas_call(
        _post_prompt_kernel,
        grid=(n // tm,),
        in_specs=[
            pl.BlockSpec((tm, d), lambda i: (i, 0)),
            pl.BlockSpec((tm, ka), lambda i: (i, 0)),
            const((ka, d)),
            const((1, d)),
            const((d, MEM_WIDTH)),
            const((1, MEM_HEAD_DIM)),
            pl.BlockSpec((MEM_LEN, MEM_WIDTH), lambda i: (i // tiles, 0)),
            pl.BlockSpec((MEM_LEN, MEM_WIDTH), lambda i: (i // tiles, 1)),
            const((MEM_WIDTH, d)),
            const((1, d)),
            slab((d, FFN_DIM)),
            slab((FFN_DIM, d)),
        ],
        out_specs=pl.BlockSpec((tm, d), lambda i: (i, 0)),
        out_shape=jax.ShapeDtypeStruct((n, d), F32),
        compiler_params=pltpu.CompilerParams(dimension_semantics=("arbitrary",), vmem_limit_bytes=VMEM_LIMIT_BYTES),
        name="post_prompt",
    )(x, a, wout, gx, wq, qn, mkv, mkv, wo, gf, w1, w2)


def _res_proj_kernel(x_ref, a_ref, w_ref, o_ref):
    o_ref[...] = x_ref[...] + _mm(a_ref[...], w_ref[...])


def res_proj(x, a, w):
    n, d = x.shape
    return pl.pallas_call(
        _res_proj_kernel,
        out_shape=jax.ShapeDtypeStruct((n, d), F32),
        compiler_params=pltpu.CompilerParams(vmem_limit_bytes=VMEM_LIMIT_BYTES),
        name="res_proj",
    )(x, a, w)


def _res_proj_ffn_kernel(x_ref, a_ref, w_ref, gf_ref, w1_ref, w2_ref, o_ref):
    x = x_ref[...] + _mm(a_ref[...], w_ref[...])
    o_ref[...] = _ffn(x, gf_ref, w1_ref, w2_ref)


def res_proj_ffn(x, a, w, gf, w1, w2, *, layer):
    n, d = x.shape
    full = lambda t: pl.BlockSpec(t.shape, lambda i: (0, 0))
    slab = lambda shape: pl.BlockSpec((None,) + shape, lambda i: (layer, 0, 0), pipeline_mode=pl.Buffered(1))
    return pl.pallas_call(
        _res_proj_ffn_kernel,
        grid=(1,),
        in_specs=[full(x), full(a), full(w), full(gf), slab((d, FFN_DIM)), slab((FFN_DIM, d))],
        out_specs=pl.BlockSpec((n, d), lambda i: (0, 0)),
        out_shape=jax.ShapeDtypeStruct((n, d), F32),
        compiler_params=pltpu.CompilerParams(dimension_semantics=("arbitrary",), vmem_limit_bytes=VMEM_LIMIT_BYTES),
        name="res_proj_ffn",
    )(x, a, w, gf, w1, w2)


def _pair_rows(x, op):
    xb = jnp.broadcast_to(x, (SUBLANES, LANES))
    return op(xb, pltpu.roll(xb, SUBLANES // 2, 0))


def _xattn_sample_kernel(q_ref, mk_ref, mv_ref, o_ref):
    row = _iota((SUBLANES, LANES), 0)
    groups = MEM_LEN * MEM_HEADS // SUBLANES
    for i in range(SUBLANES):
        q8 = jnp.zeros((SUBLANES, LANES), F32)
        for h in range(MEM_HEADS):
            q8 = jnp.where(row % MEM_HEADS == h, q_ref[i:i + 1, h * MEM_HEAD_DIM:(h + 1) * MEM_HEAD_DIM], q8)
        k3 = mk_ref[0, i].reshape(groups, SUBLANES, LANES)
        s = jnp.sum(k3 * q8[None], axis=-1, keepdims=True) * (MEM_HEAD_DIM ** -0.5)
        mx = _pair_rows(jnp.max(s, axis=0), jnp.maximum)
        p = jnp.exp(s - mx[None, :, 0:1])
        den = _pair_rows(jnp.sum(p, axis=0), jnp.add)
        v3 = mv_ref[0, i].reshape(groups, SUBLANES, LANES)
        o8 = _pair_rows(jnp.sum(p * v3, axis=0), jnp.add) / den
        for h in range(MEM_HEADS):
            o_ref[i:i + 1, h * MEM_HEAD_DIM:(h + 1) * MEM_HEAD_DIM] = o8[h:h + 1]


def xattn_sample(q, mk, mv, layer):
    b = q.shape[0]
    mem = pl.BlockSpec((1, SUBLANES, MEM_LEN * MEM_HEADS, MEM_HEAD_DIM), lambda i: (layer, i, 0, 0))
    return pl.pallas_call(
        _xattn_sample_kernel,
        grid=(b // SUBLANES,),
        in_specs=[pl.BlockSpec((SUBLANES, MEM_WIDTH), lambda i: (i, 0)), mem, mem],
        out_specs=pl.BlockSpec((SUBLANES, MEM_WIDTH), lambda i: (i, 0)),
        out_shape=jax.ShapeDtypeStruct((b, MEM_WIDTH), F32),
        compiler_params=pltpu.CompilerParams(dimension_semantics=("arbitrary",), vmem_limit_bytes=VMEM_LIMIT_BYTES),
        name="xattn_sample",
    )(q, mk, mv)


def _proj_chain(x_ref, g_ref, w_ref, dst, width):
    group, _, d = x_ref.shape
    xn = _rms(x_ref[...].reshape(group * CHUNK, d), g_ref[...]).astype(BF16)
    yield
    cols = w_ref.shape[1]
    for c0 in range(0, cols, width):
        cs = min(width, cols - c0)
        r = jnp.dot(xn, w_ref[:, c0:c0 + cs], preferred_element_type=F32)
        for b in range(group):
            dst[b, :, c0:c0 + cs] = r[b * CHUNK:(b + 1) * CHUNK]
        yield


def _even_prompt_kernel(xn_ref, x0_ref, g_ref, w_ref, cos_ref, sinlo_ref, sinhi_ref, qn_ref, kn_ref, sink_ref, segm_ref,
                        triu_ref, gb_ref, onorm_ref,
                        y_ref, kc_ref, vc_ref, caug_ref, m_ref,
                        kprev, vprev, cst, mst, proj):
    n = pl.program_id(0)
    batch = xn_ref.shape[0]
    slot = lax.rem(n, 2)

    @pl.when(n == 0)
    def _():
        kprev[...] = jnp.zeros_like(kprev)
        vprev[...] = jnp.zeros_like(vprev)
        cst[...] = jnp.zeros_like(cst)
        mst[...] = jnp.zeros_like(mst)
        for _ in _proj_chain(x0_ref, g_ref, w_ref, proj.at[0], PROJ_COLS_PER_STAGE):
            pass

    p_ref = proj.at[slot]

    cos, sinlo, sinhi = cos_ref[...], sinlo_ref[...], sinhi_ref[...]
    lane = _iota((1, LANES), 1)
    low = lane < 64
    qi = _iota((CHUNK, 2 * CHUNK), 0)
    si = _iota((CHUNK, 2 * CHUNK), 1)
    valid = (si >= qi) & (si <= qi + CHUNK) & ((si >= CHUNK) | (n > 0))
    causal = _iota((CHUNK, CHUNK), 0) >= _iota((CHUNK, CHUNK), 1)
    ones_col = jnp.where(_iota((CHUNK, LANES), 1) == 0, 1.0, 0.0)
    row64 = _iota((CHUNK, 1), 0) < 64
    new_kv = _lockstep([
        _even_prompt_chunk(p_ref.at[b], y_ref.at[b], kprev.at[b], vprev.at[b], cst.at[b], mst.at[b],
                           (cos, sinlo, sinhi), qn_ref, kn_ref, sink_ref, segm_ref[...], triu_ref, gb_ref, onorm_ref,
                           low, valid, causal, ones_col, row64)
        for b in range(batch)] + [_proj_chain(xn_ref, g_ref, w_ref, proj.at[1 - slot], 2 * PROJ_COLS_PER_STAGE)],
        every=[1] * batch + [8])[:batch]

    @pl.when(n == pl.num_programs(0) - 1)
    def _():
        for b in range(batch):
            kc_ref[b], vc_ref[b] = new_kv[b]
        caug_ref[...] = cst[...]
        m_ref[...] = mst[...]


def _lockstep(chains, every=None):
    every = every or [1] * len(chains)
    results = [None] * len(chains)
    live = list(range(len(chains)))
    rnd = 0
    while live:
        for i in list(live):
            if rnd % every[i]:
                continue
            try:
                next(chains[i])
            except StopIteration as stop:
                results[i] = stop.value
                live.remove(i)
        rnd += 1
    return results


def _even_prompt_chunk(p_ref, y_ref, kprev, vprev, cst, mst, tabs, qn_ref, kn_ref, sink_ref, segm, triu_ref, gb_ref,
                       onorm_ref, low, valid, causal, ones_col, row64):
    cos, sinlo, sinhi = tabs
    k = _rope16(_seg_rms_mxu(p_ref[:, E_SK:E_SK + LANES], kn_ref[...], segm), cos, sinlo, sinhi)
    v = p_ref[:, E_SV:E_SV + LANES]
    kk = jnp.concatenate([kprev[...], k], axis=0)
    vv = jnp.concatenate([vprev[...], v], axis=0)
    kk_sw = pltpu.roll(kk, 64, 1)
    vv_sw = pltpu.roll(vv, 64, 1)
    kvar = {(0, 0): jnp.where(low, kk, 0.0), (0, 1): jnp.where(low, 0.0, kk_sw),
            (1, 0): jnp.where(low, kk_sw, 0.0), (1, 1): jnp.where(low, 0.0, kk)}
    vvar = {(0, 0): vv, (0, 1): vv_sw, (1, 0): vv_sw, (1, 1): vv}
    yield
    for j in range(SW_HEADS // 2):
        sl = slice(E_SQ + j * LANES, E_SQ + (j + 1) * LANES)
        qb = _rope16(_seg_rms_mxu(p_ref[:, sl], qn_ref[:, sl], segm), cos, sinlo, sinhi)
        yield
        halves = []
        for pos in range(2):
            h = 2 * j + pos
            kv = h // (SW_HEADS // SW_KV_HEADS)
            s = jnp.where(valid, _mm_nt(qb, kvar[(kv, pos)]) * (SW_HEAD_DIM ** -0.5), NEG)
            yield
            sink = sink_ref[h]
            m = jnp.maximum(jnp.max(s, axis=-1, keepdims=True), sink)
            pr = jnp.exp(s - m)
            yield
            pr = pr / (jnp.sum(pr, axis=-1, keepdims=True) + jnp.exp(sink - m))
            halves.append(_mm(pr, vvar[(kv, pos)]))
            yield
        y_ref[:, j * LANES:(j + 1) * LANES] = jnp.where(low, halves[0], halves[1]).astype(y_ref.dtype)
    kprev[...] = k
    vprev[...] = v
    yield

    gt = p_ref[:, E_GATE:E_GATE + LANES].T
    gi = gt[0:SUBLANES] + gb_ref[0:SUBLANES]
    fl = _log_sigmoid(gt[GATE_F_LANE:GATE_F_LANE + SUBLANES] + gb_ref[SUBLANES:2 * SUBLANES])
    yield
    fcum = _mm_exact_rhs(fl, triu_ref[...])
    dd = gi - fcum
    mprev = mst[...]
    yield
    mt = fcum + jnp.maximum(mprev, _cummax_lanes(dd))
    fend = jnp.broadcast_to(fcum[:, CHUNK - 1:CHUNK], fcum.shape)
    mend = jnp.broadcast_to(mt[:, CHUNK - 1:CHUNK], mt.shape)
    decay = jnp.exp(fend + mprev - mend)
    rows = jnp.concatenate([fcum - mt, jnp.exp(fcum + mprev - mt), jnp.exp(-mt), jnp.exp(fend - fcum + gi - mend),
                            jnp.zeros((CHUNK - 4 * SUBLANES, CHUNK), F32)], axis=0)
    yield
    cols = rows.T
    yield
    for j in range(ML_HEADS // 2):
        qblk = p_ref[:, E_MQ + j * LANES:E_MQ + (j + 1) * LANES]
        kblk = p_ref[:, E_MK + j * LANES:E_MK + (j + 1) * LANES] * (ML_QK_DIM ** -0.5)
        c_old = cst[j]
        upd = None
        for pos in range(2):
            h = 2 * j + pos
            msk = low if pos == 0 else jnp.logical_not(low)
            qm = jnp.where(msk, qblk, 0.0)
            logw = cols[:, h:h + 1] + dd[h:h + 1, :]
            w = jnp.exp(jnp.where(causal, logw, -jnp.inf))
            sqk = _mm_nt(qm, kblk) * w
            yield
            vh = p_ref[:, E_MV + h * LANES:E_MV + (h + 1) * LANES]
            qc = _mm(qm, c_old)
            yield
            cs = cols[:, SUBLANES + h:SUBLANES + h + 1]
            num = cs * qc[:, :ML_V_DIM] + _mm(sqk, vh)
            den = cs * qc[:, ML_V_DIM:ML_V_DIM + 1] + jnp.sum(sqk, axis=-1, keepdims=True)
            hh = num / jnp.maximum(jnp.abs(den), cols[:, 2 * SUBLANES + h:2 * SUBLANES + h + 1])
            yield
            hsl = slice(h * ML_V_DIM, (h + 1) * ML_V_DIM)
            hn = _rms(hh, onorm_ref[:, hsl])
            mo = p_ref[:, E_MO + h * ML_V_DIM:E_MO + (h + 1) * ML_V_DIM]
            y_ref[:, SW_HEADS * SW_HEAD_DIM + h * ML_V_DIM:SW_HEADS * SW_HEAD_DIM + (h + 1) * ML_V_DIM] = (
                hn * _sigmoid(mo)).astype(y_ref.dtype)
            kw = jnp.where(msk, kblk, 0.0) * cols[:, 3 * SUBLANES + h:3 * SUBLANES + h + 1]
            u = _mm_tn(kw, jnp.concatenate([vh, ones_col], axis=1))
            upd = u if upd is None else upd + u
            yield
        dec = jnp.where(row64, decay[2 * j:2 * j + 1, 0:1], decay[2 * j + 1:2 * j + 2, 0:1])
        cst[j] = dec * c_old + upd
    mst[...] = mend
    return k, v


def even_prompt(x, g, w, tabs, qn, kn, sinks, segm, triu, gb, onorm):
    batch, seq, d = x.shape
    nc = seq // CHUNK
    tab = pl.BlockSpec((CHUNK, LANES), lambda n: (n, 0))
    const = lambda shape: pl.BlockSpec(shape, lambda n: (0,) * len(shape))
    state_shapes = [(batch, CHUNK, LANES), (batch, CHUNK, LANES),
                    (batch, ML_HEADS // 2, 2 * ML_QK_DIM, 2 * ML_V_DIM), (batch, SUBLANES, LANES)]
    return pl.pallas_call(
        _even_prompt_kernel,
        grid=(nc,),
        in_specs=[
            pl.BlockSpec((batch, CHUNK, d), lambda n: (0, jnp.minimum(n + 1, nc - 1), 0)),
            pl.BlockSpec((batch, CHUNK, d), lambda n: (0, 0, 0)),
            const((1, d)), pl.BlockSpec((d, EVEN_COLS), lambda n: (0, 0), pipeline_mode=pl.Buffered(1)),
            tab, tab, tab,
            const((1, SW_HEADS * SW_HEAD_DIM)), const((1, LANES)),
            pl.BlockSpec(memory_space=pltpu.SMEM),
            const((LANES, LANES)), const((CHUNK, CHUNK)), const((2 * SUBLANES, LANES)),
            const((1, ML_HEADS * ML_V_DIM)),
        ],
        out_specs=[pl.BlockSpec((batch, CHUNK, D_MODEL), lambda n: (0, n, 0))] + [const(s) for s in state_shapes],
        out_shape=[jax.ShapeDtypeStruct((batch, seq, D_MODEL), BF16)]
        + [jax.ShapeDtypeStruct(s, F32) for s in state_shapes],
        scratch_shapes=[pltpu.VMEM(s, F32) for s in state_shapes] + [pltpu.VMEM((2, batch, CHUNK, EVEN_COLS), F32)],
        compiler_params=pltpu.CompilerParams(dimension_semantics=("arbitrary",), vmem_limit_bytes=VMEM_LIMIT_BYTES),
        name="even_prompt",
    )(x, x, g, w, *tabs, qn, kn, sinks, segm, triu, gb, onorm)


def _odd_prompt_kernel(xn_ref, x0_ref, g_ref, w_ref, cosr_ref, sinr_ref, convw_ref, convb_ref, dtb_ref, arow_ref,
                       drow_ref, snorm_ref, tril_ref, dmat_ref, qs_ref, ks_ref, cd_ref, rnorm_ref,
                       y_ref, conv_ref, s_ref, r_ref,
                       ext, sst, rst, proj):
    n = pl.program_id(1)
    batch = xn_ref.shape[0]
    slot = lax.rem(n, 2)

    @pl.when(n == 0)
    def _():
        ext[:, 0:SUBLANES] = jnp.zeros((batch, SUBLANES, SSD_CONV_DIM), F32)
        sst[...] = jnp.zeros_like(sst)
        rst[...] = jnp.zeros_like(rst)
        for _ in _proj_chain(x0_ref, g_ref, w_ref, proj.at[0], PROJ_COLS_PER_STAGE):
            pass

    p_ref = proj.at[slot]

    lane = _iota((1, LANES), 1)
    low = lane < 64
    row64 = _iota((CHUNK, 1), 0) < 64
    causal = _iota((CHUNK, CHUNK), 0) >= _iota((CHUNK, CHUNK), 1)
    tails = _lockstep([
        _odd_prompt_chunk(p_ref.at[b], y_ref.at[b], ext.at[b], sst.at[b], rst.at[b], cosr_ref, sinr_ref, convw_ref,
                          convb_ref, dtb_ref, arow_ref, drow_ref, snorm_ref, tril_ref, dmat_ref, qs_ref, ks_ref,
                          cd_ref, rnorm_ref, low, row64, causal)
        for b in range(batch)] + [_proj_chain(xn_ref, g_ref, w_ref, proj.at[1 - slot], PROJ_COLS_PER_STAGE)],
        every=[1] * batch + [2])[:batch]

    @pl.when(n == pl.num_programs(1) - 1)
    def _():
        for b in range(batch):
            conv_ref[b] = tails[b]
        s_ref[...] = sst[...]
        r_ref[...] = rst[...]


def _odd_prompt_chunk(p_ref, y_ref, ext, sst, rst, cosr_ref, sinr_ref, convw_ref, convb_ref, dtb_ref, arow_ref,
                      drow_ref, snorm_ref, tril_ref, dmat_ref, qs_ref, ks_ref, cd_ref, rnorm_ref, low, row64, causal):
    tail = SUBLANES
    ext[tail:tail + CHUNK] = p_ref[:, O_XBC:O_XBC + SSD_CONV_DIM]
    yield
    xe = ext[...]
    acc = None
    for jj in range(SSD_CONV):
        shift = SSD_CONV - 1 - jj
        tap = (pltpu.roll(xe, shift, 0) if shift else xe)[tail:tail + CHUNK] * convw_ref[jj:jj + 1]
        acc = tap if acc is None else acc + tap
    xact = _silu(acc + convb_ref[...])
    new_tail = ext[CHUNK:CHUNK + tail]
    ext[0:tail] = new_tail
    yield

    dt = _softplus(p_ref[:, O_DT:O_DT + LANES] + dtb_ref[...])
    cum = _mm_exact_lhs(tril_ref[...], dt * arow_ref[...])
    yield
    cum_t = cum.T
    dt_t = dt.T
    ecum = jnp.exp(cum)
    cend = cum[CHUNK - 1:CHUNK, :]
    wend = jnp.exp(cend - cum) * dt
    eend = jnp.exp(cend)
    yield
    pairs_per_group = SSD_HEADS // SSD_GROUPS // 2
    ys = []
    for g in range(SSD_GROUPS):
        bc = xact[:, SSD_INNER + g * SSD_STATE:SSD_INNER + (g + 1) * SSD_STATE]
        cc = xact[:, SSD_INNER + (SSD_GROUPS + g) * SSD_STATE:SSD_INNER + (SSD_GROUPS + g + 1) * SSD_STATE]
        cb = _mm_nt(cc, bc)
        yield
        for jg in range(pairs_per_group):
            j = g * pairs_per_group + jg
            ha, hb = 2 * j, 2 * j + 1
            xp = xact[:, j * LANES:(j + 1) * LANES]
            s_old = sst[j]
            y = jnp.where(low, ecum[:, ha:ha + 1], ecum[:, hb:hb + 1]) * _mm_nt(cc, s_old)
            yield
            for pos, h in ((0, ha), (1, hb)):
                seg = cum[:, h:h + 1] - cum_t[h:h + 1, :]
                wmat = cb * jnp.exp(jnp.where(causal, seg, -jnp.inf)) * dt_t[h:h + 1, :]
                y = y + _mm(wmat, jnp.where(low if pos == 0 else jnp.logical_not(low), xp, 0.0))
                yield
            xw = xp * jnp.where(low, wend[:, ha:ha + 1], wend[:, hb:hb + 1])
            sst[j] = jnp.where(row64, eend[:, ha:ha + 1], eend[:, hb:hb + 1]) * s_old + _mm_tn(xw, bc)
            ys.append(y)
            yield
        gs = slice(g * SSD_INNER // SSD_GROUPS, (g + 1) * SSD_INNER // SSD_GROUPS)
        yg = jnp.concatenate(ys[g * pairs_per_group:(g + 1) * pairs_per_group], axis=1)
        yg = (yg + drow_ref[:, gs] * xact[:, gs]) * _silu(p_ref[:, O_Z + gs.start:O_Z + gs.stop])
        y_ref[:, gs] = _rms(yg, snorm_ref[:, gs]).astype(y_ref.dtype)
        yield

    cosr, sinr = cosr_ref[...], sinr_ref[...]
    for h in range(RET_HEADS):
        hs = h * LANES
        q = _rope128(p_ref[:, O_RQ + hs:O_RQ + hs + LANES], cosr, sinr)
        k = _rope128(p_ref[:, O_RK + hs:O_RK + hs + LANES], cosr, sinr) * (RET_QK_DIM ** -0.5)
        v = p_ref[:, O_RV + hs:O_RV + hs + LANES]
        yield
        r_old = rst[h]
        o = _mm(_mm_nt(q, k) * dmat_ref[h], v) + qs_ref[h] * _mm(q, r_old)
        yield
        rst[h] = cd_ref[h] * r_old + _mm_tn(k * ks_ref[h], v)
        xc = o - jnp.mean(o, axis=-1, keepdims=True)
        yn = xc * lax.rsqrt(jnp.mean(xc * xc, axis=-1, keepdims=True) + EPS) * rnorm_ref[:, hs:hs + LANES]
        y_ref[:, SSD_INNER + hs:SSD_INNER + hs + LANES] = (
            yn * _silu(p_ref[:, O_RG + hs:O_RG + hs + LANES])).astype(y_ref.dtype)
        yield
    return new_tail


def odd_prompt(x, g, w, tabs, convw, convb, dtb, arow, drow, snorm, tril, ret_consts, rnorm, *, group):
    batch, seq, d = x.shape
    nc = seq // CHUNK
    tab = pl.BlockSpec((CHUNK, LANES), lambda g, n: (n, 0))
    const = lambda shape: pl.BlockSpec(shape, lambda g, n: (0,) * len(shape))
    per_g = lambda shape: pl.BlockSpec((group,) + shape, lambda g, n: (g,) + (0,) * len(shape))
    hc = (RET_HEADS, CHUNK, LANES)
    ywidth = SSD_INNER + RET_HEADS * RET_V_DIM
    states = [(SUBLANES, SSD_CONV_DIM), (SSD_HEADS // 2, LANES, SSD_STATE), hc]
    return pl.pallas_call(
        _odd_prompt_kernel,
        grid=(batch // group, seq // CHUNK),
        in_specs=[
            pl.BlockSpec((group, CHUNK, d), lambda g, n: (g, jnp.minimum(n + 1, nc - 1), 0)),
            pl.BlockSpec((group, CHUNK, d), lambda g, n: (g, 0, 0)),
            const((1, d)), pl.BlockSpec((d, ODD_COLS), lambda g, n: (0, 0), pipeline_mode=pl.Buffered(1)),
            tab, tab,
            const((SSD_CONV, SSD_CONV_DIM)), const((1, SSD_CONV_DIM)), const((1, LANES)), const((1, LANES)),
            const((1, SSD_INNER)), const((1, SSD_INNER)), const((CHUNK, CHUNK)),
            const(hc), const(hc), const(hc), const(hc), const((1, RET_HEADS * RET_V_DIM)),
        ],
        out_specs=[pl.BlockSpec((group, CHUNK, ywidth), lambda g, n: (g, n, 0))] + [per_g(s) for s in states],
        out_shape=[jax.ShapeDtypeStruct((batch, seq, ywidth), BF16)]
        + [jax.ShapeDtypeStruct((batch,) + s, F32) for s in states],
        scratch_shapes=[pltpu.VMEM((group, SUBLANES + CHUNK, SSD_CONV_DIM), F32),
                        pltpu.VMEM((group,) + states[1], F32), pltpu.VMEM((group,) + states[2], F32),
                        pltpu.VMEM((2, group, CHUNK, ODD_COLS), F32)],
        compiler_params=pltpu.CompilerParams(dimension_semantics=("arbitrary", "arbitrary"),
                                             vmem_limit_bytes=VMEM_LIMIT_BYTES),
        name="odd_prompt",
    )(x, x, g, w, *tabs, convw, convb, dtb, arow, drow, snorm, tril, *ret_consts, rnorm)


def _lane_to_rows(g, offset):
    sel = _iota(g.shape, 1) == _iota(g.shape, 0) + offset
    return jnp.sum(jnp.where(sel, g, 0.0), axis=-1, keepdims=True)


def _block_rows(x, i, nblk, blk):
    row = _iota((SUBLANES, blk), 0)
    out = jnp.zeros((SUBLANES, blk), F32)
    for b in range(nblk):
        out = jnp.where(row == b, x[i:i + 1, b * blk:(b + 1) * blk], out)
    return out


def _even_sample_kernel(p_ref, bk_ref, bv_ref, c_ref, nrow_ref, mrow_ref, cos_ref, sinlo_ref, sinhi_ref, qn_ref,
                        kn_ref, sink_ref, gb_ref, onorm_ref,
                        y_ref, nk_ref, nv_ref, nc_ref, nn_ref, nm_ref):
    R = SUBLANES
    cos, sinlo, sinhi = cos_ref[...], sinlo_ref[...], sinhi_ref[...]
    row = _iota((R, LANES), 0)
    lane = _iota((R, LANES), 1)
    low = lane < 64
    group = SW_HEADS // SW_KV_HEADS
    scale = SW_HEAD_DIM ** -0.5
    sink = sink_ref[:, 0:1]
    last = _iota((CHUNK, LANES), 0) == CHUNK - 1

    k = _rope16(_seg_rms(p_ref[:, E_SK:E_SK + LANES], kn_ref[...], low), cos, sinlo, sinhi)
    v = p_ref[:, E_SV:E_SV + LANES]
    qb, qb_sw = [], []
    for j in range(SW_HEADS // 2):
        sl = slice(E_SQ + j * LANES, E_SQ + (j + 1) * LANES)
        qb.append(_rope16(_seg_rms(p_ref[:, sl], qn_ref[:, sl], low), cos, sinlo, sinhi))
        qb_sw.append(pltpu.roll(qb[j], 64, 1))

    g = p_ref[:, E_GATE:E_GATE + LANES]
    ic = g + gb_ref[0:1]
    fl = _log_sigmoid(pltpu.roll(g, LANES - GATE_F_LANE, 1) + gb_ref[1:2])
    mprev = mrow_ref[...]
    mt = jnp.maximum(fl + mprev, ic)
    w_all = jnp.exp(ic - mt)
    cs_all = jnp.exp(fl + mprev - mt)
    em_all = jnp.exp(-mt)
    nm_ref[...] = mt
    width = ML_HEADS * ML_QK_DIM
    own = _iota((R, width), 1) // ML_QK_DIM == _iota((R, width), 0)
    kscaled = p_ref[:, E_MK:E_MK + width] * (ML_QK_DIM ** -0.5)

    for i in range(R):
        qm = jnp.zeros((R, LANES), F32)
        for j in range(SW_HEADS // 2):
            for pos in range(2):
                h = 2 * j + pos
                kv = h // group
                src = (qb[j] if pos == kv else qb_sw[j])[i:i + 1]
                qm = jnp.where((row == h) & (low if kv == 0 else jnp.logical_not(low)), src, qm)
        bk, bv = bk_ref[i], bv_ref[i]
        ki, vi = k[i:i + 1], v[i:i + 1]
        s = _dg(qm, bk, 1, 1) * scale
        s_new = jnp.sum(qm * ki, axis=-1, keepdims=True) * scale
        m = jnp.maximum(jnp.maximum(jnp.max(s, axis=-1, keepdims=True), s_new), sink)
        pr = jnp.exp(s - m)
        p_new = jnp.exp(s_new - m)
        den = jnp.sum(pr, axis=-1, keepdims=True) + p_new + jnp.exp(sink - m)
        o = (_dg(pr, bv, 1, 0) + p_new * vi) / den
        o_sw = pltpu.roll(o, 64, 1)
        for j in range(SW_HEADS // 2):
            halves = []
            for pos in range(2):
                h = 2 * j + pos
                halves.append((o if pos == h // group else o_sw)[h:h + 1, :])
            y_ref[i:i + 1, j * LANES:(j + 1) * LANES] = jnp.where(low[0:1], halves[0], halves[1])
        nk_ref[i] = jnp.where(last, ki, pltpu.roll(bk, CHUNK - 1, 0))
        nv_ref[i] = jnp.where(last, vi, pltpu.roll(bv, CHUNK - 1, 0))

        w = _lane_to_rows(jnp.broadcast_to(w_all[i:i + 1], (R, LANES)), 0)
        cs = _lane_to_rows(jnp.broadcast_to(cs_all[i:i + 1], (R, LANES)), 0)
        em = _lane_to_rows(jnp.broadcast_to(em_all[i:i + 1], (R, LANES)), 0)
        qrows = jnp.where(own, p_ref[i:i + 1, E_MQ:E_MQ + width], 0.0)
        krows = jnp.where(own, kscaled[i:i + 1], 0.0)
        c_old = c_ref[i]
        qc = _dg(qrows, c_old, 1, 0)
        qn_dot = jnp.sum(qrows * nrow_ref[i:i + 1], axis=-1, keepdims=True)
        sqk = jnp.sum(qrows * krows, axis=-1, keepdims=True) * w
        v4 = _block_rows(p_ref[:, E_MV:E_MV + ML_HEADS * ML_V_DIM], i, ML_HEADS, ML_V_DIM)
        mo4 = _block_rows(p_ref[:, E_MO:E_MO + ML_HEADS * ML_V_DIM], i, ML_HEADS, ML_V_DIM)
        num = cs * qc + sqk * v4
        dn = cs * qn_dot + sqk
        hh = num / jnp.maximum(jnp.abs(dn), em)
        hn = _rms(hh, onorm_ref[...]) * _sigmoid(mo4)
        for h in range(ML_HEADS):
            c0 = SW_HEADS * SW_HEAD_DIM + h * ML_V_DIM
            y_ref[i:i + 1, c0:c0 + ML_V_DIM] = hn[h:h + 1]
        dec_col = jnp.concatenate(
            [jnp.broadcast_to(cs[h:h + 1, 0:1], (ML_QK_DIM, ML_V_DIM)) for h in range(ML_HEADS)], axis=0)
        nc_ref[i] = dec_col * c_old + _dg(krows * w, v4, 0, 0)
        dec_lanes = jnp.sum(jnp.where(own, cs, 0.0), axis=0, keepdims=True)
        nn_ref[i:i + 1] = dec_lanes * nrow_ref[i:i + 1] + jnp.sum(krows * w, axis=0, keepdims=True)


def even_sample(proj, bk, bv, c, nrow, mrow, tabs, qn, kn, sink_rows, gb, onorm_rows):
    b = proj.shape[0]
    R = SUBLANES
    width = ML_HEADS * ML_QK_DIM
    rows = lambda w: pl.BlockSpec((R, w), lambda i: (i, 0))
    per_b = lambda shape: pl.BlockSpec((R,) + shape, lambda i: (i,) + (0,) * len(shape))
    const = lambda shape: pl.BlockSpec(shape, lambda i: (0,) * len(shape))
    tab = const((R, LANES))
    return pl.pallas_call(
        _even_sample_kernel,
        grid=(b // R,),
        in_specs=[
            rows(EVEN_COLS), per_b((CHUNK, LANES)), per_b((CHUNK, LANES)), per_b((width, ML_V_DIM)),
            rows(width), rows(LANES), tab, tab, tab,
            const((1, SW_HEADS * SW_HEAD_DIM)), const((1, LANES)), const((R, LANES)),
            const((2, LANES)), const((R, ML_V_DIM)),
        ],
        out_specs=[
            rows(D_MODEL), per_b((CHUNK, LANES)), per_b((CHUNK, LANES)), per_b((width, ML_V_DIM)),
            rows(width), rows(LANES),
        ],
        out_shape=[
            jax.ShapeDtypeStruct((b, D_MODEL), F32),
            jax.ShapeDtypeStruct((b, CHUNK, LANES), F32), jax.ShapeDtypeStruct((b, CHUNK, LANES), F32),
            jax.ShapeDtypeStruct((b, width, ML_V_DIM), F32), jax.ShapeDtypeStruct((b, width), F32),
            jax.ShapeDtypeStruct((b, LANES), F32),
        ],
        compiler_params=pltpu.CompilerParams(dimension_semantics=("arbitrary",), vmem_limit_bytes=VMEM_LIMIT_BYTES),
        name="even_sample",
    )(proj, bk, bv, c, nrow, mrow, *tabs, qn, kn, sink_rows, gb, onorm_rows)


def _odd_sample_kernel(p_ref, cb_ref, s_ref, r_ref, cosr_ref, sinr_ref, convw_ref, convb_ref, dtb_ref, arow_ref,
                       drow_ref, snorm_ref, spread_ref, rdec_ref, rnorm_ref,
                       y_ref, ncb_ref, ns_ref, nr_ref):
    R = SUBLANES
    xbc = p_ref[:, O_XBC:O_XBC + SSD_CONV_DIM]
    acc = cb_ref[0] * convw_ref[0:1]
    for jj in range(1, SSD_CONV - 1):
        acc = acc + cb_ref[jj] * convw_ref[jj:jj + 1]
    acc = acc + xbc * convw_ref[SSD_CONV - 1:SSD_CONV]
    xact = _silu(acc + convb_ref[...])
    for jj in range(SSD_CONV - 2):
        ncb_ref[jj] = cb_ref[jj + 1]
    ncb_ref[SSD_CONV - 2] = xbc

    dt = _softplus(p_ref[:, O_DT:O_DT + LANES] + dtb_ref[...])
    dec = jnp.exp(dt * arow_ref[...])
    xs = xact[:, 0:SSD_INNER]
    xdt = xs * _mm_exact_rhs(dt, spread_ref[...])
    gown = _iota((R, SSD_INNER), 1) // (SSD_INNER // SSD_GROUPS) == _iota((R, SSD_INNER), 0)
    bpart = xact[:, SSD_INNER:SSD_INNER + SSD_GROUPS * SSD_STATE]
    cpart = xact[:, SSD_INNER + SSD_GROUPS * SSD_STATE:SSD_CONV_DIM]

    cosr, sinr = cosr_ref[...], sinr_ref[...]
    width = RET_HEADS * RET_QK_DIM
    q4 = jnp.concatenate([_rope128(p_ref[:, O_RQ + h * LANES:O_RQ + (h + 1) * LANES], cosr, sinr)
                          for h in range(RET_HEADS)], axis=1)
    k4 = jnp.concatenate([_rope128(p_ref[:, O_RK + h * LANES:O_RK + (h + 1) * LANES], cosr, sinr)
                          for h in range(RET_HEADS)], axis=1) * (RET_QK_DIM ** -0.5)
    own = _iota((R, width), 1) // RET_QK_DIM == _iota((R, width), 0)
    gam = rdec_ref[:, 0:1]
    gam_col = jnp.concatenate(
        [jnp.broadcast_to(rdec_ref[h:h + 1, :], (RET_QK_DIM, RET_V_DIM)) for h in range(RET_HEADS)], axis=0)

    ys = []
    for i in range(R):
        brows = _block_rows(bpart, i, SSD_GROUPS, SSD_STATE)
        crows = _block_rows(cpart, i, SSD_GROUPS, SSD_STATE)
        xw = jnp.where(gown, xdt[i:i + 1], 0.0)
        dec_col = jnp.concatenate(
            [jnp.broadcast_to(dec[i:i + 1, h:h + 1], (SSD_HEAD_DIM, SSD_STATE)) for h in range(SSD_HEADS)], axis=0)
        s_new = dec_col * s_ref[i] + _dg(xw, brows, 0, 0)
        ns_ref[i] = s_new
        yrows = _dg(crows, s_new, 1, 1)
        ys.append(jnp.sum(jnp.where(gown, yrows, 0.0), axis=0, keepdims=True))

        qrows = jnp.where(own, q4[i:i + 1], 0.0)
        krows = jnp.where(own, k4[i:i + 1], 0.0)
        v4 = _block_rows(p_ref[:, O_RV:O_RV + RET_HEADS * RET_V_DIM], i, RET_HEADS, RET_V_DIM)
        g4 = _block_rows(p_ref[:, O_RG:O_RG + RET_HEADS * RET_V_DIM], i, RET_HEADS, RET_V_DIM)
        r_old = r_ref[i]
        att = jnp.sum(qrows * krows, axis=-1, keepdims=True)
        o = att * v4 + gam * _dg(qrows, r_old, 1, 0)
        nr_ref[i] = gam_col * r_old + _dg(krows, v4, 0, 0)
        xc = o - jnp.mean(o, axis=-1, keepdims=True)
        yn = xc * lax.rsqrt(jnp.mean(xc * xc, axis=-1, keepdims=True) + EPS) * rnorm_ref[...] * _silu(g4)
        for h in range(RET_HEADS):
            y_ref[i:i + 1, SSD_INNER + h * RET_V_DIM:SSD_INNER + (h + 1) * RET_V_DIM] = yn[h:h + 1]

    y = (jnp.concatenate(ys, axis=0) + drow_ref[...] * xs) * _silu(p_ref[:, O_Z:O_Z + SSD_INNER])
    gw = SSD_INNER // SSD_GROUPS
    for g in range(SSD_GROUPS):
        y_ref[:, g * gw:(g + 1) * gw] = _rms(y[:, g * gw:(g + 1) * gw], snorm_ref[:, g * gw:(g + 1) * gw])


def odd_sample(proj, cbuf, s, r, tabs, convw, convb, dtb, arow, drow, snorm, spread, rdec_rows, rnorm_rows):
    b = proj.shape[0]
    R = SUBLANES
    rows = lambda w: pl.BlockSpec((R, w), lambda i: (i, 0))
    per_b = lambda shape: pl.BlockSpec((R,) + shape, lambda i: (i,) + (0,) * len(shape))
    const = lambda shape: pl.BlockSpec(shape, lambda i: (0,) * len(shape))
    tab = const((R, LANES))
    conv = pl.BlockSpec((SSD_CONV - 1, R, SSD_CONV_DIM), lambda i: (0, i, 0))
    ywidth = SSD_INNER + RET_HEADS * RET_V_DIM
    sshape = (SSD_HEADS * SSD_HEAD_DIM, SSD_STATE)
    rshape = (RET_HEADS * RET_QK_DIM, RET_V_DIM)
    return pl.pallas_call(
        _odd_sample_kernel,
        grid=(b // R,),
        in_specs=[
            rows(ODD_COLS), conv, per_b(sshape), per_b(rshape), tab, tab,
            const((SSD_CONV, SSD_CONV_DIM)), const((1, SSD_CONV_DIM)), const((1, LANES)), const((1, LANES)),
            const((1, SSD_INNER)), const((1, SSD_INNER)), const((LANES, SSD_INNER)),
            const((R, LANES)), const((R, RET_V_DIM)),
        ],
        out_specs=[rows(ywidth), conv, per_b(sshape), per_b(rshape)],
        out_shape=[
            jax.ShapeDtypeStruct((b, ywidth), F32),
            jax.ShapeDtypeStruct((SSD_CONV - 1, b, SSD_CONV_DIM), F32),
            jax.ShapeDtypeStruct((b,) + sshape, F32), jax.ShapeDtypeStruct((b,) + rshape, F32),
        ],
        compiler_params=pltpu.CompilerParams(dimension_semantics=("arbitrary",), vmem_limit_bytes=VMEM_LIMIT_BYTES),
        name="odd_sample",
    )(proj, cbuf, s, r, *tabs, convw, convb, dtb, arow, drow, snorm, spread, rdec_rows, rnorm_rows)


def _pad_cols(w, n):
    return jnp.pad(w, ((0, 0), (0, n - w.shape[1])))


def _even_w_in(w):
    sq, sk, sv, mq, mk, mv, mo, mi, mf = jnp.split(w, [512, 640, 768, 1024, 1280, 1792, 2304, 2308], axis=1)
    gates = jnp.concatenate([_pad_cols(mi, GATE_F_LANE), _pad_cols(mf, LANES - GATE_F_LANE)], axis=1)
    return jnp.concatenate([sq, sk, sv, mq, mk, mv, mo, gates], axis=1).astype(BF16)


def _odd_w_in(w):
    z, xbc, dt, rq, rk, rv, rg = jnp.split(w, [1024, 2560, 2576, 3088, 3600, 4112], axis=1)
    return jnp.concatenate([z, xbc, rq, rk, rv, rg, _pad_cols(dt, LANES)], axis=1).astype(BF16)


def _lane_angles(pos, rot_dim, theta, freq_of_lane):
    half = rot_dim // 2
    inv = jnp.power(jnp.float32(theta), -jnp.arange(half, dtype=F32) * (2.0 / rot_dim))
    return jnp.asarray(pos).astype(F32)[:, None] * inv[freq_of_lane][None, :]


def _rope16_tables(pos):
    half = SW_ROT_DIM // 2
    d = np.arange(LANES) % SW_HEAD_DIM
    ang = _lane_angles(pos, SW_ROT_DIM, ROPE_THETA, d % half)
    cos, sin = jnp.cos(ang), jnp.sin(ang)
    return (jnp.where(d < SW_ROT_DIM, cos, 1.0), jnp.where(d < half, -sin, 0.0),
            jnp.where((d >= half) & (d < SW_ROT_DIM), sin, 0.0))


def _rope128_tables(pos):
    half = RET_QK_DIM // 2
    lane = np.arange(LANES)
    ang = _lane_angles(pos, RET_QK_DIM, RET_ROPE_THETA, lane % half)
    return jnp.cos(ang), jnp.where(lane < half, -jnp.sin(ang), jnp.sin(ang))


def _ret_consts():
    L = CHUNK
    f = np.float32
    lg = np.log(f(1.0) - np.exp2(f(-5.0) - np.arange(RET_HEADS, dtype=f))).astype(f)
    idx = np.arange(L, dtype=f)
    diff = idx[:, None] - idx[None, :]
    with np.errstate(invalid="ignore"):
        dmat = np.exp(np.where(diff >= 0, diff[None] * lg[:, None, None], -np.inf)).astype(f)
    q_scale = np.exp((idx[None] + f(1.0)) * lg[:, None]).astype(f)
    k_scale = np.exp((f(L) - f(1.0) - idx[None]) * lg[:, None]).astype(f)
    chunk_decay = np.exp(f(L) * lg).astype(f)
    bc = lambda t: np.ascontiguousarray(np.broadcast_to(t[:, :, None], (RET_HEADS, L, LANES)))
    cd = np.ascontiguousarray(np.broadcast_to(chunk_decay[:, None, None], (RET_HEADS, L, LANES)))
    return dmat, bc(q_scale), bc(k_scale), cd, lg


def _rows8(t):
    return jnp.pad(t, ((0, SUBLANES - t.shape[0]), (0, 0)))


def _gate_bias_rows(gb):
    ib = jnp.broadcast_to(gb[:ML_HEADS, None], (ML_HEADS, LANES))
    fb = jnp.broadcast_to(gb[ML_HEADS:, None], (ML_HEADS, LANES))
    return jnp.concatenate([_rows8(ib), _rows8(fb)], axis=0)


def kernel(x_prompt, x_sample, cache_mem_k, cache_mem_v, cache_swa_k, cache_swa_v, state_mlstm_C, state_mlstm_n,
           state_mlstm_m, state_ssd_conv, state_ssd, state_ret, mem_prompt, norm_mix, norm_xattn, norm_mem, norm_ffn,
           even_w_in, mlstm_gate_bias, swa_q_norm, swa_k_norm, swa_sinks, mlstm_out_norm, even_w_out, odd_w_in,
           ssd_conv_w, ssd_conv_b, ssd_dt_bias, ssd_a_log, ssd_d, ssd_norm, ret_norm, odd_w_out, mem_wq, mem_wk,
           mem_wv, mem_q_norm, mem_k_norm, mem_wo, ffn_w1, ffn_w2):
    bp, seq, d = x_prompt.shape
    bs = x_sample.shape[0]
    depth = norm_mix.shape[0]
    tm = 512

    pos_p = np.arange(seq, dtype=np.int32)
    pos_s = np.full((SUBLANES,), PAST_LEN, dtype=np.int32)
    tab16_p, tab16_s = _rope16_tables(pos_p), _rope16_tables(pos_s)
    tab128_p, tab128_s = _rope128_tables(pos_p), _rope128_tables(pos_s)
    dmat, q_scale, k_scale, chunk_decay, lg = _ret_consts()
    rdec_rows = _rows8(jnp.asarray(np.broadcast_to(np.exp(lg)[:, None], (RET_HEADS, LANES))))
    ii = np.arange(CHUNK)
    triu = (ii[:, None] <= ii[None, :]).astype(np.float32)
    tril = (ii[:, None] >= ii[None, :]).astype(np.float32)
    jj = np.arange(LANES)
    segm = np.where(jj[:, None] // SW_HEAD_DIM == jj[None, :] // SW_HEAD_DIM, 1.0 / SW_HEAD_DIM, 0.0).astype(np.float32)
    spread = (jj[:, None] == np.arange(SSD_INNER)[None, :] // SSD_HEAD_DIM).astype(np.float32)
    row1 = lambda t: t.reshape(1, -1).astype(F32)
    pad_lanes = lambda t: jnp.pad(t.reshape(1, -1).astype(F32), ((0, 0), (0, LANES - t.shape[-1])))

    yp = x_prompt.reshape(bp * seq, d)
    ys = x_sample.reshape(bs, d)
    mem = mem_prompt.reshape(bp * MEM_LEN, d)
    cmk = cache_mem_k.reshape(depth, bs, MEM_LEN * MEM_HEADS, MEM_HEAD_DIM)
    cmv = cache_mem_v.reshape(depth, bs, MEM_LEN * MEM_HEADS, MEM_HEAD_DIM)
    w1s, w2s = ffn_w1.astype(BF16), ffn_w2.astype(BF16)
    p_mk, p_mv = [], []
    outs = {}
    for l in range(depth):
        g_mix = row1(norm_mix[l])
        if l % 2 == 0:
            e = l // 2
            w_in = _even_w_in(even_w_in[e])
            w_out = even_w_out[e].astype(BF16)
            qn = row1(jnp.tile(swa_q_norm[e], SW_HEADS))
            kn = row1(jnp.tile(swa_k_norm[e], SW_KV_HEADS))
            gb = _gate_bias_rows(mlstm_gate_bias[e].astype(F32))
            onorm = row1(mlstm_out_norm[e])
            sinks = swa_sinks[e].astype(F32)
            mix_p, kc, vc, caug, mm = even_prompt(yp.reshape(bp, seq, d), g_mix, w_in, tab16_p, qn, kn, sinks, segm,
                                                  triu, gb, onorm)
            mix_p = mix_p.reshape(bp * seq, -1)
            outs["p_swk"] = kc.reshape(1, bp, CHUNK, SW_KV_HEADS, SW_HEAD_DIM)
            outs["p_swv"] = vc.reshape(1, bp, CHUNK, SW_KV_HEADS, SW_HEAD_DIM)
            outs["p_c"] = caug[..., :ML_V_DIM].reshape(1, bp, ML_HEADS, ML_QK_DIM, ML_V_DIM)
            outs["p_n"] = caug[..., ML_V_DIM].reshape(1, bp, ML_HEADS, ML_QK_DIM)
            outs["p_m"] = mm[:, :ML_HEADS, 0].reshape(1, bp, ML_HEADS)

            proj_s = norm_proj(ys, g_mix, w_in, tm=bs)
            sink_rows = jnp.broadcast_to(sinks[:, None], (SW_HEADS, LANES))
            onorm_rows = _rows8(mlstm_out_norm[e].astype(F32).reshape(ML_HEADS, ML_V_DIM))
            gb_lanes = jnp.concatenate([pad_lanes(mlstm_gate_bias[e][:ML_HEADS]),
                                        pad_lanes(mlstm_gate_bias[e][ML_HEADS:])], axis=0)
            mix_s, nk, nv, ncst, nn, nm = even_sample(
                proj_s,
                cache_swa_k[e].reshape(bs, CHUNK, LANES), cache_swa_v[e].reshape(bs, CHUNK, LANES),
                state_mlstm_C[e].reshape(bs, ML_HEADS * ML_QK_DIM, ML_V_DIM),
                state_mlstm_n[e].reshape(bs, ML_HEADS * ML_QK_DIM),
                jnp.pad(state_mlstm_m[e], ((0, 0), (0, LANES - ML_HEADS))),
                tab16_s, qn, kn, sink_rows, gb_lanes, onorm_rows)
            outs["s_swk"] = nk.reshape(1, bs, CHUNK, SW_KV_HEADS, SW_HEAD_DIM)
            outs["s_swv"] = nv.reshape(1, bs, CHUNK, SW_KV_HEADS, SW_HEAD_DIM)
            outs["s_c"] = ncst.reshape(1, bs, ML_HEADS, ML_QK_DIM, ML_V_DIM)
            outs["s_n"] = nn.reshape(1, bs, ML_HEADS, ML_QK_DIM)
            outs["s_m"] = nm[:, :ML_HEADS].reshape(1, bs, ML_HEADS)
        else:
            o = l // 2
            w_in = _odd_w_in(odd_w_in[o])
            w_out = odd_w_out[o].astype(BF16)
            convw = ssd_conv_w[o].astype(F32)
            convb = row1(ssd_conv_b[o])
            dtb = pad_lanes(ssd_dt_bias[o])
            arow = pad_lanes(-jnp.exp(ssd_a_log[o].astype(F32)))
            drow = row1(jnp.repeat(ssd_d[o].astype(F32), SSD_HEAD_DIM))
            snorm = row1(ssd_norm[o])
            rnorm = row1(ret_norm[o])
            mix_p, ctail, sst, rst = odd_prompt(yp.reshape(bp, seq, d), g_mix, w_in, tab128_p, convw, convb, dtb, arow,
                                                drow, snorm, tril, (dmat, q_scale, k_scale, chunk_decay), rnorm,
                                                group=2)
            mix_p = mix_p.reshape(bp * seq, -1)
            outs["p_conv"] = ctail[:, SUBLANES - (SSD_CONV - 1):, :].reshape(1, bp, SSD_CONV - 1, SSD_CONV_DIM)
            outs["p_ssd"] = sst.reshape(1, bp, SSD_HEADS, SSD_HEAD_DIM, SSD_STATE)
            outs["p_ret"] = rst.reshape(1, bp, RET_HEADS, RET_QK_DIM, RET_V_DIM)

            proj_s = norm_proj(ys, g_mix, w_in, tm=bs)
            rnorm_rows = _rows8(ret_norm[o].astype(F32).reshape(RET_HEADS, RET_V_DIM))
            mix_s, ncb, ns, nr = odd_sample(
                proj_s, jnp.swapaxes(state_ssd_conv[o], 0, 1),
                state_ssd[o].reshape(bs, SSD_HEADS * SSD_HEAD_DIM, SSD_STATE),
                state_ret[o].reshape(bs, RET_HEADS * RET_QK_DIM, RET_V_DIM),
                tab128_s, convw, convb, dtb, arow, drow, snorm, spread, rdec_rows, rnorm_rows)
            outs["s_conv"] = jnp.swapaxes(ncb, 0, 1).reshape(1, bs, SSD_CONV - 1, SSD_CONV_DIM)
            outs["s_ssd"] = ns.reshape(1, bs, SSD_HEADS, SSD_HEAD_DIM, SSD_STATE)
            outs["s_ret"] = nr.reshape(1, bs, RET_HEADS, RET_QK_DIM, RET_V_DIM)

        wkv = jnp.concatenate([mem_wk[l], mem_wv[l]], axis=1).astype(BF16)
        qnorm = row1(mem_q_norm[l])
        mkv = norm_proj(mem, row1(norm_mem[l]), wkv, tm=tm, head_norm=row1(mem_k_norm[l]), head_norm_cols=MEM_WIDTH)
        p_mk.append(mkv[:, :MEM_WIDTH].reshape(bp, MEM_LEN, MEM_HEADS, MEM_HEAD_DIM))
        p_mv.append(mkv[:, MEM_WIDTH:].reshape(bp, MEM_LEN, MEM_HEADS, MEM_HEAD_DIM))
        wq, wo = mem_wq[l].astype(BF16), mem_wo[l].astype(BF16)
        gx, gf = row1(norm_xattn[l]), row1(norm_ffn[l])
        yp = post_prompt(yp, mix_p, w_out, gx, wq, qnorm, mkv, wo, gf, w1s, w2s, layer=l, rows_per_batch=seq, tm=tm)

        ys = res_proj(ys, mix_s, w_out)
        qs = norm_proj(ys, gx, wq, tm=bs, head_norm=qnorm, head_norm_cols=MEM_WIDTH)
        ys = res_proj_ffn(ys, xattn_sample(qs, cmk, cmv, l), wo, gf, w1s, w2s, layer=l)

    return (yp.reshape(bp, seq, d), ys.reshape(bs, 1, d),
            jnp.stack(p_mk), jnp.stack(p_mv), outs["p_swk"], outs["p_swv"], outs["p_c"], outs["p_n"], outs["p_m"],
            outs["p_conv"], outs["p_ssd"], outs["p_ret"],
            outs["s_swk"], outs["s_swv"], outs["s_c"], outs["s_n"], outs["s_m"],
            outs["s_conv"], outs["s_ssd"], outs["s_ret"])
```

```python
import functools
import math

import jax
import jax.numpy as jnp
import numpy as np
from jax import lax
from jax.experimental import pallas as pl
from jax.experimental.pallas import tpu as pltpu

F32 = jnp.float32
BF16 = jnp.bfloat16

D_MODEL = 1024
PAST_LEN = 8192
EPS = 1e-6
CHUNK = 128
NEG = -1e30

SW_HEADS, SW_KV_HEADS, SW_HEAD_DIM, SW_ROT_DIM = 8, 2, 64, 16
ROPE_THETA = 500000.0
ML_HEADS, ML_QK_DIM, ML_V_DIM = 4, 64, 128
SSD_HEADS, SSD_HEAD_DIM, SSD_GROUPS, SSD_STATE, SSD_CONV = 16, 64, 2, 128, 4
SSD_INNER = SSD_HEADS * SSD_HEAD_DIM
SSD_CONV_DIM = SSD_INNER + 2 * SSD_GROUPS * SSD_STATE
RET_HEADS, RET_QK_DIM, RET_V_DIM = 4, 128, 128
RET_ROPE_THETA = 10000.0
MEM_LEN, MEM_HEADS, MEM_HEAD_DIM = 256, 4, 128
MEM_WIDTH = MEM_HEADS * MEM_HEAD_DIM
FFN_DIM = 4 * D_MODEL
FFN_CHUNK = 512
PROJ_COLS_PER_STAGE = 256

LANES = 128
SUBLANES = 8
VMEM_LIMIT_BYTES = 56 * 1024 * 1024

E_SQ, E_SK, E_SV, E_MQ, E_MK, E_MV, E_MO, E_GATE, EVEN_COLS = 0, 512, 640, 768, 1024, 1280, 1792, 2304, 2432
GATE_F_LANE = 8
O_Z, O_XBC, O_RQ, O_RK, O_RV, O_RG, O_DT, ODD_COLS = 0, 1024, 2560, 3072, 3584, 4096, 4608, 4736


def _mm(a, b):
    return jnp.dot(a.astype(BF16), b.astype(BF16), preferred_element_type=F32)


def _mm_nt(a, b):
    return lax.dot_general(a.astype(BF16), b.astype(BF16), (((1,), (1,)), ((), ())), preferred_element_type=F32)


def _mm_tn(a, b):
    return lax.dot_general(a.astype(BF16), b.astype(BF16), (((0,), (0,)), ((), ())), preferred_element_type=F32)


def _dg(a, b, ca, cb):
    return lax.dot_general(a, b, (((ca,), (cb,)), ((), ())), preferred_element_type=F32)


def _split3(x):
    hi = x.astype(BF16).astype(F32)
    r1 = x - hi
    mid = r1.astype(BF16).astype(F32)
    lo = (r1 - mid).astype(BF16).astype(F32)
    return hi, mid, lo


def _mm_exact_rhs(x, e):
    hi, mid, lo = _split3(x)
    return _dg(hi, e, 1, 0) + _dg(mid, e, 1, 0) + _dg(lo, e, 1, 0)


def _mm_exact_lhs(e, x):
    hi, mid, lo = _split3(x)
    return _dg(e, hi, 1, 0) + _dg(e, mid, 1, 0) + _dg(e, lo, 1, 0)


def _mm_tn_exact_lhs(x, e):
    hi, mid, lo = _split3(x)
    return _dg(hi, e, 0, 0) + _dg(mid, e, 0, 0) + _dg(lo, e, 0, 0)


def _rms(x, g):
    return x * lax.rsqrt(jnp.mean(x * x, axis=-1, keepdims=True) + EPS) * g


def _seg_rms_mxu(x, g, seg_mean):
    return x * lax.rsqrt(_mm_exact_rhs(x * x, seg_mean) + EPS) * g


def _seg_rms(x, g, low):
    xx = x * x
    s_lo = jnp.sum(jnp.where(low, xx, 0.0), axis=-1, keepdims=True)
    s_hi = jnp.sum(jnp.where(low, 0.0, xx), axis=-1, keepdims=True)
    return x * lax.rsqrt(jnp.where(low, s_lo, s_hi) * (1.0 / SW_HEAD_DIM) + EPS) * g


def _sigmoid(x):
    return 1.0 / (1.0 + jnp.exp(-x))


def _silu(x):
    return x * _sigmoid(x)


def _softplus(x):
    return jnp.maximum(x, 0.0) + jnp.log1p(jnp.exp(-jnp.abs(x)))


def _log_sigmoid(x):
    return -_softplus(-x)


def _rope16(x, cos, sin_lo, sin_hi):
    return x * cos + pltpu.roll(x, LANES - 8, 1) * sin_lo + pltpu.roll(x, 8, 1) * sin_hi


def _rope128(x, cos, sin):
    return x * cos + pltpu.roll(x, 64, 1) * sin


def _iota(shape, dim):
    return lax.broadcasted_iota(jnp.int32, shape, dim)


def _cummax_lanes(x):
    lane = _iota(x.shape, 1)
    shift = 1
    while shift < x.shape[1]:
        x = jnp.maximum(x, jnp.where(lane >= shift, pltpu.roll(x, shift, 1), -jnp.inf))
        shift *= 2
    return x


def _norm_proj_kernel(x_ref, g_ref, w_ref, hn_ref, o_ref, *, chunks, head_norm_cols):
    xn = _rms(x_ref[...], g_ref[...]).astype(BF16)
    for c0, cs in chunks:
        r = jnp.dot(xn, w_ref[:, c0:c0 + cs], preferred_element_type=F32)
        if c0 < head_norm_cols:
            parts = [_rms(r[:, i:i + LANES], hn_ref[...]) for i in range(0, cs, LANES)]
            r = jnp.concatenate(parts, axis=1)
        o_ref[:, c0:c0 + cs] = r


def _col_chunks(n, width=512):
    return tuple((c, min(width, n - c)) for c in range(0, n, width))


def norm_proj(x, g, w, *, tm, head_norm=None, head_norm_cols=0):
    n, d = x.shape
    m = w.shape[1]
    if head_norm is None:
        head_norm = jnp.ones((1, LANES), F32)
    kern = functools.partial(_norm_proj_kernel, chunks=_col_chunks(m), head_norm_cols=head_norm_cols)
    return pl.pallas_call(
        kern,
        grid=(n // tm,),
        in_specs=[
            pl.BlockSpec((tm, d), lambda i: (i, 0)),
            pl.BlockSpec((1, d), lambda i: (0, 0)),
            pl.BlockSpec((d, m), lambda i: (0, 0), pipeline_mode=pl.Buffered(1)),
            pl.BlockSpec((1, LANES), lambda i: (0, 0)),
        ],
        out_specs=pl.BlockSpec((tm, m), lambda i: (i, 0)),
        out_shape=jax.ShapeDtypeStruct((n, m), F32),
        compiler_params=pltpu.CompilerParams(dimension_semantics=("arbitrary",), vmem_limit_bytes=VMEM_LIMIT_BYTES),
        name="norm_proj",
    )(x, g, w, head_norm)


def _ffn(x, g_ref, w1_ref, w2_ref):
    h = _rms(x, g_ref[...]).astype(BF16)
    acc = None
    for c in range(0, FFN_DIM, FFN_CHUNK):
        u = jnp.maximum(jnp.dot(h, w1_ref[:, c:c + FFN_CHUNK], preferred_element_type=F32), 0.0)
        t = jnp.dot((u * u).astype(BF16), w2_ref[c:c + FFN_CHUNK, :], preferred_element_type=F32)
        acc = t if acc is None else acc + t
    return x + acc


def _post_chain(x_ref, a_ref, wout_ref, gx_ref, wq_ref, qn_ref, mk_ref, mv_ref, wo_ref, gf_ref, w1_ref, w2_ref, o_ref):
    x = x_ref[...] + _mm(a_ref[...], wout_ref[...])
    yield
    q = jnp.dot(_rms(x, gx_ref[...]).astype(BF16), wq_ref[...], preferred_element_type=F32)
    yield
    outs = []
    for h in range(MEM_HEADS):
        sl = slice(h * MEM_HEAD_DIM, (h + 1) * MEM_HEAD_DIM)
        qh = _rms(q[:, sl], qn_ref[...])
        s = _mm_nt(qh, mk_ref[:, sl]) * (MEM_HEAD_DIM ** -0.5)
        p = jnp.exp(s - jnp.max(s, axis=-1, keepdims=True))
        p = p / jnp.sum(p, axis=-1, keepdims=True)
        outs.append(_mm(p, mv_ref[:, sl]))
        yield
    x = x + _mm(jnp.concatenate(outs, axis=1), wo_ref[...])
    h = _rms(x, gf_ref[...]).astype(BF16)
    yield
    acc = None
    for c in range(0, FFN_DIM, FFN_CHUNK):
        u = jnp.maximum(jnp.dot(h, w1_ref[:, c:c + FFN_CHUNK], preferred_element_type=F32), 0.0)
        t = jnp.dot((u * u).astype(BF16), w2_ref[c:c + FFN_CHUNK, :], preferred_element_type=F32)
        acc = t if acc is None else acc + t
        yield
    o_ref[...] = x + acc


def _xattn_row(q_row, k, v):
    row = _iota((SUBLANES, LANES), 0)
    groups = MEM_LEN * MEM_HEADS // SUBLANES
    q8 = jnp.zeros((SUBLANES, LANES), F32)
    for h in range(MEM_HEADS):
        q8 = jnp.where(row % MEM_HEADS == h, q_row[:, h * MEM_HEAD_DIM:(h + 1) * MEM_HEAD_DIM], q8)
    k3 = k.reshape(groups, SUBLANES, LANES)
    s = jnp.sum(k3 * q8[None], axis=-1, keepdims=True) * (MEM_HEAD_DIM ** -0.5)
    mx = _pair_rows(jnp.max(s, axis=0), jnp.maximum)
    p = jnp.exp(s - mx[None, :, 0:1])
    den = _pair_rows(jnp.sum(p, axis=0), jnp.add)
    o8 = _pair_rows(jnp.sum(p * v.reshape(groups, SUBLANES, LANES), axis=0), jnp.add) / den
    return jnp.concatenate([o8[h:h + 1] for h in range(MEM_HEADS)], axis=1)


def _xattn_chain(q_ref, mk_ref, mv_ref, o_ref):
    for i in range(q_ref.shape[0]):
        o_ref[i:i + 1, :] = _xattn_row(q_ref[i:i + 1, :], mk_ref[i], mv_ref[i])
        yield


def _post_prompt_kernel(x_ref, a_ref, wout_ref, gx_ref, wq_ref, qn_ref, mk_ref, mv_ref, wo_ref, gf_ref, w1_ref, w2_ref,
                        sq_ref, smk_ref, smv_ref,
                        o_ref, so_ref):
    _lockstep([_post_chain(x_ref, a_ref, wout_ref, gx_ref, wq_ref, qn_ref, mk_ref, mv_ref, wo_ref, gf_ref, w1_ref,
                           w2_ref, o_ref),
               _xattn_chain(sq_ref, smk_ref, smv_ref, so_ref)], every=[1, 3])


def post_prompt(x, a, wout, gx, wq, qn, mkv, wo, gf, w1, w2, sq, smk, smv, *, layer, rows_per_batch, tm):
    n, d = x.shape
    ka = a.shape[1]
    steps = n // tm
    bs = sq.shape[0]
    rs = bs // steps
    tiles = rows_per_batch // tm
    const = lambda shape: pl.BlockSpec(shape, lambda i: (0, 0), pipeline_mode=pl.Buffered(1))
    slab = lambda shape: pl.BlockSpec((None,) + shape, lambda i: (layer, 0, 0), pipeline_mode=pl.Buffered(1))
    smem = pl.BlockSpec((None, rs, MEM_LEN * MEM_HEADS, MEM_HEAD_DIM), lambda i: (layer, i, 0, 0))
    srow = pl.BlockSpec((None, rs, MEM_WIDTH), lambda i: (i, 0, 0))
    y, att = pl.pallas_call(
        _post_prompt_kernel,
        grid=(steps,),
        in_specs=[
            pl.BlockSpec((tm, d), lambda i: (i, 0)),
            pl.BlockSpec((tm, ka), lambda i: (i, 0)),
            const((ka, d)),
            const((1, d)),
            const((d, MEM_WIDTH)),
            const((1, MEM_HEAD_DIM)),
            pl.BlockSpec((MEM_LEN, MEM_WIDTH), lambda i: (i // tiles, 0)),
            pl.BlockSpec((MEM_LEN, MEM_WIDTH), lambda i: (i // tiles, 1)),
            const((MEM_WIDTH, d)),
            const((1, d)),
            slab((d, FFN_DIM)),
            slab((FFN_DIM, d)),
            srow, smem, smem,
        ],
        out_specs=[pl.BlockSpec((tm, d), lambda i: (i, 0)), srow],
        out_shape=[jax.ShapeDtypeStruct((n, d), F32), jax.ShapeDtypeStruct((steps, rs, MEM_WIDTH), F32)],
        compiler_params=pltpu.CompilerParams(dimension_semantics=("arbitrary",), vmem_limit_bytes=VMEM_LIMIT_BYTES),
        name="post_prompt",
    )(x, a, wout, gx, wq, qn, mkv, mkv, wo, gf, w1, w2, sq.reshape(steps, rs, MEM_WIDTH), smk, smv)
    return y, att.reshape(bs, MEM_WIDTH)


def _res_proj_kernel(x_ref, a_ref, w_ref, o_ref):
    o_ref[...] = x_ref[...] + _mm(a_ref[...], w_ref[...])


def res_proj(x, a, w):
    n, d = x.shape
    return pl.pallas_call(
        _res_proj_kernel,
        out_shape=jax.ShapeDtypeStruct((n, d), F32),
        compiler_params=pltpu.CompilerParams(vmem_limit_bytes=VMEM_LIMIT_BYTES),
        name="res_proj",
    )(x, a, w)


def _res_proj_ffn_kernel(x_ref, a_ref, w_ref, gf_ref, w1_ref, w2_ref, o_ref):
    x = x_ref[...] + _mm(a_ref[...], w_ref[...])
    o_ref[...] = _ffn(x, gf_ref, w1_ref, w2_ref)


def res_proj_ffn(x, a, w, gf, w1, w2, *, layer):
    n, d = x.shape
    full = lambda t: pl.BlockSpec(t.shape, lambda i: (0, 0))
    slab = lambda shape: pl.BlockSpec((None,) + shape, lambda i: (layer, 0, 0), pipeline_mode=pl.Buffered(1))
    return pl.pallas_call(
        _res_proj_ffn_kernel,
        grid=(1,),
        in_specs=[full(x), full(a), full(w), full(gf), slab((d, FFN_DIM)), slab((FFN_DIM, d))],
        out_specs=pl.BlockSpec((n, d), lambda i: (0, 0)),
        out_shape=jax.ShapeDtypeStruct((n, d), F32),
        compiler_params=pltpu.CompilerParams(dimension_semantics=("arbitrary",), vmem_limit_bytes=VMEM_LIMIT_BYTES),
        name="res_proj_ffn",
    )(x, a, w, gf, w1, w2)


def _pair_rows(x, op):
    xb = jnp.broadcast_to(x, (SUBLANES, LANES))
    return op(xb, pltpu.roll(xb, SUBLANES // 2, 0))


def _proj_chain(x_ref, g_ref, w_ref, dst, width):
    group, _, d = x_ref.shape
    xn = _rms(x_ref[...].reshape(group * CHUNK, d), g_ref[...]).astype(BF16)
    yield
    cols = w_ref.shape[1]
    for c0 in range(0, cols, width):
        cs = min(width, cols - c0)
        r = jnp.dot(xn, w_ref[:, c0:c0 + cs], preferred_element_type=F32)
        for b in range(group):
            dst[b, :, c0:c0 + cs] = r[b * CHUNK:(b + 1) * CHUNK]
        yield


def _even_prompt_kernel(xn_ref, x0_ref, g_ref, w_ref, cos_ref, sinlo_ref, sinhi_ref, qn_ref, kn_ref, sink_ref, segm_ref,
                        triu_ref, gb_ref, onorm_ref,
                        y_ref, kc_ref, vc_ref, caug_ref, m_ref,
                        kprev, vprev, cst, mst, proj):
    n = pl.program_id(0)
    batch = xn_ref.shape[0]
    slot = lax.rem(n, 2)

    @pl.when(n == 0)
    def _():
        kprev[...] = jnp.zeros_like(kprev)
        vprev[...] = jnp.zeros_like(vprev)
        cst[...] = jnp.zeros_like(cst)
        mst[...] = jnp.zeros_like(mst)
        for _ in _proj_chain(x0_ref, g_ref, w_ref, proj.at[0], PROJ_COLS_PER_STAGE):
            pass

    p_ref = proj.at[slot]

    cos, sinlo, sinhi = cos_ref[...], sinlo_ref[...], sinhi_ref[...]
    lane = _iota((1, LANES), 1)
    low = lane < 64
    qi = _iota((CHUNK, 2 * CHUNK), 0)
    si = _iota((CHUNK, 2 * CHUNK), 1)
    valid = (si >= qi) & (si <= qi + CHUNK) & ((si >= CHUNK) | (n > 0))
    causal = _iota((CHUNK, CHUNK), 0) >= _iota((CHUNK, CHUNK), 1)
    ones_col = jnp.where(_iota((CHUNK, LANES), 1) == 0, 1.0, 0.0)
    row64 = _iota((CHUNK, 1), 0) < 64
    new_kv = _lockstep([
        _even_prompt_chunk(p_ref.at[b], y_ref.at[b], kprev.at[b], vprev.at[b], cst.at[b], mst.at[b],
                           (cos, sinlo, sinhi), qn_ref, kn_ref, sink_ref, segm_ref[...], triu_ref, gb_ref, onorm_ref,
                           low, valid, causal, ones_col, row64)
        for b in range(batch)] + [_proj_chain(xn_ref, g_ref, w_ref, proj.at[1 - slot], 2 * PROJ_COLS_PER_STAGE)],
        every=[1] * batch + [8])[:batch]

    @pl.when(n == pl.num_programs(0) - 1)
    def _():
        for b in range(batch):
            kc_ref[b], vc_ref[b] = new_kv[b]
        caug_ref[...] = cst[...]
        m_ref[...] = mst[...]


def _lockstep(chains, every=None):
    every = every or [1] * len(chains)
    results = [None] * len(chains)
    live = list(range(len(chains)))
    rnd = 0
    while live:
        for i in list(live):
            if rnd % every[i]:
                continue
            try:
                next(chains[i])
            except StopIteration as stop:
                results[i] = stop.value
                live.remove(i)
        rnd += 1
    return results


def _even_prompt_chunk(p_ref, y_ref, kprev, vprev, cst, mst, tabs, qn_ref, kn_ref, sink_ref, segm, triu_ref, gb_ref,
                       onorm_ref, low, valid, causal, ones_col, row64):
    cos, sinlo, sinhi = tabs
    k = _rope16(_seg_rms_mxu(p_ref[:, E_SK:E_SK + LANES], kn_ref[...], segm), cos, sinlo, sinhi)
    v = p_ref[:, E_SV:E_SV + LANES]
    kk = jnp.concatenate([kprev[...], k], axis=0)
    vv = jnp.concatenate([vprev[...], v], axis=0)
    kk_sw = pltpu.roll(kk, 64, 1)
    vv_sw = pltpu.roll(vv, 64, 1)
    kvar = {(0, 0): jnp.where(low, kk, 0.0), (0, 1): jnp.where(low, 0.0, kk_sw),
            (1, 0): jnp.where(low, kk_sw, 0.0), (1, 1): jnp.where(low, 0.0, kk)}
    vvar = {(0, 0): vv, (0, 1): vv_sw, (1, 0): vv_sw, (1, 1): vv}
    yield
    for j in range(SW_HEADS // 2):
        sl = slice(E_SQ + j * LANES, E_SQ + (j + 1) * LANES)
        qb = _rope16(_seg_rms_mxu(p_ref[:, sl], qn_ref[:, sl], segm), cos, sinlo, sinhi)
        yield
        halves = []
        for pos in range(2):
            h = 2 * j + pos
            kv = h // (SW_HEADS // SW_KV_HEADS)
            s = jnp.where(valid, _mm_nt(qb, kvar[(kv, pos)]) * (SW_HEAD_DIM ** -0.5), NEG)
            yield
            sink = sink_ref[h]
            m = jnp.maximum(jnp.max(s, axis=-1, keepdims=True), sink)
            pr = jnp.exp(s - m)
            yield
            pr = pr / (jnp.sum(pr, axis=-1, keepdims=True) + jnp.exp(sink - m))
            halves.append(_mm(pr, vvar[(kv, pos)]))
            yield
        y_ref[:, j * LANES:(j + 1) * LANES] = jnp.where(low, halves[0], halves[1]).astype(y_ref.dtype)
    kprev[...] = k
    vprev[...] = v
    yield

    gt = p_ref[:, E_GATE:E_GATE + LANES].T
    gi = gt[0:SUBLANES] + gb_ref[0:SUBLANES]
    fl = _log_sigmoid(gt[GATE_F_LANE:GATE_F_LANE + SUBLANES] + gb_ref[SUBLANES:2 * SUBLANES])
    yield
    fcum = _mm_exact_rhs(fl, triu_ref[...])
    dd = gi - fcum
    mprev = mst[...]
    yield
    mt = fcum + jnp.maximum(mprev, _cummax_lanes(dd))
    fend = jnp.broadcast_to(fcum[:, CHUNK - 1:CHUNK], fcum.shape)
    mend = jnp.broadcast_to(mt[:, CHUNK - 1:CHUNK], mt.shape)
    decay = jnp.exp(fend + mprev - mend)
    rows = jnp.concatenate([fcum - mt, jnp.exp(fcum + mprev - mt), jnp.exp(-mt), jnp.exp(fend - fcum + gi - mend),
                            jnp.zeros((CHUNK - 4 * SUBLANES, CHUNK), F32)], axis=0)
    yield
    cols = rows.T
    yield
    for j in range(ML_HEADS // 2):
        qblk = p_ref[:, E_MQ + j * LANES:E_MQ + (j + 1) * LANES]
        kblk = p_ref[:, E_MK + j * LANES:E_MK + (j + 1) * LANES] * (ML_QK_DIM ** -0.5)
        c_old = cst[j]
        upd = None
        for pos in range(2):
            h = 2 * j + pos
            msk = low if pos == 0 else jnp.logical_not(low)
            qm = jnp.where(msk, qblk, 0.0)
            logw = cols[:, h:h + 1] + dd[h:h + 1, :]
            w = jnp.exp(jnp.where(causal, logw, -jnp.inf))
            sqk = _mm_nt(qm, kblk) * w
            yield
            vh = p_ref[:, E_MV + h * LANES:E_MV + (h + 1) * LANES]
            qc = _mm(qm, c_old)
            yield
            cs = cols[:, SUBLANES + h:SUBLANES + h + 1]
            num = cs * qc[:, :ML_V_DIM] + _mm(sqk, vh)
            den = cs * qc[:, ML_V_DIM:ML_V_DIM + 1] + jnp.sum(sqk, axis=-1, keepdims=True)
            hh = num / jnp.maximum(jnp.abs(den), cols[:, 2 * SUBLANES + h:2 * SUBLANES + h + 1])
            yield
            hsl = slice(h * ML_V_DIM, (h + 1) * ML_V_DIM)
            hn = _rms(hh, onorm_ref[:, hsl])
            mo = p_ref[:, E_MO + h * ML_V_DIM:E_MO + (h + 1) * ML_V_DIM]
            y_ref[:, SW_HEADS * SW_HEAD_DIM + h * ML_V_DIM:SW_HEADS * SW_HEAD_DIM + (h + 1) * ML_V_DIM] = (
                hn * _sigmoid(mo)).astype(y_ref.dtype)
            kw = jnp.where(msk, kblk, 0.0) * cols[:, 3 * SUBLANES + h:3 * SUBLANES + h + 1]
            u = _mm_tn(kw, jnp.concatenate([vh, ones_col], axis=1))
            upd = u if upd is None else upd + u
            yield
        dec = jnp.where(row64, decay[2 * j:2 * j + 1, 0:1], decay[2 * j + 1:2 * j + 2, 0:1])
        cst[j] = dec * c_old + upd
    mst[...] = mend
    return k, v


def even_prompt(x, g, w, tabs, qn, kn, sinks, segm, triu, gb, onorm):
    batch, seq, d = x.shape
    nc = seq // CHUNK
    tab = pl.BlockSpec((CHUNK, LANES), lambda n: (n, 0))
    const = lambda shape: pl.BlockSpec(shape, lambda n: (0,) * len(shape))
    state_shapes = [(batch, CHUNK, LANES), (batch, CHUNK, LANES),
                    (batch, ML_HEADS // 2, 2 * ML_QK_DIM, 2 * ML_V_DIM), (batch, SUBLANES, LANES)]
    return pl.pallas_call(
        _even_prompt_kernel,
        grid=(nc,),
        in_specs=[
            pl.BlockSpec((batch, CHUNK, d), lambda n: (0, jnp.minimum(n + 1, nc - 1), 0)),
            pl.BlockSpec((batch, CHUNK, d), lambda n: (0, 0, 0)),
            const((1, d)), pl.BlockSpec((d, EVEN_COLS), lambda n: (0, 0), pipeline_mode=pl.Buffered(1)),
            tab, tab, tab,
            const((1, SW_HEADS * SW_HEAD_DIM)), const((1, LANES)),
            pl.BlockSpec(memory_space=pltpu.SMEM),
            const((LANES, LANES)), const((CHUNK, CHUNK)), const((2 * SUBLANES, LANES)),
            const((1, ML_HEADS * ML_V_DIM)),
        ],
        out_specs=[pl.BlockSpec((batch, CHUNK, D_MODEL), lambda n: (0, n, 0))] + [const(s) for s in state_shapes],
        out_shape=[jax.ShapeDtypeStruct((batch, seq, D_MODEL), BF16)]
        + [jax.ShapeDtypeStruct(s, F32) for s in state_shapes],
        scratch_shapes=[pltpu.VMEM(s, F32) for s in state_shapes] + [pltpu.VMEM((2, batch, CHUNK, EVEN_COLS), F32)],
        compiler_params=pltpu.CompilerParams(dimension_semantics=("arbitrary",), vmem_limit_bytes=VMEM_LIMIT_BYTES),
        name="even_prompt",
    )(x, x, g, w, *tabs, qn, kn, sinks, segm, triu, gb, onorm)


def _odd_prompt_kernel(xn_ref, x0_ref, g_ref, w_ref, cosr_ref, sinr_ref, convw_ref, convb_ref, dtb_ref, arow_ref,
                       drow_ref, snorm_ref, tril_ref, dmat_ref, qs_ref, ks_ref, cd_ref, rnorm_ref,
                       y_ref, conv_ref, s_ref, r_ref,
                       ext, sst, rst, proj):
    n = pl.program_id(1)
    batch = xn_ref.shape[0]
    slot = lax.rem(n, 2)

    @pl.when(n == 0)
    def _():
        ext[:, 0:SUBLANES] = jnp.zeros((batch, SUBLANES, SSD_CONV_DIM), F32)
        sst[...] = jnp.zeros_like(sst)
        rst[...] = jnp.zeros_like(rst)
        for _ in _proj_chain(x0_ref, g_ref, w_ref, proj.at[0], PROJ_COLS_PER_STAGE):
            pass

    p_ref = proj.at[slot]

    lane = _iota((1, LANES), 1)
    low = lane < 64
    row64 = _iota((CHUNK, 1), 0) < 64
    causal = _iota((CHUNK, CHUNK), 0) >= _iota((CHUNK, CHUNK), 1)
    tails = _lockstep([
        _odd_prompt_chunk(p_ref.at[b], y_ref.at[b], ext.at[b], sst.at[b], rst.at[b], cosr_ref, sinr_ref, convw_ref,
                          convb_ref, dtb_ref, arow_ref, drow_ref, snorm_ref, tril_ref, dmat_ref, qs_ref, ks_ref,
                          cd_ref, rnorm_ref, low, row64, causal)
        for b in range(batch)] + [_proj_chain(xn_ref, g_ref, w_ref, proj.at[1 - slot], PROJ_COLS_PER_STAGE)],
        every=[1] * batch + [2])[:batch]

    @pl.when(n == pl.num_programs(1) - 1)
    def _():
        for b in range(batch):
            conv_ref[b] = tails[b]
        s_ref[...] = sst[...]
        r_ref[...] = rst[...]


def _odd_prompt_chunk(p_ref, y_ref, ext, sst, rst, cosr_ref, sinr_ref, convw_ref, convb_ref, dtb_ref, arow_ref,
                      drow_ref, snorm_ref, tril_ref, dmat_ref, qs_ref, ks_ref, cd_ref, rnorm_ref, low, row64, causal):
    tail = SUBLANES
    ext[tail:tail + CHUNK] = p_ref[:, O_XBC:O_XBC + SSD_CONV_DIM]
    yield
    xe = ext[...]
    acc = None
    for jj in range(SSD_CONV):
        shift = SSD_CONV - 1 - jj
        tap = (pltpu.roll(xe, shift, 0) if shift else xe)[tail:tail + CHUNK] * convw_ref[jj:jj + 1]
        acc = tap if acc is None else acc + tap
    xact = _silu(acc + convb_ref[...])
    new_tail = ext[CHUNK:CHUNK + tail]
    ext[0:tail] = new_tail
    yield

    dt = _softplus(p_ref[:, O_DT:O_DT + LANES] + dtb_ref[...])
    cum = _mm_exact_lhs(tril_ref[...], dt * arow_ref[...])
    yield
    cum_t = cum.T
    dt_t = dt.T
    ecum = jnp.exp(cum)
    cend = cum[CHUNK - 1:CHUNK, :]
    wend = jnp.exp(cend - cum) * dt
    eend = jnp.exp(cend)
    yield
    pairs_per_group = SSD_HEADS // SSD_GROUPS // 2
    ys = []
    for g in range(SSD_GROUPS):
        bc = xact[:, SSD_INNER + g * SSD_STATE:SSD_INNER + (g + 1) * SSD_STATE]
        cc = xact[:, SSD_INNER + (SSD_GROUPS + g) * SSD_STATE:SSD_INNER + (SSD_GROUPS + g + 1) * SSD_STATE]
        cb = _mm_nt(cc, bc)
        yield
        for jg in range(pairs_per_group):
            j = g * pairs_per_group + jg
            ha, hb = 2 * j, 2 * j + 1
            xp = xact[:, j * LANES:(j + 1) * LANES]
            s_old = sst[j]
            y = jnp.where(low, ecum[:, ha:ha + 1], ecum[:, hb:hb + 1]) * _mm_nt(cc, s_old)
            yield
            for pos, h in ((0, ha), (1, hb)):
                seg = cum[:, h:h + 1] - cum_t[h:h + 1, :]
                wmat = cb * jnp.exp(jnp.where(causal, seg, -jnp.inf)) * dt_t[h:h + 1, :]
                y = y + _mm(wmat, jnp.where(low if pos == 0 else jnp.logical_not(low), xp, 0.0))
                yield
            xw = xp * jnp.where(low, wend[:, ha:ha + 1], wend[:, hb:hb + 1])
            sst[j] = jnp.where(row64, eend[:, ha:ha + 1], eend[:, hb:hb + 1]) * s_old + _mm_tn(xw, bc)
            ys.append(y)
            yield
        gs = slice(g * SSD_INNER // SSD_GROUPS, (g + 1) * SSD_INNER // SSD_GROUPS)
        yg = jnp.concatenate(ys[g * pairs_per_group:(g + 1) * pairs_per_group], axis=1)
        yg = (yg + drow_ref[:, gs] * xact[:, gs]) * _silu(p_ref[:, O_Z + gs.start:O_Z + gs.stop])
        y_ref[:, gs] = _rms(yg, snorm_ref[:, gs]).astype(y_ref.dtype)
        yield

    cosr, sinr = cosr_ref[...], sinr_ref[...]
    for h in range(RET_HEADS):
        hs = h * LANES
        q = _rope128(p_ref[:, O_RQ + hs:O_RQ + hs + LANES], cosr, sinr)
        k = _rope128(p_ref[:, O_RK + hs:O_RK + hs + LANES], cosr, sinr) * (RET_QK_DIM ** -0.5)
        v = p_ref[:, O_RV + hs:O_RV + hs + LANES]
        yield
        r_old = rst[h]
        o = _mm(_mm_nt(q, k) * dmat_ref[h], v) + qs_ref[h] * _mm(q, r_old)
        yield
        rst[h] = cd_ref[h] * r_old + _mm_tn(k * ks_ref[h], v)
        xc = o - jnp.mean(o, axis=-1, keepdims=True)
        yn = xc * lax.rsqrt(jnp.mean(xc * xc, axis=-1, keepdims=True) + EPS) * rnorm_ref[:, hs:hs + LANES]
        y_ref[:, SSD_INNER + hs:SSD_INNER + hs + LANES] = (
            yn * _silu(p_ref[:, O_RG + hs:O_RG + hs + LANES])).astype(y_ref.dtype)
        yield
    return new_tail


def odd_prompt(x, g, w, tabs, convw, convb, dtb, arow, drow, snorm, tril, ret_consts, rnorm, *, group):
    batch, seq, d = x.shape
    nc = seq // CHUNK
    tab = pl.BlockSpec((CHUNK, LANES), lambda g, n: (n, 0))
    const = lambda shape: pl.BlockSpec(shape, lambda g, n: (0,) * len(shape))
    per_g = lambda shape: pl.BlockSpec((group,) + shape, lambda g, n: (g,) + (0,) * len(shape))
    hc = (RET_HEADS, CHUNK, LANES)
    ywidth = SSD_INNER + RET_HEADS * RET_V_DIM
    states = [(SUBLANES, SSD_CONV_DIM), (SSD_HEADS // 2, LANES, SSD_STATE), hc]
    return pl.pallas_call(
        _odd_prompt_kernel,
        grid=(batch // group, seq // CHUNK),
        in_specs=[
            pl.BlockSpec((group, CHUNK, d), lambda g, n: (g, jnp.minimum(n + 1, nc - 1), 0)),
            pl.BlockSpec((group, CHUNK, d), lambda g, n: (g, 0, 0)),
            const((1, d)), pl.BlockSpec((d, ODD_COLS), lambda g, n: (0, 0), pipeline_mode=pl.Buffered(1)),
            tab, tab,
            const((SSD_CONV, SSD_CONV_DIM)), const((1, SSD_CONV_DIM)), const((1, LANES)), const((1, LANES)),
            const((1, SSD_INNER)), const((1, SSD_INNER)), const((CHUNK, CHUNK)),
            const(hc), const(hc), const(hc), const(hc), const((1, RET_HEADS * RET_V_DIM)),
        ],
        out_specs=[pl.BlockSpec((group, CHUNK, ywidth), lambda g, n: (g, n, 0))] + [per_g(s) for s in states],
        out_shape=[jax.ShapeDtypeStruct((batch, seq, ywidth), BF16)]
        + [jax.ShapeDtypeStruct((batch,) + s, F32) for s in states],
        scratch_shapes=[pltpu.VMEM((group, SUBLANES + CHUNK, SSD_CONV_DIM), F32),
                        pltpu.VMEM((group,) + states[1], F32), pltpu.VMEM((group,) + states[2], F32),
                        pltpu.VMEM((2, group, CHUNK, ODD_COLS), F32)],
        compiler_params=pltpu.CompilerParams(dimension_semantics=("arbitrary", "arbitrary"),
                                             vmem_limit_bytes=VMEM_LIMIT_BYTES),
        name="odd_prompt",
    )(x, x, g, w, *tabs, convw, convb, dtb, arow, drow, snorm, tril, *ret_consts, rnorm)


def _lane_to_rows(g, offset):
    sel = _iota(g.shape, 1) == _iota(g.shape, 0) + offset
    return jnp.sum(jnp.where(sel, g, 0.0), axis=-1, keepdims=True)


def _block_rows(x, i, nblk, blk):
    row = _iota((SUBLANES, blk), 0)
    out = jnp.zeros((SUBLANES, blk), F32)
    for b in range(nblk):
        out = jnp.where(row == b, x[i:i + 1, b * blk:(b + 1) * blk], out)
    return out


def _even_sample_kernel(p_ref, bk_ref, bv_ref, c_ref, nrow_ref, mrow_ref, cos_ref, sinlo_ref, sinhi_ref, qn_ref,
                        kn_ref, sink_ref, gb_ref, onorm_ref,
                        y_ref, nk_ref, nv_ref, nc_ref, nn_ref, nm_ref):
    R = SUBLANES
    cos, sinlo, sinhi = cos_ref[...], sinlo_ref[...], sinhi_ref[...]
    row = _iota((R, LANES), 0)
    lane = _iota((R, LANES), 1)
    low = lane < 64
    group = SW_HEADS // SW_KV_HEADS
    scale = SW_HEAD_DIM ** -0.5
    sink = sink_ref[:, 0:1]
    last = _iota((CHUNK, LANES), 0) == CHUNK - 1

    k = _rope16(_seg_rms(p_ref[:, E_SK:E_SK + LANES], kn_ref[...], low), cos, sinlo, sinhi)
    v = p_ref[:, E_SV:E_SV + LANES]
    qb, qb_sw = [], []
    for j in range(SW_HEADS // 2):
        sl = slice(E_SQ + j * LANES, E_SQ + (j + 1) * LANES)
        qb.append(_rope16(_seg_rms(p_ref[:, sl], qn_ref[:, sl], low), cos, sinlo, sinhi))
        qb_sw.append(pltpu.roll(qb[j], 64, 1))

    g = p_ref[:, E_GATE:E_GATE + LANES]
    ic = g + gb_ref[0:1]
    fl = _log_sigmoid(pltpu.roll(g, LANES - GATE_F_LANE, 1) + gb_ref[1:2])
    mprev = mrow_ref[...]
    mt = jnp.maximum(fl + mprev, ic)
    w_all = jnp.exp(ic - mt)
    cs_all = jnp.exp(fl + mprev - mt)
    em_all = jnp.exp(-mt)
    nm_ref[...] = mt
    width = ML_HEADS * ML_QK_DIM
    own = _iota((R, width), 1) // ML_QK_DIM == _iota((R, width), 0)
    kscaled = p_ref[:, E_MK:E_MK + width] * (ML_QK_DIM ** -0.5)

    for i in range(R):
        qm = jnp.zeros((R, LANES), F32)
        for j in range(SW_HEADS // 2):
            for pos in range(2):
                h = 2 * j + pos
                kv = h // group
                src = (qb[j] if pos == kv else qb_sw[j])[i:i + 1]
                qm = jnp.where((row == h) & (low if kv == 0 else jnp.logical_not(low)), src, qm)
        bk, bv = bk_ref[i], bv_ref[i]
        ki, vi = k[i:i + 1], v[i:i + 1]
        s = _dg(qm, bk, 1, 1) * scale
        s_new = jnp.sum(qm * ki, axis=-1, keepdims=True) * scale
        m = jnp.maximum(jnp.maximum(jnp.max(s, axis=-1, keepdims=True), s_new), sink)
        pr = jnp.exp(s - m)
        p_new = jnp.exp(s_new - m)
        den = jnp.sum(pr, axis=-1, keepdims=True) + p_new + jnp.exp(sink - m)
        o = (_dg(pr, bv, 1, 0) + p_new * vi) / den
        o_sw = pltpu.roll(o, 64, 1)
        for j in range(SW_HEADS // 2):
            halves = []
            for pos in range(2):
                h = 2 * j + pos
                halves.append((o if pos == h // group else o_sw)[h:h + 1, :])
            y_ref[i:i + 1, j * LANES:(j + 1) * LANES] = jnp.where(low[0:1], halves[0], halves[1])
        nk_ref[i] = jnp.where(last, ki, pltpu.roll(bk, CHUNK - 1, 0))
        nv_ref[i] = jnp.where(last, vi, pltpu.roll(bv, CHUNK - 1, 0))

        w = _lane_to_rows(jnp.broadcast_to(w_all[i:i + 1], (R, LANES)), 0)
        cs = _lane_to_rows(jnp.broadcast_to(cs_all[i:i + 1], (R, LANES)), 0)
        em = _lane_to_rows(jnp.broadcast_to(em_all[i:i + 1], (R, LANES)), 0)
        qrows = jnp.where(own, p_ref[i:i + 1, E_MQ:E_MQ + width], 0.0)
        krows = jnp.where(own, kscaled[i:i + 1], 0.0)
        c_old = c_ref[i]
        qc = _dg(qrows, c_old, 1, 0)
        qn_dot = jnp.sum(qrows * nrow_ref[i:i + 1], axis=-1, keepdims=True)
        sqk = jnp.sum(qrows * krows, axis=-1, keepdims=True) * w
        v4 = _block_rows(p_ref[:, E_MV:E_MV + ML_HEADS * ML_V_DIM], i, ML_HEADS, ML_V_DIM)
        mo4 = _block_rows(p_ref[:, E_MO:E_MO + ML_HEADS * ML_V_DIM], i, ML_HEADS, ML_V_DIM)
        num = cs * qc + sqk * v4
        dn = cs * qn_dot + sqk
        hh = num / jnp.maximum(jnp.abs(dn), em)
        hn = _rms(hh, onorm_ref[...]) * _sigmoid(mo4)
        for h in range(ML_HEADS):
            c0 = SW_HEADS * SW_HEAD_DIM + h * ML_V_DIM
            y_ref[i:i + 1, c0:c0 + ML_V_DIM] = hn[h:h + 1]
        dec_col = jnp.concatenate(
            [jnp.broadcast_to(cs[h:h + 1, 0:1], (ML_QK_DIM, ML_V_DIM)) for h in range(ML_HEADS)], axis=0)
        nc_ref[i] = dec_col * c_old + _dg(krows * w, v4, 0, 0)
        dec_lanes = jnp.sum(jnp.where(own, cs, 0.0), axis=0, keepdims=True)
        nn_ref[i:i + 1] = dec_lanes * nrow_ref[i:i + 1] + jnp.sum(krows * w, axis=0, keepdims=True)


def even_sample(proj, bk, bv, c, nrow, mrow, tabs, qn, kn, sink_rows, gb, onorm_rows):
    b = proj.shape[0]
    R = SUBLANES
    width = ML_HEADS * ML_QK_DIM
    rows = lambda w: pl.BlockSpec((R, w), lambda i: (i, 0))
    per_b = lambda shape: pl.BlockSpec((R,) + shape, lambda i: (i,) + (0,) * len(shape))
    const = lambda shape: pl.BlockSpec(shape, lambda i: (0,) * len(shape))
    tab = const((R, LANES))
    return pl.pallas_call(
        _even_sample_kernel,
        grid=(b // R,),
        in_specs=[
            rows(EVEN_COLS), per_b((CHUNK, LANES)), per_b((CHUNK, LANES)), per_b((width, ML_V_DIM)),
            rows(width), rows(LANES), tab, tab, tab,
            const((1, SW_HEADS * SW_HEAD_DIM)), const((1, LANES)), const((R, LANES)),
            const((2, LANES)), const((R, ML_V_DIM)),
        ],
        out_specs=[
            rows(D_MODEL), per_b((CHUNK, LANES)), per_b((CHUNK, LANES)), per_b((width, ML_V_DIM)),
            rows(width), rows(LANES),
        ],
        out_shape=[
            jax.ShapeDtypeStruct((b, D_MODEL), F32),
            jax.ShapeDtypeStruct((b, CHUNK, LANES), F32), jax.ShapeDtypeStruct((b, CHUNK, LANES), F32),
            jax.ShapeDtypeStruct((b, width, ML_V_DIM), F32), jax.ShapeDtypeStruct((b, width), F32),
            jax.ShapeDtypeStruct((b, LANES), F32),
        ],
        compiler_params=pltpu.CompilerParams(dimension_semantics=("arbitrary",), vmem_limit_bytes=VMEM_LIMIT_BYTES),
        name="even_sample",
    )(proj, bk, bv, c, nrow, mrow, *tabs, qn, kn, sink_rows, gb, onorm_rows)


def _odd_sample_kernel(p_ref, cb_ref, s_ref, r_ref, cosr_ref, sinr_ref, convw_ref, convb_ref, dtb_ref, arow_ref,
                       drow_ref, snorm_ref, spread_ref, rdec_ref, rnorm_ref,
                       y_ref, ncb_ref, ns_ref, nr_ref):
    R = SUBLANES
    xbc = p_ref[:, O_XBC:O_XBC + SSD_CONV_DIM]
    acc = cb_ref[0] * convw_ref[0:1]
    for jj in range(1, SSD_CONV - 1):
        acc = acc + cb_ref[jj] * convw_ref[jj:jj + 1]
    acc = acc + xbc * convw_ref[SSD_CONV - 1:SSD_CONV]
    xact = _silu(acc + convb_ref[...])
    for jj in range(SSD_CONV - 2):
        ncb_ref[jj] = cb_ref[jj + 1]
    ncb_ref[SSD_CONV - 2] = xbc

    dt = _softplus(p_ref[:, O_DT:O_DT + LANES] + dtb_ref[...])
    dec = jnp.exp(dt * arow_ref[...])
    xs = xact[:, 0:SSD_INNER]
    xdt = xs * _mm_exact_rhs(dt, spread_ref[...])
    gown = _iota((R, SSD_INNER), 1) // (SSD_INNER // SSD_GROUPS) == _iota((R, SSD_INNER), 0)
    bpart = xact[:, SSD_INNER:SSD_INNER + SSD_GROUPS * SSD_STATE]
    cpart = xact[:, SSD_INNER + SSD_GROUPS * SSD_STATE:SSD_CONV_DIM]

    cosr, sinr = cosr_ref[...], sinr_ref[...]
    width = RET_HEADS * RET_QK_DIM
    q4 = jnp.concatenate([_rope128(p_ref[:, O_RQ + h * LANES:O_RQ + (h + 1) * LANES], cosr, sinr)
                          for h in range(RET_HEADS)], axis=1)
    k4 = jnp.concatenate([_rope128(p_ref[:, O_RK + h * LANES:O_RK + (h + 1) * LANES], cosr, sinr)
                          for h in range(RET_HEADS)], axis=1) * (RET_QK_DIM ** -0.5)
    own = _iota((R, width), 1) // RET_QK_DIM == _iota((R, width), 0)
    gam = rdec_ref[:, 0:1]
    gam_col = jnp.concatenate(
        [jnp.broadcast_to(rdec_ref[h:h + 1, :], (RET_QK_DIM, RET_V_DIM)) for h in range(RET_HEADS)], axis=0)

    ys = []
    for i in range(R):
        brows = _block_rows(bpart, i, SSD_GROUPS, SSD_STATE)
        crows = _block_rows(cpart, i, SSD_GROUPS, SSD_STATE)
        xw = jnp.where(gown, xdt[i:i + 1], 0.0)
        dec_col = jnp.concatenate(
            [jnp.broadcast_to(dec[i:i + 1, h:h + 1], (SSD_HEAD_DIM, SSD_STATE)) for h in range(SSD_HEADS)], axis=0)
        s_new = dec_col * s_ref[i] + _dg(xw, brows, 0, 0)
        ns_ref[i] = s_new
        yrows = _dg(crows, s_new, 1, 1)
        ys.append(jnp.sum(jnp.where(gown, yrows, 0.0), axis=0, keepdims=True))

        qrows = jnp.where(own, q4[i:i + 1], 0.0)
        krows = jnp.where(own, k4[i:i + 1], 0.0)
        v4 = _block_rows(p_ref[:, O_RV:O_RV + RET_HEADS * RET_V_DIM], i, RET_HEADS, RET_V_DIM)
        g4 = _block_rows(p_ref[:, O_RG:O_RG + RET_HEADS * RET_V_DIM], i, RET_HEADS, RET_V_DIM)
        r_old = r_ref[i]
        att = jnp.sum(qrows * krows, axis=-1, keepdims=True)
        o = att * v4 + gam * _dg(qrows, r_old, 1, 0)
        nr_ref[i] = gam_col * r_old + _dg(krows, v4, 0, 0)
        xc = o - jnp.mean(o, axis=-1, keepdims=True)
        yn = xc * lax.rsqrt(jnp.mean(xc * xc, axis=-1, keepdims=True) + EPS) * rnorm_ref[...] * _silu(g4)
        for h in range(RET_HEADS):
            y_ref[i:i + 1, SSD_INNER + h * RET_V_DIM:SSD_INNER + (h + 1) * RET_V_DIM] = yn[h:h + 1]

    y = (jnp.concatenate(ys, axis=0) + drow_ref[...] * xs) * _silu(p_ref[:, O_Z:O_Z + SSD_INNER])
    gw = SSD_INNER // SSD_GROUPS
    for g in range(SSD_GROUPS):
        y_ref[:, g * gw:(g + 1) * gw] = _rms(y[:, g * gw:(g + 1) * gw], snorm_ref[:, g * gw:(g + 1) * gw])


def odd_sample(proj, cbuf, s, r, tabs, convw, convb, dtb, arow, drow, snorm, spread, rdec_rows, rnorm_rows):
    b = proj.shape[0]
    R = SUBLANES
    rows = lambda w: pl.BlockSpec((R, w), lambda i: (i, 0))
    per_b = lambda shape: pl.BlockSpec((R,) + shape, lambda i: (i,) + (0,) * len(shape))
    const = lambda shape: pl.BlockSpec(shape, lambda i: (0,) * len(shape))
    tab = const((R, LANES))
    conv = pl.BlockSpec((SSD_CONV - 1, R, SSD_CONV_DIM), lambda i: (0, i, 0))
    ywidth = SSD_INNER + RET_HEADS * RET_V_DIM
    sshape = (SSD_HEADS * SSD_HEAD_DIM, SSD_STATE)
    rshape = (RET_HEADS * RET_QK_DIM, RET_V_DIM)
    return pl.pallas_call(
        _odd_sample_kernel,
        grid=(b // R,),
        in_specs=[
            rows(ODD_COLS), conv, per_b(sshape), per_b(rshape), tab, tab,
            const((SSD_CONV, SSD_CONV_DIM)), const((1, SSD_CONV_DIM)), const((1, LANES)), const((1, LANES)),
            const((1, SSD_INNER)), const((1, SSD_INNER)), const((LANES, SSD_INNER)),
            const((R, LANES)), const((R, RET_V_DIM)),
        ],
        out_specs=[rows(ywidth), conv, per_b(sshape), per_b(rshape)],
        out_shape=[
            jax.ShapeDtypeStruct((b, ywidth), F32),
            jax.ShapeDtypeStruct((SSD_CONV - 1, b, SSD_CONV_DIM), F32),
            jax.ShapeDtypeStruct((b,) + sshape, F32), jax.ShapeDtypeStruct((b,) + rshape, F32),
        ],
        compiler_params=pltpu.CompilerParams(dimension_semantics=("arbitrary",), vmem_limit_bytes=VMEM_LIMIT_BYTES),
        name="odd_sample",
    )(proj, cbuf, s, r, *tabs, convw, convb, dtb, arow, drow, snorm, spread, rdec_rows, rnorm_rows)


def _pad_cols(w, n):
    return jnp.pad(w, ((0, 0), (0, n - w.shape[1])))


def _even_w_in(w):
    sq, sk, sv, mq, mk, mv, mo, mi, mf = jnp.split(w, [512, 640, 768, 1024, 1280, 1792, 2304, 2308], axis=1)
    gates = jnp.concatenate([_pad_cols(mi, GATE_F_LANE), _pad_cols(mf, LANES - GATE_F_LANE)], axis=1)
    return jnp.concatenate([sq, sk, sv, mq, mk, mv, mo, gates], axis=1).astype(BF16)


def _odd_w_in(w):
    z, xbc, dt, rq, rk, rv, rg = jnp.split(w, [1024, 2560, 2576, 3088, 3600, 4112], axis=1)
    return jnp.concatenate([z, xbc, rq, rk, rv, rg, _pad_cols(dt, LANES)], axis=1).astype(BF16)


def _lane_angles(pos, rot_dim, theta, freq_of_lane):
    half = rot_dim // 2
    inv = jnp.power(jnp.float32(theta), -jnp.arange(half, dtype=F32) * (2.0 / rot_dim))
    return jnp.asarray(pos).astype(F32)[:, None] * inv[freq_of_lane][None, :]


def _rope16_tables(pos):
    half = SW_ROT_DIM // 2
    d = np.arange(LANES) % SW_HEAD_DIM
    ang = _lane_angles(pos, SW_ROT_DIM, ROPE_THETA, d % half)
    cos, sin = jnp.cos(ang), jnp.sin(ang)
    return (jnp.where(d < SW_ROT_DIM, cos, 1.0), jnp.where(d < half, -sin, 0.0),
            jnp.where((d >= half) & (d < SW_ROT_DIM), sin, 0.0))


def _rope128_tables(pos):
    half = RET_QK_DIM // 2
    lane = np.arange(LANES)
    ang = _lane_angles(pos, RET_QK_DIM, RET_ROPE_THETA, lane % half)
    return jnp.cos(ang), jnp.where(lane < half, -jnp.sin(ang), jnp.sin(ang))


def _ret_consts():
    L = CHUNK
    f = np.float32
    lg = np.log(f(1.0) - np.exp2(f(-5.0) - np.arange(RET_HEADS, dtype=f))).astype(f)
    idx = np.arange(L, dtype=f)
    diff = idx[:, None] - idx[None, :]
    with np.errstate(invalid="ignore"):
        dmat = np.exp(np.where(diff >= 0, diff[None] * lg[:, None, None], -np.inf)).astype(f)
    q_scale = np.exp((idx[None] + f(1.0)) * lg[:, None]).astype(f)
    k_scale = np.exp((f(L) - f(1.0) - idx[None]) * lg[:, None]).astype(f)
    chunk_decay = np.exp(f(L) * lg).astype(f)
    bc = lambda t: np.ascontiguousarray(np.broadcast_to(t[:, :, None], (RET_HEADS, L, LANES)))
    cd = np.ascontiguousarray(np.broadcast_to(chunk_decay[:, None, None], (RET_HEADS, L, LANES)))
    return dmat, bc(q_scale), bc(k_scale), cd, lg


def _rows8(t):
    return jnp.pad(t, ((0, SUBLANES - t.shape[0]), (0, 0)))


def _gate_bias_rows(gb):
    ib = jnp.broadcast_to(gb[:ML_HEADS, None], (ML_HEADS, LANES))
    fb = jnp.broadcast_to(gb[ML_HEADS:, None], (ML_HEADS, LANES))
    return jnp.concatenate([_rows8(ib), _rows8(fb)], axis=0)


def kernel(x_prompt, x_sample, cache_mem_k, cache_mem_v, cache_swa_k, cache_swa_v, state_mlstm_C, state_mlstm_n,
           state_mlstm_m, state_ssd_conv, state_ssd, state_ret, mem_prompt, norm_mix, norm_xattn, norm_mem, norm_ffn,
           even_w_in, mlstm_gate_bias, swa_q_norm, swa_k_norm, swa_sinks, mlstm_out_norm, even_w_out, odd_w_in,
           ssd_conv_w, ssd_conv_b, ssd_dt_bias, ssd_a_log, ssd_d, ssd_norm, ret_norm, odd_w_out, mem_wq, mem_wk,
           mem_wv, mem_q_norm, mem_k_norm, mem_wo, ffn_w1, ffn_w2):
    bp, seq, d = x_prompt.shape
    bs = x_sample.shape[0]
    depth = norm_mix.shape[0]
    tm = 512

    pos_p = np.arange(seq, dtype=np.int32)
    pos_s = np.full((SUBLANES,), PAST_LEN, dtype=np.int32)
    tab16_p, tab16_s = _rope16_tables(pos_p), _rope16_tables(pos_s)
    tab128_p, tab128_s = _rope128_tables(pos_p), _rope128_tables(pos_s)
    dmat, q_scale, k_scale, chunk_decay, lg = _ret_consts()
    rdec_rows = _rows8(jnp.asarray(np.broadcast_to(np.exp(lg)[:, None], (RET_HEADS, LANES))))
    ii = np.arange(CHUNK)
    triu = (ii[:, None] <= ii[None, :]).astype(np.float32)
    tril = (ii[:, None] >= ii[None, :]).astype(np.float32)
    jj = np.arange(LANES)
    segm = np.where(jj[:, None] // SW_HEAD_DIM == jj[None, :] // SW_HEAD_DIM, 1.0 / SW_HEAD_DIM, 0.0).astype(np.float32)
    spread = (jj[:, None] == np.arange(SSD_INNER)[None, :] // SSD_HEAD_DIM).astype(np.float32)
    row1 = lambda t: t.reshape(1, -1).astype(F32)
    pad_lanes = lambda t: jnp.pad(t.reshape(1, -1).astype(F32), ((0, 0), (0, LANES - t.shape[-1])))

    yp = x_prompt.reshape(bp * seq, d)
    ys = x_sample.reshape(bs, d)
    mem = mem_prompt.reshape(bp * MEM_LEN, d)
    cmk = cache_mem_k.reshape(depth, bs, MEM_LEN * MEM_HEADS, MEM_HEAD_DIM)
    cmv = cache_mem_v.reshape(depth, bs, MEM_LEN * MEM_HEADS, MEM_HEAD_DIM)
    w1s, w2s = ffn_w1.astype(BF16), ffn_w2.astype(BF16)
    p_mk, p_mv = [], []
    outs = {}
    for l in range(depth):
        g_mix = row1(norm_mix[l])
        if l % 2 == 0:
            e = l // 2
            w_in = _even_w_in(even_w_in[e])
            w_out = even_w_out[e].astype(BF16)
            qn = row1(jnp.tile(swa_q_norm[e], SW_HEADS))
            kn = row1(jnp.tile(swa_k_norm[e], SW_KV_HEADS))
            gb = _gate_bias_rows(mlstm_gate_bias[e].astype(F32))
            onorm = row1(mlstm_out_norm[e])
            sinks = swa_sinks[e].astype(F32)
            mix_p, kc, vc, caug, mm = even_prompt(yp.reshape(bp, seq, d), g_mix, w_in, tab16_p, qn, kn, sinks, segm,
                                                  triu, gb, onorm)
            mix_p = mix_p.reshape(bp * seq, -1)
            outs["p_swk"] = kc.reshape(1, bp, CHUNK, SW_KV_HEADS, SW_HEAD_DIM)
            outs["p_swv"] = vc.reshape(1, bp, CHUNK, SW_KV_HEADS, SW_HEAD_DIM)
            outs["p_c"] = caug[..., :ML_V_DIM].reshape(1, bp, ML_HEADS, ML_QK_DIM, ML_V_DIM)
            outs["p_n"] = caug[..., ML_V_DIM].reshape(1, bp, ML_HEADS, ML_QK_DIM)
            outs["p_m"] = mm[:, :ML_HEADS, 0].reshape(1, bp, ML_HEADS)

            proj_s = norm_proj(ys, g_mix, w_in, tm=bs)
            sink_rows = jnp.broadcast_to(sinks[:, None], (SW_HEADS, LANES))
            onorm_rows = _rows8(mlstm_out_norm[e].astype(F32).reshape(ML_HEADS, ML_V_DIM))
            gb_lanes = jnp.concatenate([pad_lanes(mlstm_gate_bias[e][:ML_HEADS]),
                                        pad_lanes(mlstm_gate_bias[e][ML_HEADS:])], axis=0)
            mix_s, nk, nv, ncst, nn, nm = even_sample(
                proj_s,
                cache_swa_k[e].reshape(bs, CHUNK, LANES), cache_swa_v[e].reshape(bs, CHUNK, LANES),
                state_mlstm_C[e].reshape(bs, ML_HEADS * ML_QK_DIM, ML_V_DIM),
                state_mlstm_n[e].reshape(bs, ML_HEADS * ML_QK_DIM),
                jnp.pad(state_mlstm_m[e], ((0, 0), (0, LANES - ML_HEADS))),
                tab16_s, qn, kn, sink_rows, gb_lanes, onorm_rows)
            outs["s_swk"] = nk.reshape(1, bs, CHUNK, SW_KV_HEADS, SW_HEAD_DIM)
            outs["s_swv"] = nv.reshape(1, bs, CHUNK, SW_KV_HEADS, SW_HEAD_DIM)
            outs["s_c"] = ncst.reshape(1, bs, ML_HEADS, ML_QK_DIM, ML_V_DIM)
            outs["s_n"] = nn.reshape(1, bs, ML_HEADS, ML_QK_DIM)
            outs["s_m"] = nm[:, :ML_HEADS].reshape(1, bs, ML_HEADS)
        else:
            o = l // 2
            w_in = _odd_w_in(odd_w_in[o])
            w_out = odd_w_out[o].astype(BF16)
            convw = ssd_conv_w[o].astype(F32)
            convb = row1(ssd_conv_b[o])
            dtb = pad_lanes(ssd_dt_bias[o])
            arow = pad_lanes(-jnp.exp(ssd_a_log[o].astype(F32)))
            drow = row1(jnp.repeat(ssd_d[o].astype(F32), SSD_HEAD_DIM))
            snorm = row1(ssd_norm[o])
            rnorm = row1(ret_norm[o])
            mix_p, ctail, sst, rst = odd_prompt(yp.reshape(bp, seq, d), g_mix, w_in, tab128_p, convw, convb, dtb, arow,
                                                drow, snorm, tril, (dmat, q_scale, k_scale, chunk_decay), rnorm,
                                                group=2)
            mix_p = mix_p.reshape(bp * seq, -1)
            outs["p_conv"] = ctail[:, SUBLANES - (SSD_CONV - 1):, :].reshape(1, bp, SSD_CONV - 1, SSD_CONV_DIM)
            outs["p_ssd"] = sst.reshape(1, bp, SSD_HEADS, SSD_HEAD_DIM, SSD_STATE)
            outs["p_ret"] = rst.reshape(1, bp, RET_HEADS, RET_QK_DIM, RET_V_DIM)

            proj_s = norm_proj(ys, g_mix, w_in, tm=bs)
            rnorm_rows = _rows8(ret_norm[o].astype(F32).reshape(RET_HEADS, RET_V_DIM))
            mix_s, ncb, ns, nr = odd_sample(
                proj_s, jnp.swapaxes(state_ssd_conv[o], 0, 1),
                state_ssd[o].reshape(bs, SSD_HEADS * SSD_HEAD_DIM, SSD_STATE),
                state_ret[o].reshape(bs, RET_HEADS * RET_QK_DIM, RET_V_DIM),
                tab128_s, convw, convb, dtb, arow, drow, snorm, spread, rdec_rows, rnorm_rows)
            outs["s_conv"] = jnp.swapaxes(ncb, 0, 1).reshape(1, bs, SSD_CONV - 1, SSD_CONV_DIM)
            outs["s_ssd"] = ns.reshape(1, bs, SSD_HEADS, SSD_HEAD_DIM, SSD_STATE)
            outs["s_ret"] = nr.reshape(1, bs, RET_HEADS, RET_QK_DIM, RET_V_DIM)

        wkv = jnp.concatenate([mem_wk[l], mem_wv[l]], axis=1).astype(BF16)
        qnorm = row1(mem_q_norm[l])
        mkv = norm_proj(mem, row1(norm_mem[l]), wkv, tm=tm, head_norm=row1(mem_k_norm[l]), head_norm_cols=MEM_WIDTH)
        p_mk.append(mkv[:, :MEM_WIDTH].reshape(bp, MEM_LEN, MEM_HEADS, MEM_HEAD_DIM))
        p_mv.append(mkv[:, MEM_WIDTH:].reshape(bp, MEM_LEN, MEM_HEADS, MEM_HEAD_DIM))
        wq, wo = mem_wq[l].astype(BF16), mem_wo[l].astype(BF16)
        gx, gf = row1(norm_xattn[l]), row1(norm_ffn[l])
        ys = res_proj(ys, mix_s, w_out)
        qs = norm_proj(ys, gx, wq, tm=bs, head_norm=qnorm, head_norm_cols=MEM_WIDTH)
        yp, att_s = post_prompt(yp, mix_p, w_out, gx, wq, qnorm, mkv, wo, gf, w1s, w2s, qs, cmk, cmv, layer=l,
                                rows_per_batch=seq, tm=tm)
        ys = res_proj_ffn(ys, att_s, wo, gf, w1s, w2s, layer=l)

    return (yp.reshape(bp, seq, d), ys.reshape(bs, 1, d),
            jnp.stack(p_mk), jnp.stack(p_mv), outs["p_swk"], outs["p_swv"], outs["p_c"], outs["p_n"], outs["p_m"],
            outs["p_conv"], outs["p_ssd"], outs["p_ret"],
            outs["s_swk"], outs["s_swv"], outs["s_c"], outs["s_n"], outs["s_m"],
            outs["s_conv"], outs["s_ssd"], outs["s_ret"])
```

```python
import functools
import math

import jax
import jax.numpy as jnp
import numpy as np
from jax import lax
from jax.experimental import pallas as pl
from jax.experimental.pallas import tpu as pltpu

F32 = jnp.float32
BF16 = jnp.bfloat16

D_MODEL = 1024
PAST_LEN = 8192
EPS = 1e-6
CHUNK = 128
NEG = -1e30

SW_HEADS, SW_KV_HEADS, SW_HEAD_DIM, SW_ROT_DIM = 8, 2, 64, 16
ROPE_THETA = 500000.0
ML_HEADS, ML_QK_DIM, ML_V_DIM = 4, 64, 128
SSD_HEADS, SSD_HEAD_DIM, SSD_GROUPS, SSD_STATE, SSD_CONV = 16, 64, 2, 128, 4
SSD_INNER = SSD_HEADS * SSD_HEAD_DIM
SSD_CONV_DIM = SSD_INNER + 2 * SSD_GROUPS * SSD_STATE
RET_HEADS, RET_QK_DIM, RET_V_DIM = 4, 128, 128
RET_ROPE_THETA = 10000.0
MEM_LEN, MEM_HEADS, MEM_HEAD_DIM = 256, 4, 128
MEM_WIDTH = MEM_HEADS * MEM_HEAD_DIM
FFN_DIM = 4 * D_MODEL
FFN_CHUNK = 512
PROJ_COLS_PER_STAGE = 256

LANES = 128
SUBLANES = 8
VMEM_LIMIT_BYTES = 56 * 1024 * 1024

E_SQ, E_SK, E_SV, E_MQ, E_MK, E_MV, E_MO, E_GATE, EVEN_COLS = 0, 512, 640, 768, 1024, 1280, 1792, 2304, 2432
GATE_F_LANE = 8
O_Z, O_XBC, O_RQ, O_RK, O_RV, O_RG, O_DT, ODD_COLS = 0, 1024, 2560, 3072, 3584, 4096, 4608, 4736


def _mm(a, b):
    return jnp.dot(a.astype(BF16), b.astype(BF16), preferred_element_type=F32)


def _mm_nt(a, b):
    return lax.dot_general(a.astype(BF16), b.astype(BF16), (((1,), (1,)), ((), ())), preferred_element_type=F32)


def _mm_tn(a, b):
    return lax.dot_general(a.astype(BF16), b.astype(BF16), (((0,), (0,)), ((), ())), preferred_element_type=F32)


def _dg(a, b, ca, cb):
    return lax.dot_general(a, b, (((ca,), (cb,)), ((), ())), preferred_element_type=F32)


def _split3(x):
    hi = x.astype(BF16).astype(F32)
    r1 = x - hi
    mid = r1.astype(BF16).astype(F32)
    lo = (r1 - mid).astype(BF16).astype(F32)
    return hi, mid, lo


def _mm_exact_rhs(x, e):
    hi, mid, lo = _split3(x)
    return _dg(hi, e, 1, 0) + _dg(mid, e, 1, 0) + _dg(lo, e, 1, 0)


def _mm_exact_lhs(e, x):
    hi, mid, lo = _split3(x)
    return _dg(e, hi, 1, 0) + _dg(e, mid, 1, 0) + _dg(e, lo, 1, 0)


def _mm_tn_exact_lhs(x, e):
    hi, mid, lo = _split3(x)
    return _dg(hi, e, 0, 0) + _dg(mid, e, 0, 0) + _dg(lo, e, 0, 0)


def _rms(x, g):
    return x * lax.rsqrt(jnp.mean(x * x, axis=-1, keepdims=True) + EPS) * g


def _seg_rms_mxu(x, g, seg_mean):
    return x * lax.rsqrt(_mm_exact_rhs(x * x, seg_mean) + EPS) * g


def _seg_rms(x, g, low):
    xx = x * x
    s_lo = jnp.sum(jnp.where(low, xx, 0.0), axis=-1, keepdims=True)
    s_hi = jnp.sum(jnp.where(low, 0.0, xx), axis=-1, keepdims=True)
    return x * lax.rsqrt(jnp.where(low, s_lo, s_hi) * (1.0 / SW_HEAD_DIM) + EPS) * g


def _sigmoid(x):
    return 1.0 / (1.0 + jnp.exp(-x))


def _silu(x):
    return x * _sigmoid(x)


def _softplus(x):
    return jnp.maximum(x, 0.0) + jnp.log1p(jnp.exp(-jnp.abs(x)))


def _log_sigmoid(x):
    return -_softplus(-x)


def _rope16(x, cos, sin_lo, sin_hi):
    return x * cos + pltpu.roll(x, LANES - 8, 1) * sin_lo + pltpu.roll(x, 8, 1) * sin_hi


def _rope128(x, cos, sin):
    return x * cos + pltpu.roll(x, 64, 1) * sin


def _iota(shape, dim):
    return lax.broadcasted_iota(jnp.int32, shape, dim)


def _cummax_lanes(x):
    lane = _iota(x.shape, 1)
    shift = 1
    while shift < x.shape[1]:
        x = jnp.maximum(x, jnp.where(lane >= shift, pltpu.roll(x, shift, 1), -jnp.inf))
        shift *= 2
    return x


def _norm_proj_kernel(x_ref, g_ref, w_ref, hn_ref, o_ref, *, chunks, head_norm_cols):
    xn = _rms(x_ref[...], g_ref[...]).astype(BF16)
    for c0, cs in chunks:
        r = jnp.dot(xn, w_ref[:, c0:c0 + cs], preferred_element_type=F32)
        if c0 < head_norm_cols:
            parts = [_rms(r[:, i:i + LANES], hn_ref[...]) for i in range(0, cs, LANES)]
            r = jnp.concatenate(parts, axis=1)
        o_ref[:, c0:c0 + cs] = r


def _col_chunks(n, width=512):
    return tuple((c, min(width, n - c)) for c in range(0, n, width))


def norm_proj(x, g, w, *, tm, head_norm=None, head_norm_cols=0):
    n, d = x.shape
    m = w.shape[1]
    if head_norm is None:
        head_norm = jnp.ones((1, LANES), F32)
    kern = functools.partial(_norm_proj_kernel, chunks=_col_chunks(m), head_norm_cols=head_norm_cols)
    return pl.pallas_call(
        kern,
        grid=(n // tm,),
        in_specs=[
            pl.BlockSpec((tm, d), lambda i: (i, 0)),
            pl.BlockSpec((1, d), lambda i: (0, 0)),
            pl.BlockSpec((d, m), lambda i: (0, 0), pipeline_mode=pl.Buffered(1)),
            pl.BlockSpec((1, LANES), lambda i: (0, 0)),
        ],
        out_specs=pl.BlockSpec((tm, m), lambda i: (i, 0)),
        out_shape=jax.ShapeDtypeStruct((n, m), F32),
        compiler_params=pltpu.CompilerParams(dimension_semantics=("arbitrary",), vmem_limit_bytes=VMEM_LIMIT_BYTES),
        name="norm_proj",
    )(x, g, w, head_norm)


def _ffn(x, g_ref, w1_ref, w2_ref):
    h = _rms(x, g_ref[...]).astype(BF16)
    acc = None
    for c in range(0, FFN_DIM, FFN_CHUNK):
        u = jnp.maximum(jnp.dot(h, w1_ref[:, c:c + FFN_CHUNK], preferred_element_type=F32), 0.0)
        t = jnp.dot((u * u).astype(BF16), w2_ref[c:c + FFN_CHUNK, :], preferred_element_type=F32)
        acc = t if acc is None else acc + t
    return x + acc


def _post_chain(x_ref, a_ref, wout_ref, gx_ref, wq_ref, qn_ref, mk_ref, mv_ref, wo_ref, gf_ref, w1_ref, w2_ref, o_ref):
    x = x_ref[...] + _mm(a_ref[...], wout_ref[...])
    yield
    q = jnp.dot(_rms(x, gx_ref[...]).astype(BF16), wq_ref[...], preferred_element_type=F32)
    yield
    outs = []
    for h in range(MEM_HEADS):
        sl = slice(h * MEM_HEAD_DIM, (h + 1) * MEM_HEAD_DIM)
        qh = _rms(q[:, sl], qn_ref[...])
        s = _mm_nt(qh, mk_ref[:, sl]) * (MEM_HEAD_DIM ** -0.5)
        p = jnp.exp(s - jnp.max(s, axis=-1, keepdims=True))
        p = p / jnp.sum(p, axis=-1, keepdims=True)
        outs.append(_mm(p, mv_ref[:, sl]))
        yield
    x = x + _mm(jnp.concatenate(outs, axis=1), wo_ref[...])
    h = _rms(x, gf_ref[...]).astype(BF16)
    yield
    acc = None
    for c in range(0, FFN_DIM, FFN_CHUNK):
        u = jnp.maximum(jnp.dot(h, w1_ref[:, c:c + FFN_CHUNK], preferred_element_type=F32), 0.0)
        t = jnp.dot((u * u).astype(BF16), w2_ref[c:c + FFN_CHUNK, :], preferred_element_type=F32)
        acc = t if acc is None else acc + t
        yield
    o_ref[...] = x + acc


def _xattn_row(q_row, k, v):
    row = _iota((SUBLANES, LANES), 0)
    groups = MEM_LEN * MEM_HEADS // SUBLANES
    q8 = jnp.zeros((SUBLANES, LANES), F32)
    for h in range(MEM_HEADS):
        q8 = jnp.where(row % MEM_HEADS == h, q_row[:, h * MEM_HEAD_DIM:(h + 1) * MEM_HEAD_DIM], q8)
    k3 = k.reshape(groups, SUBLANES, LANES)
    s = jnp.sum(k3 * q8[None], axis=-1, keepdims=True) * (MEM_HEAD_DIM ** -0.5)
    mx = _pair_rows(jnp.max(s, axis=0), jnp.maximum)
    p = jnp.exp(s - mx[None, :, 0:1])
    den = _pair_rows(jnp.sum(p, axis=0), jnp.add)
    o8 = _pair_rows(jnp.sum(p * v.reshape(groups, SUBLANES, LANES), axis=0), jnp.add) / den
    return jnp.concatenate([o8[h:h + 1] for h in range(MEM_HEADS)], axis=1)


def _xattn_chain(q_ref, mk_ref, mv_ref, o_ref):
    for i in range(q_ref.shape[0]):
        o_ref[i:i + 1, :] = _xattn_row(q_ref[i:i + 1, :], mk_ref[i], mv_ref[i])
        yield


def _post_prompt_kernel(x_ref, a_ref, wout_ref, gx_ref, wq_ref, qn_ref, mk_ref, mv_ref, wo_ref, gf_ref, w1_ref, w2_ref,
                        sq_ref, smk_ref, smv_ref,
                        o_ref, so_ref):
    _lockstep([_post_chain(x_ref, a_ref, wout_ref, gx_ref, wq_ref, qn_ref, mk_ref, mv_ref, wo_ref, gf_ref, w1_ref,
                           w2_ref, o_ref),
               _xattn_chain(sq_ref, smk_ref, smv_ref, so_ref)], every=[1, 3])


def post_prompt(x, a, wout, gx, wq, qn, mkv, wo, gf, w1, w2, sq, smk, smv, *, layer, rows_per_batch, tm):
    n, d = x.shape
    ka = a.shape[1]
    steps = n // tm
    bs = sq.shape[0]
    rs = bs // steps
    tiles = rows_per_batch // tm
    const = lambda shape: pl.BlockSpec(shape, lambda i: (0, 0), pipeline_mode=pl.Buffered(1))
    slab = lambda shape: pl.BlockSpec((None,) + shape, lambda i: (layer, 0, 0), pipeline_mode=pl.Buffered(1))
    smem = pl.BlockSpec((None, rs, MEM_LEN * MEM_HEADS, MEM_HEAD_DIM), lambda i: (layer, i, 0, 0))
    srow = pl.BlockSpec((None, rs, MEM_WIDTH), lambda i: (i, 0, 0))
    y, att = pl.pallas_call(
        _post_prompt_kernel,
        grid=(steps,),
        in_specs=[
            pl.BlockSpec((tm, d), lambda i: (i, 0)),
            pl.BlockSpec((tm, ka), lambda i: (i, 0)),
            const((ka, d)),
            const((1, d)),
            const((d, MEM_WIDTH)),
            const((1, MEM_HEAD_DIM)),
            pl.BlockSpec((MEM_LEN, MEM_WIDTH), lambda i: (i // tiles, 0)),
            pl.BlockSpec((MEM_LEN, MEM_WIDTH), lambda i: (i // tiles, 1)),
            const((MEM_WIDTH, d)),
            const((1, d)),
            slab((d, FFN_DIM)),
            slab((FFN_DIM, d)),
            srow, smem, smem,
        ],
        out_specs=[pl.BlockSpec((tm, d), lambda i: (i, 0)), srow],
        out_shape=[jax.ShapeDtypeStruct((n, d), F32), jax.ShapeDtypeStruct((steps, rs, MEM_WIDTH), F32)],
        compiler_params=pltpu.CompilerParams(dimension_semantics=("arbitrary",), vmem_limit_bytes=VMEM_LIMIT_BYTES),
        name="post_prompt",
    )(x, a, wout, gx, wq, qn, mkv, mkv, wo, gf, w1, w2, sq.reshape(steps, rs, MEM_WIDTH), smk, smv)
    return y, att.reshape(bs, MEM_WIDTH)


def _res_proj_kernel(x_ref, a_ref, w_ref, o_ref):
    o_ref[...] = x_ref[...] + _mm(a_ref[...], w_ref[...])


def res_proj(x, a, w):
    n, d = x.shape
    return pl.pallas_call(
        _res_proj_kernel,
        out_shape=jax.ShapeDtypeStruct((n, d), F32),
        compiler_params=pltpu.CompilerParams(vmem_limit_bytes=VMEM_LIMIT_BYTES),
        name="res_proj",
    )(x, a, w)


def _res_proj_ffn_kernel(x_ref, a_ref, w_ref, gf_ref, w1_ref, w2_ref, o_ref):
    x = x_ref[...] + _mm(a_ref[...], w_ref[...])
    o_ref[...] = _ffn(x, gf_ref, w1_ref, w2_ref)


def res_proj_ffn(x, a, w, gf, w1, w2, *, layer):
    n, d = x.shape
    full = lambda t: pl.BlockSpec(t.shape, lambda i: (0, 0))
    slab = lambda shape: pl.BlockSpec((None,) + shape, lambda i: (layer, 0, 0), pipeline_mode=pl.Buffered(1))
    return pl.pallas_call(
        _res_proj_ffn_kernel,
        grid=(1,),
        in_specs=[full(x), full(a), full(w), full(gf), slab((d, FFN_DIM)), slab((FFN_DIM, d))],
        out_specs=pl.BlockSpec((n, d), lambda i: (0, 0)),
        out_shape=jax.ShapeDtypeStruct((n, d), F32),
        compiler_params=pltpu.CompilerParams(dimension_semantics=("arbitrary",), vmem_limit_bytes=VMEM_LIMIT_BYTES),
        name="res_proj_ffn",
    )(x, a, w, gf, w1, w2)


def _pair_rows(x, op):
    xb = jnp.broadcast_to(x, (SUBLANES, LANES))
    return op(xb, pltpu.roll(xb, SUBLANES // 2, 0))


def _proj_chain(x_ref, g_ref, w_ref, dst, width):
    group, _, d = x_ref.shape
    xn = _rms(x_ref[...].reshape(group * CHUNK, d), g_ref[...]).astype(BF16)
    yield
    cols = w_ref.shape[1]
    for c0 in range(0, cols, width):
        cs = min(width, cols - c0)
        r = jnp.dot(xn, w_ref[:, c0:c0 + cs], preferred_element_type=F32)
        for b in range(group):
            dst[b, :, c0:c0 + cs] = r[b * CHUNK:(b + 1) * CHUNK]
        yield


def _even_prompt_kernel(xn_ref, x0_ref, g_ref, w_ref, cos_ref, sinlo_ref, sinhi_ref, qn_ref, kn_ref, sink_ref, segm_ref,
                        triu_ref, gb_ref, onorm_ref,
                        y_ref, kc_ref, vc_ref, caug_ref, m_ref,
                        kprev, vprev, cst, mst, proj):
    n = pl.program_id(0)
    batch = xn_ref.shape[0]
    slot = lax.rem(n, 2)

    @pl.when(n == 0)
    def _():
        kprev[...] = jnp.zeros_like(kprev)
        vprev[...] = jnp.zeros_like(vprev)
        cst[...] = jnp.zeros_like(cst)
        mst[...] = jnp.zeros_like(mst)
        for _ in _proj_chain(x0_ref, g_ref, w_ref, proj.at[0], PROJ_COLS_PER_STAGE):
            pass

    p_ref = proj.at[slot]

    cos, sinlo, sinhi = cos_ref[...], sinlo_ref[...], sinhi_ref[...]
    lane = _iota((1, LANES), 1)
    low = lane < 64
    qi = _iota((CHUNK, 2 * CHUNK), 0)
    si = _iota((CHUNK, 2 * CHUNK), 1)
    valid = (si >= qi) & (si <= qi + CHUNK) & ((si >= CHUNK) | (n > 0))
    causal = _iota((CHUNK, CHUNK), 0) >= _iota((CHUNK, CHUNK), 1)
    ones_col = jnp.where(_iota((CHUNK, LANES), 1) == 0, 1.0, 0.0)
    row64 = _iota((CHUNK, 1), 0) < 64
    new_kv = _lockstep([
        _even_prompt_chunk(p_ref.at[b], y_ref.at[b], kprev.at[b], vprev.at[b], cst.at[b], mst.at[b],
                           (cos, sinlo, sinhi), qn_ref, kn_ref, sink_ref, segm_ref[...], triu_ref, gb_ref, onorm_ref,
                           low, valid, causal, ones_col, row64)
        for b in range(batch)] + [_proj_chain(xn_ref, g_ref, w_ref, proj.at[1 - slot], 2 * PROJ_COLS_PER_STAGE)],
        every=[1] * batch + [8])[:batch]

    @pl.when(n == pl.num_programs(0) - 1)
    def _():
        for b in range(batch):
            kc_ref[b], vc_ref[b] = new_kv[b]
        caug_ref[...] = cst[...]
        m_ref[...] = mst[...]


def _lockstep(chains, every=None):
    every = every or [1] * len(chains)
    results = [None] * len(chains)
    live = list(range(len(chains)))
    rnd = 0
    while live:
        for i in list(live):
            if rnd % every[i]:
                continue
            try:
                next(chains[i])
            except StopIteration as stop:
                results[i] = stop.value
                live.remove(i)
        rnd += 1
    return results


def _even_prompt_chunk(p_ref, y_ref, kprev, vprev, cst, mst, tabs, qn_ref, kn_ref, sink_ref, segm, triu_ref, gb_ref,
                       onorm_ref, low, valid, causal, ones_col, row64):
    cos, sinlo, sinhi = tabs
    tiles = [p_ref[:, E_SK:E_SK + LANES]] + [p_ref[:, E_SQ + j * LANES:E_SQ + (j + 1) * LANES]
                                             for j in range(SW_HEADS // 2)]
    gains = [kn_ref[...]] + [qn_ref[:, j * LANES:(j + 1) * LANES] for j in range(SW_HEADS // 2)]
    ms = _mm_exact_rhs(jnp.concatenate([t * t for t in tiles], axis=0), segm)
    yield
    normed = [_rope16(t * lax.rsqrt(ms[i * CHUNK:(i + 1) * CHUNK] + EPS) * g, cos, sinlo, sinhi)
              for i, (t, g) in enumerate(zip(tiles, gains))]
    k = normed[0]
    v = p_ref[:, E_SV:E_SV + LANES]
    kk = jnp.concatenate([kprev[...], k], axis=0)
    vv = jnp.concatenate([vprev[...], v], axis=0)
    kk_sw = pltpu.roll(kk, 64, 1)
    vv_sw = pltpu.roll(vv, 64, 1)
    kvar = {(0, 0): jnp.where(low, kk, 0.0), (0, 1): jnp.where(low, 0.0, kk_sw),
            (1, 0): jnp.where(low, kk_sw, 0.0), (1, 1): jnp.where(low, 0.0, kk)}
    vvar = {(0, 0): vv, (0, 1): vv_sw, (1, 0): vv_sw, (1, 1): vv}
    yield
    for j in range(SW_HEADS // 2):
        qb = normed[1 + j]
        halves = []
        for pos in range(2):
            h = 2 * j + pos
            kv = h // (SW_HEADS // SW_KV_HEADS)
            s = jnp.where(valid, _mm_nt(qb, kvar[(kv, pos)]) * (SW_HEAD_DIM ** -0.5), NEG)
            yield
            sink = sink_ref[h]
            m = jnp.maximum(jnp.max(s, axis=-1, keepdims=True), sink)
            pr = jnp.exp(s - m)
            yield
            pr = pr / (jnp.sum(pr, axis=-1, keepdims=True) + jnp.exp(sink - m))
            halves.append(_mm(pr, vvar[(kv, pos)]))
            yield
        y_ref[:, j * LANES:(j + 1) * LANES] = jnp.where(low, halves[0], halves[1]).astype(y_ref.dtype)
    kprev[...] = k
    vprev[...] = v
    yield

    gt = p_ref[:, E_GATE:E_GATE + LANES].T
    gi = gt[0:SUBLANES] + gb_ref[0:SUBLANES]
    fl = _log_sigmoid(gt[GATE_F_LANE:GATE_F_LANE + SUBLANES] + gb_ref[SUBLANES:2 * SUBLANES])
    yield
    fcum = _mm_exact_rhs(fl, triu_ref[...])
    dd = gi - fcum
    mprev = mst[...]
    yield
    mt = fcum + jnp.maximum(mprev, _cummax_lanes(dd))
    fend = jnp.broadcast_to(fcum[:, CHUNK - 1:CHUNK], fcum.shape)
    mend = jnp.broadcast_to(mt[:, CHUNK - 1:CHUNK], mt.shape)
    decay = jnp.exp(fend + mprev - mend)
    rows = jnp.concatenate([fcum - mt, jnp.exp(fcum + mprev - mt), jnp.exp(-mt), jnp.exp(fend - fcum + gi - mend),
                            jnp.zeros((CHUNK - 4 * SUBLANES, CHUNK), F32)], axis=0)
    yield
    cols = rows.T
    yield
    for j in range(ML_HEADS // 2):
        qblk = p_ref[:, E_MQ + j * LANES:E_MQ + (j + 1) * LANES]
        kblk = p_ref[:, E_MK + j * LANES:E_MK + (j + 1) * LANES] * (ML_QK_DIM ** -0.5)
        c_old = cst[j]
        qm2 = jnp.concatenate([jnp.where(low, qblk, 0.0), jnp.where(low, 0.0, qblk)], axis=0)
        s2 = _mm_nt(qm2, kblk)
        qc2 = _mm(qm2, c_old)
        yield
        kws, vaugs = [], []
        for pos in range(2):
            h = 2 * j + pos
            rs = slice(pos * CHUNK, (pos + 1) * CHUNK)
            logw = cols[:, h:h + 1] + dd[h:h + 1, :]
            w = jnp.exp(jnp.where(causal, logw, -jnp.inf))
            sqk = s2[rs] * w
            yield
            vh = p_ref[:, E_MV + h * LANES:E_MV + (h + 1) * LANES]
            cs = cols[:, SUBLANES + h:SUBLANES + h + 1]
            num = cs * qc2[rs, :ML_V_DIM] + _mm(sqk, vh)
            den = cs * qc2[rs, ML_V_DIM:ML_V_DIM + 1] + jnp.sum(sqk, axis=-1, keepdims=True)
            hh = num / jnp.maximum(jnp.abs(den), cols[:, 2 * SUBLANES + h:2 * SUBLANES + h + 1])
            yield
            hsl = slice(h * ML_V_DIM, (h + 1) * ML_V_DIM)
            hn = _rms(hh, onorm_ref[:, hsl])
            mo = p_ref[:, E_MO + h * ML_V_DIM:E_MO + (h + 1) * ML_V_DIM]
            y_ref[:, SW_HEADS * SW_HEAD_DIM + h * ML_V_DIM:SW_HEADS * SW_HEAD_DIM + (h + 1) * ML_V_DIM] = (
                hn * _sigmoid(mo)).astype(y_ref.dtype)
            msk = low if pos == 0 else jnp.logical_not(low)
            kws.append(jnp.where(msk, kblk, 0.0) * cols[:, 3 * SUBLANES + h:3 * SUBLANES + h + 1])
            vaugs.append(jnp.concatenate([vh, ones_col], axis=1))
            yield
        upd = _mm_tn(jnp.concatenate(kws, axis=0), jnp.concatenate(vaugs, axis=0))
        dec = jnp.where(row64, decay[2 * j:2 * j + 1, 0:1], decay[2 * j + 1:2 * j + 2, 0:1])
        cst[j] = dec * c_old + upd
        yield
    mst[...] = mend
    return k, v


def even_prompt(x, g, w, tabs, qn, kn, sinks, segm, triu, gb, onorm):
    batch, seq, d = x.shape
    nc = seq // CHUNK
    tab = pl.BlockSpec((CHUNK, LANES), lambda n: (n, 0))
    const = lambda shape: pl.BlockSpec(shape, lambda n: (0,) * len(shape))
    state_shapes = [(batch, CHUNK, LANES), (batch, CHUNK, LANES),
                    (batch, ML_HEADS // 2, 2 * ML_QK_DIM, 2 * ML_V_DIM), (batch, SUBLANES, LANES)]
    return pl.pallas_call(
        _even_prompt_kernel,
        grid=(nc,),
        in_specs=[
            pl.BlockSpec((batch, CHUNK, d), lambda n: (0, jnp.minimum(n + 1, nc - 1), 0)),
            pl.BlockSpec((batch, CHUNK, d), lambda n: (0, 0, 0)),
            const((1, d)), pl.BlockSpec((d, EVEN_COLS), lambda n: (0, 0), pipeline_mode=pl.Buffered(1)),
            tab, tab, tab,
            const((1, SW_HEADS * SW_HEAD_DIM)), const((1, LANES)),
            pl.BlockSpec(memory_space=pltpu.SMEM),
            const((LANES, LANES)), const((CHUNK, CHUNK)), const((2 * SUBLANES, LANES)),
            const((1, ML_HEADS * ML_V_DIM)),
        ],
        out_specs=[pl.BlockSpec((batch, CHUNK, D_MODEL), lambda n: (0, n, 0))] + [const(s) for s in state_shapes],
        out_shape=[jax.ShapeDtypeStruct((batch, seq, D_MODEL), BF16)]
        + [jax.ShapeDtypeStruct(s, F32) for s in state_shapes],
        scratch_shapes=[pltpu.VMEM(s, F32) for s in state_shapes] + [pltpu.VMEM((2, batch, CHUNK, EVEN_COLS), F32)],
        compiler_params=pltpu.CompilerParams(dimension_semantics=("arbitrary",), vmem_limit_bytes=VMEM_LIMIT_BYTES),
        name="even_prompt",
    )(x, x, g, w, *tabs, qn, kn, sinks, segm, triu, gb, onorm)


def _odd_prompt_kernel(xn_ref, x0_ref, g_ref, w_ref, cosr_ref, sinr_ref, convw_ref, convb_ref, dtb_ref, arow_ref,
                       drow_ref, snorm_ref, tril_ref, dmat_ref, qs_ref, ks_ref, cd_ref, rnorm_ref,
                       y_ref, conv_ref, s_ref, r_ref,
                       ext, sst, rst, proj):
    n = pl.program_id(1)
    batch = xn_ref.shape[0]
    slot = lax.rem(n, 2)

    @pl.when(n == 0)
    def _():
        ext[:, 0:SUBLANES] = jnp.zeros((batch, SUBLANES, SSD_CONV_DIM), F32)
        sst[...] = jnp.zeros_like(sst)
        rst[...] = jnp.zeros_like(rst)
        for _ in _proj_chain(x0_ref, g_ref, w_ref, proj.at[0], PROJ_COLS_PER_STAGE):
            pass

    p_ref = proj.at[slot]

    lane = _iota((1, LANES), 1)
    low = lane < 64
    row64 = _iota((CHUNK, 1), 0) < 64
    causal = _iota((CHUNK, CHUNK), 0) >= _iota((CHUNK, CHUNK), 1)
    tails = _lockstep([
        _odd_prompt_chunk(p_ref.at[b], y_ref.at[b], ext.at[b], sst.at[b], rst.at[b], cosr_ref, sinr_ref, convw_ref,
                          convb_ref, dtb_ref, arow_ref, drow_ref, snorm_ref, tril_ref, dmat_ref, qs_ref, ks_ref,
                          cd_ref, rnorm_ref, low, row64, causal)
        for b in range(batch)] + [_proj_chain(xn_ref, g_ref, w_ref, proj.at[1 - slot], PROJ_COLS_PER_STAGE)],
        every=[1] * batch + [2])[:batch]

    @pl.when(n == pl.num_programs(1) - 1)
    def _():
        for b in range(batch):
            conv_ref[b] = tails[b]
        s_ref[...] = sst[...]
        r_ref[...] = rst[...]


def _odd_prompt_chunk(p_ref, y_ref, ext, sst, rst, cosr_ref, sinr_ref, convw_ref, convb_ref, dtb_ref, arow_ref,
                      drow_ref, snorm_ref, tril_ref, dmat_ref, qs_ref, ks_ref, cd_ref, rnorm_ref, low, row64, causal):
    tail = SUBLANES
    ext[tail:tail + CHUNK] = p_ref[:, O_XBC:O_XBC + SSD_CONV_DIM]
    yield
    xe = ext[...]
    acc = None
    for jj in range(SSD_CONV):
        shift = SSD_CONV - 1 - jj
        tap = (pltpu.roll(xe, shift, 0) if shift else xe)[tail:tail + CHUNK] * convw_ref[jj:jj + 1]
        acc = tap if acc is None else acc + tap
    xact = _silu(acc + convb_ref[...])
    new_tail = ext[CHUNK:CHUNK + tail]
    ext[0:tail] = new_tail
    yield

    dt = _softplus(p_ref[:, O_DT:O_DT + LANES] + dtb_ref[...])
    cum = _mm_exact_lhs(tril_ref[...], dt * arow_ref[...])
    yield
    cum_t = cum.T
    dt_t = dt.T
    ecum = jnp.exp(cum)
    cend = cum[CHUNK - 1:CHUNK, :]
    wend = jnp.exp(cend - cum) * dt
    eend = jnp.exp(cend)
    yield
    pairs_per_group = SSD_HEADS // SSD_GROUPS // 2
    ys = []
    for g in range(SSD_GROUPS):
        bc = xact[:, SSD_INNER + g * SSD_STATE:SSD_INNER + (g + 1) * SSD_STATE]
        cc = xact[:, SSD_INNER + (SSD_GROUPS + g) * SSD_STATE:SSD_INNER + (SSD_GROUPS + g + 1) * SSD_STATE]
        cb = _mm_nt(cc, bc)
        yield
        for jg in range(pairs_per_group):
            j = g * pairs_per_group + jg
            ha, hb = 2 * j, 2 * j + 1
            xp = xact[:, j * LANES:(j + 1) * LANES]
            s_old = sst[j]
            y = jnp.where(low, ecum[:, ha:ha + 1], ecum[:, hb:hb + 1]) * _mm_nt(cc, s_old)
            yield
            wmats = []
            for h in (ha, hb):
                seg = cum[:, h:h + 1] - cum_t[h:h + 1, :]
                wmats.append(cb * jnp.exp(jnp.where(causal, seg, -jnp.inf)) * dt_t[h:h + 1, :])
                yield
            y = y + _mm(jnp.concatenate(wmats, axis=1),
                        jnp.concatenate([jnp.where(low, xp, 0.0), jnp.where(low, 0.0, xp)], axis=0))
            yield
            xw = xp * jnp.where(low, wend[:, ha:ha + 1], wend[:, hb:hb + 1])
            sst[j] = jnp.where(row64, eend[:, ha:ha + 1], eend[:, hb:hb + 1]) * s_old + _mm_tn(xw, bc)
            ys.append(y)
            yield
        gs = slice(g * SSD_INNER // SSD_GROUPS, (g + 1) * SSD_INNER // SSD_GROUPS)
        yg = jnp.concatenate(ys[g * pairs_per_group:(g + 1) * pairs_per_group], axis=1)
        yg = (yg + drow_ref[:, gs] * xact[:, gs]) * _silu(p_ref[:, O_Z + gs.start:O_Z + gs.stop])
        y_ref[:, gs] = _rms(yg, snorm_ref[:, gs]).astype(y_ref.dtype)
        yield

    cosr, sinr = cosr_ref[...], sinr_ref[...]
    for h in range(RET_HEADS):
        hs = h * LANES
        q = _rope128(p_ref[:, O_RQ + hs:O_RQ + hs + LANES], cosr, sinr)
        k = _rope128(p_ref[:, O_RK + hs:O_RK + hs + LANES], cosr, sinr) * (RET_QK_DIM ** -0.5)
        v = p_ref[:, O_RV + hs:O_RV + hs + LANES]
        yield
        r_old = rst[h]
        o = _mm(_mm_nt(q, k) * dmat_ref[h], v) + qs_ref[h] * _mm(q, r_old)
        yield
        rst[h] = cd_ref[h] * r_old + _mm_tn(k * ks_ref[h], v)
        xc = o - jnp.mean(o, axis=-1, keepdims=True)
        yn = xc * lax.rsqrt(jnp.mean(xc * xc, axis=-1, keepdims=True) + EPS) * rnorm_ref[:, hs:hs + LANES]
        y_ref[:, SSD_INNER + hs:SSD_INNER + hs + LANES] = (
            yn * _silu(p_ref[:, O_RG + hs:O_RG + hs + LANES])).astype(y_ref.dtype)
        yield
    return new_tail


def odd_prompt(x, g, w, tabs, convw, convb, dtb, arow, drow, snorm, tril, ret_consts, rnorm, *, group):
    batch, seq, d = x.shape
    nc = seq // CHUNK
    tab = pl.BlockSpec((CHUNK, LANES), lambda g, n: (n, 0))
    const = lambda shape: pl.BlockSpec(shape, lambda g, n: (0,) * len(shape))
    per_g = lambda shape: pl.BlockSpec((group,) + shape, lambda g, n: (g,) + (0,) * len(shape))
    hc = (RET_HEADS, CHUNK, LANES)
    ywidth = SSD_INNER + RET_HEADS * RET_V_DIM
    states = [(SUBLANES, SSD_CONV_DIM), (SSD_HEADS // 2, LANES, SSD_STATE), hc]
    return pl.pallas_call(
        _odd_prompt_kernel,
        grid=(batch // group, seq // CHUNK),
        in_specs=[
            pl.BlockSpec((group, CHUNK, d), lambda g, n: (g, jnp.minimum(n + 1, nc - 1), 0)),
            pl.BlockSpec((group, CHUNK, d), lambda g, n: (g, 0, 0)),
            const((1, d)), pl.BlockSpec((d, ODD_COLS), lambda g, n: (0, 0), pipeline_mode=pl.Buffered(1)),
            tab, tab,
            const((SSD_CONV, SSD_CONV_DIM)), const((1, SSD_CONV_DIM)), const((1, LANES)), const((1, LANES)),
            const((1, SSD_INNER)), const((1, SSD_INNER)), const((CHUNK, CHUNK)),
            const(hc), const(hc), const(hc), const(hc), const((1, RET_HEADS * RET_V_DIM)),
        ],
        out_specs=[pl.BlockSpec((group, CHUNK, ywidth), lambda g, n: (g, n, 0))] + [per_g(s) for s in states],
        out_shape=[jax.ShapeDtypeStruct((batch, seq, ywidth), BF16)]
        + [jax.ShapeDtypeStruct((batch,) + s, F32) for s in states],
        scratch_shapes=[pltpu.VMEM((group, SUBLANES + CHUNK, SSD_CONV_DIM), F32),
                        pltpu.VMEM((group,) + states[1], F32), pltpu.VMEM((group,) + states[2], F32),
                        pltpu.VMEM((2, group, CHUNK, ODD_COLS), F32)],
        compiler_params=pltpu.CompilerParams(dimension_semantics=("arbitrary", "arbitrary"),
                                             vmem_limit_bytes=VMEM_LIMIT_BYTES),
        name="odd_prompt",
    )(x, x, g, w, *tabs, convw, convb, dtb, arow, drow, snorm, tril, *ret_consts, rnorm)


def _lane_to_rows(g, offset):
    sel = _iota(g.shape, 1) == _iota(g.shape, 0) + offset
    return jnp.sum(jnp.where(sel, g, 0.0), axis=-1, keepdims=True)


def _block_rows(x, i, nblk, blk):
    row = _iota((SUBLANES, blk), 0)
    out = jnp.zeros((SUBLANES, blk), F32)
    for b in range(nblk):
        out = jnp.where(row == b, x[i:i + 1, b * blk:(b + 1) * blk], out)
    return out


def _even_sample_kernel(p_ref, bk_ref, bv_ref, c_ref, nrow_ref, mrow_ref, cos_ref, sinlo_ref, sinhi_ref, qn_ref,
                        kn_ref, sink_ref, gb_ref, onorm_ref,
                        y_ref, nk_ref, nv_ref, nc_ref, nn_ref, nm_ref):
    R = SUBLANES
    cos, sinlo, sinhi = cos_ref[...], sinlo_ref[...], sinhi_ref[...]
    row = _iota((R, LANES), 0)
    lane = _iota((R, LANES), 1)
    low = lane < 64
    group = SW_HEADS // SW_KV_HEADS
    scale = SW_HEAD_DIM ** -0.5
    sink = sink_ref[:, 0:1]
    last = _iota((CHUNK, LANES), 0) == CHUNK - 1

    k = _rope16(_seg_rms(p_ref[:, E_SK:E_SK + LANES], kn_ref[...], low), cos, sinlo, sinhi)
    v = p_ref[:, E_SV:E_SV + LANES]
    qb, qb_sw = [], []
    for j in range(SW_HEADS // 2):
        sl = slice(E_SQ + j * LANES, E_SQ + (j + 1) * LANES)
        qb.append(_rope16(_seg_rms(p_ref[:, sl], qn_ref[:, sl], low), cos, sinlo, sinhi))
        qb_sw.append(pltpu.roll(qb[j], 64, 1))

    g = p_ref[:, E_GATE:E_GATE + LANES]
    ic = g + gb_ref[0:1]
    fl = _log_sigmoid(pltpu.roll(g, LANES - GATE_F_LANE, 1) + gb_ref[1:2])
    mprev = mrow_ref[...]
    mt = jnp.maximum(fl + mprev, ic)
    w_all = jnp.exp(ic - mt)
    cs_all = jnp.exp(fl + mprev - mt)
    em_all = jnp.exp(-mt)
    nm_ref[...] = mt
    width = ML_HEADS * ML_QK_DIM
    own = _iota((R, width), 1) // ML_QK_DIM == _iota((R, width), 0)
    kscaled = p_ref[:, E_MK:E_MK + width] * (ML_QK_DIM ** -0.5)

    def one_row(i):
        qm = jnp.zeros((R, LANES), F32)
        for j in range(SW_HEADS // 2):
            for pos in range(2):
                h = 2 * j + pos
                kv = h // group
                src = (qb[j] if pos == kv else qb_sw[j])[i:i + 1]
                qm = jnp.where((row == h) & (low if kv == 0 else jnp.logical_not(low)), src, qm)
        bk, bv = bk_ref[i], bv_ref[i]
        ki, vi = k[i:i + 1], v[i:i + 1]
        s = _dg(qm, bk, 1, 1) * scale
        s_new = jnp.sum(qm * ki, axis=-1, keepdims=True) * scale
        yield
        m = jnp.maximum(jnp.maximum(jnp.max(s, axis=-1, keepdims=True), s_new), sink)
        pr = jnp.exp(s - m)
        p_new = jnp.exp(s_new - m)
        yield
        den = jnp.sum(pr, axis=-1, keepdims=True) + p_new + jnp.exp(sink - m)
        o = (_dg(pr, bv, 1, 0) + p_new * vi) / den
        yield
        o_sw = pltpu.roll(o, 64, 1)
        for j in range(SW_HEADS // 2):
            halves = []
            for pos in range(2):
                h = 2 * j + pos
                halves.append((o if pos == h // group else o_sw)[h:h + 1, :])
            y_ref[i:i + 1, j * LANES:(j + 1) * LANES] = jnp.where(low[0:1], halves[0], halves[1])
        nk_ref[i] = jnp.where(last, ki, pltpu.roll(bk, CHUNK - 1, 0))
        nv_ref[i] = jnp.where(last, vi, pltpu.roll(bv, CHUNK - 1, 0))
        yield

        w = _lane_to_rows(jnp.broadcast_to(w_all[i:i + 1], (R, LANES)), 0)
        cs = _lane_to_rows(jnp.broadcast_to(cs_all[i:i + 1], (R, LANES)), 0)
        em = _lane_to_rows(jnp.broadcast_to(em_all[i:i + 1], (R, LANES)), 0)
        qrows = jnp.where(own, p_ref[i:i + 1, E_MQ:E_MQ + width], 0.0)
        krows = jnp.where(own, kscaled[i:i + 1], 0.0)
        c_old = c_ref[i]
        qc = _dg(qrows, c_old, 1, 0)
        qn_dot = jnp.sum(qrows * nrow_ref[i:i + 1], axis=-1, keepdims=True)
        sqk = jnp.sum(qrows * krows, axis=-1, keepdims=True) * w
        yield
        v4 = _block_rows(p_ref[:, E_MV:E_MV + ML_HEADS * ML_V_DIM], i, ML_HEADS, ML_V_DIM)
        mo4 = _block_rows(p_ref[:, E_MO:E_MO + ML_HEADS * ML_V_DIM], i, ML_HEADS, ML_V_DIM)
        num = cs * qc + sqk * v4
        dn = cs * qn_dot + sqk
        hh = num / jnp.maximum(jnp.abs(dn), em)
        hn = _rms(hh, onorm_ref[...]) * _sigmoid(mo4)
        yield
        for h in range(ML_HEADS):
            c0 = SW_HEADS * SW_HEAD_DIM + h * ML_V_DIM
            y_ref[i:i + 1, c0:c0 + ML_V_DIM] = hn[h:h + 1]
        dec_col = jnp.concatenate(
            [jnp.broadcast_to(cs[h:h + 1, 0:1], (ML_QK_DIM, ML_V_DIM)) for h in range(ML_HEADS)], axis=0)
        nc_ref[i] = dec_col * c_old + _dg(krows * w, v4, 0, 0)
        dec_lanes = jnp.sum(jnp.where(own, cs, 0.0), axis=0, keepdims=True)
        nn_ref[i:i + 1] = dec_lanes * nrow_ref[i:i + 1] + jnp.sum(krows * w, axis=0, keepdims=True)
        yield

    _lockstep([one_row(i) for i in range(R)])


def even_sample(proj, bk, bv, c, nrow, mrow, tabs, qn, kn, sink_rows, gb, onorm_rows):
    b = proj.shape[0]
    R = SUBLANES
    width = ML_HEADS * ML_QK_DIM
    rows = lambda w: pl.BlockSpec((R, w), lambda i: (i, 0))
    per_b = lambda shape: pl.BlockSpec((R,) + shape, lambda i: (i,) + (0,) * len(shape))
    const = lambda shape: pl.BlockSpec(shape, lambda i: (0,) * len(shape))
    tab = const((R, LANES))
    return pl.pallas_call(
        _even_sample_kernel,
        grid=(b // R,),
        in_specs=[
            rows(EVEN_COLS), per_b((CHUNK, LANES)), per_b((CHUNK, LANES)), per_b((width, ML_V_DIM)),
            rows(width), rows(LANES), tab, tab, tab,
            const((1, SW_HEADS * SW_HEAD_DIM)), const((1, LANES)), const((R, LANES)),
            const((2, LANES)), const((R, ML_V_DIM)),
        ],
        out_specs=[
            rows(D_MODEL), per_b((CHUNK, LANES)), per_b((CHUNK, LANES)), per_b((width, ML_V_DIM)),
            rows(width), rows(LANES),
        ],
        out_shape=[
            jax.ShapeDtypeStruct((b, D_MODEL), F32),
            jax.ShapeDtypeStruct((b, CHUNK, LANES), F32), jax.ShapeDtypeStruct((b, CHUNK, LANES), F32),
            jax.ShapeDtypeStruct((b, width, ML_V_DIM), F32), jax.ShapeDtypeStruct((b, width), F32),
            jax.ShapeDtypeStruct((b, LANES), F32),
        ],
        compiler_params=pltpu.CompilerParams(dimension_semantics=("arbitrary",), vmem_limit_bytes=VMEM_LIMIT_BYTES),
        name="even_sample",
    )(proj, bk, bv, c, nrow, mrow, *tabs, qn, kn, sink_rows, gb, onorm_rows)


def _odd_sample_kernel(p_ref, cb_ref, s_ref, r_ref, cosr_ref, sinr_ref, convw_ref, convb_ref, dtb_ref, arow_ref,
                       drow_ref, snorm_ref, spread_ref, rdec_ref, rnorm_ref,
                       y_ref, ncb_ref, ns_ref, nr_ref):
    R = SUBLANES
    xbc = p_ref[:, O_XBC:O_XBC + SSD_CONV_DIM]
    acc = cb_ref[0] * convw_ref[0:1]
    for jj in range(1, SSD_CONV - 1):
        acc = acc + cb_ref[jj] * convw_ref[jj:jj + 1]
    acc = acc + xbc * convw_ref[SSD_CONV - 1:SSD_CONV]
    xact = _silu(acc + convb_ref[...])
    for jj in range(SSD_CONV - 2):
        ncb_ref[jj] = cb_ref[jj + 1]
    ncb_ref[SSD_CONV - 2] = xbc

    dt = _softplus(p_ref[:, O_DT:O_DT + LANES] + dtb_ref[...])
    dec = jnp.exp(dt * arow_ref[...])
    xs = xact[:, 0:SSD_INNER]
    xdt = xs * _mm_exact_rhs(dt, spread_ref[...])
    gown = _iota((R, SSD_INNER), 1) // (SSD_INNER // SSD_GROUPS) == _iota((R, SSD_INNER), 0)
    bpart = xact[:, SSD_INNER:SSD_INNER + SSD_GROUPS * SSD_STATE]
    cpart = xact[:, SSD_INNER + SSD_GROUPS * SSD_STATE:SSD_CONV_DIM]

    cosr, sinr = cosr_ref[...], sinr_ref[...]
    width = RET_HEADS * RET_QK_DIM
    q4 = jnp.concatenate([_rope128(p_ref[:, O_RQ + h * LANES:O_RQ + (h + 1) * LANES], cosr, sinr)
                          for h in range(RET_HEADS)], axis=1)
    k4 = jnp.concatenate([_rope128(p_ref[:, O_RK + h * LANES:O_RK + (h + 1) * LANES], cosr, sinr)
                          for h in range(RET_HEADS)], axis=1) * (RET_QK_DIM ** -0.5)
    own = _iota((R, width), 1) // RET_QK_DIM == _iota((R, width), 0)
    gam = rdec_ref[:, 0:1]
    gam_col = jnp.concatenate(
        [jnp.broadcast_to(rdec_ref[h:h + 1, :], (RET_QK_DIM, RET_V_DIM)) for h in range(RET_HEADS)], axis=0)

    def one_row(i):
        brows = _block_rows(bpart, i, SSD_GROUPS, SSD_STATE)
        crows = _block_rows(cpart, i, SSD_GROUPS, SSD_STATE)
        xw = jnp.where(gown, xdt[i:i + 1], 0.0)
        dec_col = jnp.concatenate(
            [jnp.broadcast_to(dec[i:i + 1, h:h + 1], (SSD_HEAD_DIM, SSD_STATE)) for h in range(SSD_HEADS)], axis=0)
        yield
        s_new = dec_col * s_ref[i] + _dg(xw, brows, 0, 0)
        ns_ref[i] = s_new
        yield
        yrows = _dg(crows, s_new, 1, 1)
        y_row = jnp.sum(jnp.where(gown, yrows, 0.0), axis=0, keepdims=True)
        yield

        qrows = jnp.where(own, q4[i:i + 1], 0.0)
        krows = jnp.where(own, k4[i:i + 1], 0.0)
        v4 = _block_rows(p_ref[:, O_RV:O_RV + RET_HEADS * RET_V_DIM], i, RET_HEADS, RET_V_DIM)
        g4 = _block_rows(p_ref[:, O_RG:O_RG + RET_HEADS * RET_V_DIM], i, RET_HEADS, RET_V_DIM)
        r_old = r_ref[i]
        att = jnp.sum(qrows * krows, axis=-1, keepdims=True)
        o = att * v4 + gam * _dg(qrows, r_old, 1, 0)
        yield
        nr_ref[i] = gam_col * r_old + _dg(krows, v4, 0, 0)
        yield
        xc = o - jnp.mean(o, axis=-1, keepdims=True)
        yn = xc * lax.rsqrt(jnp.mean(xc * xc, axis=-1, keepdims=True) + EPS) * rnorm_ref[...] * _silu(g4)
        for h in range(RET_HEADS):
            y_ref[i:i + 1, SSD_INNER + h * RET_V_DIM:SSD_INNER + (h + 1) * RET_V_DIM] = yn[h:h + 1]
        return y_row

    ys = _lockstep([one_row(i) for i in range(R)])
    y = (jnp.concatenate(ys, axis=0) + drow_ref[...] * xs) * _silu(p_ref[:, O_Z:O_Z + SSD_INNER])
    gw = SSD_INNER // SSD_GROUPS
    for g in range(SSD_GROUPS):
        y_ref[:, g * gw:(g + 1) * gw] = _rms(y[:, g * gw:(g + 1) * gw], snorm_ref[:, g * gw:(g + 1) * gw])


def odd_sample(proj, cbuf, s, r, tabs, convw, convb, dtb, arow, drow, snorm, spread, rdec_rows, rnorm_rows):
    b = proj.shape[0]
    R = SUBLANES
    rows = lambda w: pl.BlockSpec((R, w), lambda i: (i, 0))
    per_b = lambda shape: pl.BlockSpec((R,) + shape, lambda i: (i,) + (0,) * len(shape))
    const = lambda shape: pl.BlockSpec(shape, lambda i: (0,) * len(shape))
    tab = const((R, LANES))
    conv = pl.BlockSpec((SSD_CONV - 1, R, SSD_CONV_DIM), lambda i: (0, i, 0))
    ywidth = SSD_INNER + RET_HEADS * RET_V_DIM
    sshape = (SSD_HEADS * SSD_HEAD_DIM, SSD_STATE)
    rshape = (RET_HEADS * RET_QK_DIM, RET_V_DIM)
    return pl.pallas_call(
        _odd_sample_kernel,
        grid=(b // R,),
        in_specs=[
            rows(ODD_COLS), conv, per_b(sshape), per_b(rshape), tab, tab,
            const((SSD_CONV, SSD_CONV_DIM)), const((1, SSD_CONV_DIM)), const((1, LANES)), const((1, LANES)),
            const((1, SSD_INNER)), const((1, SSD_INNER)), const((LANES, SSD_INNER)),
            const((R, LANES)), const((R, RET_V_DIM)),
        ],
        out_specs=[rows(ywidth), conv, per_b(sshape), per_b(rshape)],
        out_shape=[
            jax.ShapeDtypeStruct((b, ywidth), F32),
            jax.ShapeDtypeStruct((SSD_CONV - 1, b, SSD_CONV_DIM), F32),
            jax.ShapeDtypeStruct((b,) + sshape, F32), jax.ShapeDtypeStruct((b,) + rshape, F32),
        ],
        compiler_params=pltpu.CompilerParams(dimension_semantics=("arbitrary",), vmem_limit_bytes=VMEM_LIMIT_BYTES),
        name="odd_sample",
    )(proj, cbuf, s, r, *tabs, convw, convb, dtb, arow, drow, snorm, spread, rdec_rows, rnorm_rows)


def _pad_cols(w, n):
    return jnp.pad(w, ((0, 0), (0, n - w.shape[1])))


def _even_w_in(w):
    sq, sk, sv, mq, mk, mv, mo, mi, mf = jnp.split(w, [512, 640, 768, 1024, 1280, 1792, 2304, 2308], axis=1)
    gates = jnp.concatenate([_pad_cols(mi, GATE_F_LANE), _pad_cols(mf, LANES - GATE_F_LANE)], axis=1)
    return jnp.concatenate([sq, sk, sv, mq, mk, mv, mo, gates], axis=1).astype(BF16)


def _odd_w_in(w):
    z, xbc, dt, rq, rk, rv, rg = jnp.split(w, [1024, 2560, 2576, 3088, 3600, 4112], axis=1)
    return jnp.concatenate([z, xbc, rq, rk, rv, rg, _pad_cols(dt, LANES)], axis=1).astype(BF16)


def _lane_angles(pos, rot_dim, theta, freq_of_lane):
    half = rot_dim // 2
    inv = jnp.power(jnp.float32(theta), -jnp.arange(half, dtype=F32) * (2.0 / rot_dim))
    return jnp.asarray(pos).astype(F32)[:, None] * inv[freq_of_lane][None, :]


def _rope16_tables(pos):
    half = SW_ROT_DIM // 2
    d = np.arange(LANES) % SW_HEAD_DIM
    ang = _lane_angles(pos, SW_ROT_DIM, ROPE_THETA, d % half)
    cos, sin = jnp.cos(ang), jnp.sin(ang)
    return (jnp.where(d < SW_ROT_DIM, cos, 1.0), jnp.where(d < half, -sin, 0.0),
            jnp.where((d >= half) & (d < SW_ROT_DIM), sin, 0.0))


def _rope128_tables(pos):
    half = RET_QK_DIM // 2
    lane = np.arange(LANES)
    ang = _lane_angles(pos, RET_QK_DIM, RET_ROPE_THETA, lane % half)
    return jnp.cos(ang), jnp.where(lane < half, -jnp.sin(ang), jnp.sin(ang))


def _ret_consts():
    L = CHUNK
    f = np.float32
    lg = np.log(f(1.0) - np.exp2(f(-5.0) - np.arange(RET_HEADS, dtype=f))).astype(f)
    idx = np.arange(L, dtype=f)
    diff = idx[:, None] - idx[None, :]
    with np.errstate(invalid="ignore"):
        dmat = np.exp(np.where(diff >= 0, diff[None] * lg[:, None, None], -np.inf)).astype(f)
    q_scale = np.exp((idx[None] + f(1.0)) * lg[:, None]).astype(f)
    k_scale = np.exp((f(L) - f(1.0) - idx[None]) * lg[:, None]).astype(f)
    chunk_decay = np.exp(f(L) * lg).astype(f)
    bc = lambda t: np.ascontiguousarray(np.broadcast_to(t[:, :, None], (RET_HEADS, L, LANES)))
    cd = np.ascontiguousarray(np.broadcast_to(chunk_decay[:, None, None], (RET_HEADS, L, LANES)))
    return dmat, bc(q_scale), bc(k_scale), cd, lg


def _rows8(t):
    return jnp.pad(t, ((0, SUBLANES - t.shape[0]), (0, 0)))


def _gate_bias_rows(gb):
    ib = jnp.broadcast_to(gb[:ML_HEADS, None], (ML_HEADS, LANES))
    fb = jnp.broadcast_to(gb[ML_HEADS:, None], (ML_HEADS, LANES))
    return jnp.concatenate([_rows8(ib), _rows8(fb)], axis=0)


def kernel(x_prompt, x_sample, cache_mem_k, cache_mem_v, cache_swa_k, cache_swa_v, state_mlstm_C, state_mlstm_n,
           state_mlstm_m, state_ssd_conv, state_ssd, state_ret, mem_prompt, norm_mix, norm_xattn, norm_mem, norm_ffn,
           even_w_in, mlstm_gate_bias, swa_q_norm, swa_k_norm, swa_sinks, mlstm_out_norm, even_w_out, odd_w_in,
           ssd_conv_w, ssd_conv_b, ssd_dt_bias, ssd_a_log, ssd_d, ssd_norm, ret_norm, odd_w_out, mem_wq, mem_wk,
           mem_wv, mem_q_norm, mem_k_norm, mem_wo, ffn_w1, ffn_w2):
    bp, seq, d = x_prompt.shape
    bs = x_sample.shape[0]
    depth = norm_mix.shape[0]
    tm = 512

    pos_p = np.arange(seq, dtype=np.int32)
    pos_s = np.full((SUBLANES,), PAST_LEN, dtype=np.int32)
    tab16_p, tab16_s = _rope16_tables(pos_p), _rope16_tables(pos_s)
    tab128_p, tab128_s = _rope128_tables(pos_p), _rope128_tables(pos_s)
    dmat, q_scale, k_scale, chunk_decay, lg = _ret_consts()
    rdec_rows = _rows8(jnp.asarray(np.broadcast_to(np.exp(lg)[:, None], (RET_HEADS, LANES))))
    ii = np.arange(CHUNK)
    triu = (ii[:, None] <= ii[None, :]).astype(np.float32)
    tril = (ii[:, None] >= ii[None, :]).astype(np.float32)
    jj = np.arange(LANES)
    segm = np.where(jj[:, None] // SW_HEAD_DIM == jj[None, :] // SW_HEAD_DIM, 1.0 / SW_HEAD_DIM, 0.0).astype(np.float32)
    spread = (jj[:, None] == np.arange(SSD_INNER)[None, :] // SSD_HEAD_DIM).astype(np.float32)
    row1 = lambda t: t.reshape(1, -1).astype(F32)
    pad_lanes = lambda t: jnp.pad(t.reshape(1, -1).astype(F32), ((0, 0), (0, LANES - t.shape[-1])))

    yp = x_prompt.reshape(bp * seq, d)
    ys = x_sample.reshape(bs, d)
    mem = mem_prompt.reshape(bp * MEM_LEN, d)
    cmk = cache_mem_k.reshape(depth, bs, MEM_LEN * MEM_HEADS, MEM_HEAD_DIM)
    cmv = cache_mem_v.reshape(depth, bs, MEM_LEN * MEM_HEADS, MEM_HEAD_DIM)
    w1s, w2s = ffn_w1.astype(BF16), ffn_w2.astype(BF16)
    p_mk, p_mv = [], []
    outs = {}
    for l in range(depth):
        g_mix = row1(norm_mix[l])
        if l % 2 == 0:
            e = l // 2
            w_in = _even_w_in(even_w_in[e])
            w_out = even_w_out[e].astype(BF16)
            qn = row1(jnp.tile(swa_q_norm[e], SW_HEADS))
            kn = row1(jnp.tile(swa_k_norm[e], SW_KV_HEADS))
            gb = _gate_bias_rows(mlstm_gate_bias[e].astype(F32))
            onorm = row1(mlstm_out_norm[e])
            sinks = swa_sinks[e].astype(F32)
            mix_p, kc, vc, caug, mm = even_prompt(yp.reshape(bp, seq, d), g_mix, w_in, tab16_p, qn, kn, sinks, segm,
                                                  triu, gb, onorm)
            mix_p = mix_p.reshape(bp * seq, -1)
            outs["p_swk"] = kc.reshape(1, bp, CHUNK, SW_KV_HEADS, SW_HEAD_DIM)
            outs["p_swv"] = vc.reshape(1, bp, CHUNK, SW_KV_HEADS, SW_HEAD_DIM)
            outs["p_c"] = caug[..., :ML_V_DIM].reshape(1, bp, ML_HEADS, ML_QK_DIM, ML_V_DIM)
            outs["p_n"] = caug[..., ML_V_DIM].reshape(1, bp, ML_HEADS, ML_QK_DIM)
            outs["p_m"] = mm[:, :ML_HEADS, 0].reshape(1, bp, ML_HEADS)

            proj_s = norm_proj(ys, g_mix, w_in, tm=bs)
            sink_rows = jnp.broadcast_to(sinks[:, None], (SW_HEADS, LANES))
            onorm_rows = _rows8(mlstm_out_norm[e].astype(F32).reshape(ML_HEADS, ML_V_DIM))
            gb_lanes = jnp.concatenate([pad_lanes(mlstm_gate_bias[e][:ML_HEADS]),
                                        pad_lanes(mlstm_gate_bias[e][ML_HEADS:])], axis=0)
            mix_s, nk, nv, ncst, nn, nm = even_sample(
                proj_s,
                cache_swa_k[e].reshape(bs, CHUNK, LANES), cache_swa_v[e].reshape(bs, CHUNK, LANES),
                state_mlstm_C[e].reshape(bs, ML_HEADS * ML_QK_DIM, ML_V_DIM),
                state_mlstm_n[e].reshape(bs, ML_HEADS * ML_QK_DIM),
                jnp.pad(state_mlstm_m[e], ((0, 0), (0, LANES - ML_HEADS))),
                tab16_s, qn, kn, sink_rows, gb_lanes, onorm_rows)
            outs["s_swk"] = nk.reshape(1, bs, CHUNK, SW_KV_HEADS, SW_HEAD_DIM)
            outs["s_swv"] = nv.reshape(1, bs, CHUNK, SW_KV_HEADS, SW_HEAD_DIM)
            outs["s_c"] = ncst.reshape(1, bs, ML_HEADS, ML_QK_DIM, ML_V_DIM)
            outs["s_n"] = nn.reshape(1, bs, ML_HEADS, ML_QK_DIM)
            outs["s_m"] = nm[:, :ML_HEADS].reshape(1, bs, ML_HEADS)
        else:
            o = l // 2
            w_in = _odd_w_in(odd_w_in[o])
            w_out = odd_w_out[o].astype(BF16)
            convw = ssd_conv_w[o].astype(F32)
            convb = row1(ssd_conv_b[o])
            dtb = pad_lanes(ssd_dt_bias[o])
            arow = pad_lanes(-jnp.exp(ssd_a_log[o].astype(F32)))
            drow = row1(jnp.repeat(ssd_d[o].astype(F32), SSD_HEAD_DIM))
            snorm = row1(ssd_norm[o])
            rnorm = row1(ret_norm[o])
            mix_p, ctail, sst, rst = odd_prompt(yp.reshape(bp, seq, d), g_mix, w_in, tab128_p, convw, convb, dtb, arow,
                                                drow, snorm, tril, (dmat, q_scale, k_scale, chunk_decay), rnorm,
                                                group=2)
            mix_p = mix_p.reshape(bp * seq, -1)
            outs["p_conv"] = ctail[:, SUBLANES - (SSD_CONV - 1):, :].reshape(1, bp, SSD_CONV - 1, SSD_CONV_DIM)
            outs["p_ssd"] = sst.reshape(1, bp, SSD_HEADS, SSD_HEAD_DIM, SSD_STATE)
            outs["p_ret"] = rst.reshape(1, bp, RET_HEADS, RET_QK_DIM, RET_V_DIM)

            proj_s = norm_proj(ys, g_mix, w_in, tm=bs)
            rnorm_rows = _rows8(ret_norm[o].astype(F32).reshape(RET_HEADS, RET_V_DIM))
            mix_s, ncb, ns, nr = odd_sample(
                proj_s, jnp.swapaxes(state_ssd_conv[o], 0, 1),
                state_ssd[o].reshape(bs, SSD_HEADS * SSD_HEAD_DIM, SSD_STATE),
                state_ret[o].reshape(bs, RET_HEADS * RET_QK_DIM, RET_V_DIM),
                tab128_s, convw, convb, dtb, arow, drow, snorm, spread, rdec_rows, rnorm_rows)
            outs["s_conv"] = jnp.swapaxes(ncb, 0, 1).reshape(1, bs, SSD_CONV - 1, SSD_CONV_DIM)
            outs["s_ssd"] = ns.reshape(1, bs, SSD_HEADS, SSD_HEAD_DIM, SSD_STATE)
            outs["s_ret"] = nr.reshape(1, bs, RET_HEADS, RET_QK_DIM, RET_V_DIM)

        wkv = jnp.concatenate([mem_wk[l], mem_wv[l]], axis=1).astype(BF16)
        qnorm = row1(mem_q_norm[l])
        mkv = norm_proj(mem, row1(norm_mem[l]), wkv, tm=tm, head_norm=row1(mem_k_norm[l]), head_norm_cols=MEM_WIDTH)
        p_mk.append(mkv[:, :MEM_WIDTH].reshape(bp, MEM_LEN, MEM_HEADS, MEM_HEAD_DIM))
        p_mv.append(mkv[:, MEM_WIDTH:].reshape(bp, MEM_LEN, MEM_HEADS, MEM_HEAD_DIM))
        wq, wo = mem_wq[l].astype(BF16), mem_wo[l].astype(BF16)
        gx, gf = row1(norm_xattn[l]), row1(norm_ffn[l])
        ys = res_proj(ys, mix_s, w_out)
        qs = norm_proj(ys, gx, wq, tm=bs, head_norm=qnorm, head_norm_cols=MEM_WIDTH)
        yp, att_s = post_prompt(yp, mix_p, w_out, gx, wq, qnorm, mkv, wo, gf, w1s, w2s, qs, cmk, cmv, layer=l,
                                rows_per_batch=seq, tm=tm)
        ys = res_proj_ffn(ys, att_s, wo, gf, w1s, w2s, layer=l)

    return (yp.reshape(bp, seq, d), ys.reshape(bs, 1, d),
            jnp.stack(p_mk), jnp.stack(p_mv), outs["p_swk"], outs["p_swv"], outs["p_c"], outs["p_n"], outs["p_m"],
            outs["p_conv"], outs["p_ssd"], outs["p_ret"],
            outs["s_swk"], outs["s_swv"], outs["s_c"], outs["s_n"], outs["s_m"],
            outs["s_conv"], outs["s_ssd"], outs["s_ret"])
```

```python
import functools
import math

import jax
import jax.numpy as jnp
import numpy as np
from jax import lax
from jax.experimental import pallas as pl
from jax.experimental.pallas import tpu as pltpu

F32 = jnp.float32
BF16 = jnp.bfloat16

D_MODEL = 1024
PAST_LEN = 8192
EPS = 1e-6
CHUNK = 128
NEG = -1e30

SW_HEADS, SW_KV_HEADS, SW_HEAD_DIM, SW_ROT_DIM = 8, 2, 64, 16
ROPE_THETA = 500000.0
ML_HEADS, ML_QK_DIM, ML_V_DIM = 4, 64, 128
SSD_HEADS, SSD_HEAD_DIM, SSD_GROUPS, SSD_STATE, SSD_CONV = 16, 64, 2, 128, 4
SSD_INNER = SSD_HEADS * SSD_HEAD_DIM
SSD_CONV_DIM = SSD_INNER + 2 * SSD_GROUPS * SSD_STATE
RET_HEADS, RET_QK_DIM, RET_V_DIM = 4, 128, 128
RET_ROPE_THETA = 10000.0
MEM_LEN, MEM_HEADS, MEM_HEAD_DIM = 256, 4, 128
MEM_WIDTH = MEM_HEADS * MEM_HEAD_DIM
FFN_DIM = 4 * D_MODEL
FFN_CHUNK = 512
PROJ_COLS_PER_STAGE = 256

LANES = 128
SUBLANES = 8
VMEM_LIMIT_BYTES = 56 * 1024 * 1024

E_SQ, E_SK, E_SV, E_MQ, E_MK, E_MV, E_MO, E_GATE, EVEN_COLS = 0, 512, 640, 768, 1024, 1280, 1792, 2304, 2432
GATE_F_LANE = 8
O_Z, O_XBC, O_RQ, O_RK, O_RV, O_RG, O_DT, ODD_COLS = 0, 1024, 2560, 3072, 3584, 4096, 4608, 4736


def _mm(a, b):
    return jnp.dot(a.astype(BF16), b.astype(BF16), preferred_element_type=F32)


def _mm_nt(a, b):
    return lax.dot_general(a.astype(BF16), b.astype(BF16), (((1,), (1,)), ((), ())), preferred_element_type=F32)


def _mm_tn(a, b):
    return lax.dot_general(a.astype(BF16), b.astype(BF16), (((0,), (0,)), ((), ())), preferred_element_type=F32)


def _dg(a, b, ca, cb):
    return lax.dot_general(a, b, (((ca,), (cb,)), ((), ())), preferred_element_type=F32)


def _split3(x):
    hi = x.astype(BF16).astype(F32)
    r1 = x - hi
    mid = r1.astype(BF16).astype(F32)
    lo = (r1 - mid).astype(BF16).astype(F32)
    return hi, mid, lo


def _mm_exact_rhs(x, e):
    hi, mid, lo = _split3(x)
    return _dg(hi, e, 1, 0) + _dg(mid, e, 1, 0) + _dg(lo, e, 1, 0)


def _mm_exact_lhs(e, x):
    hi, mid, lo = _split3(x)
    return _dg(e, hi, 1, 0) + _dg(e, mid, 1, 0) + _dg(e, lo, 1, 0)


def _mm_tn_exact_lhs(x, e):
    hi, mid, lo = _split3(x)
    return _dg(hi, e, 0, 0) + _dg(mid, e, 0, 0) + _dg(lo, e, 0, 0)


def _rms(x, g):
    return x * lax.rsqrt(jnp.mean(x * x, axis=-1, keepdims=True) + EPS) * g


def _seg_rms_mxu(x, g, seg_mean):
    return x * lax.rsqrt(_mm_exact_rhs(x * x, seg_mean) + EPS) * g


def _seg_rms(x, g, low):
    xx = x * x
    s_lo = jnp.sum(jnp.where(low, xx, 0.0), axis=-1, keepdims=True)
    s_hi = jnp.sum(jnp.where(low, 0.0, xx), axis=-1, keepdims=True)
    return x * lax.rsqrt(jnp.where(low, s_lo, s_hi) * (1.0 / SW_HEAD_DIM) + EPS) * g


def _sigmoid(x):
    return 1.0 / (1.0 + jnp.exp(-x))


def _silu(x):
    return x * _sigmoid(x)


def _softplus(x):
    return jnp.maximum(x, 0.0) + jnp.log1p(jnp.exp(-jnp.abs(x)))


def _log_sigmoid(x):
    return -_softplus(-x)


def _rope16(x, cos, sin_lo, sin_hi):
    return x * cos + pltpu.roll(x, LANES - 8, 1) * sin_lo + pltpu.roll(x, 8, 1) * sin_hi


def _rope128(x, cos, sin):
    return x * cos + pltpu.roll(x, 64, 1) * sin


def _iota(shape, dim):
    return lax.broadcasted_iota(jnp.int32, shape, dim)


def _cummax_lanes(x):
    lane = _iota(x.shape, 1)
    shift = 1
    while shift < x.shape[1]:
        x = jnp.maximum(x, jnp.where(lane >= shift, pltpu.roll(x, shift, 1), -jnp.inf))
        shift *= 2
    return x


def _norm_proj_kernel(x_ref, g_ref, w_ref, hn_ref, o_ref, *, chunks, head_norm_cols):
    xn = _rms(x_ref[...], g_ref[...]).astype(BF16)
    for c0, cs in chunks:
        r = jnp.dot(xn, w_ref[:, c0:c0 + cs], preferred_element_type=F32)
        if c0 < head_norm_cols:
            parts = [_rms(r[:, i:i + LANES], hn_ref[...]) for i in range(0, cs, LANES)]
            r = jnp.concatenate(parts, axis=1)
        o_ref[:, c0:c0 + cs] = r


def _col_chunks(n, width=512):
    return tuple((c, min(width, n - c)) for c in range(0, n, width))


def norm_proj(x, g, w, *, tm, head_norm=None, head_norm_cols=0):
    n, d = x.shape
    m = w.shape[1]
    if head_norm is None:
        head_norm = jnp.ones((1, LANES), F32)
    kern = functools.partial(_norm_proj_kernel, chunks=_col_chunks(m), head_norm_cols=head_norm_cols)
    return pl.pallas_call(
        kern,
        grid=(n // tm,),
        in_specs=[
            pl.BlockSpec((tm, d), lambda i: (i, 0)),
            pl.BlockSpec((1, d), lambda i: (0, 0)),
            pl.BlockSpec((d, m), lambda i: (0, 0), pipeline_mode=pl.Buffered(1)),
            pl.BlockSpec((1, LANES), lambda i: (0, 0)),
        ],
        out_specs=pl.BlockSpec((tm, m), lambda i: (i, 0)),
        out_shape=jax.ShapeDtypeStruct((n, m), F32),
        compiler_params=pltpu.CompilerParams(dimension_semantics=("arbitrary",), vmem_limit_bytes=VMEM_LIMIT_BYTES),
        name="norm_proj",
    )(x, g, w, head_norm)


def _ffn(x, g_ref, w1_ref, w2_ref):
    h = _rms(x, g_ref[...]).astype(BF16)
    acc = None
    for c in range(0, FFN_DIM, FFN_CHUNK):
        u = jnp.maximum(jnp.dot(h, w1_ref[:, c:c + FFN_CHUNK], preferred_element_type=F32), 0.0)
        t = jnp.dot((u * u).astype(BF16), w2_ref[c:c + FFN_CHUNK, :], preferred_element_type=F32)
        acc = t if acc is None else acc + t
    return x + acc


def _post_chain(x_ref, a_ref, wout_ref, gx_ref, wq_ref, qn_ref, mk_ref, mv_ref, wo_ref, gf_ref, w1_ref, w2_ref, o_ref):
    x = x_ref[...] + _mm(a_ref[...], wout_ref[...])
    yield
    q = jnp.dot(_rms(x, gx_ref[...]).astype(BF16), wq_ref[...], preferred_element_type=F32)
    yield
    outs = []
    for h in range(MEM_HEADS):
        sl = slice(h * MEM_HEAD_DIM, (h + 1) * MEM_HEAD_DIM)
        qh = _rms(q[:, sl], qn_ref[...])
        s = _mm_nt(qh, mk_ref[:, sl]) * (MEM_HEAD_DIM ** -0.5)
        p = jnp.exp(s - jnp.max(s, axis=-1, keepdims=True))
        p = p / jnp.sum(p, axis=-1, keepdims=True)
        outs.append(_mm(p, mv_ref[:, sl]))
        yield
    x = x + _mm(jnp.concatenate(outs, axis=1), wo_ref[...])
    h = _rms(x, gf_ref[...]).astype(BF16)
    yield
    acc = None
    for c in range(0, FFN_DIM, FFN_CHUNK):
        u = jnp.maximum(jnp.dot(h, w1_ref[:, c:c + FFN_CHUNK], preferred_element_type=F32), 0.0)
        t = jnp.dot((u * u).astype(BF16), w2_ref[c:c + FFN_CHUNK, :], preferred_element_type=F32)
        acc = t if acc is None else acc + t
        yield
    o_ref[...] = x + acc


def _xattn_row(q_row, k, v):
    row = _iota((SUBLANES, LANES), 0)
    groups = MEM_LEN * MEM_HEADS // SUBLANES
    q8 = jnp.zeros((SUBLANES, LANES), F32)
    for h in range(MEM_HEADS):
        q8 = jnp.where(row % MEM_HEADS == h, q_row[:, h * MEM_HEAD_DIM:(h + 1) * MEM_HEAD_DIM], q8)
    k3 = k.reshape(groups, SUBLANES, LANES)
    s = jnp.sum(k3 * q8[None], axis=-1, keepdims=True) * (MEM_HEAD_DIM ** -0.5)
    mx = _pair_rows(jnp.max(s, axis=0), jnp.maximum)
    p = jnp.exp(s - mx[None, :, 0:1])
    den = _pair_rows(jnp.sum(p, axis=0), jnp.add)
    o8 = _pair_rows(jnp.sum(p * v.reshape(groups, SUBLANES, LANES), axis=0), jnp.add) / den
    return jnp.concatenate([o8[h:h + 1] for h in range(MEM_HEADS)], axis=1)


def _xattn_chain(q_ref, mk_ref, mv_ref, o_ref):
    for i in range(q_ref.shape[0]):
        o_ref[i:i + 1, :] = _xattn_row(q_ref[i:i + 1, :], mk_ref[i], mv_ref[i])
        yield


def _post_prompt_kernel(x_ref, a_ref, wout_ref, gx_ref, wq_ref, qn_ref, mk_ref, mv_ref, wo_ref, gf_ref, w1_ref, w2_ref,
                        sq_ref, smk_ref, smv_ref,
                        o_ref, so_ref):
    _lockstep([_post_chain(x_ref, a_ref, wout_ref, gx_ref, wq_ref, qn_ref, mk_ref, mv_ref, wo_ref, gf_ref, w1_ref,
                           w2_ref, o_ref),
               _xattn_chain(sq_ref, smk_ref, smv_ref, so_ref)], every=[1, 3])


def post_prompt(x, a, wout, gx, wq, qn, mkv, wo, gf, w1, w2, sq, smk, smv, *, layer, rows_per_batch, tm):
    n, d = x.shape
    ka = a.shape[1]
    steps = n // tm
    bs = sq.shape[0]
    rs = bs // steps
    tiles = rows_per_batch // tm
    const = lambda shape: pl.BlockSpec(shape, lambda i: (0, 0), pipeline_mode=pl.Buffered(1))
    slab = lambda shape: pl.BlockSpec((None,) + shape, lambda i: (layer, 0, 0), pipeline_mode=pl.Buffered(1))
    smem = pl.BlockSpec((None, rs, MEM_LEN * MEM_HEADS, MEM_HEAD_DIM), lambda i: (layer, i, 0, 0))
    srow = pl.BlockSpec((None, rs, MEM_WIDTH), lambda i: (i, 0, 0))
    y, att = pl.pallas_call(
        _post_prompt_kernel,
        grid=(steps,),
        in_specs=[
            pl.BlockSpec((tm, d), lambda i: (i, 0)),
            pl.BlockSpec((tm, ka), lambda i: (i, 0)),
            const((ka, d)),
            const((1, d)),
            const((d, MEM_WIDTH)),
            const((1, MEM_HEAD_DIM)),
            pl.BlockSpec((MEM_LEN, MEM_WIDTH), lambda i: (i // tiles, 0)),
            pl.BlockSpec((MEM_LEN, MEM_WIDTH), lambda i: (i // tiles, 1)),
            const((MEM_WIDTH, d)),
            const((1, d)),
            slab((d, FFN_DIM)),
            slab((FFN_DIM, d)),
            srow, smem, smem,
        ],
        out_specs=[pl.BlockSpec((tm, d), lambda i: (i, 0)), srow],
        out_shape=[jax.ShapeDtypeStruct((n, d), F32), jax.ShapeDtypeStruct((steps, rs, MEM_WIDTH), F32)],
        compiler_params=pltpu.CompilerParams(dimension_semantics=("arbitrary",), vmem_limit_bytes=VMEM_LIMIT_BYTES),
        name="post_prompt",
    )(x, a, wout, gx, wq, qn, mkv, mkv, wo, gf, w1, w2, sq.reshape(steps, rs, MEM_WIDTH), smk, smv)
    return y, att.reshape(bs, MEM_WIDTH)


def _res_proj_kernel(x_ref, a_ref, w_ref, o_ref):
    o_ref[...] = x_ref[...] + _mm(a_ref[...], w_ref[...])


def res_proj(x, a, w):
    n, d = x.shape
    return pl.pallas_call(
        _res_proj_kernel,
        out_shape=jax.ShapeDtypeStruct((n, d), F32),
        compiler_params=pltpu.CompilerParams(vmem_limit_bytes=VMEM_LIMIT_BYTES),
        name="res_proj",
    )(x, a, w)


def _res_proj_ffn_kernel(x_ref, a_ref, w_ref, gf_ref, w1_ref, w2_ref, o_ref):
    x = x_ref[...] + _mm(a_ref[...], w_ref[...])
    o_ref[...] = _ffn(x, gf_ref, w1_ref, w2_ref)


def res_proj_ffn(x, a, w, gf, w1, w2, *, layer):
    n, d = x.shape
    full = lambda t: pl.BlockSpec(t.shape, lambda i: (0, 0))
    slab = lambda shape: pl.BlockSpec((None,) + shape, lambda i: (layer, 0, 0), pipeline_mode=pl.Buffered(1))
    return pl.pallas_call(
        _res_proj_ffn_kernel,
        grid=(1,),
        in_specs=[full(x), full(a), full(w), full(gf), slab((d, FFN_DIM)), slab((FFN_DIM, d))],
        out_specs=pl.BlockSpec((n, d), lambda i: (0, 0)),
        out_shape=jax.ShapeDtypeStruct((n, d), F32),
        compiler_params=pltpu.CompilerParams(dimension_semantics=("arbitrary",), vmem_limit_bytes=VMEM_LIMIT_BYTES),
        name="res_proj_ffn",
    )(x, a, w, gf, w1, w2)


def _pair_rows(x, op):
    xb = jnp.broadcast_to(x, (SUBLANES, LANES))
    return op(xb, pltpu.roll(xb, SUBLANES // 2, 0))


def _proj_chain(x_ref, g_ref, w_ref, dst, width):
    group, _, d = x_ref.shape
    xn = _rms(x_ref[...].reshape(group * CHUNK, d), g_ref[...]).astype(BF16)
    yield
    cols = w_ref.shape[1]
    for c0 in range(0, cols, width):
        cs = min(width, cols - c0)
        r = jnp.dot(xn, w_ref[:, c0:c0 + cs], preferred_element_type=F32)
        for b in range(group):
            dst[b, :, c0:c0 + cs] = r[b * CHUNK:(b + 1) * CHUNK]
        yield


def _even_prompt_kernel(xn_ref, x0_ref, g_ref, w_ref, cos_ref, sinlo_ref, sinhi_ref, qn_ref, kn_ref, sink_ref, segm_ref,
                        triu_ref, gb_ref, onorm_ref,
                        y_ref, kc_ref, vc_ref, caug_ref, m_ref,
                        kprev, vprev, cst, mst, proj):
    n = pl.program_id(0)
    batch = xn_ref.shape[0]
    slot = lax.rem(n, 2)

    @pl.when(n == 0)
    def _():
        kprev[...] = jnp.zeros_like(kprev)
        vprev[...] = jnp.zeros_like(vprev)
        cst[...] = jnp.zeros_like(cst)
        mst[...] = jnp.zeros_like(mst)
        for _ in _proj_chain(x0_ref, g_ref, w_ref, proj.at[0], PROJ_COLS_PER_STAGE):
            pass

    p_ref = proj.at[slot]

    cos, sinlo, sinhi = cos_ref[...], sinlo_ref[...], sinhi_ref[...]
    lane = _iota((1, LANES), 1)
    low = lane < 64
    qi = _iota((CHUNK, 2 * CHUNK), 0)
    si = _iota((CHUNK, 2 * CHUNK), 1)
    valid = (si >= qi) & (si <= qi + CHUNK) & ((si >= CHUNK) | (n > 0))
    causal = _iota((CHUNK, CHUNK), 0) >= _iota((CHUNK, CHUNK), 1)
    ones_col = jnp.where(_iota((CHUNK, LANES), 1) == 0, 1.0, 0.0)
    row64 = _iota((CHUNK, 1), 0) < 64
    swa = [_swa_prompt_chunk(p_ref.at[b], y_ref.at[b], kprev.at[b], vprev.at[b], (cos, sinlo, sinhi), qn_ref, kn_ref,
                             sink_ref, segm_ref[...], low, valid) for b in range(batch)]
    mlstm = [_mlstm_prompt_chunk(p_ref.at[b], y_ref.at[b], cst.at[b], mst.at[b], triu_ref, gb_ref, onorm_ref, low,
                                 causal, ones_col, row64) for b in range(batch)]
    gemm = _proj_chain(xn_ref, g_ref, w_ref, proj.at[1 - slot], 2 * PROJ_COLS_PER_STAGE)
    new_kv = _lockstep(swa + mlstm + [gemm], every=[1] * (2 * batch) + [4])[:batch]

    @pl.when(n == pl.num_programs(0) - 1)
    def _():
        for b in range(batch):
            kc_ref[b], vc_ref[b] = new_kv[b]
        caug_ref[...] = cst[...]
        m_ref[...] = mst[...]


def _then(first, second):
    value = yield from first
    yield from second
    return value


def _lockstep(chains, every=None):
    every = every or [1] * len(chains)
    results = [None] * len(chains)
    live = list(range(len(chains)))
    rnd = 0
    while live:
        for i in list(live):
            if rnd % every[i]:
                continue
            try:
                next(chains[i])
            except StopIteration as stop:
                results[i] = stop.value
                live.remove(i)
        rnd += 1
    return results


def _swa_prompt_chunk(p_ref, y_ref, kprev, vprev, tabs, qn_ref, kn_ref, sink_ref, segm, low, valid):
    cos, sinlo, sinhi = tabs
    tiles = [p_ref[:, E_SK:E_SK + LANES]] + [p_ref[:, E_SQ + j * LANES:E_SQ + (j + 1) * LANES]
                                             for j in range(SW_HEADS // 2)]
    gains = [kn_ref[...]] + [qn_ref[:, j * LANES:(j + 1) * LANES] for j in range(SW_HEADS // 2)]
    ms = _mm_exact_rhs(jnp.concatenate([t * t for t in tiles], axis=0), segm)
    yield
    normed = [_rope16(t * lax.rsqrt(ms[i * CHUNK:(i + 1) * CHUNK] + EPS) * g, cos, sinlo, sinhi)
              for i, (t, g) in enumerate(zip(tiles, gains))]
    k = normed[0]
    v = p_ref[:, E_SV:E_SV + LANES]
    kk = jnp.concatenate([kprev[...], k], axis=0)
    vv = jnp.concatenate([vprev[...], v], axis=0)
    kk_sw = pltpu.roll(kk, 64, 1)
    vv_sw = pltpu.roll(vv, 64, 1)
    kvar = {(0, 0): jnp.where(low, kk, 0.0), (0, 1): jnp.where(low, 0.0, kk_sw),
            (1, 0): jnp.where(low, kk_sw, 0.0), (1, 1): jnp.where(low, 0.0, kk)}
    vvar = {(0, 0): vv, (0, 1): vv_sw, (1, 0): vv_sw, (1, 1): vv}
    yield
    for j in range(SW_HEADS // 2):
        qb = normed[1 + j]
        halves = []
        for pos in range(2):
            h = 2 * j + pos
            kv = h // (SW_HEADS // SW_KV_HEADS)
            s = jnp.where(valid, _mm_nt(qb, kvar[(kv, pos)]) * (SW_HEAD_DIM ** -0.5), NEG)
            yield
            sink = sink_ref[h]
            m = jnp.maximum(jnp.max(s, axis=-1, keepdims=True), sink)
            pr = jnp.exp(s - m)
            yield
            pr = pr / (jnp.sum(pr, axis=-1, keepdims=True) + jnp.exp(sink - m))
            halves.append(_mm(pr, vvar[(kv, pos)]))
            yield
        y_ref[:, j * LANES:(j + 1) * LANES] = jnp.where(low, halves[0], halves[1]).astype(y_ref.dtype)
    kprev[...] = k
    vprev[...] = v
    return k, v


def _mlstm_prompt_chunk(p_ref, y_ref, cst, mst, triu_ref, gb_ref, onorm_ref, low, causal, ones_col, row64):
    gt = p_ref[:, E_GATE:E_GATE + LANES].T
    gi = gt[0:SUBLANES] + gb_ref[0:SUBLANES]
    fl = _log_sigmoid(gt[GATE_F_LANE:GATE_F_LANE + SUBLANES] + gb_ref[SUBLANES:2 * SUBLANES])
    yield
    fcum = _mm_exact_rhs(fl, triu_ref[...])
    dd = gi - fcum
    mprev = mst[...]
    yield
    mt = fcum + jnp.maximum(mprev, _cummax_lanes(dd))
    fend = jnp.broadcast_to(fcum[:, CHUNK - 1:CHUNK], fcum.shape)
    mend = jnp.broadcast_to(mt[:, CHUNK - 1:CHUNK], mt.shape)
    decay = jnp.exp(fend + mprev - mend)
    rows = jnp.concatenate([fcum - mt, jnp.exp(fcum + mprev - mt), jnp.exp(-mt), jnp.exp(fend - fcum + gi - mend),
                            jnp.zeros((CHUNK - 4 * SUBLANES, CHUNK), F32)], axis=0)
    yield
    cols = rows.T
    yield
    for j in range(ML_HEADS // 2):
        qblk = p_ref[:, E_MQ + j * LANES:E_MQ + (j + 1) * LANES]
        kblk = p_ref[:, E_MK + j * LANES:E_MK + (j + 1) * LANES] * (ML_QK_DIM ** -0.5)
        c_old = cst[j]
        qm2 = jnp.concatenate([jnp.where(low, qblk, 0.0), jnp.where(low, 0.0, qblk)], axis=0)
        s2 = _mm_nt(qm2, kblk)
        qc2 = _mm(qm2, c_old)
        yield
        kws, vaugs = [], []
        for pos in range(2):
            h = 2 * j + pos
            rs = slice(pos * CHUNK, (pos + 1) * CHUNK)
            logw = cols[:, h:h + 1] + dd[h:h + 1, :]
            w = jnp.exp(jnp.where(causal, logw, -jnp.inf))
            sqk = s2[rs] * w
            yield
            vh = p_ref[:, E_MV + h * LANES:E_MV + (h + 1) * LANES]
            cs = cols[:, SUBLANES + h:SUBLANES + h + 1]
            num = cs * qc2[rs, :ML_V_DIM] + _mm(sqk, vh)
            den = cs * qc2[rs, ML_V_DIM:ML_V_DIM + 1] + jnp.sum(sqk, axis=-1, keepdims=True)
            hh = num / jnp.maximum(jnp.abs(den), cols[:, 2 * SUBLANES + h:2 * SUBLANES + h + 1])
            yield
            hsl = slice(h * ML_V_DIM, (h + 1) * ML_V_DIM)
            hn = _rms(hh, onorm_ref[:, hsl])
            mo = p_ref[:, E_MO + h * ML_V_DIM:E_MO + (h + 1) * ML_V_DIM]
            y_ref[:, SW_HEADS * SW_HEAD_DIM + h * ML_V_DIM:SW_HEADS * SW_HEAD_DIM + (h + 1) * ML_V_DIM] = (
                hn * _sigmoid(mo)).astype(y_ref.dtype)
            msk = low if pos == 0 else jnp.logical_not(low)
            kws.append(jnp.where(msk, kblk, 0.0) * cols[:, 3 * SUBLANES + h:3 * SUBLANES + h + 1])
            vaugs.append(jnp.concatenate([vh, ones_col], axis=1))
            yield
        upd = _mm_tn(jnp.concatenate(kws, axis=0), jnp.concatenate(vaugs, axis=0))
        dec = jnp.where(row64, decay[2 * j:2 * j + 1, 0:1], decay[2 * j + 1:2 * j + 2, 0:1])
        cst[j] = dec * c_old + upd
        yield
    mst[...] = mend


def even_prompt(x, g, w, tabs, qn, kn, sinks, segm, triu, gb, onorm):
    batch, seq, d = x.shape
    nc = seq // CHUNK
    tab = pl.BlockSpec((CHUNK, LANES), lambda n: (n, 0))
    const = lambda shape: pl.BlockSpec(shape, lambda n: (0,) * len(shape))
    state_shapes = [(batch, CHUNK, LANES), (batch, CHUNK, LANES),
                    (batch, ML_HEADS // 2, 2 * ML_QK_DIM, 2 * ML_V_DIM), (batch, SUBLANES, LANES)]
    return pl.pallas_call(
        _even_prompt_kernel,
        grid=(nc,),
        in_specs=[
            pl.BlockSpec((batch, CHUNK, d), lambda n: (0, jnp.minimum(n + 1, nc - 1), 0)),
            pl.BlockSpec((batch, CHUNK, d), lambda n: (0, 0, 0)),
            const((1, d)), pl.BlockSpec((d, EVEN_COLS), lambda n: (0, 0), pipeline_mode=pl.Buffered(1)),
            tab, tab, tab,
            const((1, SW_HEADS * SW_HEAD_DIM)), const((1, LANES)),
            pl.BlockSpec(memory_space=pltpu.SMEM),
            const((LANES, LANES)), const((CHUNK, CHUNK)), const((2 * SUBLANES, LANES)),
            const((1, ML_HEADS * ML_V_DIM)),
        ],
        out_specs=[pl.BlockSpec((batch, CHUNK, D_MODEL), lambda n: (0, n, 0))] + [const(s) for s in state_shapes],
        out_shape=[jax.ShapeDtypeStruct((batch, seq, D_MODEL), BF16)]
        + [jax.ShapeDtypeStruct(s, F32) for s in state_shapes],
        scratch_shapes=[pltpu.VMEM(s, F32) for s in state_shapes] + [pltpu.VMEM((2, batch, CHUNK, EVEN_COLS), F32)],
        compiler_params=pltpu.CompilerParams(dimension_semantics=("arbitrary",), vmem_limit_bytes=VMEM_LIMIT_BYTES),
        name="even_prompt",
    )(x, x, g, w, *tabs, qn, kn, sinks, segm, triu, gb, onorm)


def _odd_prompt_kernel(xn_ref, x0_ref, g_ref, w_ref, cosr_ref, sinr_ref, convw_ref, convb_ref, dtb_ref, arow_ref,
                       drow_ref, snorm_ref, tril_ref, dmat_ref, qs_ref, ks_ref, cd_ref, rnorm_ref,
                       y_ref, conv_ref, s_ref, r_ref,
                       ext, sst, rst, proj):
    n = pl.program_id(1)
    batch = xn_ref.shape[0]
    slot = lax.rem(n, 2)

    @pl.when(n == 0)
    def _():
        ext[:, 0:SUBLANES] = jnp.zeros((batch, SUBLANES, SSD_CONV_DIM), F32)
        sst[...] = jnp.zeros_like(sst)
        rst[...] = jnp.zeros_like(rst)
        for _ in _proj_chain(x0_ref, g_ref, w_ref, proj.at[0], PROJ_COLS_PER_STAGE):
            pass

    p_ref = proj.at[slot]

    lane = _iota((1, LANES), 1)
    low = lane < 64
    row64 = _iota((CHUNK, 1), 0) < 64
    causal = _iota((CHUNK, CHUNK), 0) >= _iota((CHUNK, CHUNK), 1)
    ssd = [_ssd_prompt_chunk(p_ref.at[b], y_ref.at[b], ext.at[b], sst.at[b], convw_ref, convb_ref, dtb_ref, arow_ref,
                             drow_ref, snorm_ref, tril_ref, low, row64, causal) for b in range(batch)]
    ret = [_ret_prompt_chunk(p_ref.at[b], y_ref.at[b], rst.at[b], cosr_ref, sinr_ref, dmat_ref, qs_ref, ks_ref, cd_ref,
                             rnorm_ref) for b in range(batch)]
    gemm = _proj_chain(xn_ref, g_ref, w_ref, proj.at[1 - slot], PROJ_COLS_PER_STAGE)
    tails = _lockstep([_then(s, r) for s, r in zip(ssd, ret)] + [gemm], every=[1] * batch + [2])[:batch]

    @pl.when(n == pl.num_programs(1) - 1)
    def _():
        for b in range(batch):
            conv_ref[b] = tails[b]
        s_ref[...] = sst[...]
        r_ref[...] = rst[...]


def _ssd_prompt_chunk(p_ref, y_ref, ext, sst, convw_ref, convb_ref, dtb_ref, arow_ref, drow_ref, snorm_ref, tril_ref,
                      low, row64, causal):
    tail = SUBLANES
    ext[tail:tail + CHUNK] = p_ref[:, O_XBC:O_XBC + SSD_CONV_DIM]
    yield
    xe = ext[...]
    acc = None
    for jj in range(SSD_CONV):
        shift = SSD_CONV - 1 - jj
        tap = (pltpu.roll(xe, shift, 0) if shift else xe)[tail:tail + CHUNK] * convw_ref[jj:jj + 1]
        acc = tap if acc is None else acc + tap
    xact = _silu(acc + convb_ref[...])
    new_tail = ext[CHUNK:CHUNK + tail]
    ext[0:tail] = new_tail
    yield

    dt = _softplus(p_ref[:, O_DT:O_DT + LANES] + dtb_ref[...])
    cum = _mm_exact_lhs(tril_ref[...], dt * arow_ref[...])
    yield
    cum_t = cum.T
    dt_t = dt.T
    ecum = jnp.exp(cum)
    cend = cum[CHUNK - 1:CHUNK, :]
    wend = jnp.exp(cend - cum) * dt
    eend = jnp.exp(cend)
    yield
    pairs_per_group = SSD_HEADS // SSD_GROUPS // 2
    ys = []
    for g in range(SSD_GROUPS):
        bc = xact[:, SSD_INNER + g * SSD_STATE:SSD_INNER + (g + 1) * SSD_STATE]
        cc = xact[:, SSD_INNER + (SSD_GROUPS + g) * SSD_STATE:SSD_INNER + (SSD_GROUPS + g + 1) * SSD_STATE]
        cb = _mm_nt(cc, bc)
        yield
        for jg in range(pairs_per_group):
            j = g * pairs_per_group + jg
            ha, hb = 2 * j, 2 * j + 1
            xp = xact[:, j * LANES:(j + 1) * LANES]
            s_old = sst[j]
            y = jnp.where(low, ecum[:, ha:ha + 1], ecum[:, hb:hb + 1]) * _mm_nt(cc, s_old)
            yield
            wmats = []
            for h in (ha, hb):
                seg = cum[:, h:h + 1] - cum_t[h:h + 1, :]
                wmats.append(cb * jnp.exp(jnp.where(causal, seg, -jnp.inf)) * dt_t[h:h + 1, :])
                yield
            y = y + _mm(jnp.concatenate(wmats, axis=1),
                        jnp.concatenate([jnp.where(low, xp, 0.0), jnp.where(low, 0.0, xp)], axis=0))
            yield
            xw = xp * jnp.where(low, wend[:, ha:ha + 1], wend[:, hb:hb + 1])
            sst[j] = jnp.where(row64, eend[:, ha:ha + 1], eend[:, hb:hb + 1]) * s_old + _mm_tn(xw, bc)
            ys.append(y)
            yield
        gs = slice(g * SSD_INNER // SSD_GROUPS, (g + 1) * SSD_INNER // SSD_GROUPS)
        yg = jnp.concatenate(ys[g * pairs_per_group:(g + 1) * pairs_per_group], axis=1)
        yg = (yg + drow_ref[:, gs] * xact[:, gs]) * _silu(p_ref[:, O_Z + gs.start:O_Z + gs.stop])
        y_ref[:, gs] = _rms(yg, snorm_ref[:, gs]).astype(y_ref.dtype)
        yield
    return new_tail


def _ret_prompt_chunk(p_ref, y_ref, rst, cosr_ref, sinr_ref, dmat_ref, qs_ref, ks_ref, cd_ref, rnorm_ref):
    cosr, sinr = cosr_ref[...], sinr_ref[...]
    for h in range(RET_HEADS):
        hs = h * LANES
        q = _rope128(p_ref[:, O_RQ + hs:O_RQ + hs + LANES], cosr, sinr)
        k = _rope128(p_ref[:, O_RK + hs:O_RK + hs + LANES], cosr, sinr) * (RET_QK_DIM ** -0.5)
        v = p_ref[:, O_RV + hs:O_RV + hs + LANES]
        yield
        r_old = rst[h]
        o = _mm(_mm_nt(q, k) * dmat_ref[h], v) + qs_ref[h] * _mm(q, r_old)
        yield
        rst[h] = cd_ref[h] * r_old + _mm_tn(k * ks_ref[h], v)
        xc = o - jnp.mean(o, axis=-1, keepdims=True)
        yn = xc * lax.rsqrt(jnp.mean(xc * xc, axis=-1, keepdims=True) + EPS) * rnorm_ref[:, hs:hs + LANES]
        y_ref[:, SSD_INNER + hs:SSD_INNER + hs + LANES] = (
            yn * _silu(p_ref[:, O_RG + hs:O_RG + hs + LANES])).astype(y_ref.dtype)
        yield


def odd_prompt(x, g, w, tabs, convw, convb, dtb, arow, drow, snorm, tril, ret_consts, rnorm, *, group):
    batch, seq, d = x.shape
    nc = seq // CHUNK
    tab = pl.BlockSpec((CHUNK, LANES), lambda g, n: (n, 0))
    const = lambda shape: pl.BlockSpec(shape, lambda g, n: (0,) * len(shape))
    per_g = lambda shape: pl.BlockSpec((group,) + shape, lambda g, n: (g,) + (0,) * len(shape))
    hc = (RET_HEADS, CHUNK, LANES)
    ywidth = SSD_INNER + RET_HEADS * RET_V_DIM
    states = [(SUBLANES, SSD_CONV_DIM), (SSD_HEADS // 2, LANES, SSD_STATE), hc]
    return pl.pallas_call(
        _odd_prompt_kernel,
        grid=(batch // group, seq // CHUNK),
        in_specs=[
            pl.BlockSpec((group, CHUNK, d), lambda g, n: (g, jnp.minimum(n + 1, nc - 1), 0)),
            pl.BlockSpec((group, CHUNK, d), lambda g, n: (g, 0, 0)),
            const((1, d)), pl.BlockSpec((d, ODD_COLS), lambda g, n: (0, 0), pipeline_mode=pl.Buffered(1)),
            tab, tab,
            const((SSD_CONV, SSD_CONV_DIM)), const((1, SSD_CONV_DIM)), const((1, LANES)), const((1, LANES)),
            const((1, SSD_INNER)), const((1, SSD_INNER)), const((CHUNK, CHUNK)),
            const(hc), const(hc), const(hc), const(hc), const((1, RET_HEADS * RET_V_DIM)),
        ],
        out_specs=[pl.BlockSpec((group, CHUNK, ywidth), lambda g, n: (g, n, 0))] + [per_g(s) for s in states],
        out_shape=[jax.ShapeDtypeStruct((batch, seq, ywidth), BF16)]
        + [jax.ShapeDtypeStruct((batch,) + s, F32) for s in states],
        scratch_shapes=[pltpu.VMEM((group, SUBLANES + CHUNK, SSD_CONV_DIM), F32),
                        pltpu.VMEM((group,) + states[1], F32), pltpu.VMEM((group,) + states[2], F32),
                        pltpu.VMEM((2, group, CHUNK, ODD_COLS), F32)],
        compiler_params=pltpu.CompilerParams(dimension_semantics=("arbitrary", "arbitrary"),
                                             vmem_limit_bytes=VMEM_LIMIT_BYTES),
        name="odd_prompt",
    )(x, x, g, w, *tabs, convw, convb, dtb, arow, drow, snorm, tril, *ret_consts, rnorm)


def _lane_to_rows(g, offset):
    sel = _iota(g.shape, 1) == _iota(g.shape, 0) + offset
    return jnp.sum(jnp.where(sel, g, 0.0), axis=-1, keepdims=True)


def _block_rows(x, i, nblk, blk):
    row = _iota((SUBLANES, blk), 0)
    out = jnp.zeros((SUBLANES, blk), F32)
    for b in range(nblk):
        out = jnp.where(row == b, x[i:i + 1, b * blk:(b + 1) * blk], out)
    return out


def _even_sample_kernel(p_ref, bk_ref, bv_ref, c_ref, nrow_ref, mrow_ref, cos_ref, sinlo_ref, sinhi_ref, qn_ref,
                        kn_ref, sink_ref, gb_ref, onorm_ref,
                        y_ref, nk_ref, nv_ref, nc_ref, nn_ref, nm_ref):
    R = SUBLANES
    cos, sinlo, sinhi = cos_ref[...], sinlo_ref[...], sinhi_ref[...]
    row = _iota((R, LANES), 0)
    lane = _iota((R, LANES), 1)
    low = lane < 64
    group = SW_HEADS // SW_KV_HEADS
    scale = SW_HEAD_DIM ** -0.5
    sink = sink_ref[:, 0:1]
    last = _iota((CHUNK, LANES), 0) == CHUNK - 1

    k = _rope16(_seg_rms(p_ref[:, E_SK:E_SK + LANES], kn_ref[...], low), cos, sinlo, sinhi)
    v = p_ref[:, E_SV:E_SV + LANES]
    qb, qb_sw = [], []
    for j in range(SW_HEADS // 2):
        sl = slice(E_SQ + j * LANES, E_SQ + (j + 1) * LANES)
        qb.append(_rope16(_seg_rms(p_ref[:, sl], qn_ref[:, sl], low), cos, sinlo, sinhi))
        qb_sw.append(pltpu.roll(qb[j], 64, 1))

    g = p_ref[:, E_GATE:E_GATE + LANES]
    ic = g + gb_ref[0:1]
    fl = _log_sigmoid(pltpu.roll(g, LANES - GATE_F_LANE, 1) + gb_ref[1:2])
    mprev = mrow_ref[...]
    mt = jnp.maximum(fl + mprev, ic)
    w_all = jnp.exp(ic - mt)
    cs_all = jnp.exp(fl + mprev - mt)
    em_all = jnp.exp(-mt)
    nm_ref[...] = mt
    width = ML_HEADS * ML_QK_DIM
    own = _iota((R, width), 1) // ML_QK_DIM == _iota((R, width), 0)
    kscaled = p_ref[:, E_MK:E_MK + width] * (ML_QK_DIM ** -0.5)

    def one_row(i):
        qm = jnp.zeros((R, LANES), F32)
        for j in range(SW_HEADS // 2):
            for pos in range(2):
                h = 2 * j + pos
                kv = h // group
                src = (qb[j] if pos == kv else qb_sw[j])[i:i + 1]
                qm = jnp.where((row == h) & (low if kv == 0 else jnp.logical_not(low)), src, qm)
        bk, bv = bk_ref[i], bv_ref[i]
        ki, vi = k[i:i + 1], v[i:i + 1]
        s = _dg(qm, bk, 1, 1) * scale
        s_new = jnp.sum(qm * ki, axis=-1, keepdims=True) * scale
        yield
        m = jnp.maximum(jnp.maximum(jnp.max(s, axis=-1, keepdims=True), s_new), sink)
        pr = jnp.exp(s - m)
        p_new = jnp.exp(s_new - m)
        yield
        den = jnp.sum(pr, axis=-1, keepdims=True) + p_new + jnp.exp(sink - m)
        o = (_dg(pr, bv, 1, 0) + p_new * vi) / den
        yield
        o_sw = pltpu.roll(o, 64, 1)
        for j in range(SW_HEADS // 2):
            halves = []
            for pos in range(2):
                h = 2 * j + pos
                halves.append((o if pos == h // group else o_sw)[h:h + 1, :])
            y_ref[i:i + 1, j * LANES:(j + 1) * LANES] = jnp.where(low[0:1], halves[0], halves[1])
        nk_ref[i] = jnp.where(last, ki, pltpu.roll(bk, CHUNK - 1, 0))
        nv_ref[i] = jnp.where(last, vi, pltpu.roll(bv, CHUNK - 1, 0))
        yield

        w = _lane_to_rows(jnp.broadcast_to(w_all[i:i + 1], (R, LANES)), 0)
        cs = _lane_to_rows(jnp.broadcast_to(cs_all[i:i + 1], (R, LANES)), 0)
        em = _lane_to_rows(jnp.broadcast_to(em_all[i:i + 1], (R, LANES)), 0)
        qrows = jnp.where(own, p_ref[i:i + 1, E_MQ:E_MQ + width], 0.0)
        krows = jnp.where(own, kscaled[i:i + 1], 0.0)
        c_old = c_ref[i]
        qc = _dg(qrows, c_old, 1, 0)
        qn_dot = jnp.sum(qrows * nrow_ref[i:i + 1], axis=-1, keepdims=True)
        sqk = jnp.sum(qrows * krows, axis=-1, keepdims=True) * w
        yield
        v4 = _block_rows(p_ref[:, E_MV:E_MV + ML_HEADS * ML_V_DIM], i, ML_HEADS, ML_V_DIM)
        mo4 = _block_rows(p_ref[:, E_MO:E_MO + ML_HEADS * ML_V_DIM], i, ML_HEADS, ML_V_DIM)
        num = cs * qc + sqk * v4
        dn = cs * qn_dot + sqk
        hh = num / jnp.maximum(jnp.abs(dn), em)
        hn = _rms(hh, onorm_ref[...]) * _sigmoid(mo4)
        yield
        for h in range(ML_HEADS):
            c0 = SW_HEADS * SW_HEAD_DIM + h * ML_V_DIM
            y_ref[i:i + 1, c0:c0 + ML_V_DIM] = hn[h:h + 1]
        dec_col = jnp.concatenate(
            [jnp.broadcast_to(cs[h:h + 1, 0:1], (ML_QK_DIM, ML_V_DIM)) for h in range(ML_HEADS)], axis=0)
        nc_ref[i] = dec_col * c_old + _dg(krows * w, v4, 0, 0)
        dec_lanes = jnp.sum(jnp.where(own, cs, 0.0), axis=0, keepdims=True)
        nn_ref[i:i + 1] = dec_lanes * nrow_ref[i:i + 1] + jnp.sum(krows * w, axis=0, keepdims=True)
        yield

    _lockstep([one_row(i) for i in range(R)])


def even_sample(proj, bk, bv, c, nrow, mrow, tabs, qn, kn, sink_rows, gb, onorm_rows):
    b = proj.shape[0]
    R = SUBLANES
    width = ML_HEADS * ML_QK_DIM
    rows = lambda w: pl.BlockSpec((R, w), lambda i: (i, 0))
    per_b = lambda shape: pl.BlockSpec((R,) + shape, lambda i: (i,) + (0,) * len(shape))
    const = lambda shape: pl.BlockSpec(shape, lambda i: (0,) * len(shape))
    tab = const((R, LANES))
    return pl.pallas_call(
        _even_sample_kernel,
        grid=(b // R,),
        in_specs=[
            rows(EVEN_COLS), per_b((CHUNK, LANES)), per_b((CHUNK, LANES)), per_b((width, ML_V_DIM)),
            rows(width), rows(LANES), tab, tab, tab,
            const((1, SW_HEADS * SW_HEAD_DIM)), const((1, LANES)), const((R, LANES)),
            const((2, LANES)), const((R, ML_V_DIM)),
        ],
        out_specs=[
            rows(D_MODEL), per_b((CHUNK, LANES)), per_b((CHUNK, LANES)), per_b((width, ML_V_DIM)),
            rows(width), rows(LANES),
        ],
        out_shape=[
            jax.ShapeDtypeStruct((b, D_MODEL), F32),
            jax.ShapeDtypeStruct((b, CHUNK, LANES), F32), jax.ShapeDtypeStruct((b, CHUNK, LANES), F32),
            jax.ShapeDtypeStruct((b, width, ML_V_DIM), F32), jax.ShapeDtypeStruct((b, width), F32),
            jax.ShapeDtypeStruct((b, LANES), F32),
        ],
        compiler_params=pltpu.CompilerParams(dimension_semantics=("arbitrary",), vmem_limit_bytes=VMEM_LIMIT_BYTES),
        name="even_sample",
    )(proj, bk, bv, c, nrow, mrow, *tabs, qn, kn, sink_rows, gb, onorm_rows)


def _odd_sample_kernel(p_ref, cb_ref, s_ref, r_ref, cosr_ref, sinr_ref, convw_ref, convb_ref, dtb_ref, arow_ref,
                       drow_ref, snorm_ref, spread_ref, rdec_ref, rnorm_ref,
                       y_ref, ncb_ref, ns_ref, nr_ref):
    R = SUBLANES
    xbc = p_ref[:, O_XBC:O_XBC + SSD_CONV_DIM]
    acc = cb_ref[0] * convw_ref[0:1]
    for jj in range(1, SSD_CONV - 1):
        acc = acc + cb_ref[jj] * convw_ref[jj:jj + 1]
    acc = acc + xbc * convw_ref[SSD_CONV - 1:SSD_CONV]
    xact = _silu(acc + convb_ref[...])
    for jj in range(SSD_CONV - 2):
        ncb_ref[jj] = cb_ref[jj + 1]
    ncb_ref[SSD_CONV - 2] = xbc

    dt = _softplus(p_ref[:, O_DT:O_DT + LANES] + dtb_ref[...])
    dec = jnp.exp(dt * arow_ref[...])
    xs = xact[:, 0:SSD_INNER]
    xdt = xs * _mm_exact_rhs(dt, spread_ref[...])
    gown = _iota((R, SSD_INNER), 1) // (SSD_INNER // SSD_GROUPS) == _iota((R, SSD_INNER), 0)
    bpart = xact[:, SSD_INNER:SSD_INNER + SSD_GROUPS * SSD_STATE]
    cpart = xact[:, SSD_INNER + SSD_GROUPS * SSD_STATE:SSD_CONV_DIM]

    cosr, sinr = cosr_ref[...], sinr_ref[...]
    width = RET_HEADS * RET_QK_DIM
    q4 = jnp.concatenate([_rope128(p_ref[:, O_RQ + h * LANES:O_RQ + (h + 1) * LANES], cosr, sinr)
                          for h in range(RET_HEADS)], axis=1)
    k4 = jnp.concatenate([_rope128(p_ref[:, O_RK + h * LANES:O_RK + (h + 1) * LANES], cosr, sinr)
                          for h in range(RET_HEADS)], axis=1) * (RET_QK_DIM ** -0.5)
    own = _iota((R, width), 1) // RET_QK_DIM == _iota((R, width), 0)
    gam = rdec_ref[:, 0:1]
    gam_col = jnp.concatenate(
        [jnp.broadcast_to(rdec_ref[h:h + 1, :], (RET_QK_DIM, RET_V_DIM)) for h in range(RET_HEADS)], axis=0)

    def one_row(i):
        brows = _block_rows(bpart, i, SSD_GROUPS, SSD_STATE)
        crows = _block_rows(cpart, i, SSD_GROUPS, SSD_STATE)
        xw = jnp.where(gown, xdt[i:i + 1], 0.0)
        dec_col = jnp.concatenate(
            [jnp.broadcast_to(dec[i:i + 1, h:h + 1], (SSD_HEAD_DIM, SSD_STATE)) for h in range(SSD_HEADS)], axis=0)
        yield
        s_new = dec_col * s_ref[i] + _dg(xw, brows, 0, 0)
        ns_ref[i] = s_new
        yield
        yrows = _dg(crows, s_new, 1, 1)
        y_row = jnp.sum(jnp.where(gown, yrows, 0.0), axis=0, keepdims=True)
        yield

        qrows = jnp.where(own, q4[i:i + 1], 0.0)
        krows = jnp.where(own, k4[i:i + 1], 0.0)
        v4 = _block_rows(p_ref[:, O_RV:O_RV + RET_HEADS * RET_V_DIM], i, RET_HEADS, RET_V_DIM)
        g4 = _block_rows(p_ref[:, O_RG:O_RG + RET_HEADS * RET_V_DIM], i, RET_HEADS, RET_V_DIM)
        r_old = r_ref[i]
        att = jnp.sum(qrows * krows, axis=-1, keepdims=True)
        o = att * v4 + gam * _dg(qrows, r_old, 1, 0)
        yield
        nr_ref[i] = gam_col * r_old + _dg(krows, v4, 0, 0)
        yield
        xc = o - jnp.mean(o, axis=-1, keepdims=True)
        yn = xc * lax.rsqrt(jnp.mean(xc * xc, axis=-1, keepdims=True) + EPS) * rnorm_ref[...] * _silu(g4)
        for h in range(RET_HEADS):
            y_ref[i:i + 1, SSD_INNER + h * RET_V_DIM:SSD_INNER + (h + 1) * RET_V_DIM] = yn[h:h + 1]
        return y_row

    ys = _lockstep([one_row(i) for i in range(R)])
    y = (jnp.concatenate(ys, axis=0) + drow_ref[...] * xs) * _silu(p_ref[:, O_Z:O_Z + SSD_INNER])
    gw = SSD_INNER // SSD_GROUPS
    for g in range(SSD_GROUPS):
        y_ref[:, g * gw:(g + 1) * gw] = _rms(y[:, g * gw:(g + 1) * gw], snorm_ref[:, g * gw:(g + 1) * gw])


def odd_sample(proj, cbuf, s, r, tabs, convw, convb, dtb, arow, drow, snorm, spread, rdec_rows, rnorm_rows):
    b = proj.shape[0]
    R = SUBLANES
    rows = lambda w: pl.BlockSpec((R, w), lambda i: (i, 0))
    per_b = lambda shape: pl.BlockSpec((R,) + shape, lambda i: (i,) + (0,) * len(shape))
    const = lambda shape: pl.BlockSpec(shape, lambda i: (0,) * len(shape))
    tab = const((R, LANES))
    conv = pl.BlockSpec((SSD_CONV - 1, R, SSD_CONV_DIM), lambda i: (0, i, 0))
    ywidth = SSD_INNER + RET_HEADS * RET_V_DIM
    sshape = (SSD_HEADS * SSD_HEAD_DIM, SSD_STATE)
    rshape = (RET_HEADS * RET_QK_DIM, RET_V_DIM)
    return pl.pallas_call(
        _odd_sample_kernel,
        grid=(b // R,),
        in_specs=[
            rows(ODD_COLS), conv, per_b(sshape), per_b(rshape), tab, tab,
            const((SSD_CONV, SSD_CONV_DIM)), const((1, SSD_CONV_DIM)), const((1, LANES)), const((1, LANES)),
            const((1, SSD_INNER)), const((1, SSD_INNER)), const((LANES, SSD_INNER)),
            const((R, LANES)), const((R, RET_V_DIM)),
        ],
        out_specs=[rows(ywidth), conv, per_b(sshape), per_b(rshape)],
        out_shape=[
            jax.ShapeDtypeStruct((b, ywidth), F32),
            jax.ShapeDtypeStruct((SSD_CONV - 1, b, SSD_CONV_DIM), F32),
            jax.ShapeDtypeStruct((b,) + sshape, F32), jax.ShapeDtypeStruct((b,) + rshape, F32),
        ],
        compiler_params=pltpu.CompilerParams(dimension_semantics=("arbitrary",), vmem_limit_bytes=VMEM_LIMIT_BYTES),
        name="odd_sample",
    )(proj, cbuf, s, r, *tabs, convw, convb, dtb, arow, drow, snorm, spread, rdec_rows, rnorm_rows)


def _pad_cols(w, n):
    return jnp.pad(w, ((0, 0), (0, n - w.shape[1])))


def _even_w_in(w):
    sq, sk, sv, mq, mk, mv, mo, mi, mf = jnp.split(w, [512, 640, 768, 1024, 1280, 1792, 2304, 2308], axis=1)
    gates = jnp.concatenate([_pad_cols(mi, GATE_F_LANE), _pad_cols(mf, LANES - GATE_F_LANE)], axis=1)
    return jnp.concatenate([sq, sk, sv, mq, mk, mv, mo, gates], axis=1).astype(BF16)


def _odd_w_in(w):
    z, xbc, dt, rq, rk, rv, rg = jnp.split(w, [1024, 2560, 2576, 3088, 3600, 4112], axis=1)
    return jnp.concatenate([z, xbc, rq, rk, rv, rg, _pad_cols(dt, LANES)], axis=1).astype(BF16)


def _lane_angles(pos, rot_dim, theta, freq_of_lane):
    half = rot_dim // 2
    inv = jnp.power(jnp.float32(theta), -jnp.arange(half, dtype=F32) * (2.0 / rot_dim))
    return jnp.asarray(pos).astype(F32)[:, None] * inv[freq_of_lane][None, :]


def _rope16_tables(pos):
    half = SW_ROT_DIM // 2
    d = np.arange(LANES) % SW_HEAD_DIM
    ang = _lane_angles(pos, SW_ROT_DIM, ROPE_THETA, d % half)
    cos, sin = jnp.cos(ang), jnp.sin(ang)
    return (jnp.where(d < SW_ROT_DIM, cos, 1.0), jnp.where(d < half, -sin, 0.0),
            jnp.where((d >= half) & (d < SW_ROT_DIM), sin, 0.0))


def _rope128_tables(pos):
    half = RET_QK_DIM // 2
    lane = np.arange(LANES)
    ang = _lane_angles(pos, RET_QK_DIM, RET_ROPE_THETA, lane % half)
    return jnp.cos(ang), jnp.where(lane < half, -jnp.sin(ang), jnp.sin(ang))


def _ret_consts():
    L = CHUNK
    f = np.float32
    lg = np.log(f(1.0) - np.exp2(f(-5.0) - np.arange(RET_HEADS, dtype=f))).astype(f)
    idx = np.arange(L, dtype=f)
    diff = idx[:, None] - idx[None, :]
    with np.errstate(invalid="ignore"):
        dmat = np.exp(np.where(diff >= 0, diff[None] * lg[:, None, None], -np.inf)).astype(f)
    q_scale = np.exp((idx[None] + f(1.0)) * lg[:, None]).astype(f)
    k_scale = np.exp((f(L) - f(1.0) - idx[None]) * lg[:, None]).astype(f)
    chunk_decay = np.exp(f(L) * lg).astype(f)
    bc = lambda t: np.ascontiguousarray(np.broadcast_to(t[:, :, None], (RET_HEADS, L, LANES)))
    cd = np.ascontiguousarray(np.broadcast_to(chunk_decay[:, None, None], (RET_HEADS, L, LANES)))
    return dmat, bc(q_scale), bc(k_scale), cd, lg


def _rows8(t):
    return jnp.pad(t, ((0, SUBLANES - t.shape[0]), (0, 0)))


def _gate_bias_rows(gb):
    ib = jnp.broadcast_to(gb[:ML_HEADS, None], (ML_HEADS, LANES))
    fb = jnp.broadcast_to(gb[ML_HEADS:, None], (ML_HEADS, LANES))
    return jnp.concatenate([_rows8(ib), _rows8(fb)], axis=0)


def kernel(x_prompt, x_sample, cache_mem_k, cache_mem_v, cache_swa_k, cache_swa_v, state_mlstm_C, state_mlstm_n,
           state_mlstm_m, state_ssd_conv, state_ssd, state_ret, mem_prompt, norm_mix, norm_xattn, norm_mem, norm_ffn,
           even_w_in, mlstm_gate_bias, swa_q_norm, swa_k_norm, swa_sinks, mlstm_out_norm, even_w_out, odd_w_in,
           ssd_conv_w, ssd_conv_b, ssd_dt_bias, ssd_a_log, ssd_d, ssd_norm, ret_norm, odd_w_out, mem_wq, mem_wk,
           mem_wv, mem_q_norm, mem_k_norm, mem_wo, ffn_w1, ffn_w2):
    bp, seq, d = x_prompt.shape
    bs = x_sample.shape[0]
    depth = norm_mix.shape[0]
    tm = 512

    pos_p = np.arange(seq, dtype=np.int32)
    pos_s = np.full((SUBLANES,), PAST_LEN, dtype=np.int32)
    tab16_p, tab16_s = _rope16_tables(pos_p), _rope16_tables(pos_s)
    tab128_p, tab128_s = _rope128_tables(pos_p), _rope128_tables(pos_s)
    dmat, q_scale, k_scale, chunk_decay, lg = _ret_consts()
    rdec_rows = _rows8(jnp.asarray(np.broadcast_to(np.exp(lg)[:, None], (RET_HEADS, LANES))))
    ii = np.arange(CHUNK)
    triu = (ii[:, None] <= ii[None, :]).astype(np.float32)
    tril = (ii[:, None] >= ii[None, :]).astype(np.float32)
    jj = np.arange(LANES)
    segm = np.where(jj[:, None] // SW_HEAD_DIM == jj[None, :] // SW_HEAD_DIM, 1.0 / SW_HEAD_DIM, 0.0).astype(np.float32)
    spread = (jj[:, None] == np.arange(SSD_INNER)[None, :] // SSD_HEAD_DIM).astype(np.float32)
    row1 = lambda t: t.reshape(1, -1).astype(F32)
    pad_lanes = lambda t: jnp.pad(t.reshape(1, -1).astype(F32), ((0, 0), (0, LANES - t.shape[-1])))

    yp = x_prompt.reshape(bp * seq, d)
    ys = x_sample.reshape(bs, d)
    mem = mem_prompt.reshape(bp * MEM_LEN, d)
    cmk = cache_mem_k.reshape(depth, bs, MEM_LEN * MEM_HEADS, MEM_HEAD_DIM)
    cmv = cache_mem_v.reshape(depth, bs, MEM_LEN * MEM_HEADS, MEM_HEAD_DIM)
    w1s, w2s = ffn_w1.astype(BF16), ffn_w2.astype(BF16)
    p_mk, p_mv = [], []
    outs = {}
    for l in range(depth):
        g_mix = row1(norm_mix[l])
        if l % 2 == 0:
            e = l // 2
            w_in = _even_w_in(even_w_in[e])
            w_out = even_w_out[e].astype(BF16)
            qn = row1(jnp.tile(swa_q_norm[e], SW_HEADS))
            kn = row1(jnp.tile(swa_k_norm[e], SW_KV_HEADS))
            gb = _gate_bias_rows(mlstm_gate_bias[e].astype(F32))
            onorm = row1(mlstm_out_norm[e])
            sinks = swa_sinks[e].astype(F32)
            mix_p, kc, vc, caug, mm = even_prompt(yp.reshape(bp, seq, d), g_mix, w_in, tab16_p, qn, kn, sinks, segm,
                                                  triu, gb, onorm)
            mix_p = mix_p.reshape(bp * seq, -1)
            outs["p_swk"] = kc.reshape(1, bp, CHUNK, SW_KV_HEADS, SW_HEAD_DIM)
            outs["p_swv"] = vc.reshape(1, bp, CHUNK, SW_KV_HEADS, SW_HEAD_DIM)
            outs["p_c"] = caug[..., :ML_V_DIM].reshape(1, bp, ML_HEADS, ML_QK_DIM, ML_V_DIM)
            outs["p_n"] = caug[..., ML_V_DIM].reshape(1, bp, ML_HEADS, ML_QK_DIM)
            outs["p_m"] = mm[:, :ML_HEADS, 0].reshape(1, bp, ML_HEADS)

            proj_s = norm_proj(ys, g_mix, w_in, tm=bs)
            sink_rows = jnp.broadcast_to(sinks[:, None], (SW_HEADS, LANES))
            onorm_rows = _rows8(mlstm_out_norm[e].astype(F32).reshape(ML_HEADS, ML_V_DIM))
            gb_lanes = jnp.concatenate([pad_lanes(mlstm_gate_bias[e][:ML_HEADS]),
                                        pad_lanes(mlstm_gate_bias[e][ML_HEADS:])], axis=0)
            mix_s, nk, nv, ncst, nn, nm = even_sample(
                proj_s,
                cache_swa_k[e].reshape(bs, CHUNK, LANES), cache_swa_v[e].reshape(bs, CHUNK, LANES),
                state_mlstm_C[e].reshape(bs, ML_HEADS * ML_QK_DIM, ML_V_DIM),
                state_mlstm_n[e].reshape(bs, ML_HEADS * ML_QK_DIM),
                jnp.pad(state_mlstm_m[e], ((0, 0), (0, LANES - ML_HEADS))),
                tab16_s, qn, kn, sink_rows, gb_lanes, onorm_rows)
            outs["s_swk"] = nk.reshape(1, bs, CHUNK, SW_KV_HEADS, SW_HEAD_DIM)
            outs["s_swv"] = nv.reshape(1, bs, CHUNK, SW_KV_HEADS, SW_HEAD_DIM)
            outs["s_c"] = ncst.reshape(1, bs, ML_HEADS, ML_QK_DIM, ML_V_DIM)
            outs["s_n"] = nn.reshape(1, bs, ML_HEADS, ML_QK_DIM)
            outs["s_m"] = nm[:, :ML_HEADS].reshape(1, bs, ML_HEADS)
        else:
            o = l // 2
            w_in = _odd_w_in(odd_w_in[o])
            w_out = odd_w_out[o].astype(BF16)
            convw = ssd_conv_w[o].astype(F32)
            convb = row1(ssd_conv_b[o])
            dtb = pad_lanes(ssd_dt_bias[o])
            arow = pad_lanes(-jnp.exp(ssd_a_log[o].astype(F32)))
            drow = row1(jnp.repeat(ssd_d[o].astype(F32), SSD_HEAD_DIM))
            snorm = row1(ssd_norm[o])
            rnorm = row1(ret_norm[o])
            mix_p, ctail, sst, rst = odd_prompt(yp.reshape(bp, seq, d), g_mix, w_in, tab128_p, convw, convb, dtb, arow,
                                                drow, snorm, tril, (dmat, q_scale, k_scale, chunk_decay), rnorm,
                                                group=2)
            mix_p = mix_p.reshape(bp * seq, -1)
            outs["p_conv"] = ctail[:, SUBLANES - (SSD_CONV - 1):, :].reshape(1, bp, SSD_CONV - 1, SSD_CONV_DIM)
            outs["p_ssd"] = sst.reshape(1, bp, SSD_HEADS, SSD_HEAD_DIM, SSD_STATE)
            outs["p_ret"] = rst.reshape(1, bp, RET_HEADS, RET_QK_DIM, RET_V_DIM)

            proj_s = norm_proj(ys, g_mix, w_in, tm=bs)
            rnorm_rows = _rows8(ret_norm[o].astype(F32).reshape(RET_HEADS, RET_V_DIM))
            mix_s, ncb, ns, nr = odd_sample(
                proj_s, jnp.swapaxes(state_ssd_conv[o], 0, 1),
                state_ssd[o].reshape(bs, SSD_HEADS * SSD_HEAD_DIM, SSD_STATE),
                state_ret[o].reshape(bs, RET_HEADS * RET_QK_DIM, RET_V_DIM),
                tab128_s, convw, convb, dtb, arow, drow, snorm, spread, rdec_rows, rnorm_rows)
            outs["s_conv"] = jnp.swapaxes(ncb, 0, 1).reshape(1, bs, SSD_CONV - 1, SSD_CONV_DIM)
            outs["s_ssd"] = ns.reshape(1, bs, SSD_HEADS, SSD_HEAD_DIM, SSD_STATE)
            outs["s_ret"] = nr.reshape(1, bs, RET_HEADS, RET_QK_DIM, RET_V_DIM)

        wkv = jnp.concatenate([mem_wk[l], mem_wv[l]], axis=1).astype(BF16)
        qnorm = row1(mem_q_norm[l])
        mkv = norm_proj(mem, row1(norm_mem[l]), wkv, tm=tm, head_norm=row1(mem_k_norm[l]), head_norm_cols=MEM_WIDTH)
        p_mk.append(mkv[:, :MEM_WIDTH].reshape(bp, MEM_LEN, MEM_HEADS, MEM_HEAD_DIM))
        p_mv.append(mkv[:, MEM_WIDTH:].reshape(bp, MEM_LEN, MEM_HEADS, MEM_HEAD_DIM))
        wq, wo = mem_wq[l].astype(BF16), mem_wo[l].astype(BF16)
        gx, gf = row1(norm_xattn[l]), row1(norm_ffn[l])
        ys = res_proj(ys, mix_s, w_out)
        qs = norm_proj(ys, gx, wq, tm=bs, head_norm=qnorm, head_norm_cols=MEM_WIDTH)
        yp, att_s = post_prompt(yp, mix_p, w_out, gx, wq, qnorm, mkv, wo, gf, w1s, w2s, qs, cmk, cmv, layer=l,
                                rows_per_batch=seq, tm=tm)
        ys = res_proj_ffn(ys, att_s, wo, gf, w1s, w2s, layer=l)

    return (yp.reshape(bp, seq, d), ys.reshape(bs, 1, d),
            jnp.stack(p_mk), jnp.stack(p_mv), outs["p_swk"], outs["p_swv"], outs["p_c"], outs["p_n"], outs["p_m"],
            outs["p_conv"], outs["p_ssd"], outs["p_ret"],
            outs["s_swk"], outs["s_swv"], outs["s_c"], outs["s_n"], outs["s_m"],
            outs["s_conv"], outs["s_ssd"], outs["s_ret"])
```

```python
import functools
import math

import jax
import jax.numpy as jnp
import numpy as np
from jax import lax
from jax.experimental import pallas as pl
from jax.experimental.pallas import tpu as pltpu

F32 = jnp.float32
BF16 = jnp.bfloat16

D_MODEL = 1024
PAST_LEN = 8192
EPS = 1e-6
CHUNK = 128
NEG = -1e30

SW_HEADS, SW_KV_HEADS, SW_HEAD_DIM, SW_ROT_DIM = 8, 2, 64, 16
ROPE_THETA = 500000.0
ML_HEADS, ML_QK_DIM, ML_V_DIM = 4, 64, 128
SSD_HEADS, SSD_HEAD_DIM, SSD_GROUPS, SSD_STATE, SSD_CONV = 16, 64, 2, 128, 4
SSD_INNER = SSD_HEADS * SSD_HEAD_DIM
SSD_CONV_DIM = SSD_INNER + 2 * SSD_GROUPS * SSD_STATE
RET_HEADS, RET_QK_DIM, RET_V_DIM = 4, 128, 128
RET_ROPE_THETA = 10000.0
MEM_LEN, MEM_HEADS, MEM_HEAD_DIM = 256, 4, 128
MEM_WIDTH = MEM_HEADS * MEM_HEAD_DIM
FFN_DIM = 4 * D_MODEL
FFN_CHUNK = 512
PROJ_COLS_PER_STAGE = 256

LANES = 128
SUBLANES = 8
VMEM_LIMIT_BYTES = 56 * 1024 * 1024

E_SQ, E_SK, E_SV, E_MQ, E_MK, E_MV, E_MO, E_GATE, EVEN_COLS = 0, 512, 640, 768, 1024, 1280, 1792, 2304, 2432
GATE_F_LANE = 8
O_Z, O_XBC, O_RQ, O_RK, O_RV, O_RG, O_DT, ODD_COLS = 0, 1024, 2560, 3072, 3584, 4096, 4608, 4736


def _mm(a, b):
    return jnp.dot(a.astype(BF16), b.astype(BF16), preferred_element_type=F32)


def _mm_nt(a, b):
    return lax.dot_general(a.astype(BF16), b.astype(BF16), (((1,), (1,)), ((), ())), preferred_element_type=F32)


def _mm_tn(a, b):
    return lax.dot_general(a.astype(BF16), b.astype(BF16), (((0,), (0,)), ((), ())), preferred_element_type=F32)


def _dg(a, b, ca, cb):
    return lax.dot_general(a, b, (((ca,), (cb,)), ((), ())), preferred_element_type=F32)


def _split3(x):
    hi = x.astype(BF16).astype(F32)
    r1 = x - hi
    mid = r1.astype(BF16).astype(F32)
    lo = (r1 - mid).astype(BF16).astype(F32)
    return hi, mid, lo


def _mm_exact_rhs(x, e):
    hi, mid, lo = _split3(x)
    return _dg(hi, e, 1, 0) + _dg(mid, e, 1, 0) + _dg(lo, e, 1, 0)


def _mm_exact_lhs(e, x):
    hi, mid, lo = _split3(x)
    return _dg(e, hi, 1, 0) + _dg(e, mid, 1, 0) + _dg(e, lo, 1, 0)


def _mm_tn_exact_lhs(x, e):
    hi, mid, lo = _split3(x)
    return _dg(hi, e, 0, 0) + _dg(mid, e, 0, 0) + _dg(lo, e, 0, 0)


def _rms(x, g):
    return x * lax.rsqrt(jnp.mean(x * x, axis=-1, keepdims=True) + EPS) * g


def _seg_rms_mxu(x, g, seg_mean):
    return x * lax.rsqrt(_mm_exact_rhs(x * x, seg_mean) + EPS) * g


def _seg_rms(x, g, low):
    xx = x * x
    s_lo = jnp.sum(jnp.where(low, xx, 0.0), axis=-1, keepdims=True)
    s_hi = jnp.sum(jnp.where(low, 0.0, xx), axis=-1, keepdims=True)
    return x * lax.rsqrt(jnp.where(low, s_lo, s_hi) * (1.0 / SW_HEAD_DIM) + EPS) * g


def _sigmoid(x):
    return 1.0 / (1.0 + jnp.exp(-x))


def _silu(x):
    return x * _sigmoid(x)


def _softplus(x):
    return jnp.maximum(x, 0.0) + jnp.log1p(jnp.exp(-jnp.abs(x)))


def _log_sigmoid(x):
    return -_softplus(-x)


def _rope16(x, cos, sin_lo, sin_hi):
    return x * cos + pltpu.roll(x, LANES - 8, 1) * sin_lo + pltpu.roll(x, 8, 1) * sin_hi


def _rope128(x, cos, sin):
    return x * cos + pltpu.roll(x, 64, 1) * sin


def _iota(shape, dim):
    return lax.broadcasted_iota(jnp.int32, shape, dim)


def _cummax_lanes(x):
    lane = _iota(x.shape, 1)
    shift = 1
    while shift < x.shape[1]:
        x = jnp.maximum(x, jnp.where(lane >= shift, pltpu.roll(x, shift, 1), -jnp.inf))
        shift *= 2
    return x


def _norm_proj_kernel(x_ref, g_ref, w_ref, hn_ref, o_ref, *, chunks, head_norm_cols):
    xn = _rms(x_ref[...], g_ref[...]).astype(BF16)
    for c0, cs in chunks:
        r = jnp.dot(xn, w_ref[:, c0:c0 + cs], preferred_element_type=F32)
        if c0 < head_norm_cols:
            parts = [_rms(r[:, i:i + LANES], hn_ref[...]) for i in range(0, cs, LANES)]
            r = jnp.concatenate(parts, axis=1)
        o_ref[:, c0:c0 + cs] = r


def _col_chunks(n, width=512):
    return tuple((c, min(width, n - c)) for c in range(0, n, width))


def norm_proj(x, g, w, *, tm, head_norm=None, head_norm_cols=0):
    n, d = x.shape
    m = w.shape[1]
    if head_norm is None:
        head_norm = jnp.ones((1, LANES), F32)
    kern = functools.partial(_norm_proj_kernel, chunks=_col_chunks(m), head_norm_cols=head_norm_cols)
    return pl.pallas_call(
        kern,
        grid=(n // tm,),
        in_specs=[
            pl.BlockSpec((tm, d), lambda i: (i, 0)),
            pl.BlockSpec((1, d), lambda i: (0, 0)),
            pl.BlockSpec((d, m), lambda i: (0, 0), pipeline_mode=pl.Buffered(1)),
            pl.BlockSpec((1, LANES), lambda i: (0, 0)),
        ],
        out_specs=pl.BlockSpec((tm, m), lambda i: (i, 0)),
        out_shape=jax.ShapeDtypeStruct((n, m), F32),
        compiler_params=pltpu.CompilerParams(dimension_semantics=("arbitrary",), vmem_limit_bytes=VMEM_LIMIT_BYTES),
        name="norm_proj",
    )(x, g, w, head_norm)


def _ffn(x, g_ref, w1_ref, w2_ref):
    h = _rms(x, g_ref[...]).astype(BF16)
    acc = None
    for c in range(0, FFN_DIM, FFN_CHUNK):
        u = jnp.maximum(jnp.dot(h, w1_ref[:, c:c + FFN_CHUNK], preferred_element_type=F32), 0.0)
        t = jnp.dot((u * u).astype(BF16), w2_ref[c:c + FFN_CHUNK, :], preferred_element_type=F32)
        acc = t if acc is None else acc + t
    return x + acc


def _post_chain(x_ref, a_ref, wout_ref, gx_ref, wq_ref, qn_ref, mk_ref, mv_ref, wo_ref, gf_ref, w1_ref, w2_ref, o_ref):
    x = x_ref[...] + _mm(a_ref[...], wout_ref[...])
    yield
    q = jnp.dot(_rms(x, gx_ref[...]).astype(BF16), wq_ref[...], preferred_element_type=F32)
    yield
    outs = []
    for h in range(MEM_HEADS):
        sl = slice(h * MEM_HEAD_DIM, (h + 1) * MEM_HEAD_DIM)
        qh = _rms(q[:, sl], qn_ref[...])
        s = _mm_nt(qh, mk_ref[:, sl]) * (MEM_HEAD_DIM ** -0.5)
        p = jnp.exp(s - jnp.max(s, axis=-1, keepdims=True))
        p = p / jnp.sum(p, axis=-1, keepdims=True)
        outs.append(_mm(p, mv_ref[:, sl]))
        yield
    x = x + _mm(jnp.concatenate(outs, axis=1), wo_ref[...])
    h = _rms(x, gf_ref[...]).astype(BF16)
    yield
    acc = None
    for c in range(0, FFN_DIM, FFN_CHUNK):
        u = jnp.maximum(jnp.dot(h, w1_ref[:, c:c + FFN_CHUNK], preferred_element_type=F32), 0.0)
        t = jnp.dot((u * u).astype(BF16), w2_ref[c:c + FFN_CHUNK, :], preferred_element_type=F32)
        acc = t if acc is None else acc + t
        yield
    o_ref[...] = x + acc


def _xattn_row(q_row, k, v):
    row = _iota((SUBLANES, LANES), 0)
    groups = MEM_LEN * MEM_HEADS // SUBLANES
    q8 = jnp.zeros((SUBLANES, LANES), F32)
    for h in range(MEM_HEADS):
        q8 = jnp.where(row % MEM_HEADS == h, q_row[:, h * MEM_HEAD_DIM:(h + 1) * MEM_HEAD_DIM], q8)
    k3 = k.reshape(groups, SUBLANES, LANES)
    s = jnp.sum(k3 * q8[None], axis=-1, keepdims=True) * (MEM_HEAD_DIM ** -0.5)
    mx = _pair_rows(jnp.max(s, axis=0), jnp.maximum)
    p = jnp.exp(s - mx[None, :, 0:1])
    den = _pair_rows(jnp.sum(p, axis=0), jnp.add)
    o8 = _pair_rows(jnp.sum(p * v.reshape(groups, SUBLANES, LANES), axis=0), jnp.add) / den
    return jnp.concatenate([o8[h:h + 1] for h in range(MEM_HEADS)], axis=1)


def _xattn_chain(q_ref, mk_ref, mv_ref, o_ref):
    for i in range(q_ref.shape[0]):
        o_ref[i:i + 1, :] = _xattn_row(q_ref[i:i + 1, :], mk_ref[i], mv_ref[i])
        yield


def _post_prompt_kernel(x_ref, a_ref, wout_ref, gx_ref, wq_ref, qn_ref, mk_ref, mv_ref, wo_ref, gf_ref, w1_ref, w2_ref,
                        sq_ref, smk_ref, smv_ref,
                        o_ref, so_ref):
    _lockstep([_post_chain(x_ref, a_ref, wout_ref, gx_ref, wq_ref, qn_ref, mk_ref, mv_ref, wo_ref, gf_ref, w1_ref,
                           w2_ref, o_ref),
               _xattn_chain(sq_ref, smk_ref, smv_ref, so_ref)], every=[1, 3])


def post_prompt(x, a, wout, gx, wq, qn, mkv, wo, gf, w1, w2, sq, smk, smv, *, layer, rows_per_batch, tm):
    n, d = x.shape
    ka = a.shape[1]
    steps = n // tm
    bs = sq.shape[0]
    rs = bs // steps
    tiles = rows_per_batch // tm
    const = lambda shape: pl.BlockSpec(shape, lambda i: (0, 0), pipeline_mode=pl.Buffered(1))
    slab = lambda shape: pl.BlockSpec((None,) + shape, lambda i: (layer, 0, 0), pipeline_mode=pl.Buffered(1))
    smem = pl.BlockSpec((None, rs, MEM_LEN * MEM_HEADS, MEM_HEAD_DIM), lambda i: (layer, i, 0, 0))
    srow = pl.BlockSpec((None, rs, MEM_WIDTH), lambda i: (i, 0, 0))
    y, att = pl.pallas_call(
        _post_prompt_kernel,
        grid=(steps,),
        in_specs=[
            pl.BlockSpec((tm, d), lambda i: (i, 0)),
            pl.BlockSpec((tm, ka), lambda i: (i, 0)),
            const((ka, d)),
            const((1, d)),
            const((d, MEM_WIDTH)),
            const((1, MEM_HEAD_DIM)),
            pl.BlockSpec((MEM_LEN, MEM_WIDTH), lambda i: (i // tiles, 0)),
            pl.BlockSpec((MEM_LEN, MEM_WIDTH), lambda i: (i // tiles, 1)),
            const((MEM_WIDTH, d)),
            const((1, d)),
            slab((d, FFN_DIM)),
            slab((FFN_DIM, d)),
            srow, smem, smem,
        ],
        out_specs=[pl.BlockSpec((tm, d), lambda i: (i, 0)), srow],
        out_shape=[jax.ShapeDtypeStruct((n, d), F32), jax.ShapeDtypeStruct((steps, rs, MEM_WIDTH), F32)],
        compiler_params=pltpu.CompilerParams(dimension_semantics=("arbitrary",), vmem_limit_bytes=VMEM_LIMIT_BYTES),
        name="post_prompt",
    )(x, a, wout, gx, wq, qn, mkv, mkv, wo, gf, w1, w2, sq.reshape(steps, rs, MEM_WIDTH), smk, smv)
    return y, att.reshape(bs, MEM_WIDTH)


def _res_proj_kernel(x_ref, a_ref, w_ref, o_ref):
    o_ref[...] = x_ref[...] + _mm(a_ref[...], w_ref[...])


def res_proj(x, a, w):
    n, d = x.shape
    return pl.pallas_call(
        _res_proj_kernel,
        out_shape=jax.ShapeDtypeStruct((n, d), F32),
        compiler_params=pltpu.CompilerParams(vmem_limit_bytes=VMEM_LIMIT_BYTES),
        name="res_proj",
    )(x, a, w)


def _res_proj_ffn_kernel(x_ref, a_ref, w_ref, gf_ref, w1_ref, w2_ref, o_ref):
    x = x_ref[...] + _mm(a_ref[...], w_ref[...])
    o_ref[...] = _ffn(x, gf_ref, w1_ref, w2_ref)


def res_proj_ffn(x, a, w, gf, w1, w2, *, layer):
    n, d = x.shape
    full = lambda t: pl.BlockSpec(t.shape, lambda i: (0, 0))
    slab = lambda shape: pl.BlockSpec((None,) + shape, lambda i: (layer, 0, 0), pipeline_mode=pl.Buffered(1))
    return pl.pallas_call(
        _res_proj_ffn_kernel,
        grid=(1,),
        in_specs=[full(x), full(a), full(w), full(gf), slab((d, FFN_DIM)), slab((FFN_DIM, d))],
        out_specs=pl.BlockSpec((n, d), lambda i: (0, 0)),
        out_shape=jax.ShapeDtypeStruct((n, d), F32),
        compiler_params=pltpu.CompilerParams(dimension_semantics=("arbitrary",), vmem_limit_bytes=VMEM_LIMIT_BYTES),
        name="res_proj_ffn",
    )(x, a, w, gf, w1, w2)


def _pair_rows(x, op):
    xb = jnp.broadcast_to(x, (SUBLANES, LANES))
    return op(xb, pltpu.roll(xb, SUBLANES // 2, 0))


def _proj_chain(x_ref, g_ref, w_ref, dst, width):
    group, _, d = x_ref.shape
    xn = _rms(x_ref[...].reshape(group * CHUNK, d), g_ref[...]).astype(BF16)
    yield
    cols = w_ref.shape[1]
    for c0 in range(0, cols, width):
        cs = min(width, cols - c0)
        r = jnp.dot(xn, w_ref[:, c0:c0 + cs], preferred_element_type=F32)
        for b in range(group):
            dst[b, :, c0:c0 + cs] = r[b * CHUNK:(b + 1) * CHUNK]
        yield


def _even_prompt_kernel(xn_ref, x0_ref, g_ref, w_ref, cos_ref, sinlo_ref, sinhi_ref, qn_ref, kn_ref, sink_ref, segm_ref,
                        triu_ref, gb_ref, onorm_ref,
                        y_ref, kc_ref, vc_ref, caug_ref, m_ref,
                        kprev, vprev, cst, mst, proj):
    n = pl.program_id(0)
    batch = xn_ref.shape[0]
    slot = lax.rem(n, 2)

    @pl.when(n == 0)
    def _():
        kprev[...] = jnp.zeros_like(kprev)
        vprev[...] = jnp.zeros_like(vprev)
        cst[...] = jnp.zeros_like(cst)
        mst[...] = jnp.zeros_like(mst)
        for _ in _proj_chain(x0_ref, g_ref, w_ref, proj.at[0], PROJ_COLS_PER_STAGE):
            pass

    p_ref = proj.at[slot]

    cos, sinlo, sinhi = cos_ref[...], sinlo_ref[...], sinhi_ref[...]
    lane = _iota((1, LANES), 1)
    low = lane < 64
    qi = _iota((CHUNK, 2 * CHUNK), 0)
    si = _iota((CHUNK, 2 * CHUNK), 1)
    valid = (si >= qi) & (si <= qi + CHUNK) & ((si >= CHUNK) | (n > 0))
    causal = _iota((CHUNK, CHUNK), 0) >= _iota((CHUNK, CHUNK), 1)
    ones_col = jnp.where(_iota((CHUNK, LANES), 1) == 0, 1.0, 0.0)
    row64 = _iota((CHUNK, 1), 0) < 64
    shared = [{} for _ in range(batch)]
    swa = [_swa_prompt_chunk(p_ref.at[b], y_ref.at[b], kprev.at[b], vprev.at[b], (cos, sinlo, sinhi), qn_ref, kn_ref,
                             sink_ref, segm_ref[...], low, valid, shared[b], part)
           for part in range(2) for b in range(batch)]
    mlstm = [_mlstm_prompt_chunk(p_ref.at[b], y_ref.at[b], cst.at[b], mst.at[b], triu_ref, gb_ref, onorm_ref, low,
                                 causal, ones_col, row64) for b in range(batch)]
    gemm = _proj_chain(xn_ref, g_ref, w_ref, proj.at[1 - slot], 2 * PROJ_COLS_PER_STAGE)
    new_kv = _lockstep(swa + mlstm + [gemm], every=[1] * (3 * batch) + [3])[:batch]

    @pl.when(n == pl.num_programs(0) - 1)
    def _():
        for b in range(batch):
            kc_ref[b], vc_ref[b] = new_kv[b]
        caug_ref[...] = cst[...]
        m_ref[...] = mst[...]


def _then(first, second):
    value = yield from first
    yield from second
    return value


def _lockstep(chains, every=None):
    every = every or [1] * len(chains)
    results = [None] * len(chains)
    live = list(range(len(chains)))
    rnd = 0
    while live:
        for i in list(live):
            if rnd % every[i]:
                continue
            try:
                next(chains[i])
            except StopIteration as stop:
                results[i] = stop.value
                live.remove(i)
        rnd += 1
    return results


def _swa_prompt_chunk(p_ref, y_ref, kprev, vprev, tabs, qn_ref, kn_ref, sink_ref, segm, low, valid, shared, part):
    cos, sinlo, sinhi = tabs
    if part == 0:
        tiles = [p_ref[:, E_SK:E_SK + LANES]] + [p_ref[:, E_SQ + j * LANES:E_SQ + (j + 1) * LANES]
                                                 for j in range(SW_HEADS // 2)]
        gains = [kn_ref[...]] + [qn_ref[:, j * LANES:(j + 1) * LANES] for j in range(SW_HEADS // 2)]
        ms = _mm_exact_rhs(jnp.concatenate([t * t for t in tiles], axis=0), segm)
        yield
        normed = [_rope16(t * lax.rsqrt(ms[i * CHUNK:(i + 1) * CHUNK] + EPS) * g, cos, sinlo, sinhi)
                  for i, (t, g) in enumerate(zip(tiles, gains))]
        k = normed[0]
        v = p_ref[:, E_SV:E_SV + LANES]
        kk = jnp.concatenate([kprev[...], k], axis=0)
        vv = jnp.concatenate([vprev[...], v], axis=0)
        kk_sw = pltpu.roll(kk, 64, 1)
        vv_sw = pltpu.roll(vv, 64, 1)
        shared["kvar"] = {(0, 0): jnp.where(low, kk, 0.0), (0, 1): jnp.where(low, 0.0, kk_sw),
                          (1, 0): jnp.where(low, kk_sw, 0.0), (1, 1): jnp.where(low, 0.0, kk)}
        shared["vvar"] = {(0, 0): vv, (0, 1): vv_sw, (1, 0): vv_sw, (1, 1): vv}
        shared["normed"] = normed
        yield
    else:
        yield
        yield
    normed, kvar, vvar = shared["normed"], shared["kvar"], shared["vvar"]
    half = SW_HEADS // 4
    for j in range(part * half, (part + 1) * half):
        qb = normed[1 + j]
        halves = []
        for pos in range(2):
            h = 2 * j + pos
            kv = h // (SW_HEADS // SW_KV_HEADS)
            s = jnp.where(valid, _mm_nt(qb, kvar[(kv, pos)]) * (SW_HEAD_DIM ** -0.5), NEG)
            yield
            sink = sink_ref[h]
            m = jnp.maximum(jnp.max(s, axis=-1, keepdims=True), sink)
            pr = jnp.exp(s - m)
            yield
            pr = pr / (jnp.sum(pr, axis=-1, keepdims=True) + jnp.exp(sink - m))
            halves.append(_mm(pr, vvar[(kv, pos)]))
            yield
        y_ref[:, j * LANES:(j + 1) * LANES] = jnp.where(low, halves[0], halves[1]).astype(y_ref.dtype)
    if part == 0:
        kprev[...] = k
        vprev[...] = v
        return k, v


def _mlstm_prompt_chunk(p_ref, y_ref, cst, mst, triu_ref, gb_ref, onorm_ref, low, causal, ones_col, row64):
    gt = p_ref[:, E_GATE:E_GATE + LANES].T
    gi = gt[0:SUBLANES] + gb_ref[0:SUBLANES]
    fl = _log_sigmoid(gt[GATE_F_LANE:GATE_F_LANE + SUBLANES] + gb_ref[SUBLANES:2 * SUBLANES])
    yield
    fcum = _mm_exact_rhs(fl, triu_ref[...])
    dd = gi - fcum
    mprev = mst[...]
    yield
    mt = fcum + jnp.maximum(mprev, _cummax_lanes(dd))
    fend = jnp.broadcast_to(fcum[:, CHUNK - 1:CHUNK], fcum.shape)
    mend = jnp.broadcast_to(mt[:, CHUNK - 1:CHUNK], mt.shape)
    decay = jnp.exp(fend + mprev - mend)
    rows = jnp.concatenate([fcum - mt, jnp.exp(fcum + mprev - mt), jnp.exp(-mt), jnp.exp(fend - fcum + gi - mend),
                            jnp.zeros((CHUNK - 4 * SUBLANES, CHUNK), F32)], axis=0)
    yield
    cols = rows.T
    mst[...] = mend
    yield
    for j in range(ML_HEADS // 2):
        qblk = p_ref[:, E_MQ + j * LANES:E_MQ + (j + 1) * LANES]
        kblk = p_ref[:, E_MK + j * LANES:E_MK + (j + 1) * LANES] * (ML_QK_DIM ** -0.5)
        c_old = cst[j]
        qm2 = jnp.concatenate([jnp.where(low, qblk, 0.0), jnp.where(low, 0.0, qblk)], axis=0)
        s2 = _mm_nt(qm2, kblk)
        qc2 = _mm(qm2, c_old)
        yield
        kws, vaugs = [], []
        for pos in range(2):
            h = 2 * j + pos
            rs = slice(pos * CHUNK, (pos + 1) * CHUNK)
            logw = cols[:, h:h + 1] + dd[h:h + 1, :]
            w = jnp.exp(jnp.where(causal, logw, -jnp.inf))
            sqk = s2[rs] * w
            yield
            vh = p_ref[:, E_MV + h * LANES:E_MV + (h + 1) * LANES]
            cs = cols[:, SUBLANES + h:SUBLANES + h + 1]
            num = cs * qc2[rs, :ML_V_DIM] + _mm(sqk, vh)
            den = cs * qc2[rs, ML_V_DIM:ML_V_DIM + 1] + jnp.sum(sqk, axis=-1, keepdims=True)
            hh = num / jnp.maximum(jnp.abs(den), cols[:, 2 * SUBLANES + h:2 * SUBLANES + h + 1])
            yield
            hsl = slice(h * ML_V_DIM, (h + 1) * ML_V_DIM)
            hn = _rms(hh, onorm_ref[:, hsl])
            mo = p_ref[:, E_MO + h * ML_V_DIM:E_MO + (h + 1) * ML_V_DIM]
            y_ref[:, SW_HEADS * SW_HEAD_DIM + h * ML_V_DIM:SW_HEADS * SW_HEAD_DIM + (h + 1) * ML_V_DIM] = (
                hn * _sigmoid(mo)).astype(y_ref.dtype)
            msk = low if pos == 0 else jnp.logical_not(low)
            kws.append(jnp.where(msk, kblk, 0.0) * cols[:, 3 * SUBLANES + h:3 * SUBLANES + h + 1])
            vaugs.append(jnp.concatenate([vh, ones_col], axis=1))
            yield
        upd = _mm_tn(jnp.concatenate(kws, axis=0), jnp.concatenate(vaugs, axis=0))
        dec = jnp.where(row64, decay[2 * j:2 * j + 1, 0:1], decay[2 * j + 1:2 * j + 2, 0:1])
        cst[j] = dec * c_old + upd
        yield


def even_prompt(x, g, w, tabs, qn, kn, sinks, segm, triu, gb, onorm):
    batch, seq, d = x.shape
    nc = seq // CHUNK
    tab = pl.BlockSpec((CHUNK, LANES), lambda n: (n, 0))
    const = lambda shape: pl.BlockSpec(shape, lambda n: (0,) * len(shape))
    state_shapes = [(batch, CHUNK, LANES), (batch, CHUNK, LANES),
                    (batch, ML_HEADS // 2, 2 * ML_QK_DIM, 2 * ML_V_DIM), (batch, SUBLANES, LANES)]
    return pl.pallas_call(
        _even_prompt_kernel,
        grid=(nc,),
        in_specs=[
            pl.BlockSpec((batch, CHUNK, d), lambda n: (0, jnp.minimum(n + 1, nc - 1), 0)),
            pl.BlockSpec((batch, CHUNK, d), lambda n: (0, 0, 0)),
            const((1, d)), pl.BlockSpec((d, EVEN_COLS), lambda n: (0, 0), pipeline_mode=pl.Buffered(1)),
            tab, tab, tab,
            const((1, SW_HEADS * SW_HEAD_DIM)), const((1, LANES)),
            pl.BlockSpec(memory_space=pltpu.SMEM),
            const((LANES, LANES)), const((CHUNK, CHUNK)), const((2 * SUBLANES, LANES)),
            const((1, ML_HEADS * ML_V_DIM)),
        ],
        out_specs=[pl.BlockSpec((batch, CHUNK, D_MODEL), lambda n: (0, n, 0))] + [const(s) for s in state_shapes],
        out_shape=[jax.ShapeDtypeStruct((batch, seq, D_MODEL), BF16)]
        + [jax.ShapeDtypeStruct(s, F32) for s in state_shapes],
        scratch_shapes=[pltpu.VMEM(s, F32) for s in state_shapes] + [pltpu.VMEM((2, batch, CHUNK, EVEN_COLS), F32)],
        compiler_params=pltpu.CompilerParams(dimension_semantics=("arbitrary",), vmem_limit_bytes=VMEM_LIMIT_BYTES),
        name="even_prompt",
    )(x, x, g, w, *tabs, qn, kn, sinks, segm, triu, gb, onorm)


def _odd_prompt_kernel(xn_ref, x0_ref, g_ref, w_ref, cosr_ref, sinr_ref, convw_ref, convb_ref, dtb_ref, arow_ref,
                       drow_ref, snorm_ref, tril_ref, dmat_ref, qs_ref, ks_ref, cd_ref, rnorm_ref,
                       y_ref, conv_ref, s_ref, r_ref,
                       ext, sst, rst, proj):
    n = pl.program_id(1)
    batch = xn_ref.shape[0]
    slot = lax.rem(n, 2)

    @pl.when(n == 0)
    def _():
        ext[:, 0:SUBLANES] = jnp.zeros((batch, SUBLANES, SSD_CONV_DIM), F32)
        sst[...] = jnp.zeros_like(sst)
        rst[...] = jnp.zeros_like(rst)
        for _ in _proj_chain(x0_ref, g_ref, w_ref, proj.at[0], PROJ_COLS_PER_STAGE):
            pass

    p_ref = proj.at[slot]

    lane = _iota((1, LANES), 1)
    low = lane < 64
    row64 = _iota((CHUNK, 1), 0) < 64
    causal = _iota((CHUNK, CHUNK), 0) >= _iota((CHUNK, CHUNK), 1)
    chains = [
        _then(_ssd_prompt_chunk(p_ref.at[b], y_ref.at[b], ext.at[b], sst.at[b], convw_ref, convb_ref, dtb_ref,
                                arow_ref, drow_ref, snorm_ref, tril_ref, low, row64, causal),
              _ret_prompt_chunk(p_ref.at[b], y_ref.at[b], rst.at[b], cosr_ref, sinr_ref, dmat_ref, qs_ref, ks_ref,
                                cd_ref, rnorm_ref))
        for b in range(batch)]
    gemm = _proj_chain(xn_ref, g_ref, w_ref, proj.at[1 - slot], PROJ_COLS_PER_STAGE)
    tails = _lockstep(chains + [gemm], every=[1] * batch + [2])[:batch]

    @pl.when(n == pl.num_programs(1) - 1)
    def _():
        for b in range(batch):
            conv_ref[b] = tails[b]
        s_ref[...] = sst[...]
        r_ref[...] = rst[...]


def _ssd_prompt_chunk(p_ref, y_ref, ext, sst, convw_ref, convb_ref, dtb_ref, arow_ref, drow_ref, snorm_ref, tril_ref,
                      low, row64, causal):
    tail = SUBLANES
    ext[tail:tail + CHUNK] = p_ref[:, O_XBC:O_XBC + SSD_CONV_DIM]
    yield
    xe = ext[...]
    acc = None
    for jj in range(SSD_CONV):
        shift = SSD_CONV - 1 - jj
        tap = (pltpu.roll(xe, shift, 0) if shift else xe)[tail:tail + CHUNK] * convw_ref[jj:jj + 1]
        acc = tap if acc is None else acc + tap
    xact = _silu(acc + convb_ref[...])
    new_tail = ext[CHUNK:CHUNK + tail]
    ext[0:tail] = new_tail
    yield

    dt = _softplus(p_ref[:, O_DT:O_DT + LANES] + dtb_ref[...])
    cum = _mm_exact_lhs(tril_ref[...], dt * arow_ref[...])
    yield
    cum_t = cum.T
    dt_t = dt.T
    ecum = jnp.exp(cum)
    cend = cum[CHUNK - 1:CHUNK, :]
    wend = jnp.exp(cend - cum) * dt
    eend = jnp.exp(cend)
    yield
    pairs_per_group = SSD_HEADS // SSD_GROUPS // 2
    for g in range(SSD_GROUPS):
        ys = []
        bc = xact[:, SSD_INNER + g * SSD_STATE:SSD_INNER + (g + 1) * SSD_STATE]
        cc = xact[:, SSD_INNER + (SSD_GROUPS + g) * SSD_STATE:SSD_INNER + (SSD_GROUPS + g + 1) * SSD_STATE]
        cb = _mm_nt(cc, bc)
        yield
        for jg in range(pairs_per_group):
            j = g * pairs_per_group + jg
            ha, hb = 2 * j, 2 * j + 1
            xp = xact[:, j * LANES:(j + 1) * LANES]
            s_old = sst[j]
            y = jnp.where(low, ecum[:, ha:ha + 1], ecum[:, hb:hb + 1]) * _mm_nt(cc, s_old)
            yield
            wmats = []
            for h in (ha, hb):
                seg = cum[:, h:h + 1] - cum_t[h:h + 1, :]
                wmats.append(cb * jnp.exp(jnp.where(causal, seg, -jnp.inf)) * dt_t[h:h + 1, :])
                yield
            y = y + _mm(jnp.concatenate(wmats, axis=1),
                        jnp.concatenate([jnp.where(low, xp, 0.0), jnp.where(low, 0.0, xp)], axis=0))
            yield
            xw = xp * jnp.where(low, wend[:, ha:ha + 1], wend[:, hb:hb + 1])
            sst[j] = jnp.where(row64, eend[:, ha:ha + 1], eend[:, hb:hb + 1]) * s_old + _mm_tn(xw, bc)
            ys.append(y)
            yield
        gs = slice(g * SSD_INNER // SSD_GROUPS, (g + 1) * SSD_INNER // SSD_GROUPS)
        yg = jnp.concatenate(ys, axis=1)
        yg = (yg + drow_ref[:, gs] * xact[:, gs]) * _silu(p_ref[:, O_Z + gs.start:O_Z + gs.stop])
        y_ref[:, gs] = _rms(yg, snorm_ref[:, gs]).astype(y_ref.dtype)
        yield
    return new_tail


def _ret_prompt_chunk(p_ref, y_ref, rst, cosr_ref, sinr_ref, dmat_ref, qs_ref, ks_ref, cd_ref, rnorm_ref):
    cosr, sinr = cosr_ref[...], sinr_ref[...]
    for h in range(RET_HEADS):
        hs = h * LANES
        q = _rope128(p_ref[:, O_RQ + hs:O_RQ + hs + LANES], cosr, sinr)
        k = _rope128(p_ref[:, O_RK + hs:O_RK + hs + LANES], cosr, sinr) * (RET_QK_DIM ** -0.5)
        v = p_ref[:, O_RV + hs:O_RV + hs + LANES]
        yield
        r_old = rst[h]
        o = _mm(_mm_nt(q, k) * dmat_ref[h], v) + qs_ref[h] * _mm(q, r_old)
        yield
        rst[h] = cd_ref[h] * r_old + _mm_tn(k * ks_ref[h], v)
        xc = o - jnp.mean(o, axis=-1, keepdims=True)
        yn = xc * lax.rsqrt(jnp.mean(xc * xc, axis=-1, keepdims=True) + EPS) * rnorm_ref[:, hs:hs + LANES]
        y_ref[:, SSD_INNER + hs:SSD_INNER + hs + LANES] = (
            yn * _silu(p_ref[:, O_RG + hs:O_RG + hs + LANES])).astype(y_ref.dtype)
        yield


def odd_prompt(x, g, w, tabs, convw, convb, dtb, arow, drow, snorm, tril, ret_consts, rnorm, *, group):
    batch, seq, d = x.shape
    nc = seq // CHUNK
    tab = pl.BlockSpec((CHUNK, LANES), lambda g, n: (n, 0))
    const = lambda shape: pl.BlockSpec(shape, lambda g, n: (0,) * len(shape))
    per_g = lambda shape: pl.BlockSpec((group,) + shape, lambda g, n: (g,) + (0,) * len(shape))
    hc = (RET_HEADS, CHUNK, LANES)
    ywidth = SSD_INNER + RET_HEADS * RET_V_DIM
    states = [(SUBLANES, SSD_CONV_DIM), (SSD_HEADS // 2, LANES, SSD_STATE), hc]
    return pl.pallas_call(
        _odd_prompt_kernel,
        grid=(batch // group, seq // CHUNK),
        in_specs=[
            pl.BlockSpec((group, CHUNK, d), lambda g, n: (g, jnp.minimum(n + 1, nc - 1), 0)),
            pl.BlockSpec((group, CHUNK, d), lambda g, n: (g, 0, 0)),
            const((1, d)), pl.BlockSpec((d, ODD_COLS), lambda g, n: (0, 0), pipeline_mode=pl.Buffered(1)),
            tab, tab,
            const((SSD_CONV, SSD_CONV_DIM)), const((1, SSD_CONV_DIM)), const((1, LANES)), const((1, LANES)),
            const((1, SSD_INNER)), const((1, SSD_INNER)), const((CHUNK, CHUNK)),
            const(hc), const(hc), const(hc), const(hc), const((1, RET_HEADS * RET_V_DIM)),
        ],
        out_specs=[pl.BlockSpec((group, CHUNK, ywidth), lambda g, n: (g, n, 0))] + [per_g(s) for s in states],
        out_shape=[jax.ShapeDtypeStruct((batch, seq, ywidth), BF16)]
        + [jax.ShapeDtypeStruct((batch,) + s, F32) for s in states],
        scratch_shapes=[pltpu.VMEM((group, SUBLANES + CHUNK, SSD_CONV_DIM), F32),
                        pltpu.VMEM((group,) + states[1], F32), pltpu.VMEM((group,) + states[2], F32),
                        pltpu.VMEM((2, group, CHUNK, ODD_COLS), F32)],
        compiler_params=pltpu.CompilerParams(dimension_semantics=("arbitrary", "arbitrary"),
                                             vmem_limit_bytes=VMEM_LIMIT_BYTES),
        name="odd_prompt",
    )(x, x, g, w, *tabs, convw, convb, dtb, arow, drow, snorm, tril, *ret_consts, rnorm)


def _lane_to_rows(g, offset):
    sel = _iota(g.shape, 1) == _iota(g.shape, 0) + offset
    return jnp.sum(jnp.where(sel, g, 0.0), axis=-1, keepdims=True)


def _block_rows(x, i, nblk, blk):
    row = _iota((SUBLANES, blk), 0)
    out = jnp.zeros((SUBLANES, blk), F32)
    for b in range(nblk):
        out = jnp.where(row == b, x[i:i + 1, b * blk:(b + 1) * blk], out)
    return out


def _even_sample_kernel(p_ref, bk_ref, bv_ref, c_ref, nrow_ref, mrow_ref, cos_ref, sinlo_ref, sinhi_ref, qn_ref,
                        kn_ref, sink_ref, gb_ref, onorm_ref,
                        y_ref, nk_ref, nv_ref, nc_ref, nn_ref, nm_ref):
    R = SUBLANES
    cos, sinlo, sinhi = cos_ref[...], sinlo_ref[...], sinhi_ref[...]
    row = _iota((R, LANES), 0)
    lane = _iota((R, LANES), 1)
    low = lane < 64
    group = SW_HEADS // SW_KV_HEADS
    scale = SW_HEAD_DIM ** -0.5
    sink = sink_ref[:, 0:1]
    last = _iota((CHUNK, LANES), 0) == CHUNK - 1

    k = _rope16(_seg_rms(p_ref[:, E_SK:E_SK + LANES], kn_ref[...], low), cos, sinlo, sinhi)
    v = p_ref[:, E_SV:E_SV + LANES]
    qb, qb_sw = [], []
    for j in range(SW_HEADS // 2):
        sl = slice(E_SQ + j * LANES, E_SQ + (j + 1) * LANES)
        qb.append(_rope16(_seg_rms(p_ref[:, sl], qn_ref[:, sl], low), cos, sinlo, sinhi))
        qb_sw.append(pltpu.roll(qb[j], 64, 1))

    g = p_ref[:, E_GATE:E_GATE + LANES]
    ic = g + gb_ref[0:1]
    fl = _log_sigmoid(pltpu.roll(g, LANES - GATE_F_LANE, 1) + gb_ref[1:2])
    mprev = mrow_ref[...]
    mt = jnp.maximum(fl + mprev, ic)
    w_all = jnp.exp(ic - mt)
    cs_all = jnp.exp(fl + mprev - mt)
    em_all = jnp.exp(-mt)
    nm_ref[...] = mt
    width = ML_HEADS * ML_QK_DIM
    own = _iota((R, width), 1) // ML_QK_DIM == _iota((R, width), 0)
    kscaled = p_ref[:, E_MK:E_MK + width] * (ML_QK_DIM ** -0.5)

    def one_row(i):
        qm = jnp.zeros((R, LANES), F32)
        for j in range(SW_HEADS // 2):
            for pos in range(2):
                h = 2 * j + pos
                kv = h // group
                src = (qb[j] if pos == kv else qb_sw[j])[i:i + 1]
                qm = jnp.where((row == h) & (low if kv == 0 else jnp.logical_not(low)), src, qm)
        bk, bv = bk_ref[i], bv_ref[i]
        ki, vi = k[i:i + 1], v[i:i + 1]
        s = _dg(qm, bk, 1, 1) * scale
        s_new = jnp.sum(qm * ki, axis=-1, keepdims=True) * scale
        yield
        m = jnp.maximum(jnp.maximum(jnp.max(s, axis=-1, keepdims=True), s_new), sink)
        pr = jnp.exp(s - m)
        p_new = jnp.exp(s_new - m)
        yield
        den = jnp.sum(pr, axis=-1, keepdims=True) + p_new + jnp.exp(sink - m)
        o = (_dg(pr, bv, 1, 0) + p_new * vi) / den
        yield
        o_sw = pltpu.roll(o, 64, 1)
        for j in range(SW_HEADS // 2):
            halves = []
            for pos in range(2):
                h = 2 * j + pos
                halves.append((o if pos == h // group else o_sw)[h:h + 1, :])
            y_ref[i:i + 1, j * LANES:(j + 1) * LANES] = jnp.where(low[0:1], halves[0], halves[1])
        nk_ref[i] = jnp.where(last, ki, pltpu.roll(bk, CHUNK - 1, 0))
        nv_ref[i] = jnp.where(last, vi, pltpu.roll(bv, CHUNK - 1, 0))
        yield

        w = _lane_to_rows(jnp.broadcast_to(w_all[i:i + 1], (R, LANES)), 0)
        cs = _lane_to_rows(jnp.broadcast_to(cs_all[i:i + 1], (R, LANES)), 0)
        em = _lane_to_rows(jnp.broadcast_to(em_all[i:i + 1], (R, LANES)), 0)
        qrows = jnp.where(own, p_ref[i:i + 1, E_MQ:E_MQ + width], 0.0)
        krows = jnp.where(own, kscaled[i:i + 1], 0.0)
        c_old = c_ref[i]
        qc = _dg(qrows, c_old, 1, 0)
        qn_dot = jnp.sum(qrows * nrow_ref[i:i + 1], axis=-1, keepdims=True)
        sqk = jnp.sum(qrows * krows, axis=-1, keepdims=True) * w
        yield
        v4 = _block_rows(p_ref[:, E_MV:E_MV + ML_HEADS * ML_V_DIM], i, ML_HEADS, ML_V_DIM)
        mo4 = _block_rows(p_ref[:, E_MO:E_MO + ML_HEADS * ML_V_DIM], i, ML_HEADS, ML_V_DIM)
        num = cs * qc + sqk * v4
        dn = cs * qn_dot + sqk
        hh = num / jnp.maximum(jnp.abs(dn), em)
        hn = _rms(hh, onorm_ref[...]) * _sigmoid(mo4)
        yield
        for h in range(ML_HEADS):
            c0 = SW_HEADS * SW_HEAD_DIM + h * ML_V_DIM
            y_ref[i:i + 1, c0:c0 + ML_V_DIM] = hn[h:h + 1]
        dec_col = jnp.concatenate(
            [jnp.broadcast_to(cs[h:h + 1, 0:1], (ML_QK_DIM, ML_V_DIM)) for h in range(ML_HEADS)], axis=0)
        nc_ref[i] = dec_col * c_old + _dg(krows * w, v4, 0, 0)
        dec_lanes = jnp.sum(jnp.where(own, cs, 0.0), axis=0, keepdims=True)
        nn_ref[i:i + 1] = dec_lanes * nrow_ref[i:i + 1] + jnp.sum(krows * w, axis=0, keepdims=True)
        yield

    _lockstep([one_row(i) for i in range(R)])


def even_sample(proj, bk, bv, c, nrow, mrow, tabs, qn, kn, sink_rows, gb, onorm_rows):
    b = proj.shape[0]
    R = SUBLANES
    width = ML_HEADS * ML_QK_DIM
    rows = lambda w: pl.BlockSpec((R, w), lambda i: (i, 0))
    per_b = lambda shape: pl.BlockSpec((R,) + shape, lambda i: (i,) + (0,) * len(shape))
    const = lambda shape: pl.BlockSpec(shape, lambda i: (0,) * len(shape))
    tab = const((R, LANES))
    return pl.pallas_call(
        _even_sample_kernel,
        grid=(b // R,),
        in_specs=[
            rows(EVEN_COLS), per_b((CHUNK, LANES)), per_b((CHUNK, LANES)), per_b((width, ML_V_DIM)),
            rows(width), rows(LANES), tab, tab, tab,
            const((1, SW_HEADS * SW_HEAD_DIM)), const((1, LANES)), const((R, LANES)),
            const((2, LANES)), const((R, ML_V_DIM)),
        ],
        out_specs=[
            rows(D_MODEL), per_b((CHUNK, LANES)), per_b((CHUNK, LANES)), per_b((width, ML_V_DIM)),
            rows(width), rows(LANES),
        ],
        out_shape=[
            jax.ShapeDtypeStruct((b, D_MODEL), F32),
            jax.ShapeDtypeStruct((b, CHUNK, LANES), F32), jax.ShapeDtypeStruct((b, CHUNK, LANES), F32),
            jax.ShapeDtypeStruct((b, width, ML_V_DIM), F32), jax.ShapeDtypeStruct((b, width), F32),
            jax.ShapeDtypeStruct((b, LANES), F32),
        ],
        compiler_params=pltpu.CompilerParams(dimension_semantics=("arbitrary",), vmem_limit_bytes=VMEM_LIMIT_BYTES),
        name="even_sample",
    )(proj, bk, bv, c, nrow, mrow, *tabs, qn, kn, sink_rows, gb, onorm_rows)


def _odd_sample_kernel(p_ref, cb_ref, s_ref, r_ref, cosr_ref, sinr_ref, convw_ref, convb_ref, dtb_ref, arow_ref,
                       drow_ref, snorm_ref, spread_ref, rdec_ref, rnorm_ref,
                       y_ref, ncb_ref, ns_ref, nr_ref):
    R = SUBLANES
    xbc = p_ref[:, O_XBC:O_XBC + SSD_CONV_DIM]
    acc = cb_ref[0] * convw_ref[0:1]
    for jj in range(1, SSD_CONV - 1):
        acc = acc + cb_ref[jj] * convw_ref[jj:jj + 1]
    acc = acc + xbc * convw_ref[SSD_CONV - 1:SSD_CONV]
    xact = _silu(acc + convb_ref[...])
    for jj in range(SSD_CONV - 2):
        ncb_ref[jj] = cb_ref[jj + 1]
    ncb_ref[SSD_CONV - 2] = xbc

    dt = _softplus(p_ref[:, O_DT:O_DT + LANES] + dtb_ref[...])
    dec = jnp.exp(dt * arow_ref[...])
    xs = xact[:, 0:SSD_INNER]
    xdt = xs * _mm_exact_rhs(dt, spread_ref[...])
    gown = _iota((R, SSD_INNER), 1) // (SSD_INNER // SSD_GROUPS) == _iota((R, SSD_INNER), 0)
    bpart = xact[:, SSD_INNER:SSD_INNER + SSD_GROUPS * SSD_STATE]
    cpart = xact[:, SSD_INNER + SSD_GROUPS * SSD_STATE:SSD_CONV_DIM]

    cosr, sinr = cosr_ref[...], sinr_ref[...]
    width = RET_HEADS * RET_QK_DIM
    q4 = jnp.concatenate([_rope128(p_ref[:, O_RQ + h * LANES:O_RQ + (h + 1) * LANES], cosr, sinr)
                          for h in range(RET_HEADS)], axis=1)
    k4 = jnp.concatenate([_rope128(p_ref[:, O_RK + h * LANES:O_RK + (h + 1) * LANES], cosr, sinr)
                          for h in range(RET_HEADS)], axis=1) * (RET_QK_DIM ** -0.5)
    own = _iota((R, width), 1) // RET_QK_DIM == _iota((R, width), 0)
    gam = rdec_ref[:, 0:1]
    gam_col = jnp.concatenate(
        [jnp.broadcast_to(rdec_ref[h:h + 1, :], (RET_QK_DIM, RET_V_DIM)) for h in range(RET_HEADS)], axis=0)

    def one_row(i):
        brows = _block_rows(bpart, i, SSD_GROUPS, SSD_STATE)
        crows = _block_rows(cpart, i, SSD_GROUPS, SSD_STATE)
        xw = jnp.where(gown, xdt[i:i + 1], 0.0)
        dec_col = jnp.concatenate(
            [jnp.broadcast_to(dec[i:i + 1, h:h + 1], (SSD_HEAD_DIM, SSD_STATE)) for h in range(SSD_HEADS)], axis=0)
        yield
        s_new = dec_col * s_ref[i] + _dg(xw, brows, 0, 0)
        ns_ref[i] = s_new
        yield
        yrows = _dg(crows, s_new, 1, 1)
        y_row = jnp.sum(jnp.where(gown, yrows, 0.0), axis=0, keepdims=True)
        yield

        qrows = jnp.where(own, q4[i:i + 1], 0.0)
        krows = jnp.where(own, k4[i:i + 1], 0.0)
        v4 = _block_rows(p_ref[:, O_RV:O_RV + RET_HEADS * RET_V_DIM], i, RET_HEADS, RET_V_DIM)
        g4 = _block_rows(p_ref[:, O_RG:O_RG + RET_HEADS * RET_V_DIM], i, RET_HEADS, RET_V_DIM)
        r_old = r_ref[i]
        att = jnp.sum(qrows * krows, axis=-1, keepdims=True)
        o = att * v4 + gam * _dg(qrows, r_old, 1, 0)
        yield
        nr_ref[i] = gam_col * r_old + _dg(krows, v4, 0, 0)
        yield
        xc = o - jnp.mean(o, axis=-1, keepdims=True)
        yn = xc * lax.rsqrt(jnp.mean(xc * xc, axis=-1, keepdims=True) + EPS) * rnorm_ref[...] * _silu(g4)
        for h in range(RET_HEADS):
            y_ref[i:i + 1, SSD_INNER + h * RET_V_DIM:SSD_INNER + (h + 1) * RET_V_DIM] = yn[h:h + 1]
        return y_row

    ys = _lockstep([one_row(i) for i in range(R)])
    y = (jnp.concatenate(ys, axis=0) + drow_ref[...] * xs) * _silu(p_ref[:, O_Z:O_Z + SSD_INNER])
    gw = SSD_INNER // SSD_GROUPS
    for g in range(SSD_GROUPS):
        y_ref[:, g * gw:(g + 1) * gw] = _rms(y[:, g * gw:(g + 1) * gw], snorm_ref[:, g * gw:(g + 1) * gw])


def odd_sample(proj, cbuf, s, r, tabs, convw, convb, dtb, arow, drow, snorm, spread, rdec_rows, rnorm_rows):
    b = proj.shape[0]
    R = SUBLANES
    rows = lambda w: pl.BlockSpec((R, w), lambda i: (i, 0))
    per_b = lambda shape: pl.BlockSpec((R,) + shape, lambda i: (i,) + (0,) * len(shape))
    const = lambda shape: pl.BlockSpec(shape, lambda i: (0,) * len(shape))
    tab = const((R, LANES))
    conv = pl.BlockSpec((SSD_CONV - 1, R, SSD_CONV_DIM), lambda i: (0, i, 0))
    ywidth = SSD_INNER + RET_HEADS * RET_V_DIM
    sshape = (SSD_HEADS * SSD_HEAD_DIM, SSD_STATE)
    rshape = (RET_HEADS * RET_QK_DIM, RET_V_DIM)
    return pl.pallas_call(
        _odd_sample_kernel,
        grid=(b // R,),
        in_specs=[
            rows(ODD_COLS), conv, per_b(sshape), per_b(rshape), tab, tab,
            const((SSD_CONV, SSD_CONV_DIM)), const((1, SSD_CONV_DIM)), const((1, LANES)), const((1, LANES)),
            const((1, SSD_INNER)), const((1, SSD_INNER)), const((LANES, SSD_INNER)),
            const((R, LANES)), const((R, RET_V_DIM)),
        ],
        out_specs=[rows(ywidth), conv, per_b(sshape), per_b(rshape)],
        out_shape=[
            jax.ShapeDtypeStruct((b, ywidth), F32),
            jax.ShapeDtypeStruct((SSD_CONV - 1, b, SSD_CONV_DIM), F32),
            jax.ShapeDtypeStruct((b,) + sshape, F32), jax.ShapeDtypeStruct((b,) + rshape, F32),
        ],
        compiler_params=pltpu.CompilerParams(dimension_semantics=("arbitrary",), vmem_limit_bytes=VMEM_LIMIT_BYTES),
        name="odd_sample",
    )(proj, cbuf, s, r, *tabs, convw, convb, dtb, arow, drow, snorm, spread, rdec_rows, rnorm_rows)


def _pad_cols(w, n):
    return jnp.pad(w, ((0, 0), (0, n - w.shape[1])))


def _even_w_in(w):
    w = w.astype(BF16)
    mi, mf = w[:, E_GATE:E_GATE + ML_HEADS], w[:, E_GATE + ML_HEADS:E_GATE + 2 * ML_HEADS]
    return jnp.concatenate([w[:, :E_GATE], _pad_cols(mi, GATE_F_LANE), _pad_cols(mf, LANES - GATE_F_LANE)], axis=1)


def _odd_w_in(w):
    w = w.astype(BF16)
    dt0 = O_XBC + SSD_CONV_DIM
    return jnp.concatenate([w[:, :dt0], w[:, dt0 + SSD_HEADS:], _pad_cols(w[:, dt0:dt0 + SSD_HEADS], LANES)], axis=1)


def _lane_angles(pos, rot_dim, theta, freq_of_lane):
    half = rot_dim // 2
    inv = jnp.power(jnp.float32(theta), -jnp.arange(half, dtype=F32) * (2.0 / rot_dim))
    return jnp.asarray(pos).astype(F32)[:, None] * inv[freq_of_lane][None, :]


def _rope16_tables(pos):
    half = SW_ROT_DIM // 2
    d = np.arange(LANES) % SW_HEAD_DIM
    ang = _lane_angles(pos, SW_ROT_DIM, ROPE_THETA, d % half)
    cos, sin = jnp.cos(ang), jnp.sin(ang)
    return (jnp.where(d < SW_ROT_DIM, cos, 1.0), jnp.where(d < half, -sin, 0.0),
            jnp.where((d >= half) & (d < SW_ROT_DIM), sin, 0.0))


def _rope128_tables(pos):
    half = RET_QK_DIM // 2
    lane = np.arange(LANES)
    ang = _lane_angles(pos, RET_QK_DIM, RET_ROPE_THETA, lane % half)
    return jnp.cos(ang), jnp.where(lane < half, -jnp.sin(ang), jnp.sin(ang))


def _ret_consts():
    L = CHUNK
    f = np.float32
    lg = np.log(f(1.0) - np.exp2(f(-5.0) - np.arange(RET_HEADS, dtype=f))).astype(f)
    idx = np.arange(L, dtype=f)
    diff = idx[:, None] - idx[None, :]
    with np.errstate(invalid="ignore"):
        dmat = np.exp(np.where(diff >= 0, diff[None] * lg[:, None, None], -np.inf)).astype(f)
    q_scale = np.exp((idx[None] + f(1.0)) * lg[:, None]).astype(f)
    k_scale = np.exp((f(L) - f(1.0) - idx[None]) * lg[:, None]).astype(f)
    chunk_decay = np.exp(f(L) * lg).astype(f)
    bc = lambda t: np.ascontiguousarray(np.broadcast_to(t[:, :, None], (RET_HEADS, L, LANES)))
    cd = np.ascontiguousarray(np.broadcast_to(chunk_decay[:, None, None], (RET_HEADS, L, LANES)))
    return dmat, bc(q_scale), bc(k_scale), cd, lg


def _rows8(t):
    return jnp.pad(t, ((0, SUBLANES - t.shape[0]), (0, 0)))


def _gate_bias_rows(gb):
    ib = jnp.broadcast_to(gb[:ML_HEADS, None], (ML_HEADS, LANES))
    fb = jnp.broadcast_to(gb[ML_HEADS:, None], (ML_HEADS, LANES))
    return jnp.concatenate([_rows8(ib), _rows8(fb)], axis=0)


def kernel(x_prompt, x_sample, cache_mem_k, cache_mem_v, cache_swa_k, cache_swa_v, state_mlstm_C, state_mlstm_n,
           state_mlstm_m, state_ssd_conv, state_ssd, state_ret, mem_prompt, norm_mix, norm_xattn, norm_mem, norm_ffn,
           even_w_in, mlstm_gate_bias, swa_q_norm, swa_k_norm, swa_sinks, mlstm_out_norm, even_w_out, odd_w_in,
           ssd_conv_w, ssd_conv_b, ssd_dt_bias, ssd_a_log, ssd_d, ssd_norm, ret_norm, odd_w_out, mem_wq, mem_wk,
           mem_wv, mem_q_norm, mem_k_norm, mem_wo, ffn_w1, ffn_w2):
    bp, seq, d = x_prompt.shape
    bs = x_sample.shape[0]
    depth = norm_mix.shape[0]
    tm = 512

    pos_p = np.arange(seq, dtype=np.int32)
    pos_s = np.full((SUBLANES,), PAST_LEN, dtype=np.int32)
    tab16_p, tab16_s = _rope16_tables(pos_p), _rope16_tables(pos_s)
    tab128_p, tab128_s = _rope128_tables(pos_p), _rope128_tables(pos_s)
    dmat, q_scale, k_scale, chunk_decay, lg = _ret_consts()
    rdec_rows = _rows8(jnp.asarray(np.broadcast_to(np.exp(lg)[:, None], (RET_HEADS, LANES))))
    ii = np.arange(CHUNK)
    triu = (ii[:, None] <= ii[None, :]).astype(np.float32)
    tril = (ii[:, None] >= ii[None, :]).astype(np.float32)
    jj = np.arange(LANES)
    segm = np.where(jj[:, None] // SW_HEAD_DIM == jj[None, :] // SW_HEAD_DIM, 1.0 / SW_HEAD_DIM, 0.0).astype(np.float32)
    spread = (jj[:, None] == np.arange(SSD_INNER)[None, :] // SSD_HEAD_DIM).astype(np.float32)
    row1 = lambda t: t.reshape(1, -1).astype(F32)
    pad_lanes = lambda t: jnp.pad(t.reshape(1, -1).astype(F32), ((0, 0), (0, LANES - t.shape[-1])))

    yp = x_prompt.reshape(bp * seq, d)
    ys = x_sample.reshape(bs, d)
    mem = mem_prompt.reshape(bp * MEM_LEN, d)
    cmk = cache_mem_k.reshape(depth, bs, MEM_LEN * MEM_HEADS, MEM_HEAD_DIM)
    cmv = cache_mem_v.reshape(depth, bs, MEM_LEN * MEM_HEADS, MEM_HEAD_DIM)
    w1s, w2s = ffn_w1.astype(BF16), ffn_w2.astype(BF16)
    p_mk, p_mv = [], []
    outs = {}
    for l in range(depth):
        g_mix = row1(norm_mix[l])
        if l % 2 == 0:
            e = l // 2
            w_in = _even_w_in(even_w_in[e])
            w_out = even_w_out[e].astype(BF16)
            qn = row1(jnp.tile(swa_q_norm[e], SW_HEADS))
            kn = row1(jnp.tile(swa_k_norm[e], SW_KV_HEADS))
            gb = _gate_bias_rows(mlstm_gate_bias[e].astype(F32))
            onorm = row1(mlstm_out_norm[e])
            sinks = swa_sinks[e].astype(F32)
            mix_p, kc, vc, caug, mm = even_prompt(yp.reshape(bp, seq, d), g_mix, w_in, tab16_p, qn, kn, sinks, segm,
                                                  triu, gb, onorm)
            mix_p = mix_p.reshape(bp * seq, -1)
            outs["p_swk"] = kc.reshape(1, bp, CHUNK, SW_KV_HEADS, SW_HEAD_DIM)
            outs["p_swv"] = vc.reshape(1, bp, CHUNK, SW_KV_HEADS, SW_HEAD_DIM)
            outs["p_c"] = caug[..., :ML_V_DIM].reshape(1, bp, ML_HEADS, ML_QK_DIM, ML_V_DIM)
            outs["p_n"] = caug[..., ML_V_DIM].reshape(1, bp, ML_HEADS, ML_QK_DIM)
            outs["p_m"] = mm[:, :ML_HEADS, 0].reshape(1, bp, ML_HEADS)

            proj_s = norm_proj(ys, g_mix, w_in, tm=bs)
            sink_rows = jnp.broadcast_to(sinks[:, None], (SW_HEADS, LANES))
            onorm_rows = _rows8(mlstm_out_norm[e].astype(F32).reshape(ML_HEADS, ML_V_DIM))
            gb_lanes = jnp.concatenate([pad_lanes(mlstm_gate_bias[e][:ML_HEADS]),
                                        pad_lanes(mlstm_gate_bias[e][ML_HEADS:])], axis=0)
            mix_s, nk, nv, ncst, nn, nm = even_sample(
                proj_s,
                cache_swa_k[e].reshape(bs, CHUNK, LANES), cache_swa_v[e].reshape(bs, CHUNK, LANES),
                state_mlstm_C[e].reshape(bs, ML_HEADS * ML_QK_DIM, ML_V_DIM),
                state_mlstm_n[e].reshape(bs, ML_HEADS * ML_QK_DIM),
                jnp.pad(state_mlstm_m[e], ((0, 0), (0, LANES - ML_HEADS))),
                tab16_s, qn, kn, sink_rows, gb_lanes, onorm_rows)
            outs["s_swk"] = nk.reshape(1, bs, CHUNK, SW_KV_HEADS, SW_HEAD_DIM)
            outs["s_swv"] = nv.reshape(1, bs, CHUNK, SW_KV_HEADS, SW_HEAD_DIM)
            outs["s_c"] = ncst.reshape(1, bs, ML_HEADS, ML_QK_DIM, ML_V_DIM)
            outs["s_n"] = nn.reshape(1, bs, ML_HEADS, ML_QK_DIM)
            outs["s_m"] = nm[:, :ML_HEADS].reshape(1, bs, ML_HEADS)
        else:
            o = l // 2
            w_in = _odd_w_in(odd_w_in[o])
            w_out = odd_w_out[o].astype(BF16)
            convw = ssd_conv_w[o].astype(F32)
            convb = row1(ssd_conv_b[o])
            dtb = pad_lanes(ssd_dt_bias[o])
            arow = pad_lanes(-jnp.exp(ssd_a_log[o].astype(F32)))
            drow = row1(jnp.repeat(ssd_d[o].astype(F32), SSD_HEAD_DIM))
            snorm = row1(ssd_norm[o])
            rnorm = row1(ret_norm[o])
            mix_p, ctail, sst, rst = odd_prompt(yp.reshape(bp, seq, d), g_mix, w_in, tab128_p, convw, convb, dtb, arow,
                                                drow, snorm, tril, (dmat, q_scale, k_scale, chunk_decay), rnorm,
                                                group=2)
            mix_p = mix_p.reshape(bp * seq, -1)
            outs["p_conv"] = ctail[:, SUBLANES - (SSD_CONV - 1):, :].reshape(1, bp, SSD_CONV - 1, SSD_CONV_DIM)
            outs["p_ssd"] = sst.reshape(1, bp, SSD_HEADS, SSD_HEAD_DIM, SSD_STATE)
            outs["p_ret"] = rst.reshape(1, bp, RET_HEADS, RET_QK_DIM, RET_V_DIM)

            proj_s = norm_proj(ys, g_mix, w_in, tm=bs)
            rnorm_rows = _rows8(ret_norm[o].astype(F32).reshape(RET_HEADS, RET_V_DIM))
            mix_s, ncb, ns, nr = odd_sample(
                proj_s, jnp.swapaxes(state_ssd_conv[o], 0, 1),
                state_ssd[o].reshape(bs, SSD_HEADS * SSD_HEAD_DIM, SSD_STATE),
                state_ret[o].reshape(bs, RET_HEADS * RET_QK_DIM, RET_V_DIM),
                tab128_s, convw, convb, dtb, arow, drow, snorm, spread, rdec_rows, rnorm_rows)
            outs["s_conv"] = jnp.swapaxes(ncb, 0, 1).reshape(1, bs, SSD_CONV - 1, SSD_CONV_DIM)
            outs["s_ssd"] = ns.reshape(1, bs, SSD_HEADS, SSD_HEAD_DIM, SSD_STATE)
            outs["s_ret"] = nr.reshape(1, bs, RET_HEADS, RET_QK_DIM, RET_V_DIM)

        wkv = jnp.concatenate([mem_wk[l], mem_wv[l]], axis=1).astype(BF16)
        qnorm = row1(mem_q_norm[l])
        mkv = norm_proj(mem, row1(norm_mem[l]), wkv, tm=tm, head_norm=row1(mem_k_norm[l]), head_norm_cols=MEM_WIDTH)
        p_mk.append(mkv[:, :MEM_WIDTH].reshape(bp, MEM_LEN, MEM_HEADS, MEM_HEAD_DIM))
        p_mv.append(mkv[:, MEM_WIDTH:].reshape(bp, MEM_LEN, MEM_HEADS, MEM_HEAD_DIM))
        wq, wo = mem_wq[l].astype(BF16), mem_wo[l].astype(BF16)
        gx, gf = row1(norm_xattn[l]), row1(norm_ffn[l])
        ys = res_proj(ys, mix_s, w_out)
        qs = norm_proj(ys, gx, wq, tm=bs, head_norm=qnorm, head_norm_cols=MEM_WIDTH)
        yp, att_s = post_prompt(yp, mix_p, w_out, gx, wq, qnorm, mkv, wo, gf, w1s, w2s, qs, cmk, cmv, layer=l,
                                rows_per_batch=seq, tm=tm)
        ys = res_proj_ffn(ys, att_s, wo, gf, w1s, w2s, layer=l)

    return (yp.reshape(bp, seq, d), ys.reshape(bs, 1, d),
            jnp.stack(p_mk), jnp.stack(p_mv), outs["p_swk"], outs["p_swv"], outs["p_c"], outs["p_n"], outs["p_m"],
            outs["p_conv"], outs["p_ssd"], outs["p_ret"],
            outs["s_swk"], outs["s_swv"], outs["s_c"], outs["s_n"], outs["s_m"],
            outs["s_conv"], outs["s_ssd"], outs["s_ret"])
```

```python
import functools
import math

import jax
import jax.numpy as jnp
import numpy as np
from jax import lax
from jax.experimental import pallas as pl
from jax.experimental.pallas import tpu as pltpu

F32 = jnp.float32
BF16 = jnp.bfloat16

D_MODEL = 1024
PAST_LEN = 8192
EPS = 1e-6
CHUNK = 128
NEG = -1e30

SW_HEADS, SW_KV_HEADS, SW_HEAD_DIM, SW_ROT_DIM = 8, 2, 64, 16
ROPE_THETA = 500000.0
ML_HEADS, ML_QK_DIM, ML_V_DIM = 4, 64, 128
SSD_HEADS, SSD_HEAD_DIM, SSD_GROUPS, SSD_STATE, SSD_CONV = 16, 64, 2, 128, 4
SSD_INNER = SSD_HEADS * SSD_HEAD_DIM
SSD_CONV_DIM = SSD_INNER + 2 * SSD_GROUPS * SSD_STATE
RET_HEADS, RET_QK_DIM, RET_V_DIM = 4, 128, 128
RET_ROPE_THETA = 10000.0
MEM_LEN, MEM_HEADS, MEM_HEAD_DIM = 256, 4, 128
MEM_WIDTH = MEM_HEADS * MEM_HEAD_DIM
FFN_DIM = 4 * D_MODEL
FFN_CHUNK = 512
PROJ_COLS_PER_STAGE = 256
SWA_CHAINS = 2

LANES = 128
SUBLANES = 8
VMEM_LIMIT_BYTES = 56 * 1024 * 1024

E_SQ, E_SK, E_SV, E_MQ, E_MK, E_MV, E_MO, E_GATE, EVEN_COLS = 0, 512, 640, 768, 1024, 1280, 1792, 2304, 2432
GATE_F_LANE = 8
O_Z, O_XBC, O_RQ, O_RK, O_RV, O_RG, O_DT, ODD_COLS = 0, 1024, 2560, 3072, 3584, 4096, 4608, 4736


def _mm(a, b):
    return jnp.dot(a.astype(BF16), b.astype(BF16), preferred_element_type=F32)


def _mm_nt(a, b):
    return lax.dot_general(a.astype(BF16), b.astype(BF16), (((1,), (1,)), ((), ())), preferred_element_type=F32)


def _mm_tn(a, b):
    return lax.dot_general(a.astype(BF16), b.astype(BF16), (((0,), (0,)), ((), ())), preferred_element_type=F32)


def _dg(a, b, ca, cb):
    return lax.dot_general(a, b, (((ca,), (cb,)), ((), ())), preferred_element_type=F32)


def _split3(x):
    hi = x.astype(BF16).astype(F32)
    r1 = x - hi
    mid = r1.astype(BF16).astype(F32)
    lo = (r1 - mid).astype(BF16).astype(F32)
    return hi, mid, lo


def _mm_exact_rhs(x, e):
    hi, mid, lo = _split3(x)
    return _dg(hi, e, 1, 0) + _dg(mid, e, 1, 0) + _dg(lo, e, 1, 0)


def _mm_exact_lhs(e, x):
    hi, mid, lo = _split3(x)
    return _dg(e, hi, 1, 0) + _dg(e, mid, 1, 0) + _dg(e, lo, 1, 0)


def _mm_tn_exact_lhs(x, e):
    hi, mid, lo = _split3(x)
    return _dg(hi, e, 0, 0) + _dg(mid, e, 0, 0) + _dg(lo, e, 0, 0)


def _rms(x, g):
    return x * lax.rsqrt(jnp.mean(x * x, axis=-1, keepdims=True) + EPS) * g


def _seg_rms_mxu(x, g, seg_mean):
    return x * lax.rsqrt(_mm_exact_rhs(x * x, seg_mean) + EPS) * g


def _seg_rms(x, g, low):
    xx = x * x
    s_lo = jnp.sum(jnp.where(low, xx, 0.0), axis=-1, keepdims=True)
    s_hi = jnp.sum(jnp.where(low, 0.0, xx), axis=-1, keepdims=True)
    return x * lax.rsqrt(jnp.where(low, s_lo, s_hi) * (1.0 / SW_HEAD_DIM) + EPS) * g


def _sigmoid(x):
    return 0.5 * jnp.tanh(0.5 * x) + 0.5


def _silu(x):
    h = 0.5 * x
    return h * jnp.tanh(h) + h


def _softplus(x):
    return jnp.maximum(x, 0.0) + jnp.log1p(jnp.exp(-jnp.abs(x)))


def _log_sigmoid(x):
    return -_softplus(-x)


def _rope16(x, cos, sin_lo, sin_hi):
    return x * cos + pltpu.roll(x, LANES - 8, 1) * sin_lo + pltpu.roll(x, 8, 1) * sin_hi


def _rope128(x, cos, sin):
    return x * cos + pltpu.roll(x, 64, 1) * sin


def _iota(shape, dim):
    return lax.broadcasted_iota(jnp.int32, shape, dim)


def _cummax_lanes(x):
    lane = _iota(x.shape, 1)
    shift = 1
    while shift < x.shape[1]:
        x = jnp.maximum(x, jnp.where(lane >= shift, pltpu.roll(x, shift, 1), -jnp.inf))
        shift *= 2
    return x


def _norm_proj_kernel(x_ref, g_ref, w_ref, hn_ref, o_ref, *, chunks, head_norm_cols):
    xn = _rms(x_ref[...], g_ref[...]).astype(BF16)
    for c0, cs in chunks:
        r = jnp.dot(xn, w_ref[:, c0:c0 + cs], preferred_element_type=F32)
        if c0 < head_norm_cols:
            parts = [_rms(r[:, i:i + LANES], hn_ref[...]) for i in range(0, cs, LANES)]
            r = jnp.concatenate(parts, axis=1)
        o_ref[:, c0:c0 + cs] = r


def _col_chunks(n, width=512):
    return tuple((c, min(width, n - c)) for c in range(0, n, width))


def norm_proj(x, g, w, *, tm, head_norm=None, head_norm_cols=0):
    n, d = x.shape
    m = w.shape[1]
    if head_norm is None:
        head_norm = jnp.ones((1, LANES), F32)
    kern = functools.partial(_norm_proj_kernel, chunks=_col_chunks(m), head_norm_cols=head_norm_cols)
    return pl.pallas_call(
        kern,
        grid=(n // tm,),
        in_specs=[
            pl.BlockSpec((tm, d), lambda i: (i, 0)),
            pl.BlockSpec((1, d), lambda i: (0, 0)),
            pl.BlockSpec((d, m), lambda i: (0, 0), pipeline_mode=pl.Buffered(1)),
            pl.BlockSpec((1, LANES), lambda i: (0, 0)),
        ],
        out_specs=pl.BlockSpec((tm, m), lambda i: (i, 0)),
        out_shape=jax.ShapeDtypeStruct((n, m), F32),
        compiler_params=pltpu.CompilerParams(dimension_semantics=("arbitrary",), vmem_limit_bytes=VMEM_LIMIT_BYTES),
        name="norm_proj",
    )(x, g, w, head_norm)


def _ffn(x, g_ref, w1_ref, w2_ref):
    h = _rms(x, g_ref[...]).astype(BF16)
    acc = None
    for c in range(0, FFN_DIM, FFN_CHUNK):
        u = jnp.maximum(jnp.dot(h, w1_ref[:, c:c + FFN_CHUNK], preferred_element_type=F32), 0.0)
        t = jnp.dot((u * u).astype(BF16), w2_ref[c:c + FFN_CHUNK, :], preferred_element_type=F32)
        acc = t if acc is None else acc + t
    return x + acc


def _post_chain(x_ref, a_ref, wout_ref, gx_ref, wq_ref, qn_ref, mk_ref, mv_ref, wo_ref, gf_ref, w1_ref, w2_ref, o_ref):
    x = x_ref[...] + _mm(a_ref[...], wout_ref[...])
    yield
    q = jnp.dot(_rms(x, gx_ref[...]).astype(BF16), wq_ref[...], preferred_element_type=F32)
    yield
    outs = []
    for h in range(MEM_HEADS):
        sl = slice(h * MEM_HEAD_DIM, (h + 1) * MEM_HEAD_DIM)
        qh = _rms(q[:, sl], qn_ref[...])
        s = _mm_nt(qh, mk_ref[:, sl]) * (MEM_HEAD_DIM ** -0.5)
        p = jnp.exp(s - jnp.max(s, axis=-1, keepdims=True))
        p = p / jnp.sum(p, axis=-1, keepdims=True)
        outs.append(_mm(p, mv_ref[:, sl]))
        yield
    x = x + _mm(jnp.concatenate(outs, axis=1), wo_ref[...])
    h = _rms(x, gf_ref[...]).astype(BF16)
    yield
    acc = None
    for c in range(0, FFN_DIM, FFN_CHUNK):
        u = jnp.maximum(jnp.dot(h, w1_ref[:, c:c + FFN_CHUNK], preferred_element_type=F32), 0.0)
        t = jnp.dot((u * u).astype(BF16), w2_ref[c:c + FFN_CHUNK, :], preferred_element_type=F32)
        acc = t if acc is None else acc + t
        yield
    o_ref[...] = x + acc


def _xattn_row(q_row, k, v):
    row = _iota((SUBLANES, LANES), 0)
    groups = MEM_LEN * MEM_HEADS // SUBLANES
    q8 = jnp.zeros((SUBLANES, LANES), F32)
    for h in range(MEM_HEADS):
        q8 = jnp.where(row % MEM_HEADS == h, q_row[:, h * MEM_HEAD_DIM:(h + 1) * MEM_HEAD_DIM], q8)
    k3 = k.reshape(groups, SUBLANES, LANES)
    s = jnp.sum(k3 * q8[None], axis=-1, keepdims=True) * (MEM_HEAD_DIM ** -0.5)
    mx = _pair_rows(jnp.max(s, axis=0), jnp.maximum)
    p = jnp.exp(s - mx[None, :, 0:1])
    den = _pair_rows(jnp.sum(p, axis=0), jnp.add)
    o8 = _pair_rows(jnp.sum(p * v.reshape(groups, SUBLANES, LANES), axis=0), jnp.add) / den
    return jnp.concatenate([o8[h:h + 1] for h in range(MEM_HEADS)], axis=1)


def _xattn_chain(q_ref, mk_ref, mv_ref, o_ref):
    for i in range(q_ref.shape[0]):
        o_ref[i:i + 1, :] = _xattn_row(q_ref[i:i + 1, :], mk_ref[i], mv_ref[i])
        yield


def _post_prompt_kernel(x_ref, a_ref, wout_ref, gx_ref, wq_ref, qn_ref, mk_ref, mv_ref, wo_ref, gf_ref, w1_ref, w2_ref,
                        sq_ref, smk_ref, smv_ref,
                        o_ref, so_ref):
    _lockstep([_post_chain(x_ref, a_ref, wout_ref, gx_ref, wq_ref, qn_ref, mk_ref, mv_ref, wo_ref, gf_ref, w1_ref,
                           w2_ref, o_ref),
               _xattn_chain(sq_ref, smk_ref, smv_ref, so_ref)], every=[1, 3])


def post_prompt(x, a, wout, gx, wq, qn, mkv, wo, gf, w1, w2, sq, smk, smv, *, layer, rows_per_batch, tm):
    n, d = x.shape
    ka = a.shape[1]
    steps = n // tm
    bs = sq.shape[0]
    rs = bs // steps
    tiles = rows_per_batch // tm
    const = lambda shape: pl.BlockSpec(shape, lambda i: (0, 0), pipeline_mode=pl.Buffered(1))
    slab = lambda shape: pl.BlockSpec((None,) + shape, lambda i: (layer, 0, 0), pipeline_mode=pl.Buffered(1))
    smem = pl.BlockSpec((None, rs, MEM_LEN * MEM_HEADS, MEM_HEAD_DIM), lambda i: (layer, i, 0, 0))
    srow = pl.BlockSpec((None, rs, MEM_WIDTH), lambda i: (i, 0, 0))
    y, att = pl.pallas_call(
        _post_prompt_kernel,
        grid=(steps,),
        in_specs=[
            pl.BlockSpec((tm, d), lambda i: (i, 0)),
            pl.BlockSpec((tm, ka), lambda i: (i, 0)),
            const((ka, d)),
            const((1, d)),
            const((d, MEM_WIDTH)),
            const((1, MEM_HEAD_DIM)),
            pl.BlockSpec((MEM_LEN, MEM_WIDTH), lambda i: (i // tiles, 0)),
            pl.BlockSpec((MEM_LEN, MEM_WIDTH), lambda i: (i // tiles, 1)),
            const((MEM_WIDTH, d)),
            const((1, d)),
            slab((d, FFN_DIM)),
            slab((FFN_DIM, d)),
            srow, smem, smem,
        ],
        out_specs=[pl.BlockSpec((tm, d), lambda i: (i, 0)), srow],
        out_shape=[jax.ShapeDtypeStruct((n, d), F32), jax.ShapeDtypeStruct((steps, rs, MEM_WIDTH), F32)],
        compiler_params=pltpu.CompilerParams(dimension_semantics=("arbitrary",), vmem_limit_bytes=VMEM_LIMIT_BYTES),
        name="post_prompt",
    )(x, a, wout, gx, wq, qn, mkv, mkv, wo, gf, w1, w2, sq.reshape(steps, rs, MEM_WIDTH), smk, smv)
    return y, att.reshape(bs, MEM_WIDTH)


def _res_proj_kernel(x_ref, a_ref, w_ref, o_ref):
    o_ref[...] = x_ref[...] + _mm(a_ref[...], w_ref[...])


def res_proj(x, a, w):
    n, d = x.shape
    return pl.pallas_call(
        _res_proj_kernel,
        out_shape=jax.ShapeDtypeStruct((n, d), F32),
        compiler_params=pltpu.CompilerParams(vmem_limit_bytes=VMEM_LIMIT_BYTES),
        name="res_proj",
    )(x, a, w)


def _res_proj_ffn_kernel(x_ref, a_ref, w_ref, gf_ref, w1_ref, w2_ref, o_ref):
    x = x_ref[...] + _mm(a_ref[...], w_ref[...])
    o_ref[...] = _ffn(x, gf_ref, w1_ref, w2_ref)


def res_proj_ffn(x, a, w, gf, w1, w2, *, layer):
    n, d = x.shape
    full = lambda t: pl.BlockSpec(t.shape, lambda i: (0, 0))
    slab = lambda shape: pl.BlockSpec((None,) + shape, lambda i: (layer, 0, 0), pipeline_mode=pl.Buffered(1))
    return pl.pallas_call(
        _res_proj_ffn_kernel,
        grid=(1,),
        in_specs=[full(x), full(a), full(w), full(gf), slab((d, FFN_DIM)), slab((FFN_DIM, d))],
        out_specs=pl.BlockSpec((n, d), lambda i: (0, 0)),
        out_shape=jax.ShapeDtypeStruct((n, d), F32),
        compiler_params=pltpu.CompilerParams(dimension_semantics=("arbitrary",), vmem_limit_bytes=VMEM_LIMIT_BYTES),
        name="res_proj_ffn",
    )(x, a, w, gf, w1, w2)


def _pair_rows(x, op):
    xb = jnp.broadcast_to(x, (SUBLANES, LANES))
    return op(xb, pltpu.roll(xb, SUBLANES // 2, 0))


def _proj_chain(x_ref, g_ref, w_ref, dst, width):
    group, _, d = x_ref.shape
    xn = _rms(x_ref[...].reshape(group * CHUNK, d), g_ref[...]).astype(BF16)
    yield
    cols = w_ref.shape[1]
    for c0 in range(0, cols, width):
        cs = min(width, cols - c0)
        r = jnp.dot(xn, w_ref[:, c0:c0 + cs], preferred_element_type=F32)
        for b in range(group):
            dst[b, :, c0:c0 + cs] = r[b * CHUNK:(b + 1) * CHUNK]
        yield


def _even_prompt_kernel(xn_ref, x0_ref, g_ref, w_ref, cos_ref, sinlo_ref, sinhi_ref, qn_ref, kn_ref, sink_ref, segm_ref,
                        triu_ref, gb_ref, onorm_ref,
                        y_ref, kc_ref, vc_ref, caug_ref, m_ref,
                        kprev, vprev, cst, mst, proj):
    n = pl.program_id(0)
    batch = xn_ref.shape[0]
    slot = lax.rem(n, 2)

    @pl.when(n == 0)
    def _():
        kprev[...] = jnp.zeros_like(kprev)
        vprev[...] = jnp.zeros_like(vprev)
        cst[...] = jnp.zeros_like(cst)
        mst[...] = jnp.zeros_like(mst)
        for _ in _proj_chain(x0_ref, g_ref, w_ref, proj.at[0], PROJ_COLS_PER_STAGE):
            pass

    p_ref = proj.at[slot]

    cos, sinlo, sinhi = cos_ref[...], sinlo_ref[...], sinhi_ref[...]
    lane = _iota((1, LANES), 1)
    low = lane < 64
    qi = _iota((CHUNK, 2 * CHUNK), 0)
    si = _iota((CHUNK, 2 * CHUNK), 1)
    valid = (si >= qi) & (si <= qi + CHUNK) & ((si >= CHUNK) | (n > 0))
    causal = _iota((CHUNK, CHUNK), 0) >= _iota((CHUNK, CHUNK), 1)
    ones_col = jnp.where(_iota((CHUNK, LANES), 1) == 0, 1.0, 0.0)
    row64 = _iota((CHUNK, 1), 0) < 64
    shared = [{} for _ in range(batch)]
    swa = [_swa_prompt_chunk(p_ref.at[b], y_ref.at[b], kprev.at[b], vprev.at[b], (cos, sinlo, sinhi), qn_ref, kn_ref,
                             sink_ref, segm_ref[...], low, valid, shared[b], part)
           for part in range(SWA_CHAINS) for b in range(batch)]
    mlstm = [_mlstm_prompt_chunk(p_ref.at[b], y_ref.at[b], cst.at[b], mst.at[b], triu_ref, gb_ref, onorm_ref, low,
                                 causal, ones_col, row64) for b in range(batch)]
    gemm = _proj_chain(xn_ref, g_ref, w_ref, proj.at[1 - slot], 2 * PROJ_COLS_PER_STAGE)
    new_kv = _lockstep(swa + mlstm + [gemm], every=[1] * ((SWA_CHAINS + 1) * batch) + [3])[:batch]

    @pl.when(n == pl.num_programs(0) - 1)
    def _():
        for b in range(batch):
            kc_ref[b], vc_ref[b] = new_kv[b]
        caug_ref[...] = cst[...]
        m_ref[...] = mst[...]


def _then(first, second):
    value = yield from first
    yield from second
    return value


def _lockstep(chains, every=None):
    every = every or [1] * len(chains)
    results = [None] * len(chains)
    live = list(range(len(chains)))
    rnd = 0
    while live:
        for i in list(live):
            if rnd % every[i]:
                continue
            try:
                next(chains[i])
            except StopIteration as stop:
                results[i] = stop.value
                live.remove(i)
        rnd += 1
    return results


def _swa_prompt_chunk(p_ref, y_ref, kprev, vprev, tabs, qn_ref, kn_ref, sink_ref, segm, low, valid, shared, part):
    cos, sinlo, sinhi = tabs
    if part == 0:
        tiles = [p_ref[:, E_SK:E_SK + LANES]] + [p_ref[:, E_SQ + j * LANES:E_SQ + (j + 1) * LANES]
                                                 for j in range(SW_HEADS // 2)]
        gains = [kn_ref[...]] + [qn_ref[:, j * LANES:(j + 1) * LANES] for j in range(SW_HEADS // 2)]
        ms = _mm_exact_rhs(jnp.concatenate([t * t for t in tiles], axis=0), segm)
        yield
        normed = [_rope16(t * lax.rsqrt(ms[i * CHUNK:(i + 1) * CHUNK] + EPS) * g, cos, sinlo, sinhi)
                  for i, (t, g) in enumerate(zip(tiles, gains))]
        normed = normed[:1] + [q * (SW_HEAD_DIM ** -0.5) for q in normed[1:]]
        k = normed[0]
        v = p_ref[:, E_SV:E_SV + LANES]
        kk = jnp.concatenate([kprev[...], k], axis=0)
        vv = jnp.concatenate([vprev[...], v], axis=0)
        kk_sw = pltpu.roll(kk, 64, 1)
        vv_sw = pltpu.roll(vv, 64, 1)
        shared["kvar"] = {(0, 0): jnp.where(low, kk, 0.0), (0, 1): jnp.where(low, 0.0, kk_sw),
                          (1, 0): jnp.where(low, kk_sw, 0.0), (1, 1): jnp.where(low, 0.0, kk)}
        shared["vvar"] = {(0, 0): vv, (0, 1): vv_sw, (1, 0): vv_sw, (1, 1): vv}
        shared["normed"] = normed
        yield
    else:
        yield
        yield
    normed, kvar, vvar = shared["normed"], shared["kvar"], shared["vvar"]
    per_chain = SW_HEADS // 2 // SWA_CHAINS
    for j in range(part * per_chain, (part + 1) * per_chain):
        qb = normed[1 + j]
        halves = []
        for pos in range(2):
            h = 2 * j + pos
            kv = h // (SW_HEADS // SW_KV_HEADS)
            s = jnp.where(valid, _mm_nt(qb, kvar[(kv, pos)]), NEG)
            yield
            sink = sink_ref[h]
            m = jnp.maximum(jnp.max(s, axis=-1, keepdims=True), sink)
            pr = jnp.exp(s - m)
            yield
            pr = pr / (jnp.sum(pr, axis=-1, keepdims=True) + jnp.exp(sink - m))
            halves.append(_mm(pr, vvar[(kv, pos)]))
            yield
        y_ref[:, j * LANES:(j + 1) * LANES] = jnp.where(low, halves[0], halves[1]).astype(y_ref.dtype)
    if part == 0:
        kprev[...] = k
        vprev[...] = v
        return k, v


def _mlstm_prompt_chunk(p_ref, y_ref, cst, mst, triu_ref, gb_ref, onorm_ref, low, causal, ones_col, row64):
    gt = p_ref[:, E_GATE:E_GATE + LANES].T
    gi = gt[0:SUBLANES] + gb_ref[0:SUBLANES]
    fl = _log_sigmoid(gt[GATE_F_LANE:GATE_F_LANE + SUBLANES] + gb_ref[SUBLANES:2 * SUBLANES])
    yield
    fcum = _mm_exact_rhs(fl, triu_ref[...])
    dd = gi - fcum
    mprev = mst[...]
    yield
    mt = fcum + jnp.maximum(mprev, _cummax_lanes(dd))
    fend = jnp.broadcast_to(fcum[:, CHUNK - 1:CHUNK], fcum.shape)
    mend = jnp.broadcast_to(mt[:, CHUNK - 1:CHUNK], mt.shape)
    decay = jnp.exp(fend + mprev - mend)
    rows = jnp.concatenate([fcum - mt, jnp.exp(fcum + mprev - mt), jnp.exp(-mt), jnp.exp(fend - fcum + gi - mend),
                            jnp.zeros((CHUNK - 4 * SUBLANES, CHUNK), F32)], axis=0)
    yield
    cols = rows.T
    mst[...] = mend
    yield
    for j in range(ML_HEADS // 2):
        qblk = p_ref[:, E_MQ + j * LANES:E_MQ + (j + 1) * LANES]
        kblk = p_ref[:, E_MK + j * LANES:E_MK + (j + 1) * LANES] * (ML_QK_DIM ** -0.5)
        c_old = cst[j]
        qm2 = jnp.concatenate([jnp.where(low, qblk, 0.0), jnp.where(low, 0.0, qblk)], axis=0)
        s2 = _mm_nt(qm2, kblk)
        qc2 = _mm(qm2, c_old)
        yield
        kws, vaugs = [], []
        for pos in range(2):
            h = 2 * j + pos
            rs = slice(pos * CHUNK, (pos + 1) * CHUNK)
            logw = cols[:, h:h + 1] + dd[h:h + 1, :]
            w = jnp.exp(jnp.where(causal, logw, -jnp.inf))
            sqk = s2[rs] * w
            yield
            vh = p_ref[:, E_MV + h * LANES:E_MV + (h + 1) * LANES]
            cs = cols[:, SUBLANES + h:SUBLANES + h + 1]
            num = cs * qc2[rs, :ML_V_DIM] + _mm(sqk, vh)
            den = cs * qc2[rs, ML_V_DIM:ML_V_DIM + 1] + jnp.sum(sqk, axis=-1, keepdims=True)
            hh = num / jnp.maximum(jnp.abs(den), cols[:, 2 * SUBLANES + h:2 * SUBLANES + h + 1])
            yield
            hsl = slice(h * ML_V_DIM, (h + 1) * ML_V_DIM)
            hn = _rms(hh, onorm_ref[:, hsl])
            mo = p_ref[:, E_MO + h * ML_V_DIM:E_MO + (h + 1) * ML_V_DIM]
            y_ref[:, SW_HEADS * SW_HEAD_DIM + h * ML_V_DIM:SW_HEADS * SW_HEAD_DIM + (h + 1) * ML_V_DIM] = (
                hn * _sigmoid(mo)).astype(y_ref.dtype)
            msk = low if pos == 0 else jnp.logical_not(low)
            kws.append(jnp.where(msk, kblk, 0.0) * cols[:, 3 * SUBLANES + h:3 * SUBLANES + h + 1])
            vaugs.append(jnp.concatenate([vh, ones_col], axis=1))
            yield
        upd = _mm_tn(jnp.concatenate(kws, axis=0), jnp.concatenate(vaugs, axis=0))
        dec = jnp.where(row64, decay[2 * j:2 * j + 1, 0:1], decay[2 * j + 1:2 * j + 2, 0:1])
        cst[j] = dec * c_old + upd
        yield


def even_prompt(x, g, w, tabs, qn, kn, sinks, segm, triu, gb, onorm):
    batch, seq, d = x.shape
    nc = seq // CHUNK
    tab = pl.BlockSpec((CHUNK, LANES), lambda n: (n, 0))
    const = lambda shape: pl.BlockSpec(shape, lambda n: (0,) * len(shape))
    state_shapes = [(batch, CHUNK, LANES), (batch, CHUNK, LANES),
                    (batch, ML_HEADS // 2, 2 * ML_QK_DIM, 2 * ML_V_DIM), (batch, SUBLANES, LANES)]
    return pl.pallas_call(
        _even_prompt_kernel,
        grid=(nc,),
        in_specs=[
            pl.BlockSpec((batch, CHUNK, d), lambda n: (0, jnp.minimum(n + 1, nc - 1), 0)),
            pl.BlockSpec((batch, CHUNK, d), lambda n: (0, 0, 0)),
            const((1, d)), pl.BlockSpec((d, EVEN_COLS), lambda n: (0, 0), pipeline_mode=pl.Buffered(1)),
            tab, tab, tab,
            const((1, SW_HEADS * SW_HEAD_DIM)), const((1, LANES)),
            pl.BlockSpec(memory_space=pltpu.SMEM),
            const((LANES, LANES)), const((CHUNK, CHUNK)), const((2 * SUBLANES, LANES)),
            const((1, ML_HEADS * ML_V_DIM)),
        ],
        out_specs=[pl.BlockSpec((batch, CHUNK, D_MODEL), lambda n: (0, n, 0))] + [const(s) for s in state_shapes],
        out_shape=[jax.ShapeDtypeStruct((batch, seq, D_MODEL), BF16)]
        + [jax.ShapeDtypeStruct(s, F32) for s in state_shapes],
        scratch_shapes=[pltpu.VMEM(s, F32) for s in state_shapes] + [pltpu.VMEM((2, batch, CHUNK, EVEN_COLS), F32)],
        compiler_params=pltpu.CompilerParams(dimension_semantics=("arbitrary",), vmem_limit_bytes=VMEM_LIMIT_BYTES),
        name="even_prompt",
    )(x, x, g, w, *tabs, qn, kn, sinks, segm, triu, gb, onorm)


def _odd_prompt_kernel(xn_ref, x0_ref, g_ref, w_ref, cosr_ref, sinr_ref, convw_ref, convb_ref, dtb_ref, arow_ref,
                       drow_ref, snorm_ref, tril_ref, dmat_ref, qs_ref, ks_ref, cd_ref, rnorm_ref,
                       y_ref, conv_ref, s_ref, r_ref,
                       ext, sst, rst, proj):
    n = pl.program_id(1)
    batch = xn_ref.shape[0]
    slot = lax.rem(n, 2)

    @pl.when(n == 0)
    def _():
        ext[:, 0:SUBLANES] = jnp.zeros((batch, SUBLANES, SSD_CONV_DIM), F32)
        sst[...] = jnp.zeros_like(sst)
        rst[...] = jnp.zeros_like(rst)
        for _ in _proj_chain(x0_ref, g_ref, w_ref, proj.at[0], PROJ_COLS_PER_STAGE):
            pass

    p_ref = proj.at[slot]

    lane = _iota((1, LANES), 1)
    low = lane < 64
    row64 = _iota((CHUNK, 1), 0) < 64
    causal = _iota((CHUNK, CHUNK), 0) >= _iota((CHUNK, CHUNK), 1)
    chains = [
        _then(_ssd_prompt_chunk(p_ref.at[b], y_ref.at[b], ext.at[b], sst.at[b], convw_ref, convb_ref, dtb_ref,
                                arow_ref, drow_ref, snorm_ref, tril_ref, low, row64, causal),
              _ret_prompt_chunk(p_ref.at[b], y_ref.at[b], rst.at[b], cosr_ref, sinr_ref, dmat_ref, qs_ref, ks_ref,
                                cd_ref, rnorm_ref))
        for b in range(batch)]
    gemm = _proj_chain(xn_ref, g_ref, w_ref, proj.at[1 - slot], PROJ_COLS_PER_STAGE)
    tails = _lockstep(chains + [gemm], every=[1] * batch + [2])[:batch]

    @pl.when(n == pl.num_programs(1) - 1)
    def _():
        for b in range(batch):
            conv_ref[b] = tails[b]
        s_ref[...] = sst[...]
        r_ref[...] = rst[...]


def _ssd_prompt_chunk(p_ref, y_ref, ext, sst, convw_ref, convb_ref, dtb_ref, arow_ref, drow_ref, snorm_ref, tril_ref,
                      low, row64, causal):
    tail = SUBLANES
    ext[tail:tail + CHUNK] = p_ref[:, O_XBC:O_XBC + SSD_CONV_DIM]
    yield
    xe = ext[...]
    acc = None
    for jj in range(SSD_CONV):
        shift = SSD_CONV - 1 - jj
        tap = (pltpu.roll(xe, shift, 0) if shift else xe)[tail:tail + CHUNK] * convw_ref[jj:jj + 1]
        acc = tap if acc is None else acc + tap
    xact = _silu(acc + convb_ref[...])
    new_tail = ext[CHUNK:CHUNK + tail]
    ext[0:tail] = new_tail
    yield

    dt = _softplus(p_ref[:, O_DT:O_DT + LANES] + dtb_ref[...])
    cum = _mm_exact_lhs(tril_ref[...], dt * arow_ref[...])
    yield
    cum_t = cum.T
    dt_t = dt.T
    ecum = jnp.exp(cum)
    cend = cum[CHUNK - 1:CHUNK, :]
    wend = jnp.exp(cend - cum) * dt
    eend = jnp.exp(cend)
    yield
    pairs_per_group = SSD_HEADS // SSD_GROUPS // 2
    for g in range(SSD_GROUPS):
        ys = []
        bc = xact[:, SSD_INNER + g * SSD_STATE:SSD_INNER + (g + 1) * SSD_STATE]
        cc = xact[:, SSD_INNER + (SSD_GROUPS + g) * SSD_STATE:SSD_INNER + (SSD_GROUPS + g + 1) * SSD_STATE]
        cb = _mm_nt(cc, bc)
        yield
        for jg in range(pairs_per_group):
            j = g * pairs_per_group + jg
            ha, hb = 2 * j, 2 * j + 1
            xp = xact[:, j * LANES:(j + 1) * LANES]
            s_old = sst[j]
            y = jnp.where(low, ecum[:, ha:ha + 1], ecum[:, hb:hb + 1]) * _mm_nt(cc, s_old)
            yield
            wmats = []
            for h in (ha, hb):
                seg = cum[:, h:h + 1] - cum_t[h:h + 1, :]
                wmats.append(cb * jnp.exp(jnp.where(causal, seg, -jnp.inf)) * dt_t[h:h + 1, :])
                yield
            y = y + _mm(jnp.concatenate(wmats, axis=1),
                        jnp.concatenate([jnp.where(low, xp, 0.0), jnp.where(low, 0.0, xp)], axis=0))
            yield
            xw = xp * jnp.where(low, wend[:, ha:ha + 1], wend[:, hb:hb + 1])
            sst[j] = jnp.where(row64, eend[:, ha:ha + 1], eend[:, hb:hb + 1]) * s_old + _mm_tn(xw, bc)
            ys.append(y)
            yield
        gs = slice(g * SSD_INNER // SSD_GROUPS, (g + 1) * SSD_INNER // SSD_GROUPS)
        yg = jnp.concatenate(ys, axis=1)
        yg = (yg + drow_ref[:, gs] * xact[:, gs]) * _silu(p_ref[:, O_Z + gs.start:O_Z + gs.stop])
        y_ref[:, gs] = _rms(yg, snorm_ref[:, gs]).astype(y_ref.dtype)
        yield
    return new_tail


def _ret_prompt_chunk(p_ref, y_ref, rst, cosr_ref, sinr_ref, dmat_ref, qs_ref, ks_ref, cd_ref, rnorm_ref):
    cosr, sinr = cosr_ref[...], sinr_ref[...]
    for h in range(RET_HEADS):
        hs = h * LANES
        q = _rope128(p_ref[:, O_RQ + hs:O_RQ + hs + LANES], cosr, sinr)
        k = _rope128(p_ref[:, O_RK + hs:O_RK + hs + LANES], cosr, sinr) * (RET_QK_DIM ** -0.5)
        v = p_ref[:, O_RV + hs:O_RV + hs + LANES]
        yield
        r_old = rst[h]
        o = _mm(_mm_nt(q, k) * dmat_ref[h], v) + qs_ref[h] * _mm(q, r_old)
        yield
        rst[h] = cd_ref[h] * r_old + _mm_tn(k * ks_ref[h], v)
        xc = o - jnp.mean(o, axis=-1, keepdims=True)
        yn = xc * lax.rsqrt(jnp.mean(xc * xc, axis=-1, keepdims=True) + EPS) * rnorm_ref[:, hs:hs + LANES]
        y_ref[:, SSD_INNER + hs:SSD_INNER + hs + LANES] = (
            yn * _silu(p_ref[:, O_RG + hs:O_RG + hs + LANES])).astype(y_ref.dtype)
        yield


def odd_prompt(x, g, w, tabs, convw, convb, dtb, arow, drow, snorm, tril, ret_consts, rnorm, *, group):
    batch, seq, d = x.shape
    nc = seq // CHUNK
    tab = pl.BlockSpec((CHUNK, LANES), lambda g, n: (n, 0))
    const = lambda shape: pl.BlockSpec(shape, lambda g, n: (0,) * len(shape))
    per_g = lambda shape: pl.BlockSpec((group,) + shape, lambda g, n: (g,) + (0,) * len(shape))
    hc = (RET_HEADS, CHUNK, LANES)
    ywidth = SSD_INNER + RET_HEADS * RET_V_DIM
    states = [(SUBLANES, SSD_CONV_DIM), (SSD_HEADS // 2, LANES, SSD_STATE), hc]
    return pl.pallas_call(
        _odd_prompt_kernel,
        grid=(batch // group, seq // CHUNK),
        in_specs=[
            pl.BlockSpec((group, CHUNK, d), lambda g, n: (g, jnp.minimum(n + 1, nc - 1), 0)),
            pl.BlockSpec((group, CHUNK, d), lambda g, n: (g, 0, 0)),
            const((1, d)), pl.BlockSpec((d, ODD_COLS), lambda g, n: (0, 0), pipeline_mode=pl.Buffered(1)),
            tab, tab,
            const((SSD_CONV, SSD_CONV_DIM)), const((1, SSD_CONV_DIM)), const((1, LANES)), const((1, LANES)),
            const((1, SSD_INNER)), const((1, SSD_INNER)), const((CHUNK, CHUNK)),
            const(hc), const(hc), const(hc), const(hc), const((1, RET_HEADS * RET_V_DIM)),
        ],
        out_specs=[pl.BlockSpec((group, CHUNK, ywidth), lambda g, n: (g, n, 0))] + [per_g(s) for s in states],
        out_shape=[jax.ShapeDtypeStruct((batch, seq, ywidth), BF16)]
        + [jax.ShapeDtypeStruct((batch,) + s, F32) for s in states],
        scratch_shapes=[pltpu.VMEM((group, SUBLANES + CHUNK, SSD_CONV_DIM), F32),
                        pltpu.VMEM((group,) + states[1], F32), pltpu.VMEM((group,) + states[2], F32),
                        pltpu.VMEM((2, group, CHUNK, ODD_COLS), F32)],
        compiler_params=pltpu.CompilerParams(dimension_semantics=("arbitrary", "arbitrary"),
                                             vmem_limit_bytes=VMEM_LIMIT_BYTES),
        name="odd_prompt",
    )(x, x, g, w, *tabs, convw, convb, dtb, arow, drow, snorm, tril, *ret_consts, rnorm)


def _lane_to_rows(g, offset):
    sel = _iota(g.shape, 1) == _iota(g.shape, 0) + offset
    return jnp.sum(jnp.where(sel, g, 0.0), axis=-1, keepdims=True)


def _block_rows(x, i, nblk, blk):
    row = _iota((SUBLANES, blk), 0)
    out = jnp.zeros((SUBLANES, blk), F32)
    for b in range(nblk):
        out = jnp.where(row == b, x[i:i + 1, b * blk:(b + 1) * blk], out)
    return out


def _even_sample_kernel(p_ref, bk_ref, bv_ref, c_ref, nrow_ref, mrow_ref, cos_ref, sinlo_ref, sinhi_ref, qn_ref,
                        kn_ref, sink_ref, gb_ref, onorm_ref,
                        y_ref, nk_ref, nv_ref, nc_ref, nn_ref, nm_ref):
    R = SUBLANES
    cos, sinlo, sinhi = cos_ref[...], sinlo_ref[...], sinhi_ref[...]
    row = _iota((R, LANES), 0)
    lane = _iota((R, LANES), 1)
    low = lane < 64
    group = SW_HEADS // SW_KV_HEADS
    scale = SW_HEAD_DIM ** -0.5
    sink = sink_ref[:, 0:1]
    last = _iota((CHUNK, LANES), 0) == CHUNK - 1

    k = _rope16(_seg_rms(p_ref[:, E_SK:E_SK + LANES], kn_ref[...], low), cos, sinlo, sinhi)
    v = p_ref[:, E_SV:E_SV + LANES]
    qb, qb_sw = [], []
    for j in range(SW_HEADS // 2):
        sl = slice(E_SQ + j * LANES, E_SQ + (j + 1) * LANES)
        qb.append(_rope16(_seg_rms(p_ref[:, sl], qn_ref[:, sl], low), cos, sinlo, sinhi))
        qb_sw.append(pltpu.roll(qb[j], 64, 1))

    g = p_ref[:, E_GATE:E_GATE + LANES]
    ic = g + gb_ref[0:1]
    fl = _log_sigmoid(pltpu.roll(g, LANES - GATE_F_LANE, 1) + gb_ref[1:2])
    mprev = mrow_ref[...]
    mt = jnp.maximum(fl + mprev, ic)
    w_all = jnp.exp(ic - mt)
    cs_all = jnp.exp(fl + mprev - mt)
    em_all = jnp.exp(-mt)
    nm_ref[...] = mt
    width = ML_HEADS * ML_QK_DIM
    own = _iota((R, width), 1) // ML_QK_DIM == _iota((R, width), 0)
    kscaled = p_ref[:, E_MK:E_MK + width] * (ML_QK_DIM ** -0.5)

    def one_row(i):
        qm = jnp.zeros((R, LANES), F32)
        for j in range(SW_HEADS // 2):
            for pos in range(2):
                h = 2 * j + pos
                kv = h // group
                src = (qb[j] if pos == kv else qb_sw[j])[i:i + 1]
                qm = jnp.where((row == h) & (low if kv == 0 else jnp.logical_not(low)), src, qm)
        bk, bv = bk_ref[i], bv_ref[i]
        ki, vi = k[i:i + 1], v[i:i + 1]
        s = _dg(qm, bk, 1, 1) * scale
        s_new = jnp.sum(qm * ki, axis=-1, keepdims=True) * scale
        yield
        m = jnp.maximum(jnp.maximum(jnp.max(s, axis=-1, keepdims=True), s_new), sink)
        pr = jnp.exp(s - m)
        p_new = jnp.exp(s_new - m)
        yield
        den = jnp.sum(pr, axis=-1, keepdims=True) + p_new + jnp.exp(sink - m)
        o = (_dg(pr, bv, 1, 0) + p_new * vi) / den
        yield
        o_sw = pltpu.roll(o, 64, 1)
        for j in range(SW_HEADS // 2):
            halves = []
            for pos in range(2):
                h = 2 * j + pos
                halves.append((o if pos == h // group else o_sw)[h:h + 1, :])
            y_ref[i:i + 1, j * LANES:(j + 1) * LANES] = jnp.where(low[0:1], halves[0], halves[1])
        nk_ref[i] = jnp.where(last, ki, pltpu.roll(bk, CHUNK - 1, 0))
        nv_ref[i] = jnp.where(last, vi, pltpu.roll(bv, CHUNK - 1, 0))
        yield

        w = _lane_to_rows(jnp.broadcast_to(w_all[i:i + 1], (R, LANES)), 0)
        cs = _lane_to_rows(jnp.broadcast_to(cs_all[i:i + 1], (R, LANES)), 0)
        em = _lane_to_rows(jnp.broadcast_to(em_all[i:i + 1], (R, LANES)), 0)
        qrows = jnp.where(own, p_ref[i:i + 1, E_MQ:E_MQ + width], 0.0)
        krows = jnp.where(own, kscaled[i:i + 1], 0.0)
        c_old = c_ref[i]
        qc = _dg(qrows, c_old, 1, 0)
        qn_dot = jnp.sum(qrows * nrow_ref[i:i + 1], axis=-1, keepdims=True)
        sqk = jnp.sum(qrows * krows, axis=-1, keepdims=True) * w
        yield
        v4 = _block_rows(p_ref[:, E_MV:E_MV + ML_HEADS * ML_V_DIM], i, ML_HEADS, ML_V_DIM)
        mo4 = _block_rows(p_ref[:, E_MO:E_MO + ML_HEADS * ML_V_DIM], i, ML_HEADS, ML_V_DIM)
        num = cs * qc + sqk * v4
        dn = cs * qn_dot + sqk
        hh = num / jnp.maximum(jnp.abs(dn), em)
        hn = _rms(hh, onorm_ref[...]) * _sigmoid(mo4)
        yield
        for h in range(ML_HEADS):
            c0 = SW_HEADS * SW_HEAD_DIM + h * ML_V_DIM
            y_ref[i:i + 1, c0:c0 + ML_V_DIM] = hn[h:h + 1]
        dec_col = jnp.concatenate(
            [jnp.broadcast_to(cs[h:h + 1, 0:1], (ML_QK_DIM, ML_V_DIM)) for h in range(ML_HEADS)], axis=0)
        nc_ref[i] = dec_col * c_old + _dg(krows * w, v4, 0, 0)
        dec_lanes = jnp.sum(jnp.where(own, cs, 0.0), axis=0, keepdims=True)
        nn_ref[i:i + 1] = dec_lanes * nrow_ref[i:i + 1] + jnp.sum(krows * w, axis=0, keepdims=True)
        yield

    _lockstep([one_row(i) for i in range(R)])


def even_sample(proj, bk, bv, c, nrow, mrow, tabs, qn, kn, sink_rows, gb, onorm_rows):
    b = proj.shape[0]
    R = SUBLANES
    width = ML_HEADS * ML_QK_DIM
    rows = lambda w: pl.BlockSpec((R, w), lambda i: (i, 0))
    per_b = lambda shape: pl.BlockSpec((R,) + shape, lambda i: (i,) + (0,) * len(shape))
    const = lambda shape: pl.BlockSpec(shape, lambda i: (0,) * len(shape))
    tab = const((R, LANES))
    return pl.pallas_call(
        _even_sample_kernel,
        grid=(b // R,),
        in_specs=[
            rows(EVEN_COLS), per_b((CHUNK, LANES)), per_b((CHUNK, LANES)), per_b((width, ML_V_DIM)),
            rows(width), rows(LANES), tab, tab, tab,
            const((1, SW_HEADS * SW_HEAD_DIM)), const((1, LANES)), const((R, LANES)),
            const((2, LANES)), const((R, ML_V_DIM)),
        ],
        out_specs=[
            rows(D_MODEL), per_b((CHUNK, LANES)), per_b((CHUNK, LANES)), per_b((width, ML_V_DIM)),
            rows(width), rows(LANES),
        ],
        out_shape=[
            jax.ShapeDtypeStruct((b, D_MODEL), F32),
            jax.ShapeDtypeStruct((b, CHUNK, LANES), F32), jax.ShapeDtypeStruct((b, CHUNK, LANES), F32),
            jax.ShapeDtypeStruct((b, width, ML_V_DIM), F32), jax.ShapeDtypeStruct((b, width), F32),
            jax.ShapeDtypeStruct((b, LANES), F32),
        ],
        compiler_params=pltpu.CompilerParams(dimension_semantics=("arbitrary",), vmem_limit_bytes=VMEM_LIMIT_BYTES),
        name="even_sample",
    )(proj, bk, bv, c, nrow, mrow, *tabs, qn, kn, sink_rows, gb, onorm_rows)


def _odd_sample_kernel(p_ref, cb_ref, s_ref, r_ref, cosr_ref, sinr_ref, convw_ref, convb_ref, dtb_ref, arow_ref,
                       drow_ref, snorm_ref, spread_ref, rdec_ref, rnorm_ref,
                       y_ref, ncb_ref, ns_ref, nr_ref):
    R = SUBLANES
    xbc = p_ref[:, O_XBC:O_XBC + SSD_CONV_DIM]
    acc = cb_ref[0] * convw_ref[0:1]
    for jj in range(1, SSD_CONV - 1):
        acc = acc + cb_ref[jj] * convw_ref[jj:jj + 1]
    acc = acc + xbc * convw_ref[SSD_CONV - 1:SSD_CONV]
    xact = _silu(acc + convb_ref[...])
    for jj in range(SSD_CONV - 2):
        ncb_ref[jj] = cb_ref[jj + 1]
    ncb_ref[SSD_CONV - 2] = xbc

    dt = _softplus(p_ref[:, O_DT:O_DT + LANES] + dtb_ref[...])
    dec = jnp.exp(dt * arow_ref[...])
    xs = xact[:, 0:SSD_INNER]
    xdt = xs * _mm_exact_rhs(dt, spread_ref[...])
    gown = _iota((R, SSD_INNER), 1) // (SSD_INNER // SSD_GROUPS) == _iota((R, SSD_INNER), 0)
    bpart = xact[:, SSD_INNER:SSD_INNER + SSD_GROUPS * SSD_STATE]
    cpart = xact[:, SSD_INNER + SSD_GROUPS * SSD_STATE:SSD_CONV_DIM]

    cosr, sinr = cosr_ref[...], sinr_ref[...]
    width = RET_HEADS * RET_QK_DIM
    q4 = jnp.concatenate([_rope128(p_ref[:, O_RQ + h * LANES:O_RQ + (h + 1) * LANES], cosr, sinr)
                          for h in range(RET_HEADS)], axis=1)
    k4 = jnp.concatenate([_rope128(p_ref[:, O_RK + h * LANES:O_RK + (h + 1) * LANES], cosr, sinr)
                          for h in range(RET_HEADS)], axis=1) * (RET_QK_DIM ** -0.5)
    own = _iota((R, width), 1) // RET_QK_DIM == _iota((R, width), 0)
    gam = rdec_ref[:, 0:1]
    gam_col = jnp.concatenate(
        [jnp.broadcast_to(rdec_ref[h:h + 1, :], (RET_QK_DIM, RET_V_DIM)) for h in range(RET_HEADS)], axis=0)

    def one_row(i):
        brows = _block_rows(bpart, i, SSD_GROUPS, SSD_STATE)
        crows = _block_rows(cpart, i, SSD_GROUPS, SSD_STATE)
        xw = jnp.where(gown, xdt[i:i + 1], 0.0)
        dec_col = jnp.concatenate(
            [jnp.broadcast_to(dec[i:i + 1, h:h + 1], (SSD_HEAD_DIM, SSD_STATE)) for h in range(SSD_HEADS)], axis=0)
        yield
        s_new = dec_col * s_ref[i] + _dg(xw, brows, 0, 0)
        ns_ref[i] = s_new
        yield
        yrows = _dg(crows, s_new, 1, 1)
        y_row = jnp.sum(jnp.where(gown, yrows, 0.0), axis=0, keepdims=True)
        yield

        qrows = jnp.where(own, q4[i:i + 1], 0.0)
        krows = jnp.where(own, k4[i:i + 1], 0.0)
        v4 = _block_rows(p_ref[:, O_RV:O_RV + RET_HEADS * RET_V_DIM], i, RET_HEADS, RET_V_DIM)
        g4 = _block_rows(p_ref[:, O_RG:O_RG + RET_HEADS * RET_V_DIM], i, RET_HEADS, RET_V_DIM)
        r_old = r_ref[i]
        att = jnp.sum(qrows * krows, axis=-1, keepdims=True)
        o = att * v4 + gam * _dg(qrows, r_old, 1, 0)
        yield
        nr_ref[i] = gam_col * r_old + _dg(krows, v4, 0, 0)
        yield
        xc = o - jnp.mean(o, axis=-1, keepdims=True)
        yn = xc * lax.rsqrt(jnp.mean(xc * xc, axis=-1, keepdims=True) + EPS) * rnorm_ref[...] * _silu(g4)
        for h in range(RET_HEADS):
            y_ref[i:i + 1, SSD_INNER + h * RET_V_DIM:SSD_INNER + (h + 1) * RET_V_DIM] = yn[h:h + 1]
        return y_row

    ys = _lockstep([one_row(i) for i in range(R)])
    y = (jnp.concatenate(ys, axis=0) + drow_ref[...] * xs) * _silu(p_ref[:, O_Z:O_Z + SSD_INNER])
    gw = SSD_INNER // SSD_GROUPS
    for g in range(SSD_GROUPS):
        y_ref[:, g * gw:(g + 1) * gw] = _rms(y[:, g * gw:(g + 1) * gw], snorm_ref[:, g * gw:(g + 1) * gw])


def odd_sample(proj, cbuf, s, r, tabs, convw, convb, dtb, arow, drow, snorm, spread, rdec_rows, rnorm_rows):
    b = proj.shape[0]
    R = SUBLANES
    rows = lambda w: pl.BlockSpec((R, w), lambda i: (i, 0))
    per_b = lambda shape: pl.BlockSpec((R,) + shape, lambda i: (i,) + (0,) * len(shape))
    const = lambda shape: pl.BlockSpec(shape, lambda i: (0,) * len(shape))
    tab = const((R, LANES))
    conv = pl.BlockSpec((SSD_CONV - 1, R, SSD_CONV_DIM), lambda i: (0, i, 0))
    ywidth = SSD_INNER + RET_HEADS * RET_V_DIM
    sshape = (SSD_HEADS * SSD_HEAD_DIM, SSD_STATE)
    rshape = (RET_HEADS * RET_QK_DIM, RET_V_DIM)
    return pl.pallas_call(
        _odd_sample_kernel,
        grid=(b // R,),
        in_specs=[
            rows(ODD_COLS), conv, per_b(sshape), per_b(rshape), tab, tab,
            const((SSD_CONV, SSD_CONV_DIM)), const((1, SSD_CONV_DIM)), const((1, LANES)), const((1, LANES)),
            const((1, SSD_INNER)), const((1, SSD_INNER)), const((LANES, SSD_INNER)),
            const((R, LANES)), const((R, RET_V_DIM)),
        ],
        out_specs=[rows(ywidth), conv, per_b(sshape), per_b(rshape)],
        out_shape=[
            jax.ShapeDtypeStruct((b, ywidth), F32),
            jax.ShapeDtypeStruct((SSD_CONV - 1, b, SSD_CONV_DIM), F32),
            jax.ShapeDtypeStruct((b,) + sshape, F32), jax.ShapeDtypeStruct((b,) + rshape, F32),
        ],
        compiler_params=pltpu.CompilerParams(dimension_semantics=("arbitrary",), vmem_limit_bytes=VMEM_LIMIT_BYTES),
        name="odd_sample",
    )(proj, cbuf, s, r, *tabs, convw, convb, dtb, arow, drow, snorm, spread, rdec_rows, rnorm_rows)


def _pad_cols(w, n):
    return jnp.pad(w, ((0, 0), (0, n - w.shape[1])))


def _even_w_in(w):
    w = w.astype(BF16)
    mi, mf = w[:, E_GATE:E_GATE + ML_HEADS], w[:, E_GATE + ML_HEADS:E_GATE + 2 * ML_HEADS]
    return jnp.concatenate([w[:, :E_GATE], _pad_cols(mi, GATE_F_LANE), _pad_cols(mf, LANES - GATE_F_LANE)], axis=1)


def _odd_w_in(w):
    w = w.astype(BF16)
    dt0 = O_XBC + SSD_CONV_DIM
    return jnp.concatenate([w[:, :dt0], w[:, dt0 + SSD_HEADS:], _pad_cols(w[:, dt0:dt0 + SSD_HEADS], LANES)], axis=1)


def _lane_angles(pos, rot_dim, theta, freq_of_lane):
    half = rot_dim // 2
    inv = jnp.power(jnp.float32(theta), -jnp.arange(half, dtype=F32) * (2.0 / rot_dim))
    return jnp.asarray(pos).astype(F32)[:, None] * inv[freq_of_lane][None, :]


def _rope16_tables(pos):
    half = SW_ROT_DIM // 2
    d = np.arange(LANES) % SW_HEAD_DIM
    ang = _lane_angles(pos, SW_ROT_DIM, ROPE_THETA, d % half)
    cos, sin = jnp.cos(ang), jnp.sin(ang)
    return (jnp.where(d < SW_ROT_DIM, cos, 1.0), jnp.where(d < half, -sin, 0.0),
            jnp.where((d >= half) & (d < SW_ROT_DIM), sin, 0.0))


def _rope128_tables(pos):
    half = RET_QK_DIM // 2
    lane = np.arange(LANES)
    ang = _lane_angles(pos, RET_QK_DIM, RET_ROPE_THETA, lane % half)
    return jnp.cos(ang), jnp.where(lane < half, -jnp.sin(ang), jnp.sin(ang))


def _ret_consts():
    L = CHUNK
    f = np.float32
    lg = np.log(f(1.0) - np.exp2(f(-5.0) - np.arange(RET_HEADS, dtype=f))).astype(f)
    idx = np.arange(L, dtype=f)
    diff = idx[:, None] - idx[None, :]
    with np.errstate(invalid="ignore"):
        dmat = np.exp(np.where(diff >= 0, diff[None] * lg[:, None, None], -np.inf)).astype(f)
    q_scale = np.exp((idx[None] + f(1.0)) * lg[:, None]).astype(f)
    k_scale = np.exp((f(L) - f(1.0) - idx[None]) * lg[:, None]).astype(f)
    chunk_decay = np.exp(f(L) * lg).astype(f)
    bc = lambda t: np.ascontiguousarray(np.broadcast_to(t[:, :, None], (RET_HEADS, L, LANES)))
    cd = np.ascontiguousarray(np.broadcast_to(chunk_decay[:, None, None], (RET_HEADS, L, LANES)))
    return dmat, bc(q_scale), bc(k_scale), cd, lg


def _rows8(t):
    return jnp.pad(t, ((0, SUBLANES - t.shape[0]), (0, 0)))


def _gate_bias_rows(gb):
    ib = jnp.broadcast_to(gb[:ML_HEADS, None], (ML_HEADS, LANES))
    fb = jnp.broadcast_to(gb[ML_HEADS:, None], (ML_HEADS, LANES))
    return jnp.concatenate([_rows8(ib), _rows8(fb)], axis=0)


def kernel(x_prompt, x_sample, cache_mem_k, cache_mem_v, cache_swa_k, cache_swa_v, state_mlstm_C, state_mlstm_n,
           state_mlstm_m, state_ssd_conv, state_ssd, state_ret, mem_prompt, norm_mix, norm_xattn, norm_mem, norm_ffn,
           even_w_in, mlstm_gate_bias, swa_q_norm, swa_k_norm, swa_sinks, mlstm_out_norm, even_w_out, odd_w_in,
           ssd_conv_w, ssd_conv_b, ssd_dt_bias, ssd_a_log, ssd_d, ssd_norm, ret_norm, odd_w_out, mem_wq, mem_wk,
           mem_wv, mem_q_norm, mem_k_norm, mem_wo, ffn_w1, ffn_w2):
    bp, seq, d = x_prompt.shape
    bs = x_sample.shape[0]
    depth = norm_mix.shape[0]
    tm = 512

    pos_p = np.arange(seq, dtype=np.int32)
    pos_s = np.full((SUBLANES,), PAST_LEN, dtype=np.int32)
    tab16_p, tab16_s = _rope16_tables(pos_p), _rope16_tables(pos_s)
    tab128_p, tab128_s = _rope128_tables(pos_p), _rope128_tables(pos_s)
    dmat, q_scale, k_scale, chunk_decay, lg = _ret_consts()
    rdec_rows = _rows8(jnp.asarray(np.broadcast_to(np.exp(lg)[:, None], (RET_HEADS, LANES))))
    ii = np.arange(CHUNK)
    triu = (ii[:, None] <= ii[None, :]).astype(np.float32)
    tril = (ii[:, None] >= ii[None, :]).astype(np.float32)
    jj = np.arange(LANES)
    segm = np.where(jj[:, None] // SW_HEAD_DIM == jj[None, :] // SW_HEAD_DIM, 1.0 / SW_HEAD_DIM, 0.0).astype(np.float32)
    spread = (jj[:, None] == np.arange(SSD_INNER)[None, :] // SSD_HEAD_DIM).astype(np.float32)
    row1 = lambda t: t.reshape(1, -1).astype(F32)
    pad_lanes = lambda t: jnp.pad(t.reshape(1, -1).astype(F32), ((0, 0), (0, LANES - t.shape[-1])))

    yp = x_prompt.reshape(bp * seq, d)
    ys = x_sample.reshape(bs, d)
    mem = mem_prompt.reshape(bp * MEM_LEN, d)
    cmk = cache_mem_k.reshape(depth, bs, MEM_LEN * MEM_HEADS, MEM_HEAD_DIM)
    cmv = cache_mem_v.reshape(depth, bs, MEM_LEN * MEM_HEADS, MEM_HEAD_DIM)
    w1s, w2s = ffn_w1.astype(BF16), ffn_w2.astype(BF16)
    p_mk, p_mv = [], []
    outs = {}
    for l in range(depth):
        g_mix = row1(norm_mix[l])
        if l % 2 == 0:
            e = l // 2
            w_in = _even_w_in(even_w_in[e])
            w_out = even_w_out[e].astype(BF16)
            qn = row1(jnp.tile(swa_q_norm[e], SW_HEADS))
            kn = row1(jnp.tile(swa_k_norm[e], SW_KV_HEADS))
            gb = _gate_bias_rows(mlstm_gate_bias[e].astype(F32))
            onorm = row1(mlstm_out_norm[e])
            sinks = swa_sinks[e].astype(F32)
            mix_p, kc, vc, caug, mm = even_prompt(yp.reshape(bp, seq, d), g_mix, w_in, tab16_p, qn, kn, sinks, segm,
                                                  triu, gb, onorm)
            mix_p = mix_p.reshape(bp * seq, -1)
            outs["p_swk"] = kc.reshape(1, bp, CHUNK, SW_KV_HEADS, SW_HEAD_DIM)
            outs["p_swv"] = vc.reshape(1, bp, CHUNK, SW_KV_HEADS, SW_HEAD_DIM)
            outs["p_c"] = caug[..., :ML_V_DIM].reshape(1, bp, ML_HEADS, ML_QK_DIM, ML_V_DIM)
            outs["p_n"] = caug[..., ML_V_DIM].reshape(1, bp, ML_HEADS, ML_QK_DIM)
            outs["p_m"] = mm[:, :ML_HEADS, 0].reshape(1, bp, ML_HEADS)

            proj_s = norm_proj(ys, g_mix, w_in, tm=bs)
            sink_rows = jnp.broadcast_to(sinks[:, None], (SW_HEADS, LANES))
            onorm_rows = _rows8(mlstm_out_norm[e].astype(F32).reshape(ML_HEADS, ML_V_DIM))
            gb_lanes = jnp.concatenate([pad_lanes(mlstm_gate_bias[e][:ML_HEADS]),
                                        pad_lanes(mlstm_gate_bias[e][ML_HEADS:])], axis=0)
            mix_s, nk, nv, ncst, nn, nm = even_sample(
                proj_s,
                cache_swa_k[e].reshape(bs, CHUNK, LANES), cache_swa_v[e].reshape(bs, CHUNK, LANES),
                state_mlstm_C[e].reshape(bs, ML_HEADS * ML_QK_DIM, ML_V_DIM),
                state_mlstm_n[e].reshape(bs, ML_HEADS * ML_QK_DIM),
                jnp.pad(state_mlstm_m[e], ((0, 0), (0, LANES - ML_HEADS))),
                tab16_s, qn, kn, sink_rows, gb_lanes, onorm_rows)
            outs["s_swk"] = nk.reshape(1, bs, CHUNK, SW_KV_HEADS, SW_HEAD_DIM)
            outs["s_swv"] = nv.reshape(1, bs, CHUNK, SW_KV_HEADS, SW_HEAD_DIM)
            outs["s_c"] = ncst.reshape(1, bs, ML_HEADS, ML_QK_DIM, ML_V_DIM)
            outs["s_n"] = nn.reshape(1, bs, ML_HEADS, ML_QK_DIM)
            outs["s_m"] = nm[:, :ML_HEADS].reshape(1, bs, ML_HEADS)
        else:
            o = l // 2
            w_in = _odd_w_in(odd_w_in[o])
            w_out = odd_w_out[o].astype(BF16)
            convw = ssd_conv_w[o].astype(F32)
            convb = row1(ssd_conv_b[o])
            dtb = pad_lanes(ssd_dt_bias[o])
            arow = pad_lanes(-jnp.exp(ssd_a_log[o].astype(F32)))
            drow = row1(jnp.repeat(ssd_d[o].astype(F32), SSD_HEAD_DIM))
            snorm = row1(ssd_norm[o])
            rnorm = row1(ret_norm[o])
            mix_p, ctail, sst, rst = odd_prompt(yp.reshape(bp, seq, d), g_mix, w_in, tab128_p, convw, convb, dtb, arow,
                                                drow, snorm, tril, (dmat, q_scale, k_scale, chunk_decay), rnorm,
                                                group=2)
            mix_p = mix_p.reshape(bp * seq, -1)
            outs["p_conv"] = ctail[:, SUBLANES - (SSD_CONV - 1):, :].reshape(1, bp, SSD_CONV - 1, SSD_CONV_DIM)
            outs["p_ssd"] = sst.reshape(1, bp, SSD_HEADS, SSD_HEAD_DIM, SSD_STATE)
            outs["p_ret"] = rst.reshape(1, bp, RET_HEADS, RET_QK_DIM, RET_V_DIM)

            proj_s = norm_proj(ys, g_mix, w_in, tm=bs)
            rnorm_rows = _rows8(ret_norm[o].astype(F32).reshape(RET_HEADS, RET_V_DIM))
            mix_s, ncb, ns, nr = odd_sample(
                proj_s, jnp.swapaxes(state_ssd_conv[o], 0, 1),
                state_ssd[o].reshape(bs, SSD_HEADS * SSD_HEAD_DIM, SSD_STATE),
                state_ret[o].reshape(bs, RET_HEADS * RET_QK_DIM, RET_V_DIM),
                tab128_s, convw, convb, dtb, arow, drow, snorm, spread, rdec_rows, rnorm_rows)
            outs["s_conv"] = jnp.swapaxes(ncb, 0, 1).reshape(1, bs, SSD_CONV - 1, SSD_CONV_DIM)
            outs["s_ssd"] = ns.reshape(1, bs, SSD_HEADS, SSD_HEAD_DIM, SSD_STATE)
            outs["s_ret"] = nr.reshape(1, bs, RET_HEADS, RET_QK_DIM, RET_V_DIM)

        wkv = jnp.concatenate([mem_wk[l], mem_wv[l]], axis=1).astype(BF16)
        qnorm = row1(mem_q_norm[l])
        mkv = norm_proj(mem, row1(norm_mem[l]), wkv, tm=tm, head_norm=row1(mem_k_norm[l]), head_norm_cols=MEM_WIDTH)
        p_mk.append(mkv[:, :MEM_WIDTH].reshape(bp, MEM_LEN, MEM_HEADS, MEM_HEAD_DIM))
        p_mv.append(mkv[:, MEM_WIDTH:].reshape(bp, MEM_LEN, MEM_HEADS, MEM_HEAD_DIM))
        wq, wo = mem_wq[l].astype(BF16), mem_wo[l].astype(BF16)
        gx, gf = row1(norm_xattn[l]), row1(norm_ffn[l])
        ys = res_proj(ys, mix_s, w_out)
        qs = norm_proj(ys, gx, wq, tm=bs, head_norm=qnorm, head_norm_cols=MEM_WIDTH)
        yp, att_s = post_prompt(yp, mix_p, w_out, gx, wq, qnorm, mkv, wo, gf, w1s, w2s, qs, cmk, cmv, layer=l,
                                rows_per_batch=seq, tm=tm)
        ys = res_proj_ffn(ys, att_s, wo, gf, w1s, w2s, layer=l)

    return (yp.reshape(bp, seq, d), ys.reshape(bs, 1, d),
            jnp.stack(p_mk), jnp.stack(p_mv), outs["p_swk"], outs["p_swv"], outs["p_c"], outs["p_n"], outs["p_m"],
            outs["p_conv"], outs["p_ssd"], outs["p_ret"],
            outs["s_swk"], outs["s_swv"], outs["s_c"], outs["s_n"], outs["s_m"],
            outs["s_conv"], outs["s_ssd"], outs["s_ret"])
```

```python
import functools
import math

import jax
import jax.numpy as jnp
import numpy as np
from jax import lax
from jax.experimental import pallas as pl
from jax.experimental.pallas import tpu as pltpu

F32 = jnp.float32
BF16 = jnp.bfloat16

D_MODEL = 1024
PAST_LEN = 8192
EPS = 1e-6
CHUNK = 128
NEG = -1e30

SW_HEADS, SW_KV_HEADS, SW_HEAD_DIM, SW_ROT_DIM = 8, 2, 64, 16
ROPE_THETA = 500000.0
ML_HEADS, ML_QK_DIM, ML_V_DIM = 4, 64, 128
SSD_HEADS, SSD_HEAD_DIM, SSD_GROUPS, SSD_STATE, SSD_CONV = 16, 64, 2, 128, 4
SSD_INNER = SSD_HEADS * SSD_HEAD_DIM
SSD_CONV_DIM = SSD_INNER + 2 * SSD_GROUPS * SSD_STATE
RET_HEADS, RET_QK_DIM, RET_V_DIM = 4, 128, 128
RET_ROPE_THETA = 10000.0
MEM_LEN, MEM_HEADS, MEM_HEAD_DIM = 256, 4, 128
MEM_WIDTH = MEM_HEADS * MEM_HEAD_DIM
FFN_DIM = 4 * D_MODEL
FFN_CHUNK = 512
PROJ_COLS_PER_STAGE = 256
SWA_CHAINS = 2
PROMPT_ROW_TILE = 512
ODD_SEQS_PER_STEP = 2

LANES = 128
SUBLANES = 8
VMEM_LIMIT_BYTES = 56 * 1024 * 1024

E_SQ, E_SK, E_SV, E_MQ, E_MK, E_MV, E_MO, E_GATE, EVEN_COLS = 0, 512, 640, 768, 1024, 1280, 1792, 2304, 2432
GATE_F_LANE = 8
O_Z, O_XBC, O_RQ, O_RK, O_RV, O_RG, O_DT, ODD_COLS = 0, 1024, 2560, 3072, 3584, 4096, 4608, 4736


def _mm(a, b):
    return jnp.dot(a.astype(BF16), b.astype(BF16), preferred_element_type=F32)


def _mm_nt(a, b):
    return lax.dot_general(a.astype(BF16), b.astype(BF16), (((1,), (1,)), ((), ())), preferred_element_type=F32)


def _mm_tn(a, b):
    return lax.dot_general(a.astype(BF16), b.astype(BF16), (((0,), (0,)), ((), ())), preferred_element_type=F32)


def _dg(a, b, ca, cb):
    return lax.dot_general(a, b, (((ca,), (cb,)), ((), ())), preferred_element_type=F32)


def _split3(x):
    hi = x.astype(BF16).astype(F32)
    r1 = x - hi
    mid = r1.astype(BF16).astype(F32)
    lo = (r1 - mid).astype(BF16).astype(F32)
    return hi, mid, lo


def _mm_exact_rhs(x, e):
    hi, mid, lo = _split3(x)
    return _dg(hi, e, 1, 0) + _dg(mid, e, 1, 0) + _dg(lo, e, 1, 0)


def _mm_exact_lhs(e, x):
    hi, mid, lo = _split3(x)
    return _dg(e, hi, 1, 0) + _dg(e, mid, 1, 0) + _dg(e, lo, 1, 0)


def _rms(x, g):
    return x * lax.rsqrt(jnp.mean(x * x, axis=-1, keepdims=True) + EPS) * g


def _seg_rms(x, g, low):
    xx = x * x
    s_lo = jnp.sum(jnp.where(low, xx, 0.0), axis=-1, keepdims=True)
    s_hi = jnp.sum(jnp.where(low, 0.0, xx), axis=-1, keepdims=True)
    return x * lax.rsqrt(jnp.where(low, s_lo, s_hi) * (1.0 / SW_HEAD_DIM) + EPS) * g


def _sigmoid(x):
    return 0.5 * jnp.tanh(0.5 * x) + 0.5


def _silu(x):
    h = 0.5 * x
    return h * jnp.tanh(h) + h


def _softplus(x):
    return jnp.maximum(x, 0.0) + jnp.log1p(jnp.exp(-jnp.abs(x)))


def _log_sigmoid(x):
    return -_softplus(-x)


def _rope16(x, cos, sin_lo, sin_hi):
    return x * cos + pltpu.roll(x, LANES - 8, 1) * sin_lo + pltpu.roll(x, 8, 1) * sin_hi


def _rope128(x, cos, sin):
    return x * cos + pltpu.roll(x, 64, 1) * sin


def _iota(shape, dim):
    return lax.broadcasted_iota(jnp.int32, shape, dim)


def _cummax_lanes(x):
    lane = _iota(x.shape, 1)
    shift = 1
    while shift < x.shape[1]:
        x = jnp.maximum(x, jnp.where(lane >= shift, pltpu.roll(x, shift, 1), -jnp.inf))
        shift *= 2
    return x


def _norm_proj_kernel(x_ref, g_ref, w_ref, hn_ref, o_ref, *, chunks, head_norm_cols):
    xn = _rms(x_ref[...], g_ref[...]).astype(BF16)
    for c0, cs in chunks:
        r = jnp.dot(xn, w_ref[:, c0:c0 + cs], preferred_element_type=F32)
        if c0 < head_norm_cols:
            parts = [_rms(r[:, i:i + LANES], hn_ref[...]) for i in range(0, cs, LANES)]
            r = jnp.concatenate(parts, axis=1)
        o_ref[:, c0:c0 + cs] = r


def _col_chunks(n, width=512):
    return tuple((c, min(width, n - c)) for c in range(0, n, width))


def norm_proj(x, g, w, *, tm, head_norm=None, head_norm_cols=0):
    n, d = x.shape
    m = w.shape[1]
    if head_norm is None:
        head_norm = jnp.ones((1, LANES), F32)
    kern = functools.partial(_norm_proj_kernel, chunks=_col_chunks(m), head_norm_cols=head_norm_cols)
    return pl.pallas_call(
        kern,
        grid=(n // tm,),
        in_specs=[
            pl.BlockSpec((tm, d), lambda i: (i, 0)),
            pl.BlockSpec((1, d), lambda i: (0, 0)),
            pl.BlockSpec((d, m), lambda i: (0, 0), pipeline_mode=pl.Buffered(1)),
            pl.BlockSpec((1, LANES), lambda i: (0, 0)),
        ],
        out_specs=pl.BlockSpec((tm, m), lambda i: (i, 0)),
        out_shape=jax.ShapeDtypeStruct((n, m), F32),
        compiler_params=pltpu.CompilerParams(dimension_semantics=("arbitrary",), vmem_limit_bytes=VMEM_LIMIT_BYTES),
        name="norm_proj",
    )(x, g, w, head_norm)


def _ffn(x, g_ref, w1_ref, w2_ref):
    h = _rms(x, g_ref[...]).astype(BF16)
    acc = None
    for c in range(0, FFN_DIM, FFN_CHUNK):
        u = jnp.maximum(jnp.dot(h, w1_ref[:, c:c + FFN_CHUNK], preferred_element_type=F32), 0.0)
        t = jnp.dot((u * u).astype(BF16), w2_ref[c:c + FFN_CHUNK, :], preferred_element_type=F32)
        acc = t if acc is None else acc + t
    return x + acc


def _post_chain(x_ref, a_ref, wout_ref, gx_ref, wq_ref, qn_ref, mk_ref, mv_ref, wo_ref, gf_ref, w1_ref, w2_ref, o_ref):
    x = x_ref[...] + _mm(a_ref[...], wout_ref[...])
    yield
    q = jnp.dot(_rms(x, gx_ref[...]).astype(BF16), wq_ref[...], preferred_element_type=F32)
    yield
    outs = []
    for h in range(MEM_HEADS):
        sl = slice(h * MEM_HEAD_DIM, (h + 1) * MEM_HEAD_DIM)
        qh = _rms(q[:, sl], qn_ref[...])
        s = _mm_nt(qh, mk_ref[:, sl]) * (MEM_HEAD_DIM ** -0.5)
        p = jnp.exp(s - jnp.max(s, axis=-1, keepdims=True))
        p = p / jnp.sum(p, axis=-1, keepdims=True)
        outs.append(_mm(p, mv_ref[:, sl]))
        yield
    x = x + _mm(jnp.concatenate(outs, axis=1), wo_ref[...])
    h = _rms(x, gf_ref[...]).astype(BF16)
    yield
    acc = None
    for c in range(0, FFN_DIM, FFN_CHUNK):
        u = jnp.maximum(jnp.dot(h, w1_ref[:, c:c + FFN_CHUNK], preferred_element_type=F32), 0.0)
        t = jnp.dot((u * u).astype(BF16), w2_ref[c:c + FFN_CHUNK, :], preferred_element_type=F32)
        acc = t if acc is None else acc + t
        yield
    o_ref[...] = x + acc


def _xattn_row(q_row, k, v):
    row = _iota((SUBLANES, LANES), 0)
    groups = MEM_LEN * MEM_HEADS // SUBLANES
    q8 = jnp.zeros((SUBLANES, LANES), F32)
    for h in range(MEM_HEADS):
        q8 = jnp.where(row % MEM_HEADS == h, q_row[:, h * MEM_HEAD_DIM:(h + 1) * MEM_HEAD_DIM], q8)
    k3 = k.reshape(groups, SUBLANES, LANES)
    s = jnp.sum(k3 * q8[None], axis=-1, keepdims=True) * (MEM_HEAD_DIM ** -0.5)
    mx = _pair_rows(jnp.max(s, axis=0), jnp.maximum)
    p = jnp.exp(s - mx[None, :, 0:1])
    den = _pair_rows(jnp.sum(p, axis=0), jnp.add)
    o8 = _pair_rows(jnp.sum(p * v.reshape(groups, SUBLANES, LANES), axis=0), jnp.add) / den
    return jnp.concatenate([o8[h:h + 1] for h in range(MEM_HEADS)], axis=1)


def _xattn_chain(q_ref, mk_ref, mv_ref, o_ref):
    for i in range(q_ref.shape[0]):
        o_ref[i:i + 1, :] = _xattn_row(q_ref[i:i + 1, :], mk_ref[i], mv_ref[i])
        yield


def _post_prompt_kernel(x_ref, a_ref, wout_ref, gx_ref, wq_ref, qn_ref, mk_ref, mv_ref, wo_ref, gf_ref, w1_ref, w2_ref,
                        sq_ref, smk_ref, smv_ref,
                        o_ref, so_ref):
    _lockstep([_post_chain(x_ref, a_ref, wout_ref, gx_ref, wq_ref, qn_ref, mk_ref, mv_ref, wo_ref, gf_ref, w1_ref,
                           w2_ref, o_ref),
               _xattn_chain(sq_ref, smk_ref, smv_ref, so_ref)], every=[1, 3])


def post_prompt(x, a, wout, gx, wq, qn, mkv, wo, gf, w1, w2, sq, smk, smv, *, layer, rows_per_batch, tm):
    n, d = x.shape
    ka = a.shape[1]
    steps = n // tm
    bs = sq.shape[0]
    rs = bs // steps
    tiles = rows_per_batch // tm
    const = lambda shape: pl.BlockSpec(shape, lambda i: (0, 0), pipeline_mode=pl.Buffered(1))
    slab = lambda shape: pl.BlockSpec((None,) + shape, lambda i: (layer, 0, 0), pipeline_mode=pl.Buffered(1))
    smem = pl.BlockSpec((None, rs, MEM_LEN * MEM_HEADS, MEM_HEAD_DIM), lambda i: (layer, i, 0, 0))
    srow = pl.BlockSpec((None, rs, MEM_WIDTH), lambda i: (i, 0, 0))
    y, att = pl.pallas_call(
        _post_prompt_kernel,
        grid=(steps,),
        in_specs=[
            pl.BlockSpec((tm, d), lambda i: (i, 0)),
            pl.BlockSpec((tm, ka), lambda i: (i, 0)),
            const((ka, d)),
            const((1, d)),
            const((d, MEM_WIDTH)),
            const((1, MEM_HEAD_DIM)),
            pl.BlockSpec((MEM_LEN, MEM_WIDTH), lambda i: (i // tiles, 0)),
            pl.BlockSpec((MEM_LEN, MEM_WIDTH), lambda i: (i // tiles, 1)),
            const((MEM_WIDTH, d)),
            const((1, d)),
            slab((d, FFN_DIM)),
            slab((FFN_DIM, d)),
            srow, smem, smem,
        ],
        out_specs=[pl.BlockSpec((tm, d), lambda i: (i, 0)), srow],
        out_shape=[jax.ShapeDtypeStruct((n, d), F32), jax.ShapeDtypeStruct((steps, rs, MEM_WIDTH), F32)],
        compiler_params=pltpu.CompilerParams(dimension_semantics=("arbitrary",), vmem_limit_bytes=VMEM_LIMIT_BYTES),
        name="post_prompt",
    )(x, a, wout, gx, wq, qn, mkv, mkv, wo, gf, w1, w2, sq.reshape(steps, rs, MEM_WIDTH), smk, smv)
    return y, att.reshape(bs, MEM_WIDTH)


def _res_proj_kernel(x_ref, a_ref, w_ref, o_ref):
    o_ref[...] = x_ref[...] + _mm(a_ref[...], w_ref[...])


def res_proj(x, a, w):
    n, d = x.shape
    return pl.pallas_call(
        _res_proj_kernel,
        out_shape=jax.ShapeDtypeStruct((n, d), F32),
        compiler_params=pltpu.CompilerParams(vmem_limit_bytes=VMEM_LIMIT_BYTES),
        name="res_proj",
    )(x, a, w)


def _res_proj_ffn_kernel(x_ref, a_ref, w_ref, gf_ref, w1_ref, w2_ref, o_ref):
    x = x_ref[...] + _mm(a_ref[...], w_ref[...])
    o_ref[...] = _ffn(x, gf_ref, w1_ref, w2_ref)


def res_proj_ffn(x, a, w, gf, w1, w2, *, layer):
    n, d = x.shape
    full = lambda t: pl.BlockSpec(t.shape, lambda i: (0, 0))
    slab = lambda shape: pl.BlockSpec((None,) + shape, lambda i: (layer, 0, 0), pipeline_mode=pl.Buffered(1))
    return pl.pallas_call(
        _res_proj_ffn_kernel,
        grid=(1,),
        in_specs=[full(x), full(a), full(w), full(gf), slab((d, FFN_DIM)), slab((FFN_DIM, d))],
        out_specs=pl.BlockSpec((n, d), lambda i: (0, 0)),
        out_shape=jax.ShapeDtypeStruct((n, d), F32),
        compiler_params=pltpu.CompilerParams(dimension_semantics=("arbitrary",), vmem_limit_bytes=VMEM_LIMIT_BYTES),
        name="res_proj_ffn",
    )(x, a, w, gf, w1, w2)


def _pair_rows(x, op):
    xb = jnp.broadcast_to(x, (SUBLANES, LANES))
    return op(xb, pltpu.roll(xb, SUBLANES // 2, 0))


def _proj_chain(x_ref, g_ref, w_ref, dst, width):
    group, _, d = x_ref.shape
    xn = _rms(x_ref[...].reshape(group * CHUNK, d), g_ref[...]).astype(BF16)
    yield
    cols = w_ref.shape[1]
    for c0 in range(0, cols, width):
        cs = min(width, cols - c0)
        r = jnp.dot(xn, w_ref[:, c0:c0 + cs], preferred_element_type=F32)
        for b in range(group):
            dst[b, :, c0:c0 + cs] = r[b * CHUNK:(b + 1) * CHUNK]
        yield


def _even_prompt_kernel(xn_ref, x0_ref, g_ref, w_ref, cos_ref, sinlo_ref, sinhi_ref, qn_ref, kn_ref, sink_ref, segm_ref,
                        triu_ref, gb_ref, onorm_ref,
                        y_ref, kc_ref, vc_ref, caug_ref, m_ref,
                        kprev, vprev, cst, mst, proj):
    n = pl.program_id(0)
    batch = xn_ref.shape[0]
    slot = lax.rem(n, 2)

    @pl.when(n == 0)
    def _():
        kprev[...] = jnp.zeros_like(kprev)
        vprev[...] = jnp.zeros_like(vprev)
        cst[...] = jnp.zeros_like(cst)
        mst[...] = jnp.zeros_like(mst)
        for _ in _proj_chain(x0_ref, g_ref, w_ref, proj.at[0], PROJ_COLS_PER_STAGE):
            pass

    p_ref = proj.at[slot]

    cos, sinlo, sinhi = cos_ref[...], sinlo_ref[...], sinhi_ref[...]
    lane = _iota((1, LANES), 1)
    low = lane < 64
    qi = _iota((CHUNK, 2 * CHUNK), 0)
    si = _iota((CHUNK, 2 * CHUNK), 1)
    valid = (si >= qi) & (si <= qi + CHUNK) & ((si >= CHUNK) | (n > 0))
    causal = _iota((CHUNK, CHUNK), 0) >= _iota((CHUNK, CHUNK), 1)
    ones_col = jnp.where(_iota((CHUNK, LANES), 1) == 0, 1.0, 0.0)
    row64 = _iota((CHUNK, 1), 0) < 64
    shared = [{} for _ in range(batch)]
    swa = [_swa_prompt_chunk(p_ref.at[b], y_ref.at[b], kprev.at[b], vprev.at[b], (cos, sinlo, sinhi), qn_ref, kn_ref,
                             sink_ref, segm_ref[...], low, valid, shared[b], part)
           for part in range(SWA_CHAINS) for b in range(batch)]
    mlstm = [_mlstm_prompt_chunk(p_ref.at[b], y_ref.at[b], cst.at[b], mst.at[b], triu_ref, gb_ref, onorm_ref, low,
                                 causal, ones_col, row64) for b in range(batch)]
    gemm = _proj_chain(xn_ref, g_ref, w_ref, proj.at[1 - slot], 2 * PROJ_COLS_PER_STAGE)
    new_kv = _lockstep(swa + mlstm + [gemm], every=[1] * ((SWA_CHAINS + 1) * batch) + [3])[:batch]

    @pl.when(n == pl.num_programs(0) - 1)
    def _():
        for b in range(batch):
            kc_ref[b], vc_ref[b] = new_kv[b]
        caug_ref[...] = cst[...]
        m_ref[...] = mst[...]


def _then(first, second):
    value = yield from first
    yield from second
    return value


def _lockstep(chains, every=None):
    every = every or [1] * len(chains)
    results = [None] * len(chains)
    live = list(range(len(chains)))
    rnd = 0
    while live:
        for i in list(live):
            if rnd % every[i]:
                continue
            try:
                next(chains[i])
            except StopIteration as stop:
                results[i] = stop.value
                live.remove(i)
        rnd += 1
    return results


def _swa_prompt_chunk(p_ref, y_ref, kprev, vprev, tabs, qn_ref, kn_ref, sink_ref, segm, low, valid, shared, part):
    cos, sinlo, sinhi = tabs
    if part == 0:
        tiles = [p_ref[:, E_SK:E_SK + LANES]] + [p_ref[:, E_SQ + j * LANES:E_SQ + (j + 1) * LANES]
                                                 for j in range(SW_HEADS // 2)]
        gains = [kn_ref[...]] + [qn_ref[:, j * LANES:(j + 1) * LANES] for j in range(SW_HEADS // 2)]
        ms = _mm_exact_rhs(jnp.concatenate([t * t for t in tiles], axis=0), segm)
        yield
        normed = [_rope16(t * lax.rsqrt(ms[i * CHUNK:(i + 1) * CHUNK] + EPS) * g, cos, sinlo, sinhi)
                  for i, (t, g) in enumerate(zip(tiles, gains))]
        normed = normed[:1] + [q * (SW_HEAD_DIM ** -0.5) for q in normed[1:]]
        k = normed[0]
        v = p_ref[:, E_SV:E_SV + LANES]
        kk = jnp.concatenate([kprev[...], k], axis=0)
        vv = jnp.concatenate([vprev[...], v], axis=0)
        kk_sw = pltpu.roll(kk, 64, 1)
        vv_sw = pltpu.roll(vv, 64, 1)
        shared["kvar"] = {(0, 0): jnp.where(low, kk, 0.0), (0, 1): jnp.where(low, 0.0, kk_sw),
                          (1, 0): jnp.where(low, kk_sw, 0.0), (1, 1): jnp.where(low, 0.0, kk)}
        shared["vvar"] = {(0, 0): vv, (0, 1): vv_sw, (1, 0): vv_sw, (1, 1): vv}
        shared["normed"] = normed
        yield
    else:
        yield
        yield
    normed, kvar, vvar = shared["normed"], shared["kvar"], shared["vvar"]
    per_chain = SW_HEADS // 2 // SWA_CHAINS
    for j in range(part * per_chain, (part + 1) * per_chain):
        qb = normed[1 + j]
        halves = []
        for pos in range(2):
            h = 2 * j + pos
            kv = h // (SW_HEADS // SW_KV_HEADS)
            s = jnp.where(valid, _mm_nt(qb, kvar[(kv, pos)]), NEG)
            yield
            sink = sink_ref[h]
            m = jnp.maximum(jnp.max(s, axis=-1, keepdims=True), sink)
            pr = jnp.exp(s - m)
            yield
            pr = pr / (jnp.sum(pr, axis=-1, keepdims=True) + jnp.exp(sink - m))
            halves.append(_mm(pr, vvar[(kv, pos)]))
            yield
        y_ref[:, j * LANES:(j + 1) * LANES] = jnp.where(low, halves[0], halves[1]).astype(y_ref.dtype)
    if part == 0:
        kprev[...] = k
        vprev[...] = v
        return k, v


def _mlstm_prompt_chunk(p_ref, y_ref, cst, mst, triu_ref, gb_ref, onorm_ref, low, causal, ones_col, row64):
    gt = p_ref[:, E_GATE:E_GATE + LANES].T
    gi = gt[0:SUBLANES] + gb_ref[0:SUBLANES]
    fl = _log_sigmoid(gt[GATE_F_LANE:GATE_F_LANE + SUBLANES] + gb_ref[SUBLANES:2 * SUBLANES])
    yield
    fcum = _mm_exact_rhs(fl, triu_ref[...])
    dd = gi - fcum
    mprev = mst[...]
    yield
    mt = fcum + jnp.maximum(mprev, _cummax_lanes(dd))
    fend = jnp.broadcast_to(fcum[:, CHUNK - 1:CHUNK], fcum.shape)
    mend = jnp.broadcast_to(mt[:, CHUNK - 1:CHUNK], mt.shape)
    decay = jnp.exp(fend + mprev - mend)
    rows = jnp.concatenate([fcum - mt, jnp.exp(fcum + mprev - mt), jnp.exp(-mt), jnp.exp(fend - fcum + gi - mend),
                            jnp.zeros((CHUNK - 4 * SUBLANES, CHUNK), F32)], axis=0)
    yield
    cols = rows.T
    mst[...] = mend
    yield
    for j in range(ML_HEADS // 2):
        qblk = p_ref[:, E_MQ + j * LANES:E_MQ + (j + 1) * LANES]
        kblk = p_ref[:, E_MK + j * LANES:E_MK + (j + 1) * LANES] * (ML_QK_DIM ** -0.5)
        c_old = cst[j]
        qm2 = jnp.concatenate([jnp.where(low, qblk, 0.0), jnp.where(low, 0.0, qblk)], axis=0)
        s2 = _mm_nt(qm2, kblk)
        qc2 = _mm(qm2, c_old)
        yield
        kws, vaugs = [], []
        for pos in range(2):
            h = 2 * j + pos
            rs = slice(pos * CHUNK, (pos + 1) * CHUNK)
            logw = cols[:, h:h + 1] + dd[h:h + 1, :]
            w = jnp.exp(jnp.where(causal, logw, -jnp.inf))
            sqk = s2[rs] * w
            yield
            vh = p_ref[:, E_MV + h * LANES:E_MV + (h + 1) * LANES]
            cs = cols[:, SUBLANES + h:SUBLANES + h + 1]
            num = cs * qc2[rs, :ML_V_DIM] + _mm(sqk, vh)
            den = cs * qc2[rs, ML_V_DIM:ML_V_DIM + 1] + jnp.sum(sqk, axis=-1, keepdims=True)
            hh = num / jnp.maximum(jnp.abs(den), cols[:, 2 * SUBLANES + h:2 * SUBLANES + h + 1])
            yield
            hsl = slice(h * ML_V_DIM, (h + 1) * ML_V_DIM)
            hn = _rms(hh, onorm_ref[:, hsl])
            mo = p_ref[:, E_MO + h * ML_V_DIM:E_MO + (h + 1) * ML_V_DIM]
            y_ref[:, SW_HEADS * SW_HEAD_DIM + h * ML_V_DIM:SW_HEADS * SW_HEAD_DIM + (h + 1) * ML_V_DIM] = (
                hn * _sigmoid(mo)).astype(y_ref.dtype)
            msk = low if pos == 0 else jnp.logical_not(low)
            kws.append(jnp.where(msk, kblk, 0.0) * cols[:, 3 * SUBLANES + h:3 * SUBLANES + h + 1])
            vaugs.append(jnp.concatenate([vh, ones_col], axis=1))
            yield
        upd = _mm_tn(jnp.concatenate(kws, axis=0), jnp.concatenate(vaugs, axis=0))
        dec = jnp.where(row64, decay[2 * j:2 * j + 1, 0:1], decay[2 * j + 1:2 * j + 2, 0:1])
        cst[j] = dec * c_old + upd
        yield


def even_prompt(x, g, w, tabs, qn, kn, sinks, segm, triu, gb, onorm):
    batch, seq, d = x.shape
    nc = seq // CHUNK
    tab = pl.BlockSpec((CHUNK, LANES), lambda n: (n, 0))
    const = lambda shape: pl.BlockSpec(shape, lambda n: (0,) * len(shape))
    state_shapes = [(batch, CHUNK, LANES), (batch, CHUNK, LANES),
                    (batch, ML_HEADS // 2, 2 * ML_QK_DIM, 2 * ML_V_DIM), (batch, SUBLANES, LANES)]
    return pl.pallas_call(
        _even_prompt_kernel,
        grid=(nc,),
        in_specs=[
            pl.BlockSpec((batch, CHUNK, d), lambda n: (0, jnp.minimum(n + 1, nc - 1), 0)),
            pl.BlockSpec((batch, CHUNK, d), lambda n: (0, 0, 0)),
            const((1, d)), pl.BlockSpec((d, EVEN_COLS), lambda n: (0, 0), pipeline_mode=pl.Buffered(1)),
            tab, tab, tab,
            const((1, SW_HEADS * SW_HEAD_DIM)), const((1, LANES)),
            pl.BlockSpec(memory_space=pltpu.SMEM),
            const((LANES, LANES)), const((CHUNK, CHUNK)), const((2 * SUBLANES, LANES)),
            const((1, ML_HEADS * ML_V_DIM)),
        ],
        out_specs=[pl.BlockSpec((batch, CHUNK, D_MODEL), lambda n: (0, n, 0))] + [const(s) for s in state_shapes],
        out_shape=[jax.ShapeDtypeStruct((batch, seq, D_MODEL), BF16)]
        + [jax.ShapeDtypeStruct(s, F32) for s in state_shapes],
        scratch_shapes=[pltpu.VMEM(s, F32) for s in state_shapes] + [pltpu.VMEM((2, batch, CHUNK, EVEN_COLS), F32)],
        compiler_params=pltpu.CompilerParams(dimension_semantics=("arbitrary",), vmem_limit_bytes=VMEM_LIMIT_BYTES),
        name="even_prompt",
    )(x, x, g, w, *tabs, qn, kn, sinks, segm, triu, gb, onorm)


def _odd_prompt_kernel(xn_ref, x0_ref, g_ref, w_ref, cosr_ref, sinr_ref, convw_ref, convb_ref, dtb_ref, alog_ref,
                       drow_ref, snorm_ref, tril_ref, dmat_ref, qs_ref, ks_ref, cd_ref, rnorm_ref,
                       y_ref, conv_ref, s_ref, r_ref,
                       ext, sst, rst, proj):
    n = pl.program_id(1)
    batch = xn_ref.shape[0]
    slot = lax.rem(n, 2)

    @pl.when(n == 0)
    def _():
        ext[:, 0:SUBLANES] = jnp.zeros((batch, SUBLANES, SSD_CONV_DIM), F32)
        sst[...] = jnp.zeros_like(sst)
        rst[...] = jnp.zeros_like(rst)
        for _ in _proj_chain(x0_ref, g_ref, w_ref, proj.at[0], PROJ_COLS_PER_STAGE):
            pass

    p_ref = proj.at[slot]

    lane = _iota((1, LANES), 1)
    low = lane < 64
    row64 = _iota((CHUNK, 1), 0) < 64
    causal = _iota((CHUNK, CHUNK), 0) >= _iota((CHUNK, CHUNK), 1)
    chains = [
        _then(_ssd_prompt_chunk(p_ref.at[b], y_ref.at[b], ext.at[b], sst.at[b], convw_ref, convb_ref, dtb_ref,
                                alog_ref, drow_ref, snorm_ref, tril_ref, low, row64, causal),
              _ret_prompt_chunk(p_ref.at[b], y_ref.at[b], rst.at[b], cosr_ref, sinr_ref, dmat_ref, qs_ref, ks_ref,
                                cd_ref, rnorm_ref))
        for b in range(batch)]
    gemm = _proj_chain(xn_ref, g_ref, w_ref, proj.at[1 - slot], PROJ_COLS_PER_STAGE)
    tails = _lockstep(chains + [gemm], every=[1] * batch + [2])[:batch]

    @pl.when(n == pl.num_programs(1) - 1)
    def _():
        for b in range(batch):
            conv_ref[b] = tails[b]
        s_ref[...] = sst[...]
        r_ref[...] = rst[...]


def _ssd_prompt_chunk(p_ref, y_ref, ext, sst, convw_ref, convb_ref, dtb_ref, alog_ref, drow_ref, snorm_ref, tril_ref,
                      low, row64, causal):
    tail = SUBLANES
    ext[tail:tail + CHUNK] = p_ref[:, O_XBC:O_XBC + SSD_CONV_DIM]
    yield
    xe = ext[...]
    acc = None
    for jj in range(SSD_CONV):
        shift = SSD_CONV - 1 - jj
        tap = (pltpu.roll(xe, shift, 0) if shift else xe)[tail:tail + CHUNK] * convw_ref[jj:jj + 1]
        acc = tap if acc is None else acc + tap
    xact = _silu(acc + convb_ref[...])
    new_tail = ext[CHUNK:CHUNK + tail]
    ext[0:tail] = new_tail
    yield

    dt = _softplus(p_ref[:, O_DT:O_DT + LANES] + dtb_ref[...])
    cum = _mm_exact_lhs(tril_ref[...], dt * -jnp.exp(alog_ref[...]))
    yield
    cum_t = cum.T
    dt_t = dt.T
    ecum = jnp.exp(cum)
    cend = cum[CHUNK - 1:CHUNK, :]
    wend = jnp.exp(cend - cum) * dt
    eend = jnp.exp(cend)
    yield
    pairs_per_group = SSD_HEADS // SSD_GROUPS // 2
    for g in range(SSD_GROUPS):
        ys = []
        bc = xact[:, SSD_INNER + g * SSD_STATE:SSD_INNER + (g + 1) * SSD_STATE]
        cc = xact[:, SSD_INNER + (SSD_GROUPS + g) * SSD_STATE:SSD_INNER + (SSD_GROUPS + g + 1) * SSD_STATE]
        cb = _mm_nt(cc, bc)
        yield
        for jg in range(pairs_per_group):
            j = g * pairs_per_group + jg
            ha, hb = 2 * j, 2 * j + 1
            xp = xact[:, j * LANES:(j + 1) * LANES]
            s_old = sst[j]
            y = jnp.where(low, ecum[:, ha:ha + 1], ecum[:, hb:hb + 1]) * _mm_nt(cc, s_old)
            yield
            wmats = []
            for h in (ha, hb):
                seg = cum[:, h:h + 1] - cum_t[h:h + 1, :]
                wmats.append(cb * jnp.exp(jnp.where(causal, seg, -jnp.inf)) * dt_t[h:h + 1, :])
                yield
            y = y + _mm(jnp.concatenate(wmats, axis=1),
                        jnp.concatenate([jnp.where(low, xp, 0.0), jnp.where(low, 0.0, xp)], axis=0))
            yield
            xw = xp * jnp.where(low, wend[:, ha:ha + 1], wend[:, hb:hb + 1])
            sst[j] = jnp.where(row64, eend[:, ha:ha + 1], eend[:, hb:hb + 1]) * s_old + _mm_tn(xw, bc)
            ys.append(y)
            yield
        gs = slice(g * SSD_INNER // SSD_GROUPS, (g + 1) * SSD_INNER // SSD_GROUPS)
        yg = jnp.concatenate(ys, axis=1)
        yg = (yg + drow_ref[:, gs] * xact[:, gs]) * _silu(p_ref[:, O_Z + gs.start:O_Z + gs.stop])
        y_ref[:, gs] = _rms(yg, snorm_ref[:, gs]).astype(y_ref.dtype)
        yield
    return new_tail


def _ret_prompt_chunk(p_ref, y_ref, rst, cosr_ref, sinr_ref, dmat_ref, qs_ref, ks_ref, cd_ref, rnorm_ref):
    cosr, sinr = cosr_ref[...], sinr_ref[...]
    for h in range(RET_HEADS):
        hs = h * LANES
        q = _rope128(p_ref[:, O_RQ + hs:O_RQ + hs + LANES], cosr, sinr)
        k = _rope128(p_ref[:, O_RK + hs:O_RK + hs + LANES], cosr, sinr) * (RET_QK_DIM ** -0.5)
        v = p_ref[:, O_RV + hs:O_RV + hs + LANES]
        yield
        r_old = rst[h]
        o = _mm(_mm_nt(q, k) * dmat_ref[h], v) + qs_ref[h] * _mm(q, r_old)
        yield
        rst[h] = cd_ref[h] * r_old + _mm_tn(k * ks_ref[h], v)
        xc = o - jnp.mean(o, axis=-1, keepdims=True)
        yn = xc * lax.rsqrt(jnp.mean(xc * xc, axis=-1, keepdims=True) + EPS) * rnorm_ref[:, hs:hs + LANES]
        y_ref[:, SSD_INNER + hs:SSD_INNER + hs + LANES] = (
            yn * _silu(p_ref[:, O_RG + hs:O_RG + hs + LANES])).astype(y_ref.dtype)
        yield


def odd_prompt(x, g, w, tabs, convw, convb, dtb, alog, drow, snorm, tril, ret_consts, rnorm, *, group):
    batch, seq, d = x.shape
    nc = seq // CHUNK
    tab = pl.BlockSpec((CHUNK, LANES), lambda g, n: (n, 0))
    const = lambda shape: pl.BlockSpec(shape, lambda g, n: (0,) * len(shape))
    per_g = lambda shape: pl.BlockSpec((group,) + shape, lambda g, n: (g,) + (0,) * len(shape))
    hc = (RET_HEADS, CHUNK, LANES)
    ywidth = SSD_INNER + RET_HEADS * RET_V_DIM
    states = [(SUBLANES, SSD_CONV_DIM), (SSD_HEADS // 2, LANES, SSD_STATE), hc]
    return pl.pallas_call(
        _odd_prompt_kernel,
        grid=(batch // group, seq // CHUNK),
        in_specs=[
            pl.BlockSpec((group, CHUNK, d), lambda g, n: (g, jnp.minimum(n + 1, nc - 1), 0)),
            pl.BlockSpec((group, CHUNK, d), lambda g, n: (g, 0, 0)),
            const((1, d)), pl.BlockSpec((d, ODD_COLS), lambda g, n: (0, 0), pipeline_mode=pl.Buffered(1)),
            tab, tab,
            const((SSD_CONV, SSD_CONV_DIM)), const((1, SSD_CONV_DIM)), const((1, LANES)), const((1, LANES)),
            const((1, SSD_INNER)), const((1, SSD_INNER)), const((CHUNK, CHUNK)),
            const(hc), const(hc), const(hc), const(hc), const((1, RET_HEADS * RET_V_DIM)),
        ],
        out_specs=[pl.BlockSpec((group, CHUNK, ywidth), lambda g, n: (g, n, 0))] + [per_g(s) for s in states],
        out_shape=[jax.ShapeDtypeStruct((batch, seq, ywidth), BF16)]
        + [jax.ShapeDtypeStruct((batch,) + s, F32) for s in states],
        scratch_shapes=[pltpu.VMEM((group, SUBLANES + CHUNK, SSD_CONV_DIM), F32),
                        pltpu.VMEM((group,) + states[1], F32), pltpu.VMEM((group,) + states[2], F32),
                        pltpu.VMEM((2, group, CHUNK, ODD_COLS), F32)],
        compiler_params=pltpu.CompilerParams(dimension_semantics=("arbitrary", "arbitrary"),
                                             vmem_limit_bytes=VMEM_LIMIT_BYTES),
        name="odd_prompt",
    )(x, x, g, w, *tabs, convw, convb, dtb, alog, drow, snorm, tril, *ret_consts, rnorm)


def _lane_to_rows(g, offset):
    sel = _iota(g.shape, 1) == _iota(g.shape, 0) + offset
    return jnp.sum(jnp.where(sel, g, 0.0), axis=-1, keepdims=True)


def _block_rows(x, i, nblk, blk):
    row = _iota((SUBLANES, blk), 0)
    out = jnp.zeros((SUBLANES, blk), F32)
    for b in range(nblk):
        out = jnp.where(row == b, x[i:i + 1, b * blk:(b + 1) * blk], out)
    return out


def _even_sample_kernel(p_ref, bk_ref, bv_ref, c_ref, nrow_ref, mrow_ref, cos_ref, sinlo_ref, sinhi_ref, qn_ref,
                        kn_ref, sink_ref, gb_ref, onorm_ref,
                        y_ref, nk_ref, nv_ref, nc_ref, nn_ref, nm_ref):
    R = SUBLANES
    cos, sinlo, sinhi = cos_ref[...], sinlo_ref[...], sinhi_ref[...]
    row = _iota((R, LANES), 0)
    lane = _iota((R, LANES), 1)
    low = lane < 64
    group = SW_HEADS // SW_KV_HEADS
    scale = SW_HEAD_DIM ** -0.5
    sink = sink_ref[:, 0:1]
    last = _iota((CHUNK, LANES), 0) == CHUNK - 1

    k = _rope16(_seg_rms(p_ref[:, E_SK:E_SK + LANES], kn_ref[...], low), cos, sinlo, sinhi)
    v = p_ref[:, E_SV:E_SV + LANES]
    qb, qb_sw = [], []
    for j in range(SW_HEADS // 2):
        sl = slice(E_SQ + j * LANES, E_SQ + (j + 1) * LANES)
        qb.append(_rope16(_seg_rms(p_ref[:, sl], qn_ref[:, sl], low), cos, sinlo, sinhi))
        qb_sw.append(pltpu.roll(qb[j], 64, 1))

    g = p_ref[:, E_GATE:E_GATE + LANES]
    ic = g + gb_ref[0:1]
    fl = _log_sigmoid(pltpu.roll(g, LANES - GATE_F_LANE, 1) + gb_ref[1:2])
    mprev = mrow_ref[...]
    mt = jnp.maximum(fl + mprev, ic)
    w_all = jnp.exp(ic - mt)
    cs_all = jnp.exp(fl + mprev - mt)
    em_all = jnp.exp(-mt)
    nm_ref[...] = mt
    width = ML_HEADS * ML_QK_DIM
    own = _iota((R, width), 1) // ML_QK_DIM == _iota((R, width), 0)
    kscaled = p_ref[:, E_MK:E_MK + width] * (ML_QK_DIM ** -0.5)

    def one_row(i):
        qm = jnp.zeros((R, LANES), F32)
        for j in range(SW_HEADS // 2):
            for pos in range(2):
                h = 2 * j + pos
                kv = h // group
                src = (qb[j] if pos == kv else qb_sw[j])[i:i + 1]
                qm = jnp.where((row == h) & (low if kv == 0 else jnp.logical_not(low)), src, qm)
        bk, bv = bk_ref[i], bv_ref[i]
        ki, vi = k[i:i + 1], v[i:i + 1]
        s = _dg(qm, bk, 1, 1) * scale
        s_new = jnp.sum(qm * ki, axis=-1, keepdims=True) * scale
        yield
        m = jnp.maximum(jnp.maximum(jnp.max(s, axis=-1, keepdims=True), s_new), sink)
        pr = jnp.exp(s - m)
        p_new = jnp.exp(s_new - m)
        yield
        den = jnp.sum(pr, axis=-1, keepdims=True) + p_new + jnp.exp(sink - m)
        o = (_dg(pr, bv, 1, 0) + p_new * vi) / den
        yield
        o_sw = pltpu.roll(o, 64, 1)
        for j in range(SW_HEADS // 2):
            halves = []
            for pos in range(2):
                h = 2 * j + pos
                halves.append((o if pos == h // group else o_sw)[h:h + 1, :])
            y_ref[i:i + 1, j * LANES:(j + 1) * LANES] = jnp.where(low[0:1], halves[0], halves[1])
        nk_ref[i] = jnp.where(last, ki, pltpu.roll(bk, CHUNK - 1, 0))
        nv_ref[i] = jnp.where(last, vi, pltpu.roll(bv, CHUNK - 1, 0))
        yield

        w = _lane_to_rows(jnp.broadcast_to(w_all[i:i + 1], (R, LANES)), 0)
        cs = _lane_to_rows(jnp.broadcast_to(cs_all[i:i + 1], (R, LANES)), 0)
        em = _lane_to_rows(jnp.broadcast_to(em_all[i:i + 1], (R, LANES)), 0)
        qrows = jnp.where(own, p_ref[i:i + 1, E_MQ:E_MQ + width], 0.0)
        krows = jnp.where(own, kscaled[i:i + 1], 0.0)
        c_old = c_ref[i]
        qc = _dg(qrows, c_old, 1, 0)
        qn_dot = jnp.sum(qrows * nrow_ref[i:i + 1], axis=-1, keepdims=True)
        sqk = jnp.sum(qrows * krows, axis=-1, keepdims=True) * w
        yield
        v4 = _block_rows(p_ref[:, E_MV:E_MV + ML_HEADS * ML_V_DIM], i, ML_HEADS, ML_V_DIM)
        mo4 = _block_rows(p_ref[:, E_MO:E_MO + ML_HEADS * ML_V_DIM], i, ML_HEADS, ML_V_DIM)
        num = cs * qc + sqk * v4
        dn = cs * qn_dot + sqk
        hh = num / jnp.maximum(jnp.abs(dn), em)
        hn = _rms(hh, onorm_ref[...]) * _sigmoid(mo4)
        yield
        for h in range(ML_HEADS):
            c0 = SW_HEADS * SW_HEAD_DIM + h * ML_V_DIM
            y_ref[i:i + 1, c0:c0 + ML_V_DIM] = hn[h:h + 1]
        dec_col = jnp.concatenate(
            [jnp.broadcast_to(cs[h:h + 1, 0:1], (ML_QK_DIM, ML_V_DIM)) for h in range(ML_HEADS)], axis=0)
        nc_ref[i] = dec_col * c_old + _dg(krows * w, v4, 0, 0)
        dec_lanes = jnp.sum(jnp.where(own, cs, 0.0), axis=0, keepdims=True)
        nn_ref[i:i + 1] = dec_lanes * nrow_ref[i:i + 1] + jnp.sum(krows * w, axis=0, keepdims=True)
        yield

    _lockstep([one_row(i) for i in range(R)])


def even_sample(proj, bk, bv, c, nrow, mrow, tabs, qn, kn, sink_rows, gb, onorm_rows):
    b = proj.shape[0]
    R = SUBLANES
    width = ML_HEADS * ML_QK_DIM
    rows = lambda w: pl.BlockSpec((R, w), lambda i: (i, 0))
    per_b = lambda shape: pl.BlockSpec((R,) + shape, lambda i: (i,) + (0,) * len(shape))
    const = lambda shape: pl.BlockSpec(shape, lambda i: (0,) * len(shape))
    tab = const((R, LANES))
    return pl.pallas_call(
        _even_sample_kernel,
        grid=(b // R,),
        in_specs=[
            rows(EVEN_COLS), per_b((CHUNK, LANES)), per_b((CHUNK, LANES)), per_b((width, ML_V_DIM)),
            rows(width), rows(LANES), tab, tab, tab,
            const((1, SW_HEADS * SW_HEAD_DIM)), const((1, LANES)), const((R, LANES)),
            const((2, LANES)), const((R, ML_V_DIM)),
        ],
        out_specs=[
            rows(D_MODEL), per_b((CHUNK, LANES)), per_b((CHUNK, LANES)), per_b((width, ML_V_DIM)),
            rows(width), rows(LANES),
        ],
        out_shape=[
            jax.ShapeDtypeStruct((b, D_MODEL), F32),
            jax.ShapeDtypeStruct((b, CHUNK, LANES), F32), jax.ShapeDtypeStruct((b, CHUNK, LANES), F32),
            jax.ShapeDtypeStruct((b, width, ML_V_DIM), F32), jax.ShapeDtypeStruct((b, width), F32),
            jax.ShapeDtypeStruct((b, LANES), F32),
        ],
        compiler_params=pltpu.CompilerParams(dimension_semantics=("arbitrary",), vmem_limit_bytes=VMEM_LIMIT_BYTES),
        name="even_sample",
    )(proj, bk, bv, c, nrow, mrow, *tabs, qn, kn, sink_rows, gb, onorm_rows)


def _odd_sample_kernel(p_ref, cb_ref, s_ref, r_ref, cosr_ref, sinr_ref, convw_ref, convb_ref, dtb_ref, alog_ref,
                       drow_ref, snorm_ref, spread_ref, rdec_ref, rnorm_ref,
                       y_ref, ncb_ref, ns_ref, nr_ref):
    R = SUBLANES
    xbc = p_ref[:, O_XBC:O_XBC + SSD_CONV_DIM]
    acc = cb_ref[0] * convw_ref[0:1]
    for jj in range(1, SSD_CONV - 1):
        acc = acc + cb_ref[jj] * convw_ref[jj:jj + 1]
    acc = acc + xbc * convw_ref[SSD_CONV - 1:SSD_CONV]
    xact = _silu(acc + convb_ref[...])
    for jj in range(SSD_CONV - 2):
        ncb_ref[jj] = cb_ref[jj + 1]
    ncb_ref[SSD_CONV - 2] = xbc

    dt = _softplus(p_ref[:, O_DT:O_DT + LANES] + dtb_ref[...])
    dec = jnp.exp(dt * -jnp.exp(alog_ref[...]))
    xs = xact[:, 0:SSD_INNER]
    xdt = xs * _mm_exact_rhs(dt, spread_ref[...])
    gown = _iota((R, SSD_INNER), 1) // (SSD_INNER // SSD_GROUPS) == _iota((R, SSD_INNER), 0)
    bpart = xact[:, SSD_INNER:SSD_INNER + SSD_GROUPS * SSD_STATE]
    cpart = xact[:, SSD_INNER + SSD_GROUPS * SSD_STATE:SSD_CONV_DIM]

    cosr, sinr = cosr_ref[...], sinr_ref[...]
    width = RET_HEADS * RET_QK_DIM
    q4 = jnp.concatenate([_rope128(p_ref[:, O_RQ + h * LANES:O_RQ + (h + 1) * LANES], cosr, sinr)
                          for h in range(RET_HEADS)], axis=1)
    k4 = jnp.concatenate([_rope128(p_ref[:, O_RK + h * LANES:O_RK + (h + 1) * LANES], cosr, sinr)
                          for h in range(RET_HEADS)], axis=1) * (RET_QK_DIM ** -0.5)
    own = _iota((R, width), 1) // RET_QK_DIM == _iota((R, width), 0)
    gam = rdec_ref[:, 0:1]
    gam_col = jnp.concatenate(
        [jnp.broadcast_to(rdec_ref[h:h + 1, :], (RET_QK_DIM, RET_V_DIM)) for h in range(RET_HEADS)], axis=0)

    def one_row(i):
        brows = _block_rows(bpart, i, SSD_GROUPS, SSD_STATE)
        crows = _block_rows(cpart, i, SSD_GROUPS, SSD_STATE)
        xw = jnp.where(gown, xdt[i:i + 1], 0.0)
        dec_col = jnp.concatenate(
            [jnp.broadcast_to(dec[i:i + 1, h:h + 1], (SSD_HEAD_DIM, SSD_STATE)) for h in range(SSD_HEADS)], axis=0)
        yield
        s_new = dec_col * s_ref[i] + _dg(xw, brows, 0, 0)
        ns_ref[i] = s_new
        yield
        yrows = _dg(crows, s_new, 1, 1)
        y_row = jnp.sum(jnp.where(gown, yrows, 0.0), axis=0, keepdims=True)
        yield

        qrows = jnp.where(own, q4[i:i + 1], 0.0)
        krows = jnp.where(own, k4[i:i + 1], 0.0)
        v4 = _block_rows(p_ref[:, O_RV:O_RV + RET_HEADS * RET_V_DIM], i, RET_HEADS, RET_V_DIM)
        g4 = _block_rows(p_ref[:, O_RG:O_RG + RET_HEADS * RET_V_DIM], i, RET_HEADS, RET_V_DIM)
        r_old = r_ref[i]
        att = jnp.sum(qrows * krows, axis=-1, keepdims=True)
        o = att * v4 + gam * _dg(qrows, r_old, 1, 0)
        yield
        nr_ref[i] = gam_col * r_old + _dg(krows, v4, 0, 0)
        yield
        xc = o - jnp.mean(o, axis=-1, keepdims=True)
        yn = xc * lax.rsqrt(jnp.mean(xc * xc, axis=-1, keepdims=True) + EPS) * rnorm_ref[...] * _silu(g4)
        for h in range(RET_HEADS):
            y_ref[i:i + 1, SSD_INNER + h * RET_V_DIM:SSD_INNER + (h + 1) * RET_V_DIM] = yn[h:h + 1]
        return y_row

    ys = _lockstep([one_row(i) for i in range(R)])
    y = (jnp.concatenate(ys, axis=0) + drow_ref[...] * xs) * _silu(p_ref[:, O_Z:O_Z + SSD_INNER])
    gw = SSD_INNER // SSD_GROUPS
    for g in range(SSD_GROUPS):
        y_ref[:, g * gw:(g + 1) * gw] = _rms(y[:, g * gw:(g + 1) * gw], snorm_ref[:, g * gw:(g + 1) * gw])


def odd_sample(proj, cbuf, s, r, tabs, convw, convb, dtb, alog, drow, snorm, spread, rdec_rows, rnorm_rows):
    b = proj.shape[0]
    R = SUBLANES
    rows = lambda w: pl.BlockSpec((R, w), lambda i: (i, 0))
    per_b = lambda shape: pl.BlockSpec((R,) + shape, lambda i: (i,) + (0,) * len(shape))
    const = lambda shape: pl.BlockSpec(shape, lambda i: (0,) * len(shape))
    tab = const((R, LANES))
    conv = pl.BlockSpec((SSD_CONV - 1, R, SSD_CONV_DIM), lambda i: (0, i, 0))
    ywidth = SSD_INNER + RET_HEADS * RET_V_DIM
    sshape = (SSD_HEADS * SSD_HEAD_DIM, SSD_STATE)
    rshape = (RET_HEADS * RET_QK_DIM, RET_V_DIM)
    return pl.pallas_call(
        _odd_sample_kernel,
        grid=(b // R,),
        in_specs=[
            rows(ODD_COLS), conv, per_b(sshape), per_b(rshape), tab, tab,
            const((SSD_CONV, SSD_CONV_DIM)), const((1, SSD_CONV_DIM)), const((1, LANES)), const((1, LANES)),
            const((1, SSD_INNER)), const((1, SSD_INNER)), const((LANES, SSD_INNER)),
            const((R, LANES)), const((R, RET_V_DIM)),
        ],
        out_specs=[rows(ywidth), conv, per_b(sshape), per_b(rshape)],
        out_shape=[
            jax.ShapeDtypeStruct((b, ywidth), F32),
            jax.ShapeDtypeStruct((SSD_CONV - 1, b, SSD_CONV_DIM), F32),
            jax.ShapeDtypeStruct((b,) + sshape, F32), jax.ShapeDtypeStruct((b,) + rshape, F32),
        ],
        compiler_params=pltpu.CompilerParams(dimension_semantics=("arbitrary",), vmem_limit_bytes=VMEM_LIMIT_BYTES),
        name="odd_sample",
    )(proj, cbuf, s, r, *tabs, convw, convb, dtb, alog, drow, snorm, spread, rdec_rows, rnorm_rows)


def _pad_cols(w, n):
    return jnp.pad(w, ((0, 0), (0, n - w.shape[1])))


def _even_w_in(w):
    w = w.astype(BF16)
    mi, mf = w[:, E_GATE:E_GATE + ML_HEADS], w[:, E_GATE + ML_HEADS:E_GATE + 2 * ML_HEADS]
    return jnp.concatenate([w[:, :E_GATE], _pad_cols(mi, GATE_F_LANE), _pad_cols(mf, LANES - GATE_F_LANE)], axis=1)


def _odd_w_in(w):
    w = w.astype(BF16)
    dt0 = O_XBC + SSD_CONV_DIM
    return jnp.concatenate([w[:, :dt0], w[:, dt0 + SSD_HEADS:], _pad_cols(w[:, dt0:dt0 + SSD_HEADS], LANES)], axis=1)


def _lane_angles(pos, rot_dim, theta, freq_of_lane):
    half = rot_dim // 2
    inv = jnp.power(jnp.float32(theta), -jnp.arange(half, dtype=F32) * (2.0 / rot_dim))
    return jnp.asarray(pos).astype(F32)[:, None] * inv[freq_of_lane][None, :]


def _rope16_tables(pos):
    half = SW_ROT_DIM // 2
    d = np.arange(LANES) % SW_HEAD_DIM
    ang = _lane_angles(pos, SW_ROT_DIM, ROPE_THETA, d % half)
    cos, sin = jnp.cos(ang), jnp.sin(ang)
    return (jnp.where(d < SW_ROT_DIM, cos, 1.0), jnp.where(d < half, -sin, 0.0),
            jnp.where((d >= half) & (d < SW_ROT_DIM), sin, 0.0))


def _rope128_tables(pos):
    half = RET_QK_DIM // 2
    lane = np.arange(LANES)
    ang = _lane_angles(pos, RET_QK_DIM, RET_ROPE_THETA, lane % half)
    return jnp.cos(ang), jnp.where(lane < half, -jnp.sin(ang), jnp.sin(ang))


def _ret_consts():
    L = CHUNK
    f = np.float32
    lg = np.log(f(1.0) - np.exp2(f(-5.0) - np.arange(RET_HEADS, dtype=f))).astype(f)
    idx = np.arange(L, dtype=f)
    diff = idx[:, None] - idx[None, :]
    with np.errstate(invalid="ignore"):
        dmat = np.exp(np.where(diff >= 0, diff[None] * lg[:, None, None], -np.inf)).astype(f)
    q_scale = np.exp((idx[None] + f(1.0)) * lg[:, None]).astype(f)
    k_scale = np.exp((f(L) - f(1.0) - idx[None]) * lg[:, None]).astype(f)
    chunk_decay = np.exp(f(L) * lg).astype(f)
    bc = lambda t: np.ascontiguousarray(np.broadcast_to(t[:, :, None], (RET_HEADS, L, LANES)))
    cd = np.ascontiguousarray(np.broadcast_to(chunk_decay[:, None, None], (RET_HEADS, L, LANES)))
    return dmat, bc(q_scale), bc(k_scale), cd, lg


def _rows8(t):
    return jnp.pad(t, ((0, SUBLANES - t.shape[0]), (0, 0)))


def _gate_bias_rows(gb):
    ib = jnp.broadcast_to(gb[:ML_HEADS, None], (ML_HEADS, LANES))
    fb = jnp.broadcast_to(gb[ML_HEADS:, None], (ML_HEADS, LANES))
    return jnp.concatenate([_rows8(ib), _rows8(fb)], axis=0)


def kernel(x_prompt, x_sample, cache_mem_k, cache_mem_v, cache_swa_k, cache_swa_v, state_mlstm_C, state_mlstm_n,
           state_mlstm_m, state_ssd_conv, state_ssd, state_ret, mem_prompt, norm_mix, norm_xattn, norm_mem, norm_ffn,
           even_w_in, mlstm_gate_bias, swa_q_norm, swa_k_norm, swa_sinks, mlstm_out_norm, even_w_out, odd_w_in,
           ssd_conv_w, ssd_conv_b, ssd_dt_bias, ssd_a_log, ssd_d, ssd_norm, ret_norm, odd_w_out, mem_wq, mem_wk,
           mem_wv, mem_q_norm, mem_k_norm, mem_wo, ffn_w1, ffn_w2):
    bp, seq, d = x_prompt.shape
    bs = x_sample.shape[0]
    depth = norm_mix.shape[0]
    tm = PROMPT_ROW_TILE

    pos_p = np.arange(seq, dtype=np.int32)
    pos_s = np.full((SUBLANES,), PAST_LEN, dtype=np.int32)
    tab16_p, tab16_s = _rope16_tables(pos_p), _rope16_tables(pos_s)
    tab128_p, tab128_s = _rope128_tables(pos_p), _rope128_tables(pos_s)
    dmat, q_scale, k_scale, chunk_decay, lg = _ret_consts()
    rdec_rows = _rows8(jnp.asarray(np.broadcast_to(np.exp(lg)[:, None], (RET_HEADS, LANES))))
    ii = np.arange(CHUNK)
    triu = (ii[:, None] <= ii[None, :]).astype(np.float32)
    tril = (ii[:, None] >= ii[None, :]).astype(np.float32)
    jj = np.arange(LANES)
    segm = np.where(jj[:, None] // SW_HEAD_DIM == jj[None, :] // SW_HEAD_DIM, 1.0 / SW_HEAD_DIM, 0.0).astype(np.float32)
    spread = (jj[:, None] == np.arange(SSD_INNER)[None, :] // SSD_HEAD_DIM).astype(np.float32)
    row1 = lambda t: t.reshape(1, -1).astype(F32)
    pad_lanes = lambda t: jnp.pad(t.reshape(1, -1).astype(F32), ((0, 0), (0, LANES - t.shape[-1])))

    yp = x_prompt.reshape(bp * seq, d)
    ys = x_sample.reshape(bs, d)
    mem = mem_prompt.reshape(bp * MEM_LEN, d)
    cmk = cache_mem_k.reshape(depth, bs, MEM_LEN * MEM_HEADS, MEM_HEAD_DIM)
    cmv = cache_mem_v.reshape(depth, bs, MEM_LEN * MEM_HEADS, MEM_HEAD_DIM)
    w1s, w2s = ffn_w1.astype(BF16), ffn_w2.astype(BF16)
    p_mk, p_mv = [], []
    outs = {}
    for l in range(depth):
        g_mix = row1(norm_mix[l])
        if l % 2 == 0:
            e = l // 2
            w_in = _even_w_in(even_w_in[e])
            w_out = even_w_out[e].astype(BF16)
            qn = row1(jnp.tile(swa_q_norm[e], SW_HEADS))
            kn = row1(jnp.tile(swa_k_norm[e], SW_KV_HEADS))
            gb = _gate_bias_rows(mlstm_gate_bias[e].astype(F32))
            onorm = row1(mlstm_out_norm[e])
            sinks = swa_sinks[e].astype(F32)
            mix_p, kc, vc, caug, mm = even_prompt(yp.reshape(bp, seq, d), g_mix, w_in, tab16_p, qn, kn, sinks, segm,
                                                  triu, gb, onorm)
            mix_p = mix_p.reshape(bp * seq, -1)
            outs["p_swk"] = kc.reshape(1, bp, CHUNK, SW_KV_HEADS, SW_HEAD_DIM)
            outs["p_swv"] = vc.reshape(1, bp, CHUNK, SW_KV_HEADS, SW_HEAD_DIM)
            outs["p_c"] = caug[..., :ML_V_DIM].reshape(1, bp, ML_HEADS, ML_QK_DIM, ML_V_DIM)
            outs["p_n"] = caug[..., ML_V_DIM].reshape(1, bp, ML_HEADS, ML_QK_DIM)
            outs["p_m"] = mm[:, :ML_HEADS, 0].reshape(1, bp, ML_HEADS)

            proj_s = norm_proj(ys, g_mix, w_in, tm=bs)
            sink_rows = jnp.broadcast_to(sinks[:, None], (SW_HEADS, LANES))
            onorm_rows = _rows8(mlstm_out_norm[e].astype(F32).reshape(ML_HEADS, ML_V_DIM))
            gb_lanes = jnp.concatenate([pad_lanes(mlstm_gate_bias[e][:ML_HEADS]),
                                        pad_lanes(mlstm_gate_bias[e][ML_HEADS:])], axis=0)
            mix_s, nk, nv, ncst, nn, nm = even_sample(
                proj_s,
                cache_swa_k[e].reshape(bs, CHUNK, LANES), cache_swa_v[e].reshape(bs, CHUNK, LANES),
                state_mlstm_C[e].reshape(bs, ML_HEADS * ML_QK_DIM, ML_V_DIM),
                state_mlstm_n[e].reshape(bs, ML_HEADS * ML_QK_DIM),
                jnp.pad(state_mlstm_m[e], ((0, 0), (0, LANES - ML_HEADS))),
                tab16_s, qn, kn, sink_rows, gb_lanes, onorm_rows)
            outs["s_swk"] = nk.reshape(1, bs, CHUNK, SW_KV_HEADS, SW_HEAD_DIM)
            outs["s_swv"] = nv.reshape(1, bs, CHUNK, SW_KV_HEADS, SW_HEAD_DIM)
            outs["s_c"] = ncst.reshape(1, bs, ML_HEADS, ML_QK_DIM, ML_V_DIM)
            outs["s_n"] = nn.reshape(1, bs, ML_HEADS, ML_QK_DIM)
            outs["s_m"] = nm[:, :ML_HEADS].reshape(1, bs, ML_HEADS)
        else:
            o = l // 2
            w_in = _odd_w_in(odd_w_in[o])
            w_out = odd_w_out[o].astype(BF16)
            convw = ssd_conv_w[o].astype(F32)
            convb = row1(ssd_conv_b[o])
            dtb = pad_lanes(ssd_dt_bias[o])
            alog = pad_lanes(ssd_a_log[o])
            drow = row1(jnp.repeat(ssd_d[o].astype(F32), SSD_HEAD_DIM))
            snorm = row1(ssd_norm[o])
            rnorm = row1(ret_norm[o])
            mix_p, ctail, sst, rst = odd_prompt(yp.reshape(bp, seq, d), g_mix, w_in, tab128_p, convw, convb, dtb, alog,
                                                drow, snorm, tril, (dmat, q_scale, k_scale, chunk_decay), rnorm,
                                                group=ODD_SEQS_PER_STEP)
            mix_p = mix_p.reshape(bp * seq, -1)
            outs["p_conv"] = ctail[:, SUBLANES - (SSD_CONV - 1):, :].reshape(1, bp, SSD_CONV - 1, SSD_CONV_DIM)
            outs["p_ssd"] = sst.reshape(1, bp, SSD_HEADS, SSD_HEAD_DIM, SSD_STATE)
            outs["p_ret"] = rst.reshape(1, bp, RET_HEADS, RET_QK_DIM, RET_V_DIM)

            proj_s = norm_proj(ys, g_mix, w_in, tm=bs)
            rnorm_rows = _rows8(ret_norm[o].astype(F32).reshape(RET_HEADS, RET_V_DIM))
            mix_s, ncb, ns, nr = odd_sample(
                proj_s, jnp.swapaxes(state_ssd_conv[o], 0, 1),
                state_ssd[o].reshape(bs, SSD_HEADS * SSD_HEAD_DIM, SSD_STATE),
                state_ret[o].reshape(bs, RET_HEADS * RET_QK_DIM, RET_V_DIM),
                tab128_s, convw, convb, dtb, alog, drow, snorm, spread, rdec_rows, rnorm_rows)
            outs["s_conv"] = jnp.swapaxes(ncb, 0, 1).reshape(1, bs, SSD_CONV - 1, SSD_CONV_DIM)
            outs["s_ssd"] = ns.reshape(1, bs, SSD_HEADS, SSD_HEAD_DIM, SSD_STATE)
            outs["s_ret"] = nr.reshape(1, bs, RET_HEADS, RET_QK_DIM, RET_V_DIM)

        wkv = jnp.concatenate([mem_wk[l], mem_wv[l]], axis=1).astype(BF16)
        qnorm = row1(mem_q_norm[l])
        mkv = norm_proj(mem, row1(norm_mem[l]), wkv, tm=tm, head_norm=row1(mem_k_norm[l]), head_norm_cols=MEM_WIDTH)
        p_mk.append(mkv[:, :MEM_WIDTH].reshape(bp, MEM_LEN, MEM_HEADS, MEM_HEAD_DIM))
        p_mv.append(mkv[:, MEM_WIDTH:].reshape(bp, MEM_LEN, MEM_HEADS, MEM_HEAD_DIM))
        wq, wo = mem_wq[l].astype(BF16), mem_wo[l].astype(BF16)
        gx, gf = row1(norm_xattn[l]), row1(norm_ffn[l])
        ys = res_proj(ys, mix_s, w_out)
        qs = norm_proj(ys, gx, wq, tm=bs, head_norm=qnorm, head_norm_cols=MEM_WIDTH)
        yp, att_s = post_prompt(yp, mix_p, w_out, gx, wq, qnorm, mkv, wo, gf, w1s, w2s, qs, cmk, cmv, layer=l,
                                rows_per_batch=seq, tm=tm)
        ys = res_proj_ffn(ys, att_s, wo, gf, w1s, w2s, layer=l)

    return (yp.reshape(bp, seq, d), ys.reshape(bs, 1, d),
            jnp.stack(p_mk), jnp.stack(p_mv), outs["p_swk"], outs["p_swv"], outs["p_c"], outs["p_n"], outs["p_m"],
            outs["p_conv"], outs["p_ssd"], outs["p_ret"],
            outs["s_swk"], outs["s_swv"], outs["s_c"], outs["s_n"], outs["s_m"],
            outs["s_conv"], outs["s_ssd"], outs["s_ret"])
```
